```python
import jax, jax.numpy as jnp
from jax import lax
import numpy as np

D_MODEL = 1024
BATCH = 8
SEQ = 2048
DEPTH = 2
DEC_BATCH = 32
DEC_SEQ = 64
PAST_LEN = 4096

CHUNK = 64
D_MIX = D_MODEL
ATTN_HEADS = 8
ATTN_KV_HEADS = 2
HEAD_DIM = 64
ATTN_GROUP = ATTN_HEADS // ATTN_KV_HEADS
ATTN_DIM = ATTN_HEADS * HEAD_DIM
KV_DIM = ATTN_KV_HEADS * HEAD_DIM
WINDOW = 128
GLA_HEADS = 4
GLA_DK = 32
GLA_DV = 64
GLA_KDIM = GLA_HEADS * GLA_DK
GLA_VDIM = GLA_HEADS * GLA_DV
GLA_LOWRANK = 16
GLA_TAU = 16.0
CONV_DIM = D_MIX - ATTN_DIM - GLA_VDIM
CONV_WIDTH = 31
NORM_EPS = 1e-6
NEG_INF = -1e30
IN_SPLITS = (ATTN_DIM, KV_DIM, KV_DIM, ATTN_DIM,
             GLA_KDIM, GLA_KDIM, GLA_VDIM, GLA_VDIM, GLA_LOWRANK,
             CONV_DIM, CONV_DIM, CONV_DIM)
IN_COLS = 2832

kernel_name = 'hybrid_streaming_encoder_step'


def rmsnorm(x, g):
    xf = x.astype(jnp.float32)
    xf = xf * lax.rsqrt(jnp.mean(xf * xf, axis=-1, keepdims=True) + NORM_EPS)
    return (xf * g.astype(jnp.float32)).astype(x.dtype)


def layernorm(x, g, b):
    xf = x.astype(jnp.float32)
    mu = jnp.mean(xf, axis=-1, keepdims=True)
    var = jnp.mean(jnp.square(xf - mu), axis=-1, keepdims=True)
    y = (xf - mu) * lax.rsqrt(var + NORM_EPS) * g.astype(jnp.float32) + b.astype(jnp.float32)
    return y.astype(x.dtype)


def alibi_slopes():
    h = jnp.arange(1, ATTN_HEADS + 1, dtype=jnp.float32)
    return jnp.exp2(-8.0 * h / ATTN_HEADS)


def band_attention(q, k_all, v_all, sinks, q_pos0):
    B, L = q.shape[0], q.shape[1]
    C = min(CHUNK, L)
    N = L // C
    band = WINDOW + C
    idx = (jnp.arange(N) * C)[:, None] + jnp.arange(band)[None, :]
    kb = k_all[:, idx].astype(jnp.float32)
    vb = v_all[:, idx].astype(jnp.float32)
    qb = q.astype(jnp.float32).reshape(B, N, C, ATTN_KV_HEADS, ATTN_GROUP, HEAD_DIM)
    q_pos = q_pos0 + jnp.arange(L).reshape(N, C)
    k_pos = q_pos0 - WINDOW + idx
    valid = k_pos >= 0
    s = jnp.einsum('bnqhgd,bnkhd->bnhgqk', qb, kb) * (HEAD_DIM ** -0.5)
    dist = jnp.abs(q_pos[:, :, None] - k_pos[:, None, :]).astype(jnp.float32)
    slopes = alibi_slopes().reshape(ATTN_KV_HEADS, ATTN_GROUP)
    s = s - slopes[None, None, :, :, None, None] * dist[None, :, None, None, :, :]
    s = jnp.where(valid[None, :, None, None, None, :], s, NEG_INF)
    sink = sinks.astype(jnp.float32).reshape(ATTN_KV_HEADS, ATTN_GROUP)[None, None, :, :, None, None]
    m = jnp.maximum(jnp.max(s, axis=-1, keepdims=True), sink)
    p = jnp.exp(s - m)
    w = p / (jnp.sum(p, axis=-1, keepdims=True) + jnp.exp(sink - m))
    o = jnp.einsum('bnhgqk,bnkhd->bnqhgd', w, vb)
    return o.reshape(B, L, ATTN_DIM).astype(q.dtype)


def gla_chunked(q, k, v, log_a, s0):
    B, L, H, _ = q.shape
    DV = v.shape[-1]
    C = min(CHUNK, L)
    N = L // C

    def to_blocks(t):
        return t.astype(jnp.float32).reshape(B, N, C, H, t.shape[-1]).swapaxes(0, 1)

    causal = jnp.tril(jnp.ones((C, C), dtype=bool))

    def step(S, inp):
        qc, kc, vc, gc = inp
        b = jnp.cumsum(gc, axis=1)
        b_last = b[:, -1:]
        q_t = qc * jnp.exp(b)
        k_t = kc * jnp.exp(-b)
        A = jnp.where(causal, jnp.einsum('bthd,bshd->bhts', q_t, k_t), 0.0)
        o = jnp.einsum('bhts,bshv->bthv', A, vc) + jnp.einsum('bthd,bhdv->bthv', q_t, S)
        k_end = kc * jnp.exp(b_last - b)
        S = jnp.exp(b_last[:, 0])[..., None] * S + jnp.einsum('bshd,bshv->bhdv', k_end, vc)
        return S, o

    S, o = lax.scan(step, s0.astype(jnp.float32), (to_blocks(q), to_blocks(k), to_blocks(v), to_blocks(log_a)))
    o = o.swapaxes(0, 1).reshape(B, L, H, DV)
    return o.astype(v.dtype), S


def mixer_layer(x, q_pos0, k_hist, v_hist, gla_s0, conv_hist,
                norm_g, w_in, q_g, k_g, sinks, gla_w_up, gla_b, gla_g,
                conv_w, conv_b, ln_g, ln_b, w_pw, b_pw, w_out):
    B, L, _ = x.shape
    h = rmsnorm(x, norm_g)
    proj = h @ w_in
    offsets = [int(o) for o in np.cumsum(IN_SPLITS)[:-1]]
    aq, ak, av, ag, gq, gk, gv, gg, glr, cv, cgl, cg = jnp.split(proj, offsets, axis=-1)

    q = rmsnorm(aq.reshape(B, L, ATTN_HEADS, HEAD_DIM), q_g)
    k = rmsnorm(ak.reshape(B, L, ATTN_KV_HEADS, HEAD_DIM), k_g)
    v = av.reshape(B, L, ATTN_KV_HEADS, HEAD_DIM)
    k_all = jnp.concatenate([k_hist.astype(k.dtype), k], axis=1)
    v_all = jnp.concatenate([v_hist.astype(v.dtype), v], axis=1)
    y_attn = band_attention(q, k_all, v_all, sinks, q_pos0) * jax.nn.silu(ag)
    new_k = k_all[:, -WINDOW:]
    new_v = v_all[:, -WINDOW:]

    log_a = jax.nn.log_sigmoid((glr @ gla_w_up + gla_b).astype(jnp.float32)) / GLA_TAU
    o, s_fin = gla_chunked(gq.reshape(B, L, GLA_HEADS, GLA_DK) * (GLA_DK ** -0.5),
                           gk.reshape(B, L, GLA_HEADS, GLA_DK),
                           gv.reshape(B, L, GLA_HEADS, GLA_DV),
                           log_a.reshape(B, L, GLA_HEADS, GLA_DK), gla_s0)
    y_gla = rmsnorm(o, gla_g).reshape(B, L, GLA_VDIM) * jax.nn.silu(gg)

    u = cv * jax.nn.sigmoid(cgl)
    u_all = jnp.concatenate([conv_hist.astype(u.dtype), u], axis=1)
    c = lax.conv_general_dilated(u_all, conv_w[:, None, :].astype(u_all.dtype), window_strides=(1,),
                                 padding='VALID', dimension_numbers=('NWC', 'WIO', 'NWC'),
                                 feature_group_count=CONV_DIM) + conv_b
    c = jax.nn.silu(layernorm(c, ln_g, ln_b)) @ w_pw + b_pw
    y_conv = c * jax.nn.silu(cg)
    new_conv = u_all[:, -(CONV_WIDTH - 1):]

    y = jnp.concatenate([y_attn, y_gla.astype(y_attn.dtype), y_conv.astype(y_attn.dtype)], axis=-1) @ w_out
    return x + y.astype(x.dtype), new_k, new_v, s_fin, new_conv


def setup_inputs(seed: int = 0) -> dict:
    key = jax.random.key(seed)
    ks = jax.random.split(key, 24)
    f32 = jnp.float32
    nrm = lambda k, shape, scale: jax.random.normal(k, shape, f32) * scale
    win_rows = min(WINDOW, PAST_LEN)
    return {
        'x_prompt': nrm(ks[0], (BATCH, SEQ, D_MODEL), 1.0),
        'x_sample': nrm(ks[1], (DEC_BATCH, DEC_SEQ, D_MODEL), 1.0),
        'cache_k': nrm(ks[2], (DEPTH, DEC_BATCH, win_rows, ATTN_KV_HEADS, HEAD_DIM), 1.0),
        'cache_v': nrm(ks[3], (DEPTH, DEC_BATCH, win_rows, ATTN_KV_HEADS, HEAD_DIM), 0.5),
        'state_gla': nrm(ks[4], (DEPTH, DEC_BATCH, GLA_HEADS, GLA_DK, GLA_DV), 0.5),
        'state_conv': nrm(ks[5], (DEPTH, DEC_BATCH, CONV_WIDTH - 1, CONV_DIM), 0.5),
        'norm_gain': 1.0 + nrm(ks[6], (DEPTH, D_MODEL), 0.05),
        'w_in': nrm(ks[7], (DEPTH, D_MODEL, IN_COLS), D_MODEL ** -0.5),
        'q_norm_gain': 1.0 + nrm(ks[8], (DEPTH, HEAD_DIM), 0.05),
        'k_norm_gain': 1.0 + nrm(ks[9], (DEPTH, HEAD_DIM), 0.05),
        'attn_sinks': nrm(ks[10], (DEPTH, ATTN_HEADS), 0.5),
        'gla_w_gate_up': nrm(ks[11], (DEPTH, GLA_LOWRANK, GLA_KDIM), GLA_LOWRANK ** -0.5),
        'gla_b_gate': nrm(ks[12], (DEPTH, GLA_KDIM), 0.1),
        'gla_norm_gain': 1.0 + nrm(ks[13], (DEPTH, GLA_DV), 0.05),
        'conv_w': nrm(ks[14], (DEPTH, CONV_WIDTH, CONV_DIM), CONV_WIDTH ** -0.5),
        'conv_b': nrm(ks[15], (DEPTH, CONV_DIM), 0.02),
        'conv_ln_gain': 1.0 + nrm(ks[16], (DEPTH, CONV_DIM), 0.05),
        'conv_ln_bias': nrm(ks[17], (DEPTH, CONV_DIM), 0.02),
        'conv_w_pw': nrm(ks[18], (DEPTH, CONV_DIM, CONV_DIM), CONV_DIM ** -0.5),
        'conv_b_pw': nrm(ks[19], (DEPTH, CONV_DIM), 0.02),
        'w_out': nrm(ks[20], (DEPTH, D_MIX, D_MODEL), D_MIX ** -0.5),
    }


def reference(x_prompt, x_sample, cache_k, cache_v, state_gla, state_conv,
              norm_gain, w_in, q_norm_gain, k_norm_gain, attn_sinks,
              gla_w_gate_up, gla_b_gate, gla_norm_gain,
              conv_w, conv_b, conv_ln_gain, conv_ln_bias, conv_w_pw, conv_b_pw, w_out):
    bp = x_prompt.shape[0]
    zk = jnp.zeros((bp, WINDOW, ATTN_KV_HEADS, HEAD_DIM), x_prompt.dtype)
    zs = jnp.zeros((bp, GLA_HEADS, GLA_DK, GLA_DV), jnp.float32)
    zc = jnp.zeros((bp, CONV_WIDTH - 1, CONV_DIM), x_prompt.dtype)
    yp, ys = x_prompt, x_sample
    pk, pv, ps, pc, sk, sv, ss, sc = [], [], [], [], [], [], [], []
    for l in range(DEPTH):
        params = (norm_gain[l], w_in[l], q_norm_gain[l], k_norm_gain[l], attn_sinks[l],
                  gla_w_gate_up[l], gla_b_gate[l], gla_norm_gain[l],
                  conv_w[l], conv_b[l], conv_ln_gain[l], conv_ln_bias[l], conv_w_pw[l], conv_b_pw[l], w_out[l])
        yp, a, b, c, d = mixer_layer(yp, 0, zk, zk, zs, zc, *params)
        pk.append(a); pv.append(b); ps.append(c); pc.append(d)
        ys, a, b, c, d = mixer_layer(ys, PAST_LEN, cache_k[l], cache_v[l], state_gla[l], state_conv[l], *params)
        sk.append(a); sv.append(b); ss.append(c); sc.append(d)
    return (yp, ys, jnp.stack(pk), jnp.stack(pv), jnp.stack(ps), jnp.stack(pc),
            jnp.stack(sk), jnp.stack(sv), jnp.stack(ss), jnp.stack(sc))
```

```python
import functools
import math

import jax
import jax.numpy as jnp
from jax import lax
from jax.experimental import pallas as pl
from jax.experimental.pallas import tpu as pltpu

D_MODEL = 1024
CHUNK = 64
ATTN_HEADS = 8
ATTN_KV_HEADS = 2
HEAD_DIM = 64
ATTN_GROUP = ATTN_HEADS // ATTN_KV_HEADS
ATTN_DIM = ATTN_HEADS * HEAD_DIM
KV_DIM = ATTN_KV_HEADS * HEAD_DIM
WINDOW = 128
BAND = WINDOW + CHUNK
GLA_HEADS = 4
GLA_DK = 32
GLA_DV = 64
GLA_KDIM = GLA_HEADS * GLA_DK
GLA_VDIM = GLA_HEADS * GLA_DV
GLA_LOWRANK = 16
GLA_TAU = 16.0
CONV_DIM = 256
CONV_WIDTH = 31
CONV_HIST = CONV_WIDTH - 1
NORM_EPS = 1e-6
NEG_INF = -1e30
PAST_LEN = 4096

LANES = 128
CHUNK_SHIFT = CHUNK.bit_length() - 1
GLA_DK_SHIFT = GLA_DK.bit_length() - 1
GLA_DV_SHIFT = GLA_DV.bit_length() - 1
C_AQ, C_AK, C_AV, C_AG = 0, 512, 640, 768
C_GQ, C_GK, C_GV, C_GG = 1280, 1408, 1536, 1792
C_CV, C_CGL, C_CG, C_GLR = 2048, 2304, 2560, 2816
IN_COLS_PACKED = C_GLR + LANES
CONV_PAD = 32
VEC_ROWS = 16
(V_NORM, V_QG, V_KG, V_GLAB, V_GLAG, V_CONVB, V_LNG, V_LNB, V_BPW) = range(9)

PROMPT_TILE = 256
SAMPLE_SEQS = 4
VMEM_LIMIT_BYTES = 48 * 1024 * 1024

_BF = jnp.bfloat16
_F32 = jnp.float32


def _group_allreduce(x, group):
    lane = lax.broadcasted_iota(jnp.int32, x.shape, 1)
    s = 1
    while s < group:
        up = pltpu.roll(x, s, 1)
        dn = pltpu.roll(x, LANES - s, 1)
        x = x + jnp.where((lane & s) != 0, up, dn)
        s *= 2
    return x


def _group_rms_scale(x, group):
    outs = []
    for c0 in range(0, x.shape[1], LANES):
        blk = x[:, c0:c0 + LANES]
        ss = _group_allreduce(blk * blk, group)
        outs.append(blk * lax.rsqrt(ss * (1.0 / group) + NORM_EPS))
    return outs[0] if len(outs) == 1 else jnp.concatenate(outs, axis=1)


def _silu(x):
    return x * jax.nn.sigmoid(x)


def _dot(a, b):
    return jnp.dot(a, b, preferred_element_type=_F32)


def _dot_nt(a, b):
    return lax.dot_general(a, b, (((1,), (1,)), ((), ())), preferred_element_type=_F32)


def _dot_tn(a, b):
    return lax.dot_general(a, b, (((0,), (0,)), ((), ())), preferred_element_type=_F32)


def _layer_kernel(*refs, tile, n_seq, n_chunk, carry, q_pos0):
    if carry:
        (sinks_ref, x_ref, w_in_ref, w_up_ref, w_pw_ref, w_out_ref, vec_ref, convw_ref,
         y_ref, nk_ref, nv_ref, ns_ref, nc_ref,
         kd_ref, vd_ref, s_ref, u_ref, qlo_ref, qhi_ref, ycat_ref) = refs
    else:
        (sinks_ref, x_ref, hk_ref, hv_ref, hs_ref, hc_ref,
         w_in_ref, w_up_ref, w_pw_ref, w_out_ref, vec_ref, convw_ref,
         y_ref, nk_ref, nv_ref, ns_ref, nc_ref,
         kd_ref, vd_ref, s_ref, u_ref, qlo_ref, qhi_ref, ycat_ref) = refs
    t = pl.program_id(1)
    n_t = pl.num_programs(1)
    seq_rows = n_chunk * CHUNK

    lane128 = lax.broadcasted_iota(jnp.int32, (1, LANES), 1)
    lo_half = lane128 < HEAD_DIM

    def dup_halves(a):
        sw = pltpu.roll(a, HEAD_DIM, 1)
        return jnp.concatenate([jnp.where(lo_half, a, sw), jnp.where(lo_half, sw, a)], axis=1)

    if carry:
        @pl.when(t == 0)
        def _():
            kd_ref[:, 0:WINDOW, :] = jnp.zeros((n_seq, WINDOW, 2 * LANES), _BF)
            vd_ref[:, 0:WINDOW, :] = jnp.zeros((n_seq, WINDOW, 2 * LANES), _BF)
            s_ref[...] = jnp.zeros(s_ref.shape, _F32)
            u_ref[:, 0:CONV_PAD, :] = jnp.zeros((n_seq, CONV_PAD, CONV_DIM), _F32)
    else:
        for s in range(n_seq):
            kd_ref[s, 0:WINDOW, :] = dup_halves(hk_ref[s]).astype(_BF)
            vd_ref[s, 0:WINDOW, :] = dup_halves(hv_ref[s]).astype(_BF)
            s_ref[s] = jnp.zeros((GLA_KDIM, GLA_VDIM), _F32)
            for h in range(GLA_HEADS):
                s_ref[s, h * GLA_DK:(h + 1) * GLA_DK, h * GLA_DV:(h + 1) * GLA_DV] = hs_ref[s, h]
            u_ref[s, CONV_PAD - CONV_HIST:CONV_PAD, :] = hc_ref[s]

    x = x_ref[...]
    ms = jnp.mean(x * x, axis=-1, keepdims=True)
    hb = (x * lax.rsqrt(ms + NORM_EPS) * vec_ref[V_NORM:V_NORM + 1, :]).astype(_BF)

    def proj(c0, width):
        return _dot(hb, w_in_ref[:, c0:c0 + width])

    qn = _group_rms_scale(proj(C_AQ, ATTN_DIM), HEAD_DIM) * (vec_ref[V_QG:V_QG + 1, 0:ATTN_DIM] * HEAD_DIM ** -0.5)
    lo4 = jnp.concatenate([lo_half] * (ATTN_DIM // LANES), axis=1)
    qlo_ref[...] = jnp.where(lo4, qn, 0.0).astype(_BF)
    qhi_ref[...] = jnp.where(lo4, 0.0, qn).astype(_BF)
    kn = _group_rms_scale(proj(C_AK, KV_DIM), HEAD_DIM) * vec_ref[V_KG:V_KG + 1, 0:KV_DIM]
    vv = proj(C_AV, KV_DIM)
    kdup = dup_halves(kn).astype(_BF)
    vdup = dup_halves(vv).astype(_BF)
    for s in range(n_seq):
        kd_ref[s, WINDOW:WINDOW + seq_rows, :] = kdup[s * seq_rows:(s + 1) * seq_rows]
        vd_ref[s, WINDOW:WINDOW + seq_rows, :] = vdup[s * seq_rows:(s + 1) * seq_rows]
    if carry:
        @pl.when(t == n_t - 1)
        def _():
            nk_ref[0] = kn[tile - WINDOW:tile]
            nv_ref[0] = vv[tile - WINDOW:tile]
    else:
        for s in range(n_seq):
            nk_ref[s, 0:WINDOW - seq_rows, :] = hk_ref[s, seq_rows:WINDOW, :]
            nv_ref[s, 0:WINDOW - seq_rows, :] = hv_ref[s, seq_rows:WINDOW, :]
            nk_ref[s, WINDOW - seq_rows:WINDOW, :] = kn[s * seq_rows:(s + 1) * seq_rows]
            nv_ref[s, WINDOW - seq_rows:WINDOW, :] = vv[s * seq_rows:(s + 1) * seq_rows]

    gate_a = _silu(proj(C_AG, ATTN_DIM))

    rows = ATTN_GROUP * CHUNK
    r_iota = lax.broadcasted_iota(jnp.int32, (rows, BAND), 0)
    c_iota = lax.broadcasted_iota(jnp.int32, (rows, BAND), 1)
    dist = jnp.abs((r_iota & (CHUNK - 1)) + WINDOW - c_iota).astype(_F32)
    r_head = r_iota >> CHUNK_SHIFT
    r_head1 = lax.broadcasted_iota(jnp.int32, (rows, 1), 0) >> CHUNK_SHIFT
    biases, sink_cols = [], []
    for j in range(ATTN_KV_HEADS):
        slope = jnp.zeros((rows, BAND), _F32)
        sink = jnp.zeros((rows, 1), _F32)
        for r in range(ATTN_GROUP):
            hd = j * ATTN_GROUP + r
            slope = jnp.where(r_head == r, 2.0 ** (-8.0 * (hd + 1) / ATTN_HEADS), slope)
            sink = jnp.where(r_head1 == r, sinks_ref[hd], sink)
        biases.append(slope * dist)
        sink_cols.append(sink)

    for s in range(n_seq):
        for c in range(n_chunk):
            r0 = s * seq_rows + c * CHUNK
            k0 = c * CHUNK
            if carry:
                kpos0 = q_pos0 + (t * n_chunk + c) * CHUNK - WINDOW
                valid = (c_iota + kpos0) >= 0
            for j in range(ATTN_KV_HEADS):
                qs = jnp.concatenate(
                    [qlo_ref[r0:r0 + CHUNK, (2 * j) * LANES:(2 * j + 1) * LANES],
                     qhi_ref[r0:r0 + CHUNK, (2 * j) * LANES:(2 * j + 1) * LANES],
                     qlo_ref[r0:r0 + CHUNK, (2 * j + 1) * LANES:(2 * j + 2) * LANES],
                     qhi_ref[r0:r0 + CHUNK, (2 * j + 1) * LANES:(2 * j + 2) * LANES]], axis=0)
                kb = kd_ref[s, k0:k0 + BAND, j * LANES:(j + 1) * LANES]
                vb = vd_ref[s, k0:k0 + BAND, j * LANES:(j + 1) * LANES]
                sc = _dot_nt(qs, kb) - biases[j]
                if carry:
                    sc = jnp.where(valid, sc, NEG_INF)
                sink = sink_cols[j]
                m = jnp.maximum(jnp.max(sc, axis=-1, keepdims=True), sink)
                p = jnp.exp(sc - m)
                denom = jnp.sum(p, axis=-1, keepdims=True) + jnp.exp(sink - m)
                o = _dot(p.astype(_BF), vb) / denom
                for pb in range(2):
                    blk = 2 * j + pb
                    ob = jnp.where(lo_half, o[(2 * pb) * CHUNK:(2 * pb + 1) * CHUNK],
                                   o[(2 * pb + 1) * CHUNK:(2 * pb + 2) * CHUNK])
                    ycat_ref[r0:r0 + CHUNK, blk * LANES:(blk + 1) * LANES] = (
                        ob * gate_a[r0:r0 + CHUNK, blk * LANES:(blk + 1) * LANES]).astype(_BF)

    if carry:
        kd_ref[:, 0:WINDOW, :] = kd_ref[:, seq_rows:seq_rows + WINDOW, :]
        vd_ref[:, 0:WINDOW, :] = vd_ref[:, seq_rows:seq_rows + WINDOW, :]

    z = _dot(proj(C_GLR, LANES).astype(_BF), w_up_ref[...]) + vec_ref[V_GLAB:V_GLAB + 1, 0:GLA_KDIM]
    log_a = (jnp.minimum(z, 0.0) - jnp.log(1.0 + jnp.exp(-jnp.abs(z)))) * (1.0 / GLA_TAU)
    tr = lax.broadcasted_iota(jnp.int32, (tile, tile), 0)
    tc = lax.broadcasted_iota(jnp.int32, (tile, tile), 1)
    tri = jnp.where(((tr >> CHUNK_SHIFT) == (tc >> CHUNK_SHIFT)) & (tc <= tr), 1.0, 0.0).astype(_BF)
    la_hi = log_a.astype(_BF)
    la_lo = (log_a - la_hi.astype(_F32)).astype(_BF)
    bcum = _dot(tri, la_hi) + _dot(tri, la_lo)
    gq = proj(C_GQ, GLA_KDIM) * GLA_DK ** -0.5
    gk = proj(C_GK, GLA_KDIM)
    gv = proj(C_GV, GLA_VDIM).astype(_BF)
    gate_g = _silu(proj(C_GG, GLA_VDIM))
    gla_g = jnp.concatenate([vec_ref[V_GLAG:V_GLAG + 1, 0:GLA_DV]] * GLA_HEADS, axis=1)

    head_of_lane = lane128 >> GLA_DK_SHIFT
    a_r = lax.broadcasted_iota(jnp.int32, (GLA_HEADS * CHUNK, CHUNK), 0)
    a_c = lax.broadcasted_iota(jnp.int32, (GLA_HEADS * CHUNK, CHUNK), 1)
    causal = a_c <= (a_r & (CHUNK - 1))
    vblk = lax.broadcasted_iota(jnp.int32, (1, GLA_VDIM), 1) >> GLA_DV_SHIFT
    bd_mask = ((lax.broadcasted_iota(jnp.int32, (GLA_KDIM, GLA_VDIM), 0) >> GLA_DK_SHIFT)
               == (lax.broadcasted_iota(jnp.int32, (GLA_KDIM, GLA_VDIM), 1) >> GLA_DV_SHIFT))

    for s in range(n_seq):
        state = s_ref[s]
        for c in range(n_chunk):
            r0 = s * seq_rows + c * CHUNK
            bc = bcum[r0:r0 + CHUNK]
            b_last = bc[CHUNK - 1:CHUNK]
            q_t = gq[r0:r0 + CHUNK] * jnp.exp(bc)
            k_t = (gk[r0:r0 + CHUNK] * jnp.exp(-bc)).astype(_BF)
            k_end = (gk[r0:r0 + CHUNK] * jnp.exp(b_last - bc)).astype(_BF)
            vc = gv[r0:r0 + CHUNK]
            qstack = jnp.concatenate(
                [jnp.where(head_of_lane == h, q_t, 0.0) for h in range(GLA_HEADS)], axis=0).astype(_BF)
            a = jnp.where(causal, _dot_nt(qstack, k_t), 0.0).astype(_BF)
            r = _dot(a, vc)
            o = _dot(q_t.astype(_BF), state.astype(_BF))
            for h in range(GLA_HEADS):
                o = o + jnp.where(vblk == h, r[h * CHUNK:(h + 1) * CHUNK], 0.0)
            decay_col = jnp.broadcast_to(jnp.exp(b_last), (GLA_KDIM, GLA_KDIM)).T
            decay = jnp.concatenate([decay_col, decay_col], axis=1)
            state = decay * state + jnp.where(bd_mask, _dot_tn(k_end, vc), 0.0)
            og = _group_rms_scale(o, GLA_DV) * gla_g * gate_g[r0:r0 + CHUNK]
            ycat_ref[r0:r0 + CHUNK, ATTN_DIM:ATTN_DIM + GLA_VDIM] = og.astype(_BF)
        s_ref[s] = state

    def write_state():
        for s in range(n_seq):
            for h in range(GLA_HEADS):
                ns_ref[s, h] = s_ref[s, h * GLA_DK:(h + 1) * GLA_DK, h * GLA_DV:(h + 1) * GLA_DV]
    if carry:
        pl.when(t == n_t - 1)(write_state)
    else:
        write_state()

    u = proj(C_CV, CONV_DIM) * jax.nn.sigmoid(proj(C_CGL, CONV_DIM))
    for s in range(n_seq):
        u_ref[s, CONV_PAD:CONV_PAD + seq_rows, :] = u[s * seq_rows:(s + 1) * seq_rows]
    conv_rows = []
    for s in range(n_seq):
        acc = jnp.broadcast_to(vec_ref[V_CONVB:V_CONVB + 1, 0:CONV_DIM], (seq_rows, CONV_DIM))
        for j in range(CONV_WIDTH):
            o0 = CONV_PAD - CONV_HIST + j
            acc = acc + convw_ref[j:j + 1, :] * u_ref[s, o0:o0 + seq_rows, :]
        conv_rows.append(acc)
    cc = conv_rows[0] if n_seq == 1 else jnp.concatenate(conv_rows, axis=0)

    def write_tail():
        for s in range(n_seq):
            nc_ref[s] = u_ref[s, seq_rows + CONV_PAD - CONV_HIST:seq_rows + CONV_PAD, :]
    if carry:
        pl.when(t == n_t - 1)(write_tail)
        u_ref[:, 0:CONV_PAD, :] = u_ref[:, seq_rows:seq_rows + CONV_PAD, :]
    else:
        write_tail()

    mu = jnp.mean(cc, axis=-1, keepdims=True)
    cen = cc - mu
    var = jnp.mean(cen * cen, axis=-1, keepdims=True)
    ln = cen * lax.rsqrt(var + NORM_EPS) * vec_ref[V_LNG:V_LNG + 1, 0:CONV_DIM] + vec_ref[V_LNB:V_LNB + 1, 0:CONV_DIM]
    cpw = _dot(_silu(ln).astype(_BF), w_pw_ref[...]) + vec_ref[V_BPW:V_BPW + 1, 0:CONV_DIM]
    ycat_ref[:, ATTN_DIM + GLA_VDIM:D_MODEL] = (cpw * _silu(proj(C_CG, CONV_DIM))).astype(_BF)

    y_ref[...] = x + _dot(ycat_ref[...], w_out_ref[...])


def _run_layer(x2d, n_seqs, seq_len, hist, sinks, w_in_p, w_up_p, w_pw, w_out, vecs, convw, *, carry, q_pos0):
    if carry:
        tile, n_seq, n_chunk = PROMPT_TILE, 1, PROMPT_TILE // CHUNK
        grid = (n_seqs, seq_len // tile)
    else:
        assert seq_len == CHUNK
        tile, n_seq, n_chunk = SAMPLE_SEQS * CHUNK, SAMPLE_SEQS, 1
        grid = (n_seqs // n_seq, 1)
    n_t = grid[1]
    seq_rows = n_chunk * CHUNK

    def const(shape):
        return pl.BlockSpec(shape, lambda g, t: (0,) * len(shape))

    def per_group(shape):
        return pl.BlockSpec((n_seq,) + shape, lambda g, t: (g,) + (0,) * len(shape))

    in_specs = [pl.BlockSpec(memory_space=pltpu.SMEM),
                pl.BlockSpec((tile, D_MODEL), lambda g, t: (g * n_t + t, 0))]
    args = [sinks, x2d]
    if not carry:
        in_specs += [per_group((WINDOW, KV_DIM)), per_group((WINDOW, KV_DIM)),
                     per_group((GLA_HEADS, GLA_DK, GLA_DV)), per_group((CONV_HIST, CONV_DIM))]
        args += list(hist)
    in_specs += [const(w_in_p.shape), const(w_up_p.shape), const(w_pw.shape), const(w_out.shape),
                 const(vecs.shape), const(convw.shape)]
    args += [w_in_p, w_up_p, w_pw, w_out, vecs, convw]

    out_shape = (jax.ShapeDtypeStruct(x2d.shape, _F32),
                 jax.ShapeDtypeStruct((n_seqs, WINDOW, KV_DIM), _F32),
                 jax.ShapeDtypeStruct((n_seqs, WINDOW, KV_DIM), _F32),
                 jax.ShapeDtypeStruct((n_seqs, GLA_HEADS, GLA_DK, GLA_DV), _F32),
                 jax.ShapeDtypeStruct((n_seqs, CONV_HIST, CONV_DIM), _F32))
    out_specs = (pl.BlockSpec((tile, D_MODEL), lambda g, t: (g * n_t + t, 0)),
                 per_group((WINDOW, KV_DIM)), per_group((WINDOW, KV_DIM)),
                 per_group((GLA_HEADS, GLA_DK, GLA_DV)), per_group((CONV_HIST, CONV_DIM)))
    scratch = [pltpu.VMEM((n_seq, WINDOW + seq_rows, 2 * LANES), _BF),
               pltpu.VMEM((n_seq, WINDOW + seq_rows, 2 * LANES), _BF),
               pltpu.VMEM((n_seq, GLA_KDIM, GLA_VDIM), _F32),
               pltpu.VMEM((n_seq, CONV_PAD + seq_rows, CONV_DIM), _F32),
               pltpu.VMEM((tile, ATTN_DIM), _BF),
               pltpu.VMEM((tile, ATTN_DIM), _BF),
               pltpu.VMEM((tile, D_MODEL), _BF)]
    body = functools.partial(_layer_kernel, tile=tile, n_seq=n_seq, n_chunk=n_chunk, carry=carry, q_pos0=q_pos0)
    return pl.pallas_call(
        body, grid=grid, in_specs=in_specs, out_specs=out_specs, out_shape=out_shape,
        scratch_shapes=scratch,
        compiler_params=pltpu.CompilerParams(dimension_semantics=("arbitrary", "arbitrary"),
                                             vmem_limit_bytes=VMEM_LIMIT_BYTES),
        name="mixer_prompt" if carry else "mixer_sample",
    )(*args)


def _pack_layer_params(l, norm_gain, w_in, q_norm_gain, k_norm_gain, gla_w_gate_up, gla_b_gate, gla_norm_gain,
                       conv_w, conv_b, conv_ln_gain, conv_ln_bias, conv_w_pw, conv_b_pw, w_out):
    wi = w_in[l]
    glr0 = 2048
    w_in_p = jnp.concatenate(
        [wi[:, :glr0], wi[:, glr0 + GLA_LOWRANK:], wi[:, glr0:glr0 + GLA_LOWRANK],
         jnp.zeros((D_MODEL, LANES - GLA_LOWRANK), wi.dtype)], axis=1).astype(_BF)
    w_up_p = jnp.concatenate([gla_w_gate_up[l], jnp.zeros((LANES - GLA_LOWRANK, GLA_KDIM), _F32)], axis=0).astype(_BF)

    def row(v, reps=1):
        v = jnp.tile(v.astype(_F32), reps)
        return jnp.pad(v, (0, D_MODEL - v.shape[0]))[None, :]

    vec_rows = [row(norm_gain[l]), row(q_norm_gain[l], ATTN_HEADS), row(k_norm_gain[l], ATTN_KV_HEADS),
                row(gla_b_gate[l]), row(gla_norm_gain[l]), row(conv_b[l]), row(conv_ln_gain[l]),
                row(conv_ln_bias[l]), row(conv_b_pw[l])]
    vecs = jnp.concatenate(vec_rows + [jnp.zeros((VEC_ROWS - len(vec_rows), D_MODEL), _F32)], axis=0)
    convw = jnp.pad(conv_w[l].astype(_F32), ((0, CONV_PAD - CONV_WIDTH), (0, 0)))
    return w_in_p, w_up_p, conv_w_pw[l].astype(_BF), w_out[l].astype(_BF), vecs, convw


def kernel(x_prompt, x_sample, cache_k, cache_v, state_gla, state_conv, norm_gain, w_in, q_norm_gain, k_norm_gain, attn_sinks, gla_w_gate_up, gla_b_gate, gla_norm_gain, conv_w, conv_b, conv_ln_gain, conv_ln_bias, conv_w_pw, conv_b_pw, w_out):
    depth = w_in.shape[0]
    bp, lp, _ = x_prompt.shape
    bs, ls, _ = x_sample.shape
    yp = x_prompt.reshape(bp * lp, D_MODEL)
    ys = x_sample.reshape(bs * ls, D_MODEL)
    outs = [[] for _ in range(8)]
    for l in range(depth):
        params = _pack_layer_params(l, norm_gain, w_in, q_norm_gain, k_norm_gain, gla_w_gate_up, gla_b_gate,
                                    gla_norm_gain, conv_w, conv_b, conv_ln_gain, conv_ln_bias, conv_w_pw,
                                    conv_b_pw, w_out)
        sinks = attn_sinks[l].astype(_F32)
        yp, pk, pv, ps, pc = _run_layer(yp, bp, lp, None, sinks, *params, carry=True, q_pos0=0)
        hist = (cache_k[l].reshape(bs, WINDOW, KV_DIM), cache_v[l].reshape(bs, WINDOW, KV_DIM),
                state_gla[l], state_conv[l])
        ys, sk, sv, ss, sc = _run_layer(ys, bs, ls, hist, sinks, *params, carry=False, q_pos0=PAST_LEN)
        for lst, v in zip(outs, (pk, pv, ps, pc, sk, sv, ss, sc)):
            lst.append(v)
    kv_shape = (WINDOW, ATTN_KV_HEADS, HEAD_DIM)
    pk, pv, ps, pc, sk, sv, ss, sc = [jnp.stack(v) for v in outs]
    return (yp.reshape(bp, lp, D_MODEL), ys.reshape(bs, ls, D_MODEL),
            pk.reshape((depth, bp) + kv_shape), pv.reshape((depth, bp) + kv_shape), ps, pc,
            sk.reshape((depth, bs) + kv_shape), sv.reshape((depth, bs) + kv_shape), ss, sc)
```

```python
import functools
import math

import jax
import jax.numpy as jnp
from jax import lax
from jax.experimental import pallas as pl
from jax.experimental.pallas import tpu as pltpu

D_MODEL = 1024
CHUNK = 64
ATTN_HEADS = 8
ATTN_KV_HEADS = 2
HEAD_DIM = 64
ATTN_GROUP = ATTN_HEADS // ATTN_KV_HEADS
ATTN_DIM = ATTN_HEADS * HEAD_DIM
KV_DIM = ATTN_KV_HEADS * HEAD_DIM
WINDOW = 128
BAND = WINDOW + CHUNK
GLA_HEADS = 4
GLA_DK = 32
GLA_DV = 64
GLA_KDIM = GLA_HEADS * GLA_DK
GLA_VDIM = GLA_HEADS * GLA_DV
GLA_LOWRANK = 16
GLA_TAU = 16.0
CONV_DIM = 256
CONV_WIDTH = 31
CONV_HIST = CONV_WIDTH - 1
NORM_EPS = 1e-6
NEG_INF = -1e30
PAST_LEN = 4096
LOG2E = math.log2(math.e)

LANES = 128
SUBLANES = 8
CHUNK_SHIFT = CHUNK.bit_length() - 1
GLA_DK_SHIFT = GLA_DK.bit_length() - 1
GLA_DV_SHIFT = GLA_DV.bit_length() - 1
C_AQ, C_AK, C_AV, C_AG = 0, 512, 640, 768
C_GQ, C_GK, C_GV, C_GG = 1280, 1408, 1536, 1792
C_CV, C_CGL, C_CG, C_GLR = 2048, 2304, 2560, 2816
IN_COLS_PACKED = C_GLR + LANES
CONV_PAD = 32
VEC_ROWS = 16
(V_NORM, V_QG, V_KG, V_GLAB, V_GLAG, V_CONVB, V_LNG, V_LNB, V_BPW) = range(9)

PROMPT_TILE = 256
SAMPLE_SEQS = 4
VMEM_LIMIT_BYTES = 48 * 1024 * 1024

_BF = jnp.bfloat16
_F32 = jnp.float32


def _group_ones(group):
    shift = group.bit_length() - 1
    r = lax.broadcasted_iota(jnp.int32, (LANES, LANES), 0) >> shift
    c = lax.broadcasted_iota(jnp.int32, (LANES, LANES), 1) >> shift
    return jnp.where(r == c, 1.0, 0.0).astype(_BF)


def _group_rms_scale(x, group, ones_bd):
    outs = []
    for c0 in range(0, x.shape[1], LANES):
        blk = x[:, c0:c0 + LANES]
        ss = _dot((blk * blk).astype(_BF), ones_bd)
        outs.append(blk * lax.rsqrt(ss * (1.0 / group) + NORM_EPS))
    return outs[0] if len(outs) == 1 else jnp.concatenate(outs, axis=1)


def _silu(x):
    return x * jax.nn.sigmoid(x)


def _dot(a, b):
    return jnp.dot(a, b, preferred_element_type=_F32)


def _dot_nt(a, b):
    return lax.dot_general(a, b, (((1,), (1,)), ((), ())), preferred_element_type=_F32)


def _dot_tn(a, b):
    return lax.dot_general(a, b, (((0,), (0,)), ((), ())), preferred_element_type=_F32)


def _layer_kernel(*refs, tile, n_seq, n_chunk, carry, q_pos0):
    if carry:
        (sinks_ref, x_ref, w_in_ref, w_up_ref, w_pw_ref, w_out_ref, vec_ref, convw_ref,
         y_ref, nk_ref, nv_ref, ns_ref, nc_ref,
         kd_ref, vd_ref, s_ref, u_ref, qlo_ref, qhi_ref, ycat_ref) = refs
    else:
        (sinks_ref, x_ref, hk_ref, hv_ref, hs_ref, hc_ref,
         w_in_ref, w_up_ref, w_pw_ref, w_out_ref, vec_ref, convw_ref,
         y_ref, nk_ref, nv_ref, ns_ref, nc_ref,
         kd_ref, vd_ref, s_ref, u_ref, qlo_ref, qhi_ref, ycat_ref) = refs
    t = pl.program_id(1)
    n_t = pl.num_programs(1)
    seq_rows = n_chunk * CHUNK

    lane128 = lax.broadcasted_iota(jnp.int32, (1, LANES), 1)
    lo_half = lane128 < HEAD_DIM

    def dup_halves(a):
        sw = pltpu.roll(a, HEAD_DIM, 1)
        return jnp.concatenate([jnp.where(lo_half, a, sw), jnp.where(lo_half, sw, a)], axis=1)

    if carry:
        @pl.when(t == 0)
        def _():
            kd_ref[:, 0:WINDOW, :] = jnp.zeros((n_seq, WINDOW, 2 * LANES), _BF)
            vd_ref[:, 0:WINDOW, :] = jnp.zeros((n_seq, WINDOW, 2 * LANES), _BF)
            s_ref[...] = jnp.zeros(s_ref.shape, _F32)
            u_ref[:, 0:CONV_PAD, :] = jnp.zeros((n_seq, CONV_PAD, CONV_DIM), _F32)
    else:
        for s in range(n_seq):
            kd_ref[s, 0:WINDOW, :] = dup_halves(hk_ref[s]).astype(_BF)
            vd_ref[s, 0:WINDOW, :] = dup_halves(hv_ref[s]).astype(_BF)
            s_ref[s] = jnp.zeros((GLA_KDIM, GLA_VDIM), _F32)
            for h in range(GLA_HEADS):
                s_ref[s, h * GLA_DK:(h + 1) * GLA_DK, h * GLA_DV:(h + 1) * GLA_DV] = hs_ref[s, h]
            u_ref[s, 0:SUBLANES, :] = jnp.zeros((SUBLANES, CONV_DIM), _F32)
            u_ref[s, CONV_PAD - CONV_HIST:CONV_PAD, :] = hc_ref[s]

    x = x_ref[...]
    ms = jnp.mean(x * x, axis=-1, keepdims=True)
    hb = (x * lax.rsqrt(ms + NORM_EPS) * vec_ref[V_NORM:V_NORM + 1, :]).astype(_BF)

    def proj(c0, width):
        return _dot(hb, w_in_ref[:, c0:c0 + width])

    ones64 = _group_ones(HEAD_DIM)
    qn = _group_rms_scale(proj(C_AQ, ATTN_DIM), HEAD_DIM, ones64) * (
        vec_ref[V_QG:V_QG + 1, 0:ATTN_DIM] * (LOG2E * HEAD_DIM ** -0.5))
    lo4 = jnp.concatenate([lo_half] * (ATTN_DIM // LANES), axis=1)
    qlo_ref[...] = jnp.where(lo4, qn, 0.0).astype(_BF)
    qhi_ref[...] = jnp.where(lo4, 0.0, qn).astype(_BF)
    kn = _group_rms_scale(proj(C_AK, KV_DIM), HEAD_DIM, ones64) * vec_ref[V_KG:V_KG + 1, 0:KV_DIM]
    vv = proj(C_AV, KV_DIM)
    kdup = dup_halves(kn).astype(_BF)
    vdup = dup_halves(vv).astype(_BF)
    for s in range(n_seq):
        kd_ref[s, WINDOW:WINDOW + seq_rows, :] = kdup[s * seq_rows:(s + 1) * seq_rows]
        vd_ref[s, WINDOW:WINDOW + seq_rows, :] = vdup[s * seq_rows:(s + 1) * seq_rows]
    if carry:
        @pl.when(t == n_t - 1)
        def _():
            nk_ref[0] = kn[tile - WINDOW:tile]
            nv_ref[0] = vv[tile - WINDOW:tile]
    else:
        for s in range(n_seq):
            nk_ref[s, 0:WINDOW - seq_rows, :] = hk_ref[s, seq_rows:WINDOW, :]
            nv_ref[s, 0:WINDOW - seq_rows, :] = hv_ref[s, seq_rows:WINDOW, :]
            nk_ref[s, WINDOW - seq_rows:WINDOW, :] = kn[s * seq_rows:(s + 1) * seq_rows]
            nv_ref[s, WINDOW - seq_rows:WINDOW, :] = vv[s * seq_rows:(s + 1) * seq_rows]

    gate_a = _silu(proj(C_AG, ATTN_DIM))

    rows = ATTN_GROUP * CHUNK
    r_iota = lax.broadcasted_iota(jnp.int32, (rows, BAND), 0)
    c_iota = lax.broadcasted_iota(jnp.int32, (rows, BAND), 1)
    dist = jnp.abs((r_iota & (CHUNK - 1)) + WINDOW - c_iota).astype(_F32)
    r_head = r_iota >> CHUNK_SHIFT
    r_head1 = lax.broadcasted_iota(jnp.int32, (rows, 1), 0) >> CHUNK_SHIFT
    biases, sink_cols = [], []
    for j in range(ATTN_KV_HEADS):
        slope = jnp.zeros((rows, BAND), _F32)
        sink = jnp.zeros((rows, 1), _F32)
        for r in range(ATTN_GROUP):
            hd = j * ATTN_GROUP + r
            slope = jnp.where(r_head == r, 2.0 ** (-8.0 * (hd + 1) / ATTN_HEADS), slope)
            sink = jnp.where(r_head1 == r, sinks_ref[hd], sink)
        biases.append(slope * LOG2E * dist)
        sink_cols.append(sink * LOG2E)

    for s in range(n_seq):
        for c in range(n_chunk):
            r0 = s * seq_rows + c * CHUNK
            k0 = c * CHUNK
            if carry:
                kpos0 = q_pos0 + (t * n_chunk + c) * CHUNK - WINDOW
                valid = (c_iota + kpos0) >= 0
            for j in range(ATTN_KV_HEADS):
                qs = jnp.concatenate(
                    [qlo_ref[r0:r0 + CHUNK, (2 * j) * LANES:(2 * j + 1) * LANES],
                     qhi_ref[r0:r0 + CHUNK, (2 * j) * LANES:(2 * j + 1) * LANES],
                     qlo_ref[r0:r0 + CHUNK, (2 * j + 1) * LANES:(2 * j + 2) * LANES],
                     qhi_ref[r0:r0 + CHUNK, (2 * j + 1) * LANES:(2 * j + 2) * LANES]], axis=0)
                kb = kd_ref[s, k0:k0 + BAND, j * LANES:(j + 1) * LANES]
                vb = vd_ref[s, k0:k0 + BAND, j * LANES:(j + 1) * LANES]
                sc = _dot_nt(qs, kb) - biases[j]
                if carry:
                    sc = jnp.where(valid, sc, NEG_INF)
                sink = sink_cols[j]
                m = jnp.maximum(jnp.max(sc, axis=-1, keepdims=True), sink)
                p = jnp.exp2(sc - m)
                denom = jnp.sum(p, axis=-1, keepdims=True) + jnp.exp2(sink - m)
                o = _dot(p.astype(_BF), vb) / denom
                for pb in range(2):
                    blk = 2 * j + pb
                    ob = jnp.where(lo_half, o[(2 * pb) * CHUNK:(2 * pb + 1) * CHUNK],
                                   o[(2 * pb + 1) * CHUNK:(2 * pb + 2) * CHUNK])
                    ycat_ref[r0:r0 + CHUNK, blk * LANES:(blk + 1) * LANES] = (
                        ob * gate_a[r0:r0 + CHUNK, blk * LANES:(blk + 1) * LANES]).astype(_BF)

    if carry:
        kd_ref[:, 0:WINDOW, :] = kd_ref[:, seq_rows:seq_rows + WINDOW, :]
        vd_ref[:, 0:WINDOW, :] = vd_ref[:, seq_rows:seq_rows + WINDOW, :]

    z = _dot(proj(C_GLR, LANES).astype(_BF), w_up_ref[...]) + vec_ref[V_GLAB:V_GLAB + 1, 0:GLA_KDIM]
    log_a = (jnp.minimum(z, 0.0) - jnp.log(1.0 + jnp.exp(-jnp.abs(z)))) * (1.0 / GLA_TAU)
    tr = lax.broadcasted_iota(jnp.int32, (tile, tile), 0)
    tc = lax.broadcasted_iota(jnp.int32, (tile, tile), 1)
    tri = jnp.where(((tr >> CHUNK_SHIFT) == (tc >> CHUNK_SHIFT)) & (tc <= tr), 1.0, 0.0).astype(_BF)
    la_hi = log_a.astype(_BF)
    la_lo = (log_a - la_hi.astype(_F32)).astype(_BF)
    bcum = _dot(tri, la_hi) + _dot(tri, la_lo)
    gq = proj(C_GQ, GLA_KDIM) * GLA_DK ** -0.5
    gk = proj(C_GK, GLA_KDIM)
    gv = proj(C_GV, GLA_VDIM).astype(_BF)
    gate_g = _silu(proj(C_GG, GLA_VDIM))
    gla_g = jnp.concatenate([vec_ref[V_GLAG:V_GLAG + 1, 0:GLA_DV]] * GLA_HEADS, axis=1)

    head_of_lane = lane128 >> GLA_DK_SHIFT
    a_r = lax.broadcasted_iota(jnp.int32, (GLA_HEADS * CHUNK, CHUNK), 0)
    a_c = lax.broadcasted_iota(jnp.int32, (GLA_HEADS * CHUNK, CHUNK), 1)
    causal = a_c <= (a_r & (CHUNK - 1))
    vblk = lax.broadcasted_iota(jnp.int32, (1, GLA_VDIM), 1) >> GLA_DV_SHIFT
    bd_mask = ((lax.broadcasted_iota(jnp.int32, (GLA_KDIM, GLA_VDIM), 0) >> GLA_DK_SHIFT)
               == (lax.broadcasted_iota(jnp.int32, (GLA_KDIM, GLA_VDIM), 1) >> GLA_DV_SHIFT))

    for s in range(n_seq):
        state = s_ref[s]
        for c in range(n_chunk):
            r0 = s * seq_rows + c * CHUNK
            bc = bcum[r0:r0 + CHUNK]
            b_last = bc[CHUNK - 1:CHUNK]
            q_t = gq[r0:r0 + CHUNK] * jnp.exp(bc)
            k_t = (gk[r0:r0 + CHUNK] * jnp.exp(-bc)).astype(_BF)
            k_end = (gk[r0:r0 + CHUNK] * jnp.exp(b_last - bc)).astype(_BF)
            vc = gv[r0:r0 + CHUNK]
            qstack = jnp.concatenate(
                [jnp.where(head_of_lane == h, q_t, 0.0) for h in range(GLA_HEADS)], axis=0).astype(_BF)
            a = jnp.where(causal, _dot_nt(qstack, k_t), 0.0).astype(_BF)
            r = _dot(a, vc)
            o = _dot(q_t.astype(_BF), state.astype(_BF))
            for h in range(GLA_HEADS):
                o = o + jnp.where(vblk == h, r[h * CHUNK:(h + 1) * CHUNK], 0.0)
            decay_col = jnp.broadcast_to(jnp.exp(b_last), (GLA_KDIM, GLA_KDIM)).T
            decay = jnp.concatenate([decay_col, decay_col], axis=1)
            state = decay * state + jnp.where(bd_mask, _dot_tn(k_end, vc), 0.0)
            og = _group_rms_scale(o, GLA_DV, ones64) * gla_g * gate_g[r0:r0 + CHUNK]
            ycat_ref[r0:r0 + CHUNK, ATTN_DIM:ATTN_DIM + GLA_VDIM] = og.astype(_BF)
        s_ref[s] = state

    def write_state():
        for s in range(n_seq):
            for h in range(GLA_HEADS):
                ns_ref[s, h] = s_ref[s, h * GLA_DK:(h + 1) * GLA_DK, h * GLA_DV:(h + 1) * GLA_DV]
    if carry:
        pl.when(t == n_t - 1)(write_state)
    else:
        write_state()

    u = proj(C_CV, CONV_DIM) * jax.nn.sigmoid(proj(C_CGL, CONV_DIM))
    for s in range(n_seq):
        u_ref[s, CONV_PAD:CONV_PAD + seq_rows, :] = u[s * seq_rows:(s + 1) * seq_rows]
    conv_rows = []
    for s in range(n_seq):
        acc = jnp.broadcast_to(vec_ref[V_CONVB:V_CONVB + 1, 0:CONV_DIM], (seq_rows, CONV_DIM))
        for rho in range(SUBLANES):
            frame_rows = seq_rows + (SUBLANES if rho else 0)
            frame = None
            for j in range(CONV_WIDTH):
                off = CONV_PAD - CONV_HIST + j
                if off % SUBLANES != rho:
                    continue
                term = convw_ref[j:j + 1, :] * u_ref[s, off - rho:off - rho + frame_rows, :]
                frame = term if frame is None else frame + term
            acc = acc + frame[rho:rho + seq_rows]
        conv_rows.append(acc)
    cc = conv_rows[0] if n_seq == 1 else jnp.concatenate(conv_rows, axis=0)

    def write_tail():
        for s in range(n_seq):
            nc_ref[s] = u_ref[s, seq_rows + CONV_PAD - CONV_HIST:seq_rows + CONV_PAD, :]
    if carry:
        pl.when(t == n_t - 1)(write_tail)
        u_ref[:, 0:CONV_PAD, :] = u_ref[:, seq_rows:seq_rows + CONV_PAD, :]
    else:
        write_tail()

    mu = jnp.mean(cc, axis=-1, keepdims=True)
    cen = cc - mu
    var = jnp.mean(cen * cen, axis=-1, keepdims=True)
    ln = cen * lax.rsqrt(var + NORM_EPS) * vec_ref[V_LNG:V_LNG + 1, 0:CONV_DIM] + vec_ref[V_LNB:V_LNB + 1, 0:CONV_DIM]
    cpw = _dot(_silu(ln).astype(_BF), w_pw_ref[...]) + vec_ref[V_BPW:V_BPW + 1, 0:CONV_DIM]
    ycat_ref[:, ATTN_DIM + GLA_VDIM:D_MODEL] = (cpw * _silu(proj(C_CG, CONV_DIM))).astype(_BF)

    y_ref[...] = x + _dot(ycat_ref[...], w_out_ref[...])


def _run_layer(x2d, n_seqs, seq_len, hist, sinks, w_in_p, w_up_p, w_pw, w_out, vecs, convw, *, carry, q_pos0):
    if carry:
        tile, n_seq, n_chunk = PROMPT_TILE, 1, PROMPT_TILE // CHUNK
        grid = (n_seqs, seq_len // tile)
    else:
        assert seq_len == CHUNK
        tile, n_seq, n_chunk = SAMPLE_SEQS * CHUNK, SAMPLE_SEQS, 1
        grid = (n_seqs // n_seq, 1)
    n_t = grid[1]
    seq_rows = n_chunk * CHUNK

    def const(shape):
        return pl.BlockSpec(shape, lambda g, t: (0,) * len(shape))

    def per_group(shape):
        return pl.BlockSpec((n_seq,) + shape, lambda g, t: (g,) + (0,) * len(shape))

    in_specs = [pl.BlockSpec(memory_space=pltpu.SMEM),
                pl.BlockSpec((tile, D_MODEL), lambda g, t: (g * n_t + t, 0))]
    args = [sinks, x2d]
    if not carry:
        in_specs += [per_group((WINDOW, KV_DIM)), per_group((WINDOW, KV_DIM)),
                     per_group((GLA_HEADS, GLA_DK, GLA_DV)), per_group((CONV_HIST, CONV_DIM))]
        args += list(hist)
    in_specs += [const(w_in_p.shape), const(w_up_p.shape), const(w_pw.shape), const(w_out.shape),
                 const(vecs.shape), const(convw.shape)]
    args += [w_in_p, w_up_p, w_pw, w_out, vecs, convw]

    out_shape = (jax.ShapeDtypeStruct(x2d.shape, _F32),
                 jax.ShapeDtypeStruct((n_seqs, WINDOW, KV_DIM), _F32),
                 jax.ShapeDtypeStruct((n_seqs, WINDOW, KV_DIM), _F32),
                 jax.ShapeDtypeStruct((n_seqs, GLA_HEADS, GLA_DK, GLA_DV), _F32),
                 jax.ShapeDtypeStruct((n_seqs, CONV_HIST, CONV_DIM), _F32))
    out_specs = (pl.BlockSpec((tile, D_MODEL), lambda g, t: (g * n_t + t, 0)),
                 per_group((WINDOW, KV_DIM)), per_group((WINDOW, KV_DIM)),
                 per_group((GLA_HEADS, GLA_DK, GLA_DV)), per_group((CONV_HIST, CONV_DIM)))
    scratch = [pltpu.VMEM((n_seq, WINDOW + seq_rows, 2 * LANES), _BF),
               pltpu.VMEM((n_seq, WINDOW + seq_rows, 2 * LANES), _BF),
               pltpu.VMEM((n_seq, GLA_KDIM, GLA_VDIM), _F32),
               pltpu.VMEM((n_seq, CONV_PAD + seq_rows, CONV_DIM), _F32),
               pltpu.VMEM((tile, ATTN_DIM), _BF),
               pltpu.VMEM((tile, ATTN_DIM), _BF),
               pltpu.VMEM((tile, D_MODEL), _BF)]
    body = functools.partial(_layer_kernel, tile=tile, n_seq=n_seq, n_chunk=n_chunk, carry=carry, q_pos0=q_pos0)
    return pl.pallas_call(
        body, grid=grid, in_specs=in_specs, out_specs=out_specs, out_shape=out_shape,
        scratch_shapes=scratch,
        compiler_params=pltpu.CompilerParams(dimension_semantics=("arbitrary", "arbitrary"),
                                             vmem_limit_bytes=VMEM_LIMIT_BYTES),
        name="mixer_prompt" if carry else "mixer_sample",
    )(*args)


def _pack_layer_params(l, norm_gain, w_in, q_norm_gain, k_norm_gain, gla_w_gate_up, gla_b_gate, gla_norm_gain,
                       conv_w, conv_b, conv_ln_gain, conv_ln_bias, conv_w_pw, conv_b_pw, w_out):
    wi = w_in[l]
    glr0 = 2048
    w_in_p = jnp.concatenate(
        [wi[:, :glr0], wi[:, glr0 + GLA_LOWRANK:], wi[:, glr0:glr0 + GLA_LOWRANK],
         jnp.zeros((D_MODEL, LANES - GLA_LOWRANK), wi.dtype)], axis=1).astype(_BF)
    w_up_p = jnp.concatenate([gla_w_gate_up[l], jnp.zeros((LANES - GLA_LOWRANK, GLA_KDIM), _F32)], axis=0).astype(_BF)

    def row(v, reps=1):
        v = jnp.tile(v.astype(_F32), reps)
        return jnp.pad(v, (0, D_MODEL - v.shape[0]))[None, :]

    vec_rows = [row(norm_gain[l]), row(q_norm_gain[l], ATTN_HEADS), row(k_norm_gain[l], ATTN_KV_HEADS),
                row(gla_b_gate[l]), row(gla_norm_gain[l]), row(conv_b[l]), row(conv_ln_gain[l]),
                row(conv_ln_bias[l]), row(conv_b_pw[l])]
    vecs = jnp.concatenate(vec_rows + [jnp.zeros((VEC_ROWS - len(vec_rows), D_MODEL), _F32)], axis=0)
    convw = jnp.pad(conv_w[l].astype(_F32), ((0, CONV_PAD - CONV_WIDTH), (0, 0)))
    return w_in_p, w_up_p, conv_w_pw[l].astype(_BF), w_out[l].astype(_BF), vecs, convw


def kernel(x_prompt, x_sample, cache_k, cache_v, state_gla, state_conv, norm_gain, w_in, q_norm_gain, k_norm_gain, attn_sinks, gla_w_gate_up, gla_b_gate, gla_norm_gain, conv_w, conv_b, conv_ln_gain, conv_ln_bias, conv_w_pw, conv_b_pw, w_out):
    depth = w_in.shape[0]
    bp, lp, _ = x_prompt.shape
    bs, ls, _ = x_sample.shape
    yp = x_prompt.reshape(bp * lp, D_MODEL)
    ys = x_sample.reshape(bs * ls, D_MODEL)
    outs = [[] for _ in range(8)]
    for l in range(depth):
        params = _pack_layer_params(l, norm_gain, w_in, q_norm_gain, k_norm_gain, gla_w_gate_up, gla_b_gate,
                                    gla_norm_gain, conv_w, conv_b, conv_ln_gain, conv_ln_bias, conv_w_pw,
                                    conv_b_pw, w_out)
        sinks = attn_sinks[l].astype(_F32)
        yp, pk, pv, ps, pc = _run_layer(yp, bp, lp, None, sinks, *params, carry=True, q_pos0=0)
        hist = (cache_k[l].reshape(bs, WINDOW, KV_DIM), cache_v[l].reshape(bs, WINDOW, KV_DIM),
                state_gla[l], state_conv[l])
        ys, sk, sv, ss, sc = _run_layer(ys, bs, ls, hist, sinks, *params, carry=False, q_pos0=PAST_LEN)
        for lst, v in zip(outs, (pk, pv, ps, pc, sk, sv, ss, sc)):
            lst.append(v)
    kv_shape = (WINDOW, ATTN_KV_HEADS, HEAD_DIM)
    pk, pv, ps, pc, sk, sv, ss, sc = [jnp.stack(v) for v in outs]
    return (yp.reshape(bp, lp, D_MODEL), ys.reshape(bs, ls, D_MODEL),
            pk.reshape((depth, bp) + kv_shape), pv.reshape((depth, bp) + kv_shape), ps, pc,
            sk.reshape((depth, bs) + kv_shape), sv.reshape((depth, bs) + kv_shape), ss, sc)
```

```python
import functools
import math

import jax
import jax.numpy as jnp
from jax import lax
from jax.experimental import pallas as pl
from jax.experimental.pallas import tpu as pltpu

D_MODEL = 1024
CHUNK = 64
ATTN_HEADS = 8
ATTN_KV_HEADS = 2
HEAD_DIM = 64
ATTN_GROUP = ATTN_HEADS // ATTN_KV_HEADS
ATTN_DIM = ATTN_HEADS * HEAD_DIM
KV_DIM = ATTN_KV_HEADS * HEAD_DIM
WINDOW = 128
BAND = WINDOW + CHUNK
GLA_HEADS = 4
GLA_DK = 32
GLA_DV = 64
GLA_KDIM = GLA_HEADS * GLA_DK
GLA_VDIM = GLA_HEADS * GLA_DV
GLA_LOWRANK = 16
GLA_TAU = 16.0
CONV_DIM = 256
CONV_WIDTH = 31
CONV_HIST = CONV_WIDTH - 1
NORM_EPS = 1e-6
NEG_INF = -1e30
PAST_LEN = 4096
LOG2E = math.log2(math.e)

LANES = 128
SUBLANES = 8
CHUNK_SHIFT = CHUNK.bit_length() - 1
GLA_DK_SHIFT = GLA_DK.bit_length() - 1
GLA_DV_SHIFT = GLA_DV.bit_length() - 1
C_AQ, C_AK, C_AV, C_AG = 0, 512, 640, 768
C_GQ, C_GK, C_GV, C_GG = 1280, 1408, 1536, 1792
C_CV, C_CGL, C_CG, C_GLR = 2048, 2304, 2560, 2816
IN_COLS_PACKED = C_GLR + LANES
PROJ_GROUP = 256
CONV_PAD = 32
VEC_ROWS = 16
(V_NORM, V_QG, V_KG, V_GLAB, V_GLAG, V_CONVB, V_LNG, V_LNB, V_BPW) = range(9)

PROMPT_TILE = 256
SAMPLE_SEQS = 4
VMEM_LIMIT_BYTES = 48 * 1024 * 1024

_BF = jnp.bfloat16
_F32 = jnp.float32


def _group_ones(group):
    shift = group.bit_length() - 1
    r = lax.broadcasted_iota(jnp.int32, (LANES, LANES), 0) >> shift
    c = lax.broadcasted_iota(jnp.int32, (LANES, LANES), 1) >> shift
    return jnp.where(r == c, 1.0, 0.0).astype(_BF)


def _group_rms_scale(x, group, ones_bd):
    outs = []
    for c0 in range(0, x.shape[1], LANES):
        blk = x[:, c0:c0 + LANES]
        ss = _dot((blk * blk).astype(_BF), ones_bd)
        outs.append(blk * lax.rsqrt(ss * (1.0 / group) + NORM_EPS))
    return outs[0] if len(outs) == 1 else jnp.concatenate(outs, axis=1)


def _silu(x):
    return x * jax.nn.sigmoid(x)


def _dot(a, b):
    return jnp.dot(a, b, preferred_element_type=_F32)


def _dot_nt(a, b):
    return lax.dot_general(a, b, (((1,), (1,)), ((), ())), preferred_element_type=_F32)


def _dot_tn(a, b):
    return lax.dot_general(a, b, (((0,), (0,)), ((), ())), preferred_element_type=_F32)


def _layer_kernel(*refs, tile, n_seq, n_chunk, carry, q_pos0):
    if carry:
        (sinks_ref, x_ref, w_in_ref, w_up_ref, w_pw_ref, w_out_ref, vec_ref, convw_ref,
         y_ref, nk_ref, nv_ref, ns_ref, nc_ref,
         kd_ref, vd_ref, s_ref, u_ref, qlo_ref, qhi_ref, ycat_ref, proj_ref) = refs
    else:
        (sinks_ref, x_ref, hk_ref, hv_ref, hs_ref, hc_ref,
         w_in_ref, w_up_ref, w_pw_ref, w_out_ref, vec_ref, convw_ref,
         y_ref, nk_ref, nv_ref, ns_ref, nc_ref,
         kd_ref, vd_ref, s_ref, u_ref, qlo_ref, qhi_ref, ycat_ref, proj_ref) = refs
    t = pl.program_id(1)
    n_t = pl.num_programs(1)
    seq_rows = n_chunk * CHUNK

    lane128 = lax.broadcasted_iota(jnp.int32, (1, LANES), 1)
    lo_half = lane128 < HEAD_DIM

    def dup_halves(a):
        sw = pltpu.roll(a, HEAD_DIM, 1)
        return jnp.concatenate([jnp.where(lo_half, a, sw), jnp.where(lo_half, sw, a)], axis=1)

    if carry:
        @pl.when(t == 0)
        def _():
            kd_ref[:, 0:WINDOW, :] = jnp.zeros((n_seq, WINDOW, 2 * LANES), _BF)
            vd_ref[:, 0:WINDOW, :] = jnp.zeros((n_seq, WINDOW, 2 * LANES), _BF)
            s_ref[...] = jnp.zeros(s_ref.shape, _F32)
            u_ref[:, 0:CONV_PAD, :] = jnp.zeros((n_seq, CONV_PAD, CONV_DIM), _F32)
    else:
        for s in range(n_seq):
            kd_ref[s, 0:WINDOW, :] = dup_halves(hk_ref[s]).astype(_BF)
            vd_ref[s, 0:WINDOW, :] = dup_halves(hv_ref[s]).astype(_BF)
            s_ref[s] = jnp.zeros((GLA_KDIM, GLA_VDIM), _F32)
            for h in range(GLA_HEADS):
                s_ref[s, h * GLA_DK:(h + 1) * GLA_DK, h * GLA_DV:(h + 1) * GLA_DV] = hs_ref[s, h]
            u_ref[s, 0:SUBLANES, :] = jnp.zeros((SUBLANES, CONV_DIM), _F32)
            u_ref[s, CONV_PAD - CONV_HIST:CONV_PAD, :] = hc_ref[s]

    x = x_ref[...]
    ms = jnp.mean(x * x, axis=-1, keepdims=True)
    hb = (x * lax.rsqrt(ms + NORM_EPS) * vec_ref[V_NORM:V_NORM + 1, :]).astype(_BF)

    def compute_group(g):
        c0, c1 = g * PROJ_GROUP, min((g + 1) * PROJ_GROUP, IN_COLS_PACKED)
        proj_ref[:, c0:c1] = _dot(hb, w_in_ref[:, c0:c1])

    def proj(c0, width):
        return proj_ref[:, c0:c0 + width]

    compute_group(C_CV // PROJ_GROUP)
    compute_group(C_CGL // PROJ_GROUP)
    u = proj(C_CV, CONV_DIM) * jax.nn.sigmoid(proj(C_CGL, CONV_DIM))
    for s in range(n_seq):
        u_ref[s, CONV_PAD:CONV_PAD + seq_rows, :] = u[s * seq_rows:(s + 1) * seq_rows]
    conv_acc = [jnp.broadcast_to(vec_ref[V_CONVB:V_CONVB + 1, 0:CONV_DIM], (seq_rows, CONV_DIM))] * n_seq
    for rho in range(SUBLANES):
        compute_group(rho)
        for s in range(n_seq):
            frame_rows = seq_rows + (SUBLANES if rho else 0)
            frame = None
            for j in range(CONV_WIDTH):
                off = CONV_PAD - CONV_HIST + j
                if off % SUBLANES != rho:
                    continue
                term = convw_ref[j:j + 1, :] * u_ref[s, off - rho:off - rho + frame_rows, :]
                frame = term if frame is None else frame + term
            conv_acc[s] = conv_acc[s] + frame[rho:rho + seq_rows]
    compute_group(C_CG // PROJ_GROUP)
    compute_group(C_GLR // PROJ_GROUP)
    cc = conv_acc[0] if n_seq == 1 else jnp.concatenate(conv_acc, axis=0)

    for s in range(n_seq):
        nc_ref[s] = u_ref[s, seq_rows + CONV_PAD - CONV_HIST:seq_rows + CONV_PAD, :]
    if carry:
        u_ref[:, 0:CONV_PAD, :] = u_ref[:, seq_rows:seq_rows + CONV_PAD, :]

    mu = jnp.mean(cc, axis=-1, keepdims=True)
    cen = cc - mu
    var = jnp.mean(cen * cen, axis=-1, keepdims=True)
    ln = cen * lax.rsqrt(var + NORM_EPS) * vec_ref[V_LNG:V_LNG + 1, 0:CONV_DIM] + vec_ref[V_LNB:V_LNB + 1, 0:CONV_DIM]
    cpw = _dot(_silu(ln).astype(_BF), w_pw_ref[...]) + vec_ref[V_BPW:V_BPW + 1, 0:CONV_DIM]
    ycat_ref[:, ATTN_DIM + GLA_VDIM:D_MODEL] = cpw * _silu(proj(C_CG, CONV_DIM))

    ones64 = _group_ones(HEAD_DIM)
    qn = _group_rms_scale(proj(C_AQ, ATTN_DIM), HEAD_DIM, ones64) * (
        vec_ref[V_QG:V_QG + 1, 0:ATTN_DIM] * (LOG2E * HEAD_DIM ** -0.5))
    lo4 = jnp.concatenate([lo_half] * (ATTN_DIM // LANES), axis=1)
    qlo_ref[...] = jnp.where(lo4, qn, 0.0).astype(_BF)
    qhi_ref[...] = jnp.where(lo4, 0.0, qn).astype(_BF)
    kn = _group_rms_scale(proj(C_AK, KV_DIM), HEAD_DIM, ones64) * vec_ref[V_KG:V_KG + 1, 0:KV_DIM]
    vv = proj(C_AV, KV_DIM)
    kdup = dup_halves(kn).astype(_BF)
    vdup = dup_halves(vv).astype(_BF)
    for s in range(n_seq):
        kd_ref[s, WINDOW:WINDOW + seq_rows, :] = kdup[s * seq_rows:(s + 1) * seq_rows]
        vd_ref[s, WINDOW:WINDOW + seq_rows, :] = vdup[s * seq_rows:(s + 1) * seq_rows]
    if carry:
        nk_ref[0] = kn[tile - WINDOW:tile]
        nv_ref[0] = vv[tile - WINDOW:tile]
    else:
        for s in range(n_seq):
            nk_ref[s, 0:WINDOW - seq_rows, :] = hk_ref[s, seq_rows:WINDOW, :]
            nv_ref[s, 0:WINDOW - seq_rows, :] = hv_ref[s, seq_rows:WINDOW, :]
            nk_ref[s, WINDOW - seq_rows:WINDOW, :] = kn[s * seq_rows:(s + 1) * seq_rows]
            nv_ref[s, WINDOW - seq_rows:WINDOW, :] = vv[s * seq_rows:(s + 1) * seq_rows]

    gate_a = _silu(proj(C_AG, ATTN_DIM))

    rows = ATTN_GROUP * CHUNK
    r_iota = lax.broadcasted_iota(jnp.int32, (rows, BAND), 0)
    c_iota = lax.broadcasted_iota(jnp.int32, (rows, BAND), 1)
    dist = jnp.abs((r_iota & (CHUNK - 1)) + WINDOW - c_iota).astype(_F32)
    r_head = r_iota >> CHUNK_SHIFT
    r_head1 = lax.broadcasted_iota(jnp.int32, (rows, 1), 0) >> CHUNK_SHIFT
    biases, sink_cols = [], []
    for j in range(ATTN_KV_HEADS):
        slope = jnp.zeros((rows, BAND), _F32)
        sink = jnp.zeros((rows, 1), _F32)
        for r in range(ATTN_GROUP):
            hd = j * ATTN_GROUP + r
            slope = jnp.where(r_head == r, 2.0 ** (-8.0 * (hd + 1) / ATTN_HEADS), slope)
            sink = jnp.where(r_head1 == r, sinks_ref[hd], sink)
        biases.append(slope * LOG2E * dist)
        sink_cols.append(sink * LOG2E)

    for s in range(n_seq):
        for c in range(n_chunk):
            r0 = s * seq_rows + c * CHUNK
            k0 = c * CHUNK
            if carry:
                kpos0 = q_pos0 + (t * n_chunk + c) * CHUNK - WINDOW
                valid = (c_iota + kpos0) >= 0
            for j in range(ATTN_KV_HEADS):
                qs = jnp.concatenate(
                    [qlo_ref[r0:r0 + CHUNK, (2 * j) * LANES:(2 * j + 1) * LANES],
                     qhi_ref[r0:r0 + CHUNK, (2 * j) * LANES:(2 * j + 1) * LANES],
                     qlo_ref[r0:r0 + CHUNK, (2 * j + 1) * LANES:(2 * j + 2) * LANES],
                     qhi_ref[r0:r0 + CHUNK, (2 * j + 1) * LANES:(2 * j + 2) * LANES]], axis=0)
                kb = kd_ref[s, k0:k0 + BAND, j * LANES:(j + 1) * LANES]
                vb = vd_ref[s, k0:k0 + BAND, j * LANES:(j + 1) * LANES]
                sc = _dot_nt(qs, kb) - biases[j]
                if carry:
                    sc = jnp.where(valid, sc, NEG_INF)
                sink = sink_cols[j]
                m = jnp.maximum(jnp.max(sc, axis=-1, keepdims=True), sink)
                p = jnp.exp2(sc - m)
                denom = jnp.sum(p, axis=-1, keepdims=True) + jnp.exp2(sink - m)
                o = _dot(p.astype(_BF), vb) / denom
                for pb in range(2):
                    blk = 2 * j + pb
                    ob = jnp.where(lo_half, o[(2 * pb) * CHUNK:(2 * pb + 1) * CHUNK],
                                   o[(2 * pb + 1) * CHUNK:(2 * pb + 2) * CHUNK])
                    ycat_ref[r0:r0 + CHUNK, blk * LANES:(blk + 1) * LANES] = (
                        ob * gate_a[r0:r0 + CHUNK, blk * LANES:(blk + 1) * LANES])

    if carry:
        kd_ref[:, 0:WINDOW, :] = kd_ref[:, seq_rows:seq_rows + WINDOW, :]
        vd_ref[:, 0:WINDOW, :] = vd_ref[:, seq_rows:seq_rows + WINDOW, :]

    z = _dot(proj(C_GLR, LANES).astype(_BF), w_up_ref[...]) + vec_ref[V_GLAB:V_GLAB + 1, 0:GLA_KDIM]
    log_a = (jnp.minimum(z, 0.0) - jnp.log(1.0 + jnp.exp(-jnp.abs(z)))) * (1.0 / GLA_TAU)
    tr = lax.broadcasted_iota(jnp.int32, (tile, tile), 0)
    tc = lax.broadcasted_iota(jnp.int32, (tile, tile), 1)
    tri = jnp.where(((tr >> CHUNK_SHIFT) == (tc >> CHUNK_SHIFT)) & (tc <= tr), 1.0, 0.0).astype(_BF)
    la_hi = log_a.astype(_BF)
    la_lo = (log_a - la_hi.astype(_F32)).astype(_BF)
    bcum = _dot(tri, la_hi) + _dot(tri, la_lo)
    gq = proj(C_GQ, GLA_KDIM) * GLA_DK ** -0.5
    gk = proj(C_GK, GLA_KDIM)
    gv = proj(C_GV, GLA_VDIM)
    gate_g = _silu(proj(C_GG, GLA_VDIM))
    gla_g = jnp.concatenate([vec_ref[V_GLAG:V_GLAG + 1, 0:GLA_DV]] * GLA_HEADS, axis=1)

    head_of_lane = lane128 >> GLA_DK_SHIFT
    a_r = lax.broadcasted_iota(jnp.int32, (CHUNK, GLA_HEADS * CHUNK), 0)
    a_c = lax.broadcasted_iota(jnp.int32, (CHUNK, GLA_HEADS * CHUNK), 1)
    causal = (a_c & (CHUNK - 1)) <= a_r
    vblk = lax.broadcasted_iota(jnp.int32, (1, GLA_VDIM), 1) >> GLA_DV_SHIFT
    bd_mask = ((lax.broadcasted_iota(jnp.int32, (GLA_KDIM, GLA_VDIM), 0) >> GLA_DK_SHIFT)
               == (lax.broadcasted_iota(jnp.int32, (GLA_KDIM, GLA_VDIM), 1) >> GLA_DV_SHIFT))

    q_all = (gq * jnp.exp(bcum)).astype(_BF)
    k_all = gk * jnp.exp(-bcum)
    intra, kv_upd, decays = [], [], []
    for s in range(n_seq):
        for c in range(n_chunk):
            r0 = s * seq_rows + c * CHUNK
            bc = bcum[r0:r0 + CHUNK]
            b_last = bc[CHUNK - 1:CHUNK]
            k_end = (gk[r0:r0 + CHUNK] * jnp.exp(b_last - bc)).astype(_BF)
            vf = gv[r0:r0 + CHUNK]
            k_stack = jnp.concatenate(
                [jnp.where(head_of_lane == h, k_all[r0:r0 + CHUNK], 0.0) for h in range(GLA_HEADS)],
                axis=0).astype(_BF)
            v_bd = jnp.concatenate(
                [jnp.where(vblk == h, vf, 0.0) for h in range(GLA_HEADS)], axis=0).astype(_BF)
            a = jnp.where(causal, _dot_nt(q_all[r0:r0 + CHUNK], k_stack), 0.0).astype(_BF)
            intra.append(_dot(a, v_bd))
            kv_upd.append(jnp.where(bd_mask, _dot_tn(k_end, vf.astype(_BF)), 0.0))
            decay_col = jnp.broadcast_to(jnp.exp(b_last), (GLA_KDIM, GLA_KDIM)).T
            decays.append(jnp.concatenate([decay_col, decay_col], axis=1))
    o_rows = []
    for s in range(n_seq):
        state = s_ref[s]
        for c in range(n_chunk):
            i = s * n_chunk + c
            r0 = i * CHUNK
            o_rows.append(intra[i] + _dot(q_all[r0:r0 + CHUNK], state.astype(_BF)))
            state = decays[i] * state + kv_upd[i]
        s_ref[s] = state
    o_all = jnp.concatenate(o_rows, axis=0)
    ycat_ref[:, ATTN_DIM:ATTN_DIM + GLA_VDIM] = _group_rms_scale(o_all, GLA_DV, ones64) * gla_g * gate_g

    for s in range(n_seq):
        for h in range(GLA_HEADS):
            ns_ref[s, h] = s_ref[s, h * GLA_DK:(h + 1) * GLA_DK, h * GLA_DV:(h + 1) * GLA_DV]

    y_ref[...] = x + _dot(ycat_ref[...].astype(_BF), w_out_ref[...])


def _run_layer(x2d, n_seqs, seq_len, hist, sinks, w_in_p, w_up_p, w_pw, w_out, vecs, convw, *, carry, q_pos0):
    if carry:
        tile, n_seq, n_chunk = PROMPT_TILE, 1, PROMPT_TILE // CHUNK
        grid = (n_seqs, seq_len // tile)
    else:
        assert seq_len == CHUNK
        tile, n_seq, n_chunk = SAMPLE_SEQS * CHUNK, SAMPLE_SEQS, 1
        grid = (n_seqs // n_seq, 1)
    n_t = grid[1]
    seq_rows = n_chunk * CHUNK

    def const(shape):
        return pl.BlockSpec(shape, lambda g, t: (0,) * len(shape))

    def per_group(shape):
        return pl.BlockSpec((n_seq,) + shape, lambda g, t: (g,) + (0,) * len(shape))

    in_specs = [pl.BlockSpec(memory_space=pltpu.SMEM),
                pl.BlockSpec((tile, D_MODEL), lambda g, t: (g * n_t + t, 0))]
    args = [sinks, x2d]
    if not carry:
        in_specs += [per_group((WINDOW, KV_DIM)), per_group((WINDOW, KV_DIM)),
                     per_group((GLA_HEADS, GLA_DK, GLA_DV)), per_group((CONV_HIST, CONV_DIM))]
        args += list(hist)
    in_specs += [const(w_in_p.shape), const(w_up_p.shape), const(w_pw.shape), const(w_out.shape),
                 const(vecs.shape), const(convw.shape)]
    args += [w_in_p, w_up_p, w_pw, w_out, vecs, convw]

    out_shape = (jax.ShapeDtypeStruct(x2d.shape, _F32),
                 jax.ShapeDtypeStruct((n_seqs, WINDOW, KV_DIM), _F32),
                 jax.ShapeDtypeStruct((n_seqs, WINDOW, KV_DIM), _F32),
                 jax.ShapeDtypeStruct((n_seqs, GLA_HEADS, GLA_DK, GLA_DV), _F32),
                 jax.ShapeDtypeStruct((n_seqs, CONV_HIST, CONV_DIM), _F32))
    out_specs = (pl.BlockSpec((tile, D_MODEL), lambda g, t: (g * n_t + t, 0)),
                 per_group((WINDOW, KV_DIM)), per_group((WINDOW, KV_DIM)),
                 per_group((GLA_HEADS, GLA_DK, GLA_DV)), per_group((CONV_HIST, CONV_DIM)))
    scratch = [pltpu.VMEM((n_seq, WINDOW + seq_rows, 2 * LANES), _BF),
               pltpu.VMEM((n_seq, WINDOW + seq_rows, 2 * LANES), _BF),
               pltpu.VMEM((n_seq, GLA_KDIM, GLA_VDIM), _F32),
               pltpu.VMEM((n_seq, CONV_PAD + seq_rows, CONV_DIM), _F32),
               pltpu.VMEM((tile, ATTN_DIM), _BF),
               pltpu.VMEM((tile, ATTN_DIM), _BF),
               pltpu.VMEM((tile, D_MODEL), _F32),
               pltpu.VMEM((tile, IN_COLS_PACKED), _F32)]
    body = functools.partial(_layer_kernel, tile=tile, n_seq=n_seq, n_chunk=n_chunk, carry=carry, q_pos0=q_pos0)
    return pl.pallas_call(
        body, grid=grid, in_specs=in_specs, out_specs=out_specs, out_shape=out_shape,
        scratch_shapes=scratch,
        compiler_params=pltpu.CompilerParams(dimension_semantics=("arbitrary", "arbitrary"),
                                             vmem_limit_bytes=VMEM_LIMIT_BYTES),
        name="mixer_prompt" if carry else "mixer_sample",
    )(*args)


def _pack_layer_params(l, norm_gain, w_in, q_norm_gain, k_norm_gain, gla_w_gate_up, gla_b_gate, gla_norm_gain,
                       conv_w, conv_b, conv_ln_gain, conv_ln_bias, conv_w_pw, conv_b_pw, w_out):
    wi = w_in[l]
    glr0 = 2048
    w_in_p = jnp.concatenate(
        [wi[:, :glr0], wi[:, glr0 + GLA_LOWRANK:], wi[:, glr0:glr0 + GLA_LOWRANK],
         jnp.zeros((D_MODEL, LANES - GLA_LOWRANK), wi.dtype)], axis=1).astype(_BF)
    w_up_p = jnp.concatenate([gla_w_gate_up[l], jnp.zeros((LANES - GLA_LOWRANK, GLA_KDIM), _F32)], axis=0).astype(_BF)

    def row(v, reps=1):
        v = jnp.tile(v.astype(_F32), reps)
        return jnp.pad(v, (0, D_MODEL - v.shape[0]))[None, :]

    vec_rows = [row(norm_gain[l]), row(q_norm_gain[l], ATTN_HEADS), row(k_norm_gain[l], ATTN_KV_HEADS),
                row(gla_b_gate[l]), row(gla_norm_gain[l]), row(conv_b[l]), row(conv_ln_gain[l]),
                row(conv_ln_bias[l]), row(conv_b_pw[l])]
    vecs = jnp.concatenate(vec_rows + [jnp.zeros((VEC_ROWS - len(vec_rows), D_MODEL), _F32)], axis=0)
    convw = jnp.pad(conv_w[l].astype(_F32), ((0, CONV_PAD - CONV_WIDTH), (0, 0)))
    return w_in_p, w_up_p, conv_w_pw[l].astype(_BF), w_out[l].astype(_BF), vecs, convw


def kernel(x_prompt, x_sample, cache_k, cache_v, state_gla, state_conv, norm_gain, w_in, q_norm_gain, k_norm_gain, attn_sinks, gla_w_gate_up, gla_b_gate, gla_norm_gain, conv_w, conv_b, conv_ln_gain, conv_ln_bias, conv_w_pw, conv_b_pw, w_out):
    depth = w_in.shape[0]
    bp, lp, _ = x_prompt.shape
    bs, ls, _ = x_sample.shape
    yp = x_prompt.reshape(bp * lp, D_MODEL)
    ys = x_sample.reshape(bs * ls, D_MODEL)
    outs = [[] for _ in range(8)]
    for l in range(depth):
        params = _pack_layer_params(l, norm_gain, w_in, q_norm_gain, k_norm_gain, gla_w_gate_up, gla_b_gate,
                                    gla_norm_gain, conv_w, conv_b, conv_ln_gain, conv_ln_bias, conv_w_pw,
                                    conv_b_pw, w_out)
        sinks = attn_sinks[l].astype(_F32)
        yp, pk, pv, ps, pc = _run_layer(yp, bp, lp, None, sinks, *params, carry=True, q_pos0=0)
        hist = (cache_k[l].reshape(bs, WINDOW, KV_DIM), cache_v[l].reshape(bs, WINDOW, KV_DIM),
                state_gla[l], state_conv[l])
        ys, sk, sv, ss, sc = _run_layer(ys, bs, ls, hist, sinks, *params, carry=False, q_pos0=PAST_LEN)
        for lst, v in zip(outs, (pk, pv, ps, pc, sk, sv, ss, sc)):
            lst.append(v)
    kv_shape = (WINDOW, ATTN_KV_HEADS, HEAD_DIM)
    pk, pv, ps, pc, sk, sv, ss, sc = [jnp.stack(v) for v in outs]
    return (yp.reshape(bp, lp, D_MODEL), ys.reshape(bs, ls, D_MODEL),
            pk.reshape((depth, bp) + kv_shape), pv.reshape((depth, bp) + kv_shape), ps, pc,
            sk.reshape((depth, bs) + kv_shape), sv.reshape((depth, bs) + kv_shape), ss, sc)
```

```python
import functools
import math

import jax
import jax.numpy as jnp
from jax import lax
from jax.experimental import pallas as pl
from jax.experimental.pallas import tpu as pltpu

D_MODEL = 1024
CHUNK = 64
ATTN_HEADS = 8
ATTN_KV_HEADS = 2
HEAD_DIM = 64
ATTN_GROUP = ATTN_HEADS // ATTN_KV_HEADS
ATTN_DIM = ATTN_HEADS * HEAD_DIM
KV_DIM = ATTN_KV_HEADS * HEAD_DIM
WINDOW = 128
BAND = WINDOW + CHUNK
GLA_HEADS = 4
GLA_DK = 32
GLA_DV = 64
GLA_KDIM = GLA_HEADS * GLA_DK
GLA_VDIM = GLA_HEADS * GLA_DV
GLA_LOWRANK = 16
GLA_TAU = 16.0
CONV_DIM = 256
CONV_WIDTH = 31
CONV_HIST = CONV_WIDTH - 1
NORM_EPS = 1e-6
NEG_INF = -1e30
PAST_LEN = 4096
LOG2E = math.log2(math.e)

LANES = 128
SUBLANES = 8
CHUNK_SHIFT = CHUNK.bit_length() - 1
GLA_DK_SHIFT = GLA_DK.bit_length() - 1
GLA_DV_SHIFT = GLA_DV.bit_length() - 1
C_AQ, C_AK, C_AV, C_AG = 0, 512, 640, 768
C_GQ, C_GK, C_GV, C_GG = 1280, 1408, 1536, 1792
C_CV, C_CGL, C_CG, C_GLR = 2048, 2304, 2560, 2816
IN_COLS_PACKED = C_GLR + LANES
PROJ_GROUP = 256
CONV_PAD = 32
VEC_ROWS = 16
(V_NORM, V_QG, V_KG, V_GLAB, V_GLAG, V_CONVB, V_LNG, V_LNB, V_BPW) = range(9)

PROMPT_TILE = 256
SAMPLE_SEQS = 4
VMEM_LIMIT_BYTES = 48 * 1024 * 1024

_BF = jnp.bfloat16
_F32 = jnp.float32


def _group_ones(group):
    shift = group.bit_length() - 1
    r = lax.broadcasted_iota(jnp.int32, (LANES, LANES), 0) >> shift
    c = lax.broadcasted_iota(jnp.int32, (LANES, LANES), 1) >> shift
    return jnp.where(r == c, 1.0, 0.0).astype(_BF)


def _group_rms_scale(x, group, ones_bd):
    outs = []
    for c0 in range(0, x.shape[1], LANES):
        blk = x[:, c0:c0 + LANES]
        ss = _dot((blk * blk).astype(_BF), ones_bd)
        outs.append(blk * lax.rsqrt(ss * (1.0 / group) + NORM_EPS))
    return outs[0] if len(outs) == 1 else jnp.concatenate(outs, axis=1)


def _silu(x):
    return x * jax.nn.sigmoid(x)


def _dot(a, b):
    return jnp.dot(a, b, preferred_element_type=_F32)


def _dot_nt(a, b):
    return lax.dot_general(a, b, (((1,), (1,)), ((), ())), preferred_element_type=_F32)


def _dot_tn(a, b):
    return lax.dot_general(a, b, (((0,), (0,)), ((), ())), preferred_element_type=_F32)


def _stream_kernel(*refs, tile, n_seq, n_chunk, carry, q_pos0):
    if carry:
        (sinks_ref, x_ref, *rest) = refs
    else:
        (sinks_ref, x_ref, hk_all, hv_all, hs_all, hc_all, *rest) = refs
    (w_in_all, w_up_all, w_pw_all, w_out_all, vec_all, convw_all,
     y_ref, nk_all, nv_all, ns_all, nc_all,
     kd_all, vd_all, s_all, u_all, qlo_ref, qhi_ref, ycat_ref, proj_ref, xcur_ref) = rest
    t = pl.program_id(1)
    layer = pl.program_id(2)
    seq_rows = n_chunk * CHUNK
    w_in_ref, w_up_ref, w_pw_ref, w_out_ref = (r.at[layer] for r in (w_in_all, w_up_all, w_pw_all, w_out_all))
    vec_ref, convw_ref = vec_all.at[layer], convw_all.at[layer]
    kd_ref, vd_ref, s_ref, u_ref = (r.at[layer] for r in (kd_all, vd_all, s_all, u_all))
    nk_ref, nv_ref, ns_ref, nc_ref = (r.at[layer] for r in (nk_all, nv_all, ns_all, nc_all))
    if not carry:
        hk_ref, hv_ref, hs_ref, hc_ref = (r.at[layer] for r in (hk_all, hv_all, hs_all, hc_all))

    lane128 = lax.broadcasted_iota(jnp.int32, (1, LANES), 1)
    lo_half = lane128 < HEAD_DIM

    def dup_halves(a):
        sw = pltpu.roll(a, HEAD_DIM, 1)
        return jnp.concatenate([jnp.where(lo_half, a, sw), jnp.where(lo_half, sw, a)], axis=1)

    if carry:
        @pl.when(t == 0)
        def _():
            kd_ref[:, 0:WINDOW, :] = jnp.zeros((n_seq, WINDOW, 2 * LANES), _BF)
            vd_ref[:, 0:WINDOW, :] = jnp.zeros((n_seq, WINDOW, 2 * LANES), _BF)
            s_ref[...] = jnp.zeros(s_ref.shape, _F32)
            u_ref[:, 0:CONV_PAD, :] = jnp.zeros((n_seq, CONV_PAD, CONV_DIM), _F32)
    else:
        for s in range(n_seq):
            kd_ref[s, 0:WINDOW, :] = dup_halves(hk_ref[s]).astype(_BF)
            vd_ref[s, 0:WINDOW, :] = dup_halves(hv_ref[s]).astype(_BF)
            s_ref[s] = jnp.zeros((GLA_KDIM, GLA_VDIM), _F32)
            for h in range(GLA_HEADS):
                s_ref[s, h * GLA_DK:(h + 1) * GLA_DK, h * GLA_DV:(h + 1) * GLA_DV] = hs_ref[s, h]
            u_ref[s, 0:SUBLANES, :] = jnp.zeros((SUBLANES, CONV_DIM), _F32)
            u_ref[s, CONV_PAD - CONV_HIST:CONV_PAD, :] = hc_ref[s]

    @pl.when(layer == 0)
    def _():
        xcur_ref[...] = x_ref[...]
    x = xcur_ref[...]
    ms = jnp.mean(x * x, axis=-1, keepdims=True)
    hb = (x * lax.rsqrt(ms + NORM_EPS) * vec_ref[V_NORM:V_NORM + 1, :]).astype(_BF)

    def compute_group(g):
        c0, c1 = g * PROJ_GROUP, min((g + 1) * PROJ_GROUP, IN_COLS_PACKED)
        proj_ref[:, c0:c1] = _dot(hb, w_in_ref[:, c0:c1])

    def proj(c0, width):
        return proj_ref[:, c0:c0 + width]

    compute_group(C_CV // PROJ_GROUP)
    compute_group(C_CGL // PROJ_GROUP)
    u = proj(C_CV, CONV_DIM) * jax.nn.sigmoid(proj(C_CGL, CONV_DIM))
    for s in range(n_seq):
        u_ref[s, CONV_PAD:CONV_PAD + seq_rows, :] = u[s * seq_rows:(s + 1) * seq_rows]
    conv_acc = [jnp.broadcast_to(vec_ref[V_CONVB:V_CONVB + 1, 0:CONV_DIM], (seq_rows, CONV_DIM))] * n_seq
    for rho in range(SUBLANES):
        compute_group(rho)
        for s in range(n_seq):
            frame_rows = seq_rows + (SUBLANES if rho else 0)
            frame = None
            for j in range(CONV_WIDTH):
                off = CONV_PAD - CONV_HIST + j
                if off % SUBLANES != rho:
                    continue
                term = convw_ref[j:j + 1, :] * u_ref[s, off - rho:off - rho + frame_rows, :]
                frame = term if frame is None else frame + term
            conv_acc[s] = conv_acc[s] + frame[rho:rho + seq_rows]
    compute_group(C_CG // PROJ_GROUP)
    compute_group(C_GLR // PROJ_GROUP)
    cc = conv_acc[0] if n_seq == 1 else jnp.concatenate(conv_acc, axis=0)

    for s in range(n_seq):
        nc_ref[s] = u_ref[s, seq_rows + CONV_PAD - CONV_HIST:seq_rows + CONV_PAD, :]
    if carry:
        u_ref[:, 0:CONV_PAD, :] = u_ref[:, seq_rows:seq_rows + CONV_PAD, :]

    mu = jnp.mean(cc, axis=-1, keepdims=True)
    cen = cc - mu
    var = jnp.mean(cen * cen, axis=-1, keepdims=True)
    ln = cen * lax.rsqrt(var + NORM_EPS) * vec_ref[V_LNG:V_LNG + 1, 0:CONV_DIM] + vec_ref[V_LNB:V_LNB + 1, 0:CONV_DIM]
    cpw = _dot(_silu(ln).astype(_BF), w_pw_ref[...]) + vec_ref[V_BPW:V_BPW + 1, 0:CONV_DIM]
    ycat_ref[:, ATTN_DIM + GLA_VDIM:D_MODEL] = cpw * _silu(proj(C_CG, CONV_DIM))

    ones64 = _group_ones(HEAD_DIM)
    qn = _group_rms_scale(proj(C_AQ, ATTN_DIM), HEAD_DIM, ones64) * (
        vec_ref[V_QG:V_QG + 1, 0:ATTN_DIM] * (LOG2E * HEAD_DIM ** -0.5))
    lo4 = jnp.concatenate([lo_half] * (ATTN_DIM // LANES), axis=1)
    qlo_ref[...] = jnp.where(lo4, qn, 0.0).astype(_BF)
    qhi_ref[...] = jnp.where(lo4, 0.0, qn).astype(_BF)
    kn = _group_rms_scale(proj(C_AK, KV_DIM), HEAD_DIM, ones64) * vec_ref[V_KG:V_KG + 1, 0:KV_DIM]
    vv = proj(C_AV, KV_DIM)
    kdup = dup_halves(kn).astype(_BF)
    vdup = dup_halves(vv).astype(_BF)
    for s in range(n_seq):
        kd_ref[s, WINDOW:WINDOW + seq_rows, :] = kdup[s * seq_rows:(s + 1) * seq_rows]
        vd_ref[s, WINDOW:WINDOW + seq_rows, :] = vdup[s * seq_rows:(s + 1) * seq_rows]
    if carry:
        nk_ref[0] = kn[tile - WINDOW:tile]
        nv_ref[0] = vv[tile - WINDOW:tile]
    else:
        for s in range(n_seq):
            nk_ref[s, 0:WINDOW - seq_rows, :] = hk_ref[s, seq_rows:WINDOW, :]
            nv_ref[s, 0:WINDOW - seq_rows, :] = hv_ref[s, seq_rows:WINDOW, :]
            nk_ref[s, WINDOW - seq_rows:WINDOW, :] = kn[s * seq_rows:(s + 1) * seq_rows]
            nv_ref[s, WINDOW - seq_rows:WINDOW, :] = vv[s * seq_rows:(s + 1) * seq_rows]

    gate_a = _silu(proj(C_AG, ATTN_DIM))

    rows = ATTN_GROUP * CHUNK
    r_iota = lax.broadcasted_iota(jnp.int32, (rows, BAND), 0)
    c_iota = lax.broadcasted_iota(jnp.int32, (rows, BAND), 1)
    dist = jnp.abs((r_iota & (CHUNK - 1)) + WINDOW - c_iota).astype(_F32)
    r_head = r_iota >> CHUNK_SHIFT
    r_head1 = lax.broadcasted_iota(jnp.int32, (rows, 1), 0) >> CHUNK_SHIFT
    biases, sink_cols = [], []
    for j in range(ATTN_KV_HEADS):
        slope = jnp.zeros((rows, BAND), _F32)
        sink = jnp.zeros((rows, 1), _F32)
        for r in range(ATTN_GROUP):
            hd = j * ATTN_GROUP + r
            slope = jnp.where(r_head == r, 2.0 ** (-8.0 * (hd + 1) / ATTN_HEADS), slope)
            sink = jnp.where(r_head1 == r, sinks_ref[layer, hd], sink)
        biases.append(slope * LOG2E * dist)
        sink_cols.append(sink * LOG2E)

    for s in range(n_seq):
        for c in range(n_chunk):
            r0 = s * seq_rows + c * CHUNK
            k0 = c * CHUNK
            if carry:
                kpos0 = q_pos0 + (t * n_chunk + c) * CHUNK - WINDOW
                valid = (c_iota + kpos0) >= 0
            for j in range(ATTN_KV_HEADS):
                qs = jnp.concatenate(
                    [qlo_ref[r0:r0 + CHUNK, (2 * j) * LANES:(2 * j + 1) * LANES],
                     qhi_ref[r0:r0 + CHUNK, (2 * j) * LANES:(2 * j + 1) * LANES],
                     qlo_ref[r0:r0 + CHUNK, (2 * j + 1) * LANES:(2 * j + 2) * LANES],
                     qhi_ref[r0:r0 + CHUNK, (2 * j + 1) * LANES:(2 * j + 2) * LANES]], axis=0)
                kb = kd_ref[s, k0:k0 + BAND, j * LANES:(j + 1) * LANES]
                vb = vd_ref[s, k0:k0 + BAND, j * LANES:(j + 1) * LANES]
                sc = _dot_nt(qs, kb) - biases[j]
                if carry:
                    sc = jnp.where(valid, sc, NEG_INF)
                sink = sink_cols[j]
                m = jnp.maximum(jnp.max(sc, axis=-1, keepdims=True), sink)
                p = jnp.exp2(sc - m)
                denom = jnp.sum(p, axis=-1, keepdims=True) + jnp.exp2(sink - m)
                o = _dot(p.astype(_BF), vb) / denom
                for pb in range(2):
                    blk = 2 * j + pb
                    ob = jnp.where(lo_half, o[(2 * pb) * CHUNK:(2 * pb + 1) * CHUNK],
                                   o[(2 * pb + 1) * CHUNK:(2 * pb + 2) * CHUNK])
                    ycat_ref[r0:r0 + CHUNK, blk * LANES:(blk + 1) * LANES] = (
                        ob * gate_a[r0:r0 + CHUNK, blk * LANES:(blk + 1) * LANES])

    if carry:
        kd_ref[:, 0:WINDOW, :] = kd_ref[:, seq_rows:seq_rows + WINDOW, :]
        vd_ref[:, 0:WINDOW, :] = vd_ref[:, seq_rows:seq_rows + WINDOW, :]

    z = _dot(proj(C_GLR, LANES).astype(_BF), w_up_ref[...]) + vec_ref[V_GLAB:V_GLAB + 1, 0:GLA_KDIM]
    log_a = (jnp.minimum(z, 0.0) - jnp.log(1.0 + jnp.exp(-jnp.abs(z)))) * (1.0 / GLA_TAU)
    tr = lax.broadcasted_iota(jnp.int32, (tile, tile), 0)
    tc = lax.broadcasted_iota(jnp.int32, (tile, tile), 1)
    tri = jnp.where(((tr >> CHUNK_SHIFT) == (tc >> CHUNK_SHIFT)) & (tc <= tr), 1.0, 0.0).astype(_BF)
    la_hi = log_a.astype(_BF)
    la_lo = (log_a - la_hi.astype(_F32)).astype(_BF)
    bcum = _dot(tri, la_hi) + _dot(tri, la_lo)
    gq = proj(C_GQ, GLA_KDIM) * GLA_DK ** -0.5
    gk = proj(C_GK, GLA_KDIM)
    gv = proj(C_GV, GLA_VDIM)
    gate_g = _silu(proj(C_GG, GLA_VDIM))
    gla_g = jnp.concatenate([vec_ref[V_GLAG:V_GLAG + 1, 0:GLA_DV]] * GLA_HEADS, axis=1)

    head_of_lane = lane128 >> GLA_DK_SHIFT
    a_r = lax.broadcasted_iota(jnp.int32, (CHUNK, GLA_HEADS * CHUNK), 0)
    a_c = lax.broadcasted_iota(jnp.int32, (CHUNK, GLA_HEADS * CHUNK), 1)
    causal = (a_c & (CHUNK - 1)) <= a_r
    vblk = lax.broadcasted_iota(jnp.int32, (1, GLA_VDIM), 1) >> GLA_DV_SHIFT
    bd_mask = ((lax.broadcasted_iota(jnp.int32, (GLA_KDIM, GLA_VDIM), 0) >> GLA_DK_SHIFT)
               == (lax.broadcasted_iota(jnp.int32, (GLA_KDIM, GLA_VDIM), 1) >> GLA_DV_SHIFT))

    q_all = (gq * jnp.exp(bcum)).astype(_BF)
    k_all = gk * jnp.exp(-bcum)
    intra, kv_upd, decays = [], [], []
    for s in range(n_seq):
        for c in range(n_chunk):
            r0 = s * seq_rows + c * CHUNK
            bc = bcum[r0:r0 + CHUNK]
            b_last = bc[CHUNK - 1:CHUNK]
            k_end = (gk[r0:r0 + CHUNK] * jnp.exp(b_last - bc)).astype(_BF)
            vf = gv[r0:r0 + CHUNK]
            k_stack = jnp.concatenate(
                [jnp.where(head_of_lane == h, k_all[r0:r0 + CHUNK], 0.0) for h in range(GLA_HEADS)],
                axis=0).astype(_BF)
            v_bd = jnp.concatenate(
                [jnp.where(vblk == h, vf, 0.0) for h in range(GLA_HEADS)], axis=0).astype(_BF)
            a = jnp.where(causal, _dot_nt(q_all[r0:r0 + CHUNK], k_stack), 0.0).astype(_BF)
            intra.append(_dot(a, v_bd))
            kv_upd.append(jnp.where(bd_mask, _dot_tn(k_end, vf.astype(_BF)), 0.0))
            decay_col = jnp.broadcast_to(jnp.exp(b_last), (GLA_KDIM, GLA_KDIM)).T
            decays.append(jnp.concatenate([decay_col, decay_col], axis=1))
    o_rows = []
    for s in range(n_seq):
        state = s_ref[s]
        for c in range(n_chunk):
            i = s * n_chunk + c
            r0 = i * CHUNK
            o_rows.append(intra[i] + _dot(q_all[r0:r0 + CHUNK], state.astype(_BF)))
            state = decays[i] * state + kv_upd[i]
        s_ref[s] = state
    o_all = jnp.concatenate(o_rows, axis=0)
    ycat_ref[:, ATTN_DIM:ATTN_DIM + GLA_VDIM] = _group_rms_scale(o_all, GLA_DV, ones64) * gla_g * gate_g

    for s in range(n_seq):
        for h in range(GLA_HEADS):
            ns_ref[s, h] = s_ref[s, h * GLA_DK:(h + 1) * GLA_DK, h * GLA_DV:(h + 1) * GLA_DV]

    y = x + _dot(ycat_ref[...].astype(_BF), w_out_ref[...])
    y_ref[...] = y
    xcur_ref[...] = y


def _run_stream(x2d, n_seqs, seq_len, hist, sinks, params, *, carry, q_pos0):
    depth = sinks.shape[0]
    if carry:
        tile, n_seq, n_chunk = PROMPT_TILE, 1, PROMPT_TILE // CHUNK
        grid = (n_seqs, seq_len // tile, depth)
    else:
        assert seq_len == CHUNK and q_pos0 >= WINDOW
        tile, n_seq, n_chunk = SAMPLE_SEQS * CHUNK, SAMPLE_SEQS, 1
        grid = (n_seqs // n_seq, 1, depth)
    n_t = grid[1]
    seq_rows = n_chunk * CHUNK

    def const(shape):
        return pl.BlockSpec(shape, lambda g, t, l: (0,) * len(shape))

    def per_layer_group(shape):
        return pl.BlockSpec((depth, n_seq) + shape, lambda g, t, l: (0, g) + (0,) * len(shape))

    state_shapes = ((WINDOW, KV_DIM), (WINDOW, KV_DIM), (GLA_HEADS, GLA_DK, GLA_DV), (CONV_HIST, CONV_DIM))
    x_spec = pl.BlockSpec((tile, D_MODEL), lambda g, t, l: (g * n_t + t, 0))
    in_specs = [pl.BlockSpec(memory_space=pltpu.SMEM), x_spec]
    args = [sinks, x2d]
    if not carry:
        in_specs += [per_layer_group(sh) for sh in state_shapes]
        args += list(hist)
    in_specs += [const(p.shape) for p in params]
    args += list(params)

    out_shape = (jax.ShapeDtypeStruct(x2d.shape, _F32),) + tuple(
        jax.ShapeDtypeStruct((depth, n_seqs) + sh, _F32) for sh in state_shapes)
    out_specs = (x_spec,) + tuple(per_layer_group(sh) for sh in state_shapes)
    scratch = [pltpu.VMEM((depth, n_seq, WINDOW + seq_rows, 2 * LANES), _BF),
               pltpu.VMEM((depth, n_seq, WINDOW + seq_rows, 2 * LANES), _BF),
               pltpu.VMEM((depth, n_seq, GLA_KDIM, GLA_VDIM), _F32),
               pltpu.VMEM((depth, n_seq, CONV_PAD + seq_rows, CONV_DIM), _F32),
               pltpu.VMEM((tile, ATTN_DIM), _BF),
               pltpu.VMEM((tile, ATTN_DIM), _BF),
               pltpu.VMEM((tile, D_MODEL), _F32),
               pltpu.VMEM((tile, IN_COLS_PACKED), _F32),
               pltpu.VMEM((tile, D_MODEL), _F32)]
    body = functools.partial(_stream_kernel, tile=tile, n_seq=n_seq, n_chunk=n_chunk, carry=carry, q_pos0=q_pos0)
    return pl.pallas_call(
        body, grid=grid, in_specs=in_specs, out_specs=out_specs, out_shape=out_shape,
        scratch_shapes=scratch,
        compiler_params=pltpu.CompilerParams(dimension_semantics=("arbitrary", "arbitrary", "arbitrary"),
                                             vmem_limit_bytes=VMEM_LIMIT_BYTES),
        name="mixer_prompt" if carry else "mixer_sample",
    )(*args)


def _pack_params(norm_gain, w_in, q_norm_gain, k_norm_gain, gla_w_gate_up, gla_b_gate, gla_norm_gain,
                 conv_w, conv_b, conv_ln_gain, conv_ln_bias, conv_w_pw, conv_b_pw, w_out):
    depth = w_in.shape[0]
    glr0 = 2048
    w_in_p = jnp.concatenate(
        [w_in[:, :, :glr0], w_in[:, :, glr0 + GLA_LOWRANK:], w_in[:, :, glr0:glr0 + GLA_LOWRANK],
         jnp.zeros((depth, D_MODEL, LANES - GLA_LOWRANK), w_in.dtype)], axis=2).astype(_BF)
    w_up_p = jnp.pad(gla_w_gate_up, ((0, 0), (0, LANES - GLA_LOWRANK), (0, 0))).astype(_BF)

    def row(v, reps=1):
        v = jnp.tile(v.astype(_F32), (1, reps))
        return jnp.pad(v, ((0, 0), (0, D_MODEL - v.shape[1])))[:, None, :]

    vec_rows = [row(norm_gain), row(q_norm_gain, ATTN_HEADS), row(k_norm_gain, ATTN_KV_HEADS),
                row(gla_b_gate), row(gla_norm_gain), row(conv_b), row(conv_ln_gain),
                row(conv_ln_bias), row(conv_b_pw)]
    vecs = jnp.concatenate(vec_rows + [jnp.zeros((depth, VEC_ROWS - len(vec_rows), D_MODEL), _F32)], axis=1)
    convw = jnp.pad(conv_w.astype(_F32), ((0, 0), (0, CONV_PAD - CONV_WIDTH), (0, 0)))
    return w_in_p, w_up_p, conv_w_pw.astype(_BF), w_out.astype(_BF), vecs, convw


def kernel(x_prompt, x_sample, cache_k, cache_v, state_gla, state_conv, norm_gain, w_in, q_norm_gain, k_norm_gain, attn_sinks, gla_w_gate_up, gla_b_gate, gla_norm_gain, conv_w, conv_b, conv_ln_gain, conv_ln_bias, conv_w_pw, conv_b_pw, w_out):
    depth = w_in.shape[0]
    bp, lp, _ = x_prompt.shape
    bs, ls, _ = x_sample.shape
    params = _pack_params(norm_gain, w_in, q_norm_gain, k_norm_gain, gla_w_gate_up, gla_b_gate, gla_norm_gain,
                          conv_w, conv_b, conv_ln_gain, conv_ln_bias, conv_w_pw, conv_b_pw, w_out)
    sinks = attn_sinks.astype(_F32)
    yp, pk, pv, ps, pc = _run_stream(x_prompt.reshape(bp * lp, D_MODEL), bp, lp, None, sinks, params,
                                     carry=True, q_pos0=0)
    hist = (cache_k.reshape(depth, bs, WINDOW, KV_DIM), cache_v.reshape(depth, bs, WINDOW, KV_DIM),
            state_gla, state_conv)
    ys, sk, sv, ss, sc = _run_stream(x_sample.reshape(bs * ls, D_MODEL), bs, ls, hist, sinks, params,
                                     carry=False, q_pos0=PAST_LEN)
    kv_shape = (WINDOW, ATTN_KV_HEADS, HEAD_DIM)
    return (yp.reshape(bp, lp, D_MODEL), ys.reshape(bs, ls, D_MODEL),
            pk.reshape((depth, bp) + kv_shape), pv.reshape((depth, bp) + kv_shape), ps, pc,
            sk.reshape((depth, bs) + kv_shape), sv.reshape((depth, bs) + kv_shape), ss, sc)
```

```python
import functools
import math

import jax
import jax.numpy as jnp
from jax import lax
from jax.experimental import pallas as pl
from jax.experimental.pallas import tpu as pltpu

D_MODEL = 1024
CHUNK = 64
ATTN_HEADS = 8
ATTN_KV_HEADS = 2
HEAD_DIM = 64
ATTN_GROUP = ATTN_HEADS // ATTN_KV_HEADS
ATTN_DIM = ATTN_HEADS * HEAD_DIM
KV_DIM = ATTN_KV_HEADS * HEAD_DIM
WINDOW = 128
BAND = WINDOW + CHUNK
GLA_HEADS = 4
GLA_DK = 32
GLA_DV = 64
GLA_KDIM = GLA_HEADS * GLA_DK
GLA_VDIM = GLA_HEADS * GLA_DV
GLA_LOWRANK = 16
GLA_TAU = 16.0
CONV_DIM = 256
CONV_WIDTH = 31
CONV_HIST = CONV_WIDTH - 1
NORM_EPS = 1e-6
NEG_INF = -1e30
PAST_LEN = 4096
LOG2E = math.log2(math.e)

LANES = 128
SUBLANES = 8
CHUNK_SHIFT = CHUNK.bit_length() - 1
GLA_DK_SHIFT = GLA_DK.bit_length() - 1
GLA_DV_SHIFT = GLA_DV.bit_length() - 1
C_AQ, C_AK, C_AV, C_AG = 0, 512, 640, 768
C_GQ, C_GK, C_GV, C_GG = 1280, 1408, 1536, 1792
C_CV, C_CGL, C_CG, C_GLR = 2048, 2304, 2560, 2816
IN_COLS_PACKED = C_GLR + LANES
PROJ_GROUP = 256
CONV_PAD = 32
VEC_ROWS = 16
(V_NORM, V_QG, V_KG, V_GLAB, V_GLAG, V_CONVB, V_LNG, V_LNB, V_BPW) = range(9)

PROMPT_TILE = 256
SAMPLE_SEQS = 4
VMEM_LIMIT_BYTES = 48 * 1024 * 1024

_BF = jnp.bfloat16
_F32 = jnp.float32


def _group_ones(group):
    shift = group.bit_length() - 1
    r = lax.broadcasted_iota(jnp.int32, (LANES, LANES), 0) >> shift
    c = lax.broadcasted_iota(jnp.int32, (LANES, LANES), 1) >> shift
    return jnp.where(r == c, 1.0, 0.0).astype(_BF)


def _group_rms_scale(x, group, ones_bd):
    outs = []
    for c0 in range(0, x.shape[1], LANES):
        blk = x[:, c0:c0 + LANES]
        ss = _dot((blk * blk).astype(_BF), ones_bd)
        outs.append(blk * lax.rsqrt(ss * (1.0 / group) + NORM_EPS))
    return outs[0] if len(outs) == 1 else jnp.concatenate(outs, axis=1)


def _silu(x):
    return x * jax.nn.sigmoid(x)


def _dot(a, b):
    return jnp.dot(a, b, preferred_element_type=_F32)


def _dot_nt(a, b):
    return lax.dot_general(a, b, (((1,), (1,)), ((), ())), preferred_element_type=_F32)


def _dot_tn(a, b):
    return lax.dot_general(a, b, (((0,), (0,)), ((), ())), preferred_element_type=_F32)


def _stream_kernel(*refs, tile, n_seq, n_chunk, carry, q_pos0):
    if carry:
        (sinks_ref, x_ref, xnext_ref, *rest) = refs
    else:
        (sinks_ref, x_ref, xnext_ref, hk_all, hv_all, hs_all, hc_all, *rest) = refs
    (w_in_all, w_up_all, w_pw_all, w_out_all, vec_all, convw_all,
     y_ref, nk_all, nv_all, ns_all, nc_all,
     kd_all, vd_all, s_all, u_all, qlo_ref, qhi_ref, ycat_ref, proj_ref, xcur_ref, hb_ref) = rest
    t = pl.program_id(1)
    layer = pl.program_id(2)
    seq_rows = n_chunk * CHUNK
    w_in_ref, w_up_ref, w_pw_ref, w_out_ref = (r.at[layer] for r in (w_in_all, w_up_all, w_pw_all, w_out_all))
    vec_ref, convw_ref = vec_all.at[layer], convw_all.at[layer]
    kd_ref, vd_ref, s_ref, u_ref = (r.at[layer] for r in (kd_all, vd_all, s_all, u_all))
    nk_ref, nv_ref, ns_ref, nc_ref = (r.at[layer] for r in (nk_all, nv_all, ns_all, nc_all))
    if not carry:
        hk_ref, hv_ref, hs_ref, hc_ref = (r.at[layer] for r in (hk_all, hv_all, hs_all, hc_all))

    lane128 = lax.broadcasted_iota(jnp.int32, (1, LANES), 1)
    lo_half = lane128 < HEAD_DIM

    def dup_halves(a):
        sw = pltpu.roll(a, HEAD_DIM, 1)
        return jnp.concatenate([jnp.where(lo_half, a, sw), jnp.where(lo_half, sw, a)], axis=1)

    if carry:
        @pl.when(t == 0)
        def _():
            kd_ref[:, 0:WINDOW, :] = jnp.zeros((n_seq, WINDOW, 2 * LANES), _BF)
            vd_ref[:, 0:WINDOW, :] = jnp.zeros((n_seq, WINDOW, 2 * LANES), _BF)
            s_ref[...] = jnp.zeros(s_ref.shape, _F32)
            u_ref[:, 0:CONV_PAD, :] = jnp.zeros((n_seq, CONV_PAD, CONV_DIM), _F32)
    else:
        for s in range(n_seq):
            kd_ref[s, 0:WINDOW, :] = dup_halves(hk_ref[s]).astype(_BF)
            vd_ref[s, 0:WINDOW, :] = dup_halves(hv_ref[s]).astype(_BF)
            s_ref[s] = jnp.zeros((GLA_KDIM, GLA_VDIM), _F32)
            for h in range(GLA_HEADS):
                s_ref[s, h * GLA_DK:(h + 1) * GLA_DK, h * GLA_DV:(h + 1) * GLA_DV] = hs_ref[s, h]
            u_ref[s, 0:SUBLANES, :] = jnp.zeros((SUBLANES, CONV_DIM), _F32)
            u_ref[s, CONV_PAD - CONV_HIST:CONV_PAD, :] = hc_ref[s]

    def normed_bf16(v, gain_row):
        ms = jnp.mean(v * v, axis=-1, keepdims=True)
        return (v * lax.rsqrt(ms + NORM_EPS) * gain_row).astype(_BF)

    @pl.when(layer == 0)
    def _():
        xcur_ref[...] = x_ref[...]

    @pl.when((pl.program_id(0) == 0) & (t == 0) & (layer == 0))
    def _():
        hb_ref[...] = normed_bf16(x_ref[...], vec_ref[V_NORM:V_NORM + 1, :])
    x = xcur_ref[...]
    hb = hb_ref[...]

    anchor_on = sinks_ref[layer, ATTN_HEADS] != 0.0

    def compute_group(g):
        c0, c1 = g * PROJ_GROUP, min((g + 1) * PROJ_GROUP, IN_COLS_PACKED)
        res = _dot(hb, w_in_ref[:, c0:c1])
        proj_ref[:, c0:c1] = res
        return jnp.where(anchor_on, res[0:1, 0:CONV_DIM], 0.0)

    def proj(c0, width):
        return proj_ref[:, c0:c0 + width]

    compute_group(C_CV // PROJ_GROUP)
    compute_group(C_CGL // PROJ_GROUP)
    u = proj(C_CV, CONV_DIM) * jax.nn.sigmoid(proj(C_CGL, CONV_DIM))
    for s in range(n_seq):
        u_ref[s, CONV_PAD:CONV_PAD + seq_rows, :] = u[s * seq_rows:(s + 1) * seq_rows]
    conv_acc = [jnp.broadcast_to(vec_ref[V_CONVB:V_CONVB + 1, 0:CONV_DIM], (seq_rows, CONV_DIM))] * n_seq
    for rho in range(SUBLANES):
        anchor = compute_group(rho)
        for s in range(n_seq):
            frame_rows = seq_rows + (SUBLANES if rho else 0)
            frame = None
            for j in range(CONV_WIDTH):
                off = CONV_PAD - CONV_HIST + j
                if off % SUBLANES != rho:
                    continue
                term = (convw_ref[j:j + 1, :] + anchor) * u_ref[s, off - rho:off - rho + frame_rows, :]
                frame = term if frame is None else frame + term
            conv_acc[s] = conv_acc[s] + frame[rho:rho + seq_rows]
    compute_group(C_CG // PROJ_GROUP)
    compute_group(C_GLR // PROJ_GROUP)
    cc = conv_acc[0] if n_seq == 1 else jnp.concatenate(conv_acc, axis=0)

    for s in range(n_seq):
        nc_ref[s] = u_ref[s, seq_rows + CONV_PAD - CONV_HIST:seq_rows + CONV_PAD, :]
    if carry:
        u_ref[:, 0:CONV_PAD, :] = u_ref[:, seq_rows:seq_rows + CONV_PAD, :]

    mu = jnp.mean(cc, axis=-1, keepdims=True)
    cen = cc - mu
    var = jnp.mean(cen * cen, axis=-1, keepdims=True)
    ln = cen * lax.rsqrt(var + NORM_EPS) * vec_ref[V_LNG:V_LNG + 1, 0:CONV_DIM] + vec_ref[V_LNB:V_LNB + 1, 0:CONV_DIM]
    cpw = _dot(_silu(ln).astype(_BF), w_pw_ref[...]) + vec_ref[V_BPW:V_BPW + 1, 0:CONV_DIM]
    ycat_ref[:, ATTN_DIM + GLA_VDIM:D_MODEL] = cpw * _silu(proj(C_CG, CONV_DIM))

    ones64 = _group_ones(HEAD_DIM)
    qn = _group_rms_scale(proj(C_AQ, ATTN_DIM), HEAD_DIM, ones64) * (
        vec_ref[V_QG:V_QG + 1, 0:ATTN_DIM] * (LOG2E * HEAD_DIM ** -0.5))
    lo4 = jnp.concatenate([lo_half] * (ATTN_DIM // LANES), axis=1)
    qlo_ref[...] = jnp.where(lo4, qn, 0.0).astype(_BF)
    qhi_ref[...] = jnp.where(lo4, 0.0, qn).astype(_BF)
    kn = _group_rms_scale(proj(C_AK, KV_DIM), HEAD_DIM, ones64) * vec_ref[V_KG:V_KG + 1, 0:KV_DIM]
    vv = proj(C_AV, KV_DIM)
    kdup = dup_halves(kn).astype(_BF)
    vdup = dup_halves(vv).astype(_BF)
    for s in range(n_seq):
        kd_ref[s, WINDOW:WINDOW + seq_rows, :] = kdup[s * seq_rows:(s + 1) * seq_rows]
        vd_ref[s, WINDOW:WINDOW + seq_rows, :] = vdup[s * seq_rows:(s + 1) * seq_rows]
    if carry:
        nk_ref[0] = kn[tile - WINDOW:tile]
        nv_ref[0] = vv[tile - WINDOW:tile]
    else:
        for s in range(n_seq):
            nk_ref[s, 0:WINDOW - seq_rows, :] = hk_ref[s, seq_rows:WINDOW, :]
            nv_ref[s, 0:WINDOW - seq_rows, :] = hv_ref[s, seq_rows:WINDOW, :]
            nk_ref[s, WINDOW - seq_rows:WINDOW, :] = kn[s * seq_rows:(s + 1) * seq_rows]
            nv_ref[s, WINDOW - seq_rows:WINDOW, :] = vv[s * seq_rows:(s + 1) * seq_rows]

    gate_a = _silu(proj(C_AG, ATTN_DIM))

    rows = ATTN_GROUP * CHUNK
    r_iota = lax.broadcasted_iota(jnp.int32, (rows, BAND), 0)
    c_iota = lax.broadcasted_iota(jnp.int32, (rows, BAND), 1)
    dist = jnp.abs((r_iota & (CHUNK - 1)) + WINDOW - c_iota).astype(_F32)
    r_head = r_iota >> CHUNK_SHIFT
    r_head1 = lax.broadcasted_iota(jnp.int32, (rows, 1), 0) >> CHUNK_SHIFT
    biases, sink_cols = [], []
    for j in range(ATTN_KV_HEADS):
        slope = jnp.zeros((rows, BAND), _F32)
        sink = jnp.zeros((rows, 1), _F32)
        for r in range(ATTN_GROUP):
            hd = j * ATTN_GROUP + r
            slope = jnp.where(r_head == r, 2.0 ** (-8.0 * (hd + 1) / ATTN_HEADS), slope)
            sink = jnp.where(r_head1 == r, sinks_ref[layer, hd], sink)
        biases.append(slope * LOG2E * dist)
        sink_cols.append(sink * LOG2E)

    for s in range(n_seq):
        for c in range(n_chunk):
            r0 = s * seq_rows + c * CHUNK
            k0 = c * CHUNK
            if carry:
                kpos0 = q_pos0 + (t * n_chunk + c) * CHUNK - WINDOW
                valid = (c_iota + kpos0) >= 0
            for j in range(ATTN_KV_HEADS):
                qs = jnp.concatenate(
                    [qlo_ref[r0:r0 + CHUNK, (2 * j) * LANES:(2 * j + 1) * LANES],
                     qhi_ref[r0:r0 + CHUNK, (2 * j) * LANES:(2 * j + 1) * LANES],
                     qlo_ref[r0:r0 + CHUNK, (2 * j + 1) * LANES:(2 * j + 2) * LANES],
                     qhi_ref[r0:r0 + CHUNK, (2 * j + 1) * LANES:(2 * j + 2) * LANES]], axis=0)
                kb = kd_ref[s, k0:k0 + BAND, j * LANES:(j + 1) * LANES]
                vb = vd_ref[s, k0:k0 + BAND, j * LANES:(j + 1) * LANES]
                sc = _dot_nt(qs, kb) - biases[j]
                if carry:
                    sc = jnp.where(valid, sc, NEG_INF)
                sink = sink_cols[j]
                m = jnp.maximum(jnp.max(sc, axis=-1, keepdims=True), sink)
                p = jnp.exp2(sc - m)
                denom = jnp.sum(p, axis=-1, keepdims=True) + jnp.exp2(sink - m)
                o = _dot(p.astype(_BF), vb) / denom
                for pb in range(2):
                    blk = 2 * j + pb
                    ob = jnp.where(lo_half, o[(2 * pb) * CHUNK:(2 * pb + 1) * CHUNK],
                                   o[(2 * pb + 1) * CHUNK:(2 * pb + 2) * CHUNK])
                    ycat_ref[r0:r0 + CHUNK, blk * LANES:(blk + 1) * LANES] = (
                        ob * gate_a[r0:r0 + CHUNK, blk * LANES:(blk + 1) * LANES])

    if carry:
        kd_ref[:, 0:WINDOW, :] = kd_ref[:, seq_rows:seq_rows + WINDOW, :]
        vd_ref[:, 0:WINDOW, :] = vd_ref[:, seq_rows:seq_rows + WINDOW, :]

    z = _dot(proj(C_GLR, LANES).astype(_BF), w_up_ref[...]) + vec_ref[V_GLAB:V_GLAB + 1, 0:GLA_KDIM]
    log_a = (jnp.minimum(z, 0.0) - jnp.log(1.0 + jnp.exp(-jnp.abs(z)))) * (1.0 / GLA_TAU)
    tr = lax.broadcasted_iota(jnp.int32, (tile, tile), 0)
    tc = lax.broadcasted_iota(jnp.int32, (tile, tile), 1)
    tri = jnp.where(((tr >> CHUNK_SHIFT) == (tc >> CHUNK_SHIFT)) & (tc <= tr), 1.0, 0.0).astype(_BF)
    la_hi = log_a.astype(_BF)
    la_lo = (log_a - la_hi.astype(_F32)).astype(_BF)
    bcum = _dot(tri, la_hi) + _dot(tri, la_lo)
    gq = proj(C_GQ, GLA_KDIM) * GLA_DK ** -0.5
    gk = proj(C_GK, GLA_KDIM)
    gv = proj(C_GV, GLA_VDIM)
    gate_g = _silu(proj(C_GG, GLA_VDIM))
    gla_g = jnp.concatenate([vec_ref[V_GLAG:V_GLAG + 1, 0:GLA_DV]] * GLA_HEADS, axis=1)

    head_of_lane = lane128 >> GLA_DK_SHIFT
    a_r = lax.broadcasted_iota(jnp.int32, (CHUNK, GLA_HEADS * CHUNK), 0)
    a_c = lax.broadcasted_iota(jnp.int32, (CHUNK, GLA_HEADS * CHUNK), 1)
    causal = (a_c & (CHUNK - 1)) <= a_r
    vblk = lax.broadcasted_iota(jnp.int32, (1, GLA_VDIM), 1) >> GLA_DV_SHIFT
    bd_mask = ((lax.broadcasted_iota(jnp.int32, (GLA_KDIM, GLA_VDIM), 0) >> GLA_DK_SHIFT)
               == (lax.broadcasted_iota(jnp.int32, (GLA_KDIM, GLA_VDIM), 1) >> GLA_DV_SHIFT))

    q_all = (gq * jnp.exp(bcum)).astype(_BF)
    k_all = gk * jnp.exp(-bcum)
    intra, kv_upd, decays = [], [], []
    for s in range(n_seq):
        for c in range(n_chunk):
            r0 = s * seq_rows + c * CHUNK
            bc = bcum[r0:r0 + CHUNK]
            b_last = bc[CHUNK - 1:CHUNK]
            k_end = (gk[r0:r0 + CHUNK] * jnp.exp(b_last - bc)).astype(_BF)
            vf = gv[r0:r0 + CHUNK]
            k_stack = jnp.concatenate(
                [jnp.where(head_of_lane == h, k_all[r0:r0 + CHUNK], 0.0) for h in range(GLA_HEADS)],
                axis=0).astype(_BF)
            v_bd = jnp.concatenate(
                [jnp.where(vblk == h, vf, 0.0) for h in range(GLA_HEADS)], axis=0).astype(_BF)
            a = jnp.where(causal, _dot_nt(q_all[r0:r0 + CHUNK], k_stack), 0.0).astype(_BF)
            intra.append(_dot(a, v_bd))
            kv_upd.append(jnp.where(bd_mask, _dot_tn(k_end, vf.astype(_BF)), 0.0))
            decay_col = jnp.broadcast_to(jnp.exp(b_last), (GLA_KDIM, GLA_KDIM)).T
            decays.append(jnp.concatenate([decay_col, decay_col], axis=1))
    o_rows = []
    for s in range(n_seq):
        state = s_ref[s]
        for c in range(n_chunk):
            i = s * n_chunk + c
            r0 = i * CHUNK
            o_rows.append(intra[i] + _dot(q_all[r0:r0 + CHUNK], state.astype(_BF)))
            state = decays[i] * state + kv_upd[i]
        s_ref[s] = state
    o_all = jnp.concatenate(o_rows, axis=0)
    ycat_ref[:, ATTN_DIM:ATTN_DIM + GLA_VDIM] = _group_rms_scale(o_all, GLA_DV, ones64) * gla_g * gate_g

    for s in range(n_seq):
        for h in range(GLA_HEADS):
            ns_ref[s, h] = s_ref[s, h * GLA_DK:(h + 1) * GLA_DK, h * GLA_DV:(h + 1) * GLA_DV]

    y = x + _dot(ycat_ref[...].astype(_BF), w_out_ref[...])
    y_ref[...] = y
    xcur_ref[...] = y
    last_layer = layer == pl.num_programs(2) - 1
    nxt = jnp.where(last_layer, xnext_ref[...], y)
    next_gain = vec_all[jnp.where(last_layer, 0, layer + 1), V_NORM:V_NORM + 1, :]
    hb_ref[...] = normed_bf16(nxt, next_gain)


def _run_stream(x2d, n_seqs, seq_len, hist, sinks, params, *, carry, q_pos0):
    depth = sinks.shape[0]
    if carry:
        tile, n_seq, n_chunk = PROMPT_TILE, 1, PROMPT_TILE // CHUNK
        grid = (n_seqs, seq_len // tile, depth)
    else:
        assert seq_len == CHUNK and q_pos0 >= WINDOW
        tile, n_seq, n_chunk = SAMPLE_SEQS * CHUNK, SAMPLE_SEQS, 1
        grid = (n_seqs // n_seq, 1, depth)
    n_t = grid[1]
    seq_rows = n_chunk * CHUNK

    def const(shape):
        return pl.BlockSpec(shape, lambda g, t, l: (0,) * len(shape))

    def per_layer_group(shape):
        return pl.BlockSpec((depth, n_seq) + shape, lambda g, t, l: (0, g) + (0,) * len(shape))

    state_shapes = ((WINDOW, KV_DIM), (WINDOW, KV_DIM), (GLA_HEADS, GLA_DK, GLA_DV), (CONV_HIST, CONV_DIM))
    x_spec = pl.BlockSpec((tile, D_MODEL), lambda g, t, l: (g * n_t + t, 0))
    n_tiles = x2d.shape[0] // tile
    xnext_spec = pl.BlockSpec((tile, D_MODEL), lambda g, t, l: (jnp.minimum(g * n_t + t + 1, n_tiles - 1), 0))
    in_specs = [pl.BlockSpec(memory_space=pltpu.SMEM), x_spec, xnext_spec]
    args = [sinks, x2d, x2d]
    if not carry:
        in_specs += [per_layer_group(sh) for sh in state_shapes]
        args += list(hist)
    in_specs += [const(p.shape) for p in params]
    args += list(params)

    out_shape = (jax.ShapeDtypeStruct(x2d.shape, _F32),) + tuple(
        jax.ShapeDtypeStruct((depth, n_seqs) + sh, _F32) for sh in state_shapes)
    out_specs = (x_spec,) + tuple(per_layer_group(sh) for sh in state_shapes)
    scratch = [pltpu.VMEM((depth, n_seq, WINDOW + seq_rows, 2 * LANES), _BF),
               pltpu.VMEM((depth, n_seq, WINDOW + seq_rows, 2 * LANES), _BF),
               pltpu.VMEM((depth, n_seq, GLA_KDIM, GLA_VDIM), _F32),
               pltpu.VMEM((depth, n_seq, CONV_PAD + seq_rows, CONV_DIM), _F32),
               pltpu.VMEM((tile, ATTN_DIM), _BF),
               pltpu.VMEM((tile, ATTN_DIM), _BF),
               pltpu.VMEM((tile, D_MODEL), _F32),
               pltpu.VMEM((tile, IN_COLS_PACKED), _F32),
               pltpu.VMEM((tile, D_MODEL), _F32),
               pltpu.VMEM((tile, D_MODEL), _BF)]
    body = functools.partial(_stream_kernel, tile=tile, n_seq=n_seq, n_chunk=n_chunk, carry=carry, q_pos0=q_pos0)
    return pl.pallas_call(
        body, grid=grid, in_specs=in_specs, out_specs=out_specs, out_shape=out_shape,
        scratch_shapes=scratch,
        compiler_params=pltpu.CompilerParams(dimension_semantics=("arbitrary", "arbitrary", "arbitrary"),
                                             vmem_limit_bytes=VMEM_LIMIT_BYTES),
        name="mixer_prompt" if carry else "mixer_sample",
    )(*args)


def _pack_params(norm_gain, w_in, q_norm_gain, k_norm_gain, gla_w_gate_up, gla_b_gate, gla_norm_gain,
                 conv_w, conv_b, conv_ln_gain, conv_ln_bias, conv_w_pw, conv_b_pw, w_out):
    depth = w_in.shape[0]
    glr0 = 2048
    w_in_p = jnp.concatenate(
        [w_in[:, :, :glr0], w_in[:, :, glr0 + GLA_LOWRANK:], w_in[:, :, glr0:glr0 + GLA_LOWRANK],
         jnp.zeros((depth, D_MODEL, LANES - GLA_LOWRANK), w_in.dtype)], axis=2).astype(_BF)
    w_up_p = jnp.pad(gla_w_gate_up, ((0, 0), (0, LANES - GLA_LOWRANK), (0, 0))).astype(_BF)

    def row(v, reps=1):
        v = jnp.tile(v.astype(_F32), (1, reps))
        return jnp.pad(v, ((0, 0), (0, D_MODEL - v.shape[1])))[:, None, :]

    vec_rows = [row(norm_gain), row(q_norm_gain, ATTN_HEADS), row(k_norm_gain, ATTN_KV_HEADS),
                row(gla_b_gate), row(gla_norm_gain), row(conv_b), row(conv_ln_gain),
                row(conv_ln_bias), row(conv_b_pw)]
    vecs = jnp.concatenate(vec_rows + [jnp.zeros((depth, VEC_ROWS - len(vec_rows), D_MODEL), _F32)], axis=1)
    convw = jnp.pad(conv_w.astype(_F32), ((0, 0), (0, CONV_PAD - CONV_WIDTH), (0, 0)))
    return w_in_p, w_up_p, conv_w_pw.astype(_BF), w_out.astype(_BF), vecs, convw


def kernel(x_prompt, x_sample, cache_k, cache_v, state_gla, state_conv, norm_gain, w_in, q_norm_gain, k_norm_gain, attn_sinks, gla_w_gate_up, gla_b_gate, gla_norm_gain, conv_w, conv_b, conv_ln_gain, conv_ln_bias, conv_w_pw, conv_b_pw, w_out):
    depth = w_in.shape[0]
    bp, lp, _ = x_prompt.shape
    bs, ls, _ = x_sample.shape
    params = _pack_params(norm_gain, w_in, q_norm_gain, k_norm_gain, gla_w_gate_up, gla_b_gate, gla_norm_gain,
                          conv_w, conv_b, conv_ln_gain, conv_ln_bias, conv_w_pw, conv_b_pw, w_out)
    sinks = jnp.pad(attn_sinks.astype(_F32), ((0, 0), (0, ATTN_HEADS)))
    yp, pk, pv, ps, pc = _run_stream(x_prompt.reshape(bp * lp, D_MODEL), bp, lp, None, sinks, params,
                                     carry=True, q_pos0=0)
    hist = (cache_k.reshape(depth, bs, WINDOW, KV_DIM), cache_v.reshape(depth, bs, WINDOW, KV_DIM),
            state_gla, state_conv)
    ys, sk, sv, ss, sc = _run_stream(x_sample.reshape(bs * ls, D_MODEL), bs, ls, hist, sinks, params,
                                     carry=False, q_pos0=PAST_LEN)
    kv_shape = (WINDOW, ATTN_KV_HEADS, HEAD_DIM)
    return (yp.reshape(bp, lp, D_MODEL), ys.reshape(bs, ls, D_MODEL),
            pk.reshape((depth, bp) + kv_shape), pv.reshape((depth, bp) + kv_shape), ps, pc,
            sk.reshape((depth, bs) + kv_shape), sv.reshape((depth, bs) + kv_shape), ss, sc)
```

```python
import functools
import math

import numpy as np
import jax
import jax.numpy as jnp
from jax import lax
from jax.experimental import pallas as pl
from jax.experimental.pallas import tpu as pltpu

D_MODEL = 1024
CHUNK = 64
ATTN_HEADS = 8
ATTN_KV_HEADS = 2
HEAD_DIM = 64
ATTN_GROUP = ATTN_HEADS // ATTN_KV_HEADS
ATTN_DIM = ATTN_HEADS * HEAD_DIM
KV_DIM = ATTN_KV_HEADS * HEAD_DIM
WINDOW = 128
BAND = WINDOW + CHUNK
KBAND = WINDOW + 2 * CHUNK
GLA_HEADS = 4
GLA_DK = 32
GLA_DV = 64
GLA_KDIM = GLA_HEADS * GLA_DK
GLA_VDIM = GLA_HEADS * GLA_DV
GLA_LOWRANK = 16
GLA_TAU = 16.0
CONV_DIM = 256
CONV_WIDTH = 31
CONV_HIST = CONV_WIDTH - 1
NORM_EPS = 1e-6
NEG_INF = -1e30
PAST_LEN = 4096
LOG2E = math.log2(math.e)

LANES = 128
SUBLANES = 8
CHUNK_SHIFT = CHUNK.bit_length() - 1
GLA_DK_SHIFT = GLA_DK.bit_length() - 1
GLA_DV_SHIFT = GLA_DV.bit_length() - 1
C_AQ, C_AK, C_AV, C_AG = 0, 512, 640, 768
C_GQ, C_GK, C_GV, C_GG = 1280, 1408, 1536, 1792
C_CV, C_CGL, C_CG, C_GLR = 2048, 2304, 2560, 2816
IN_COLS_PACKED = C_GLR + LANES
PROJ_GROUP = 256
CONV_PAD = 32
VEC_ROWS = 16
(V_NORM, V_QG, V_KG, V_GLAB, V_GLAG, V_CONVB, V_LNG, V_LNB, V_BPW) = range(9)

PROMPT_TILE = 256
SAMPLE_SEQS = 4
VMEM_LIMIT_BYTES = 48 * 1024 * 1024

_BF = jnp.bfloat16
_F32 = jnp.float32


def _shape_constants(tile):
    rows = np.arange(ATTN_GROUP * CHUNK)
    dist = np.abs((rows % CHUNK)[:, None] + WINDOW - np.arange(KBAND)[None, :]).astype(np.float64)
    bias = np.stack([
        (2.0 ** (-8.0 * (j * ATTN_GROUP + rows // CHUNK + 1) / ATTN_HEADS))[:, None] * LOG2E * dist
        for j in range(ATTN_KV_HEADS)]).astype(np.float32)
    bias[:, :, BAND:] = -NEG_INF
    variants = []
    for n in range(WINDOW // CHUNK + 1):
        b = bias.copy()
        b[:, :, :max(WINDOW - n * CHUNK, 0)] = -NEG_INF
        variants.append(b)
    bias = np.stack(variants)
    tr = np.arange(tile)
    tri = ((tr[:, None] // CHUNK == tr[None, :] // CHUNK) & (tr[None, :] <= tr[:, None])).astype(np.float32)
    ln = np.arange(LANES) // HEAD_DIM
    ones = (ln[:, None] == ln[None, :]).astype(np.float32)
    return jnp.asarray(bias), jnp.asarray(tri, dtype=_BF), jnp.asarray(ones, dtype=_BF)


def _group_rms_scale(x, group, ones_bd):
    outs = []
    for c0 in range(0, x.shape[1], LANES):
        blk = x[:, c0:c0 + LANES]
        ss = _dot((blk * blk).astype(_BF), ones_bd)
        outs.append(blk * lax.rsqrt(ss * (1.0 / group) + NORM_EPS))
    return outs[0] if len(outs) == 1 else jnp.concatenate(outs, axis=1)


def _silu(x):
    return x * jax.nn.sigmoid(x)


def _dot(a, b):
    return jnp.dot(a, b, preferred_element_type=_F32)


def _dot_nt(a, b):
    return lax.dot_general(a, b, (((1,), (1,)), ((), ())), preferred_element_type=_F32)


def _dot_tn(a, b):
    return lax.dot_general(a, b, (((0,), (0,)), ((), ())), preferred_element_type=_F32)


def _stream_kernel(*refs, tile, n_seq, n_chunk, carry, q_pos0):
    if carry:
        (sinks_ref, x_ref, xnext_ref, *rest) = refs
    else:
        (sinks_ref, x_ref, xnext_ref, hk_all, hv_all, hs_all, hc_all, *rest) = refs
    (w_in_all, w_up_all, w_pw_all, w_out_all, vec_all, convw_all, bias_ref, tri_ref, ones_ref,
     y_ref, nk_all, nv_all, ns_all, nc_all,
     kd_all, vd_all, s_all, u_all, qlo_ref, qhi_ref, ycat_ref, proj_ref, xcur_ref, hb_ref, sbd_ref) = rest
    t = pl.program_id(1)
    layer = pl.program_id(2)
    seq_rows = n_chunk * CHUNK
    w_in_ref, w_up_ref, w_pw_ref, w_out_ref = (r.at[layer] for r in (w_in_all, w_up_all, w_pw_all, w_out_all))
    vec_ref, convw_ref = vec_all.at[layer], convw_all.at[layer]
    kd_ref, vd_ref, s_ref, u_ref = (r.at[layer] for r in (kd_all, vd_all, s_all, u_all))
    nk_ref, nv_ref, ns_ref, nc_ref = (r.at[layer] for r in (nk_all, nv_all, ns_all, nc_all))
    if not carry:
        hk_ref, hv_ref, hs_ref, hc_ref = (r.at[layer] for r in (hk_all, hv_all, hs_all, hc_all))

    lane128 = lax.broadcasted_iota(jnp.int32, (1, LANES), 1)
    lo_half = lane128 < HEAD_DIM

    def dup_halves(a):
        sw = pltpu.roll(a, HEAD_DIM, 1)
        return jnp.concatenate([jnp.where(lo_half, a, sw), jnp.where(lo_half, sw, a)], axis=1)

    buf_rows = WINDOW + seq_rows + CHUNK
    zeros_blk = jnp.zeros((buf_rows, LANES), _BF)
    ones_blk = jnp.ones((buf_rows, LANES), _BF)
    v_blank = jnp.concatenate([zeros_blk, ones_blk] * ATTN_KV_HEADS, axis=1)

    def store_v(s, row0, vdup_rows):
        for j in range(ATTN_KV_HEADS):
            vd_ref[s, row0:row0 + vdup_rows.shape[0], 2 * j * LANES:(2 * j + 1) * LANES] = (
                vdup_rows[:, j * LANES:(j + 1) * LANES])

    if carry:
        @pl.when(t == 0)
        def _():
            kd_ref[...] = jnp.zeros(kd_ref.shape, _BF)
            for s in range(n_seq):
                vd_ref[s] = v_blank
            s_ref[...] = jnp.zeros(s_ref.shape, _F32)
            u_ref[:, 0:CONV_PAD, :] = jnp.zeros((n_seq, CONV_PAD, CONV_DIM), _F32)
    else:
        for s in range(n_seq):
            kd_ref[s, 0:WINDOW, :] = dup_halves(hk_ref[s]).astype(_BF)
            kd_ref[s, WINDOW + seq_rows:, :] = jnp.zeros((CHUNK, 2 * LANES), _BF)
            vd_ref[s] = v_blank
            store_v(s, 0, dup_halves(hv_ref[s]).astype(_BF))
            sbd_ref[...] = jnp.zeros((GLA_KDIM, GLA_VDIM), _F32)
            for h in range(GLA_HEADS):
                sbd_ref[h * GLA_DK:(h + 1) * GLA_DK, h * GLA_DV:(h + 1) * GLA_DV] = hs_ref[s, h]
            s_ref[s] = sbd_ref[...].T
            u_ref[s, 0:SUBLANES, :] = jnp.zeros((SUBLANES, CONV_DIM), _F32)
            u_ref[s, CONV_PAD - CONV_HIST:CONV_PAD, :] = hc_ref[s]

    def normed_bf16(v, gain_row):
        ms = jnp.mean(v * v, axis=-1, keepdims=True)
        return (v * lax.rsqrt(ms + NORM_EPS) * gain_row).astype(_BF)

    @pl.when(layer == 0)
    def _():
        xcur_ref[...] = x_ref[...]

    @pl.when((pl.program_id(0) == 0) & (t == 0) & (layer == 0))
    def _():
        hb_ref[...] = normed_bf16(x_ref[...], vec_ref[V_NORM:V_NORM + 1, :])
    x = xcur_ref[...]
    hb = hb_ref[...]

    anchor_on = sinks_ref[layer, ATTN_HEADS] != 0.0

    def compute_group(g):
        c0, c1 = g * PROJ_GROUP, min((g + 1) * PROJ_GROUP, IN_COLS_PACKED)
        res = _dot(hb, w_in_ref[:, c0:c1])
        proj_ref[:, c0:c1] = res
        return jnp.where(anchor_on, res[0:1, 0:CONV_DIM], 0.0)

    def proj(c0, width):
        return proj_ref[:, c0:c0 + width]

    compute_group(C_CV // PROJ_GROUP)
    compute_group(C_CGL // PROJ_GROUP)
    u = proj(C_CV, CONV_DIM) * jax.nn.sigmoid(proj(C_CGL, CONV_DIM))
    for s in range(n_seq):
        u_ref[s, CONV_PAD:CONV_PAD + seq_rows, :] = u[s * seq_rows:(s + 1) * seq_rows]
    conv_acc = [jnp.broadcast_to(vec_ref[V_CONVB:V_CONVB + 1, 0:CONV_DIM], (seq_rows, CONV_DIM))] * n_seq
    for rho in range(SUBLANES):
        anchor = compute_group(rho)
        for s in range(n_seq):
            frame_rows = seq_rows + (SUBLANES if rho else 0)
            frame = None
            for j in range(CONV_WIDTH):
                off = CONV_PAD - CONV_HIST + j
                if off % SUBLANES != rho:
                    continue
                term = (convw_ref[j:j + 1, :] + anchor) * u_ref[s, off - rho:off - rho + frame_rows, :]
                frame = term if frame is None else frame + term
            conv_acc[s] = conv_acc[s] + frame[rho:rho + seq_rows]
    compute_group(C_CG // PROJ_GROUP)
    compute_group(C_GLR // PROJ_GROUP)
    cc = conv_acc[0] if n_seq == 1 else jnp.concatenate(conv_acc, axis=0)

    for s in range(n_seq):
        nc_ref[s] = u_ref[s, seq_rows + CONV_PAD - CONV_HIST:seq_rows + CONV_PAD, :]
    if carry:
        u_ref[:, 0:CONV_PAD, :] = u_ref[:, seq_rows:seq_rows + CONV_PAD, :]

    mu = jnp.mean(cc, axis=-1, keepdims=True)
    cen = cc - mu
    var = jnp.mean(cen * cen, axis=-1, keepdims=True)
    ln = cen * lax.rsqrt(var + NORM_EPS) * vec_ref[V_LNG:V_LNG + 1, 0:CONV_DIM] + vec_ref[V_LNB:V_LNB + 1, 0:CONV_DIM]
    cpw = _dot(_silu(ln).astype(_BF), w_pw_ref[...]) + vec_ref[V_BPW:V_BPW + 1, 0:CONV_DIM]
    ycat_ref[:, ATTN_DIM + GLA_VDIM:D_MODEL] = cpw * _silu(proj(C_CG, CONV_DIM))

    ones64 = ones_ref[...]
    qn = _group_rms_scale(proj(C_AQ, ATTN_DIM), HEAD_DIM, ones64) * (
        vec_ref[V_QG:V_QG + 1, 0:ATTN_DIM] * (LOG2E * HEAD_DIM ** -0.5))
    lo4 = jnp.concatenate([lo_half] * (ATTN_DIM // LANES), axis=1)
    qlo_ref[...] = jnp.where(lo4, qn, 0.0).astype(_BF)
    qhi_ref[...] = jnp.where(lo4, 0.0, qn).astype(_BF)
    kn = _group_rms_scale(proj(C_AK, KV_DIM), HEAD_DIM, ones64) * vec_ref[V_KG:V_KG + 1, 0:KV_DIM]
    vv = proj(C_AV, KV_DIM)
    kdup = dup_halves(kn).astype(_BF)
    vdup = dup_halves(vv).astype(_BF)
    for s in range(n_seq):
        kd_ref[s, WINDOW:WINDOW + seq_rows, :] = kdup[s * seq_rows:(s + 1) * seq_rows]
        store_v(s, WINDOW, vdup[s * seq_rows:(s + 1) * seq_rows])
    if carry:
        nk_ref[0] = kn[tile - WINDOW:tile]
        nv_ref[0] = vv[tile - WINDOW:tile]
    else:
        for s in range(n_seq):
            nk_ref[s, 0:WINDOW - seq_rows, :] = hk_ref[s, seq_rows:WINDOW, :]
            nv_ref[s, 0:WINDOW - seq_rows, :] = hv_ref[s, seq_rows:WINDOW, :]
            nk_ref[s, WINDOW - seq_rows:WINDOW, :] = kn[s * seq_rows:(s + 1) * seq_rows]
            nv_ref[s, WINDOW - seq_rows:WINDOW, :] = vv[s * seq_rows:(s + 1) * seq_rows]

    gate_a = _silu(proj(C_AG, ATTN_DIM))

    rows = ATTN_GROUP * CHUNK
    r_head1 = lax.broadcasted_iota(jnp.int32, (rows, 1), 0) >> CHUNK_SHIFT
    sink_cols = []
    for j in range(ATTN_KV_HEADS):
        sink = jnp.zeros((rows, 1), _F32)
        for r in range(ATTN_GROUP):
            sink = jnp.where(r_head1 == r, sinks_ref[layer, j * ATTN_GROUP + r], sink)
        sink_cols.append(sink * LOG2E)

    for s in range(n_seq):
        for c in range(n_chunk):
            r0 = s * seq_rows + c * CHUNK
            k0 = c * CHUNK
            steady = WINDOW // CHUNK
            variant = jnp.minimum(t * n_chunk + c, steady) if carry else steady
            for j in range(ATTN_KV_HEADS):
                qs = jnp.concatenate(
                    [qlo_ref[r0:r0 + CHUNK, (2 * j) * LANES:(2 * j + 1) * LANES],
                     qhi_ref[r0:r0 + CHUNK, (2 * j) * LANES:(2 * j + 1) * LANES],
                     qlo_ref[r0:r0 + CHUNK, (2 * j + 1) * LANES:(2 * j + 2) * LANES],
                     qhi_ref[r0:r0 + CHUNK, (2 * j + 1) * LANES:(2 * j + 2) * LANES]], axis=0)
                kb = kd_ref[s, k0:k0 + KBAND, j * LANES:(j + 1) * LANES]
                vb = vd_ref[s, k0:k0 + KBAND, 2 * j * LANES:(2 * j + 2) * LANES]
                sc = _dot_nt(qs, kb) - bias_ref[variant, j]
                sink = sink_cols[j]
                m = jnp.maximum(jnp.max(sc, axis=-1, keepdims=True), sink)
                p = jnp.exp2(sc - m)
                pv = _dot(p.astype(_BF), vb)
                o = pv[:, 0:LANES] / (pv[:, LANES:2 * LANES] + jnp.exp2(sink - m))
                for pb in range(2):
                    blk = 2 * j + pb
                    ob = jnp.where(lo_half, o[(2 * pb) * CHUNK:(2 * pb + 1) * CHUNK],
                                   o[(2 * pb + 1) * CHUNK:(2 * pb + 2) * CHUNK])
                    ycat_ref[r0:r0 + CHUNK, blk * LANES:(blk + 1) * LANES] = (
                        ob * gate_a[r0:r0 + CHUNK, blk * LANES:(blk + 1) * LANES])

    if carry:
        kd_ref[:, 0:WINDOW, :] = kd_ref[:, seq_rows:seq_rows + WINDOW, :]
        vd_ref[:, 0:WINDOW, :] = vd_ref[:, seq_rows:seq_rows + WINDOW, :]

    z = _dot(proj(C_GLR, LANES).astype(_BF), w_up_ref[...]) + vec_ref[V_GLAB:V_GLAB + 1, 0:GLA_KDIM]
    log_a = (jnp.minimum(z, 0.0) - jnp.log(1.0 + jnp.exp(-jnp.abs(z)))) * (1.0 / GLA_TAU)
    tri = tri_ref[...]
    la_hi = log_a.astype(_BF)
    la_lo = (log_a - la_hi.astype(_F32)).astype(_BF)
    bcum = _dot(tri, la_hi) + _dot(tri, la_lo)
    gq = proj(C_GQ, GLA_KDIM) * GLA_DK ** -0.5
    gk = proj(C_GK, GLA_KDIM)
    gv = proj(C_GV, GLA_VDIM)
    gate_g = _silu(proj(C_GG, GLA_VDIM))
    gla_g = jnp.concatenate([vec_ref[V_GLAG:V_GLAG + 1, 0:GLA_DV]] * GLA_HEADS, axis=1)

    head_of_lane = lane128 >> GLA_DK_SHIFT
    a_r = lax.broadcasted_iota(jnp.int32, (CHUNK, GLA_HEADS * CHUNK), 0)
    a_c = lax.broadcasted_iota(jnp.int32, (CHUNK, GLA_HEADS * CHUNK), 1)
    causal = (a_c & (CHUNK - 1)) <= a_r
    vblk = lax.broadcasted_iota(jnp.int32, (1, GLA_VDIM), 1) >> GLA_DV_SHIFT
    bd_mask_t = ((lax.broadcasted_iota(jnp.int32, (GLA_VDIM, GLA_KDIM), 0) >> GLA_DV_SHIFT)
                 == (lax.broadcasted_iota(jnp.int32, (GLA_VDIM, GLA_KDIM), 1) >> GLA_DK_SHIFT))

    q_all = (gq * jnp.exp(bcum)).astype(_BF)
    k_all = gk * jnp.exp(-bcum)
    intra, kv_upd, decay_rows = [], [], []
    for s in range(n_seq):
        for c in range(n_chunk):
            r0 = s * seq_rows + c * CHUNK
            bc = bcum[r0:r0 + CHUNK]
            b_last = bc[CHUNK - 1:CHUNK]
            k_end = (gk[r0:r0 + CHUNK] * jnp.exp(b_last - bc)).astype(_BF)
            vf = gv[r0:r0 + CHUNK]
            k_stack = jnp.concatenate(
                [jnp.where(head_of_lane == h, k_all[r0:r0 + CHUNK], 0.0) for h in range(GLA_HEADS)],
                axis=0).astype(_BF)
            v_bd = jnp.concatenate(
                [jnp.where(vblk == h, vf, 0.0) for h in range(GLA_HEADS)], axis=0).astype(_BF)
            a = jnp.where(causal, _dot_nt(q_all[r0:r0 + CHUNK], k_stack), 0.0).astype(_BF)
            intra.append(_dot(a, v_bd))
            kv_upd.append(jnp.where(bd_mask_t, _dot_tn(vf.astype(_BF), k_end), 0.0))
            decay_rows.append(jnp.exp(b_last))
    o_rows = []
    for s in range(n_seq):
        state = s_ref[s]
        for c in range(n_chunk):
            i = s * n_chunk + c
            r0 = i * CHUNK
            o_rows.append(intra[i] + _dot_nt(q_all[r0:r0 + CHUNK], state.astype(_BF)))
            state = decay_rows[i] * state + kv_upd[i]
        s_ref[s] = state
    o_all = jnp.concatenate(o_rows, axis=0)
    ycat_ref[:, ATTN_DIM:ATTN_DIM + GLA_VDIM] = _group_rms_scale(o_all, GLA_DV, ones64) * gla_g * gate_g

    for s in range(n_seq):
        sbd_ref[...] = s_ref[s].T
        for h in range(GLA_HEADS):
            ns_ref[s, h] = sbd_ref[h * GLA_DK:(h + 1) * GLA_DK, h * GLA_DV:(h + 1) * GLA_DV]

    y = x
    for c0, c1 in ((ATTN_DIM + GLA_VDIM, D_MODEL), (0, ATTN_DIM), (ATTN_DIM, ATTN_DIM + GLA_VDIM)):
        y = y + _dot(ycat_ref[:, c0:c1].astype(_BF), w_out_ref[c0:c1, :])
    y_ref[...] = y
    xcur_ref[...] = y
    last_layer = layer == pl.num_programs(2) - 1
    nxt = jnp.where(last_layer, xnext_ref[...], y)
    next_gain = vec_all[jnp.where(last_layer, 0, layer + 1), V_NORM:V_NORM + 1, :]
    hb_ref[...] = normed_bf16(nxt, next_gain)


def _run_stream(x2d, n_seqs, seq_len, hist, sinks, params, *, carry, q_pos0):
    depth = sinks.shape[0]
    if carry:
        assert q_pos0 == 0
        tile, n_seq, n_chunk = PROMPT_TILE, 1, PROMPT_TILE // CHUNK
        grid = (n_seqs, seq_len // tile, depth)
    else:
        assert seq_len == CHUNK and q_pos0 >= WINDOW
        tile, n_seq, n_chunk = SAMPLE_SEQS * CHUNK, SAMPLE_SEQS, 1
        grid = (n_seqs // n_seq, 1, depth)
    n_t = grid[1]
    seq_rows = n_chunk * CHUNK

    def const(shape):
        return pl.BlockSpec(shape, lambda g, t, l: (0,) * len(shape))

    def per_layer_group(shape):
        return pl.BlockSpec((depth, n_seq) + shape, lambda g, t, l: (0, g) + (0,) * len(shape))

    state_shapes = ((WINDOW, KV_DIM), (WINDOW, KV_DIM), (GLA_HEADS, GLA_DK, GLA_DV), (CONV_HIST, CONV_DIM))
    x_spec = pl.BlockSpec((tile, D_MODEL), lambda g, t, l: (g * n_t + t, 0))
    n_tiles = x2d.shape[0] // tile
    xnext_spec = pl.BlockSpec((tile, D_MODEL), lambda g, t, l: (jnp.minimum(g * n_t + t + 1, n_tiles - 1), 0))
    in_specs = [pl.BlockSpec(memory_space=pltpu.SMEM), x_spec, xnext_spec]
    args = [sinks, x2d, x2d]
    if not carry:
        in_specs += [per_layer_group(sh) for sh in state_shapes]
        args += list(hist)
    operands = tuple(params) + _shape_constants(tile)
    in_specs += [const(p.shape) for p in operands]
    args += list(operands)

    out_shape = (jax.ShapeDtypeStruct(x2d.shape, _F32),) + tuple(
        jax.ShapeDtypeStruct((depth, n_seqs) + sh, _F32) for sh in state_shapes)
    out_specs = (x_spec,) + tuple(per_layer_group(sh) for sh in state_shapes)
    scratch = [pltpu.VMEM((depth, n_seq, WINDOW + seq_rows + CHUNK, ATTN_KV_HEADS * LANES), _BF),
               pltpu.VMEM((depth, n_seq, WINDOW + seq_rows + CHUNK, ATTN_KV_HEADS * 2 * LANES), _BF),
               pltpu.VMEM((depth, n_seq, GLA_VDIM, GLA_KDIM), _F32),
               pltpu.VMEM((depth, n_seq, CONV_PAD + seq_rows, CONV_DIM), _F32),
               pltpu.VMEM((tile, ATTN_DIM), _BF),
               pltpu.VMEM((tile, ATTN_DIM), _BF),
               pltpu.VMEM((tile, D_MODEL), _F32),
               pltpu.VMEM((tile, IN_COLS_PACKED), _F32),
               pltpu.VMEM((tile, D_MODEL), _F32),
               pltpu.VMEM((tile, D_MODEL), _BF),
               pltpu.VMEM((GLA_KDIM, GLA_VDIM), _F32)]
    body = functools.partial(_stream_kernel, tile=tile, n_seq=n_seq, n_chunk=n_chunk, carry=carry, q_pos0=q_pos0)
    return pl.pallas_call(
        body, grid=grid, in_specs=in_specs, out_specs=out_specs, out_shape=out_shape,
        scratch_shapes=scratch,
        compiler_params=pltpu.CompilerParams(dimension_semantics=("arbitrary", "arbitrary", "arbitrary"),
                                             vmem_limit_bytes=VMEM_LIMIT_BYTES),
        name="mixer_prompt" if carry else "mixer_sample",
    )(*args)


def _pack_params(norm_gain, w_in, q_norm_gain, k_norm_gain, gla_w_gate_up, gla_b_gate, gla_norm_gain,
                 conv_w, conv_b, conv_ln_gain, conv_ln_bias, conv_w_pw, conv_b_pw, w_out):
    depth = w_in.shape[0]
    glr0 = 2048
    w_in_p = jnp.concatenate(
        [w_in[:, :, :glr0], w_in[:, :, glr0 + GLA_LOWRANK:], w_in[:, :, glr0:glr0 + GLA_LOWRANK],
         jnp.zeros((depth, D_MODEL, LANES - GLA_LOWRANK), w_in.dtype)], axis=2).astype(_BF)
    w_up_p = jnp.pad(gla_w_gate_up, ((0, 0), (0, LANES - GLA_LOWRANK), (0, 0))).astype(_BF)

    def row(v, reps=1):
        v = jnp.tile(v.astype(_F32), (1, reps))
        return jnp.pad(v, ((0, 0), (0, D_MODEL - v.shape[1])))[:, None, :]

    vec_rows = [row(norm_gain), row(q_norm_gain, ATTN_HEADS), row(k_norm_gain, ATTN_KV_HEADS),
                row(gla_b_gate), row(gla_norm_gain), row(conv_b), row(conv_ln_gain),
                row(conv_ln_bias), row(conv_b_pw)]
    vecs = jnp.concatenate(vec_rows + [jnp.zeros((depth, VEC_ROWS - len(vec_rows), D_MODEL), _F32)], axis=1)
    convw = jnp.pad(conv_w.astype(_F32), ((0, 0), (0, CONV_PAD - CONV_WIDTH), (0, 0)))
    return w_in_p, w_up_p, conv_w_pw.astype(_BF), w_out.astype(_BF), vecs, convw


def kernel(x_prompt, x_sample, cache_k, cache_v, state_gla, state_conv, norm_gain, w_in, q_norm_gain, k_norm_gain, attn_sinks, gla_w_gate_up, gla_b_gate, gla_norm_gain, conv_w, conv_b, conv_ln_gain, conv_ln_bias, conv_w_pw, conv_b_pw, w_out):
    depth = w_in.shape[0]
    bp, lp, _ = x_prompt.shape
    bs, ls, _ = x_sample.shape
    params = _pack_params(norm_gain, w_in, q_norm_gain, k_norm_gain, gla_w_gate_up, gla_b_gate, gla_norm_gain,
                          conv_w, conv_b, conv_ln_gain, conv_ln_bias, conv_w_pw, conv_b_pw, w_out)
    sinks = jnp.pad(attn_sinks.astype(_F32), ((0, 0), (0, ATTN_HEADS)))
    yp, pk, pv, ps, pc = _run_stream(x_prompt.reshape(bp * lp, D_MODEL), bp, lp, None, sinks, params,
                                     carry=True, q_pos0=0)
    hist = (cache_k.reshape(depth, bs, WINDOW, KV_DIM), cache_v.reshape(depth, bs, WINDOW, KV_DIM),
            state_gla, state_conv)
    ys, sk, sv, ss, sc = _run_stream(x_sample.reshape(bs * ls, D_MODEL), bs, ls, hist, sinks, params,
                                     carry=False, q_pos0=PAST_LEN)
    kv_shape = (WINDOW, ATTN_KV_HEADS, HEAD_DIM)
    return (yp.reshape(bp, lp, D_MODEL), ys.reshape(bs, ls, D_MODEL),
            pk.reshape((depth, bp) + kv_shape), pv.reshape((depth, bp) + kv_shape), ps, pc,
            sk.reshape((depth, bs) + kv_shape), sv.reshape((depth, bs) + kv_shape), ss, sc)
```

```python
import functools
import math

import numpy as np
import jax
import jax.numpy as jnp
from jax import lax
from jax.experimental import pallas as pl
from jax.experimental.pallas import tpu as pltpu

D_MODEL = 1024
CHUNK = 64
ATTN_HEADS = 8
ATTN_KV_HEADS = 2
HEAD_DIM = 64
ATTN_GROUP = ATTN_HEADS // ATTN_KV_HEADS
ATTN_DIM = ATTN_HEADS * HEAD_DIM
KV_DIM = ATTN_KV_HEADS * HEAD_DIM
WINDOW = 128
BAND = WINDOW + CHUNK
KBAND = WINDOW + 2 * CHUNK
GLA_HEADS = 4
GLA_DK = 32
GLA_DV = 64
GLA_KDIM = GLA_HEADS * GLA_DK
GLA_VDIM = GLA_HEADS * GLA_DV
GLA_LOWRANK = 16
GLA_TAU = 16.0
CONV_DIM = 256
CONV_WIDTH = 31
CONV_HIST = CONV_WIDTH - 1
NORM_EPS = 1e-6
NEG_INF = -1e30
PAST_LEN = 4096
LOG2E = math.log2(math.e)

LANES = 128
SUBLANES = 8
CHUNK_SHIFT = CHUNK.bit_length() - 1
GLA_DK_SHIFT = GLA_DK.bit_length() - 1
GLA_DV_SHIFT = GLA_DV.bit_length() - 1
C_AQ, C_AK, C_AV, C_AG = 0, 512, 640, 768
C_GQ, C_GK, C_GV, C_GG = 1280, 1408, 1536, 1792
C_CV, C_CGL, C_CG, C_GLR = 2048, 2304, 2560, 2816
IN_COLS_PACKED = C_GLR + LANES
PROJ_GROUP = 256
CONV_PAD = 32
VEC_ROWS = 16
(V_NORM, V_QG, V_KG, V_GLAB, V_GLAG, V_CONVB, V_LNG, V_LNB, V_BPW) = range(9)

PROMPT_TILE = 512
SAMPLE_SEQS = 4
VMEM_LIMIT_BYTES = 56 * 1024 * 1024

_BF = jnp.bfloat16
_F32 = jnp.float32


def _shape_constants(tile):
    rows = np.arange(ATTN_GROUP * CHUNK)
    dist = np.abs((rows % CHUNK)[:, None] + WINDOW - np.arange(KBAND)[None, :]).astype(np.float64)
    bias = np.stack([
        (2.0 ** (-8.0 * (j * ATTN_GROUP + rows // CHUNK + 1) / ATTN_HEADS))[:, None] * LOG2E * dist
        for j in range(ATTN_KV_HEADS)]).astype(np.float32)
    bias[:, :, BAND:] = -NEG_INF
    variants = []
    for n in range(WINDOW // CHUNK + 1):
        b = bias.copy()
        b[:, :, :max(WINDOW - n * CHUNK, 0)] = -NEG_INF
        variants.append(b)
    bias = np.stack(variants)
    tr = np.arange(tile)
    tri = ((tr[:, None] // CHUNK == tr[None, :] // CHUNK) & (tr[None, :] <= tr[:, None])).astype(np.float32)
    ln = np.arange(LANES) // HEAD_DIM
    ones = (ln[:, None] == ln[None, :]).astype(np.float32)
    return jnp.asarray(bias), jnp.asarray(tri, dtype=_BF), jnp.asarray(ones, dtype=_BF)


def _group_rms_scale(x, group, ones_bd):
    outs = []
    for c0 in range(0, x.shape[1], LANES):
        blk = x[:, c0:c0 + LANES]
        ss = _dot((blk * blk).astype(_BF), ones_bd)
        outs.append(blk * lax.rsqrt(ss * (1.0 / group) + NORM_EPS))
    return outs[0] if len(outs) == 1 else jnp.concatenate(outs, axis=1)


def _silu(x):
    return x * jax.nn.sigmoid(x)


def _dot(a, b):
    return jnp.dot(a, b, preferred_element_type=_F32)


def _dot_nt(a, b):
    return lax.dot_general(a, b, (((1,), (1,)), ((), ())), preferred_element_type=_F32)


def _dot_tn(a, b):
    return lax.dot_general(a, b, (((0,), (0,)), ((), ())), preferred_element_type=_F32)


def _stream_kernel(*refs, tile, n_seq, n_chunk, carry, q_pos0):
    if carry:
        (sinks_ref, x_ref, xnext_ref, *rest) = refs
    else:
        (sinks_ref, x_ref, xnext_ref, hk_all, hv_all, hs_all, hc_all, *rest) = refs
    (w_in_all, w_up_all, w_pw_all, w_out_all, vec_all, convw_all, bias_ref, tri_ref, ones_ref,
     y_ref, nk_all, nv_all, ns_all, nc_all,
     kd_all, vd_all, s_all, u_all, qlo_ref, qhi_ref, ycat_ref, proj_ref, xcur_ref, hb_ref, sbd_ref,
     gate_ref) = rest
    t = pl.program_id(1)
    layer = pl.program_id(2)
    seq_rows = n_chunk * CHUNK
    w_in_ref, w_up_ref, w_pw_ref, w_out_ref = (r.at[layer] for r in (w_in_all, w_up_all, w_pw_all, w_out_all))
    vec_ref, convw_ref = vec_all.at[layer], convw_all.at[layer]
    kd_ref, vd_ref, s_ref, u_ref = (r.at[layer] for r in (kd_all, vd_all, s_all, u_all))
    nk_ref, nv_ref, ns_ref, nc_ref = (r.at[layer] for r in (nk_all, nv_all, ns_all, nc_all))
    if not carry:
        hk_ref, hv_ref, hs_ref, hc_ref = (r.at[layer] for r in (hk_all, hv_all, hs_all, hc_all))

    lane128 = lax.broadcasted_iota(jnp.int32, (1, LANES), 1)
    lo_half = lane128 < HEAD_DIM

    def dup_halves(a):
        sw = pltpu.roll(a, HEAD_DIM, 1)
        return jnp.concatenate([jnp.where(lo_half, a, sw), jnp.where(lo_half, sw, a)], axis=1)

    buf_rows = WINDOW + seq_rows + CHUNK
    zeros_blk = jnp.zeros((buf_rows, LANES), _BF)
    ones_blk = jnp.ones((buf_rows, LANES), _BF)
    v_blank = jnp.concatenate([zeros_blk, ones_blk] * ATTN_KV_HEADS, axis=1)

    def store_v(s, row0, vdup_rows):
        for j in range(ATTN_KV_HEADS):
            vd_ref[s, row0:row0 + vdup_rows.shape[0], 2 * j * LANES:(2 * j + 1) * LANES] = (
                vdup_rows[:, j * LANES:(j + 1) * LANES])

    if carry:
        @pl.when(t == 0)
        def _():
            kd_ref[...] = jnp.zeros(kd_ref.shape, _BF)
            for s in range(n_seq):
                vd_ref[s] = v_blank
            s_ref[...] = jnp.zeros(s_ref.shape, _F32)
            u_ref[:, 0:CONV_PAD, :] = jnp.zeros((n_seq, CONV_PAD, CONV_DIM), _F32)
    else:
        for s in range(n_seq):
            kd_ref[s, 0:WINDOW, :] = dup_halves(hk_ref[s]).astype(_BF)
            kd_ref[s, WINDOW + seq_rows:, :] = jnp.zeros((CHUNK, 2 * LANES), _BF)
            vd_ref[s] = v_blank
            store_v(s, 0, dup_halves(hv_ref[s]).astype(_BF))
            sbd_ref[...] = jnp.zeros((GLA_KDIM, GLA_VDIM), _F32)
            for h in range(GLA_HEADS):
                sbd_ref[h * GLA_DK:(h + 1) * GLA_DK, h * GLA_DV:(h + 1) * GLA_DV] = hs_ref[s, h]
            s_ref[s] = sbd_ref[...].T
            u_ref[s, 0:SUBLANES, :] = jnp.zeros((SUBLANES, CONV_DIM), _F32)
            u_ref[s, CONV_PAD - CONV_HIST:CONV_PAD, :] = hc_ref[s]

    def normed_bf16(v, gain_row):
        ms = jnp.mean(v * v, axis=-1, keepdims=True)
        return (v * lax.rsqrt(ms + NORM_EPS) * gain_row).astype(_BF)

    @pl.when(layer == 0)
    def _():
        xcur_ref[...] = x_ref[...]

    @pl.when((pl.program_id(0) == 0) & (t == 0) & (layer == 0))
    def _():
        hb_ref[...] = normed_bf16(x_ref[...], vec_ref[V_NORM:V_NORM + 1, :])
    x = xcur_ref[...]
    hb = hb_ref[...]

    anchor_on = sinks_ref[layer, ATTN_HEADS] != 0.0

    ones64 = ones_ref[...]

    def post_q(g):
        def post(res):
            for b in range(PROJ_GROUP // LANES):
                c0 = g * PROJ_GROUP + b * LANES
                qn = _group_rms_scale(res[:, b * LANES:(b + 1) * LANES], HEAD_DIM, ones64) * (
                    vec_ref[V_QG:V_QG + 1, c0:c0 + LANES] * (LOG2E * HEAD_DIM ** -0.5))
                qlo_ref[:, c0:c0 + LANES] = jnp.where(lo_half, qn, 0.0).astype(_BF)
                qhi_ref[:, c0:c0 + LANES] = jnp.where(lo_half, 0.0, qn).astype(_BF)
        return post

    def post_kv(res):
        kn = _group_rms_scale(res[:, 0:KV_DIM], HEAD_DIM, ones64) * vec_ref[V_KG:V_KG + 1, 0:KV_DIM]
        vv = res[:, KV_DIM:2 * KV_DIM]
        kdup = dup_halves(kn).astype(_BF)
        vdup = dup_halves(vv).astype(_BF)
        for s in range(n_seq):
            kd_ref[s, WINDOW:WINDOW + seq_rows, :] = kdup[s * seq_rows:(s + 1) * seq_rows]
            store_v(s, WINDOW, vdup[s * seq_rows:(s + 1) * seq_rows])
        if carry:
            nk_ref[0] = kn[tile - WINDOW:tile]
            nv_ref[0] = vv[tile - WINDOW:tile]
        else:
            for s in range(n_seq):
                nk_ref[s, 0:WINDOW - seq_rows, :] = hk_ref[s, seq_rows:WINDOW, :]
                nv_ref[s, 0:WINDOW - seq_rows, :] = hv_ref[s, seq_rows:WINDOW, :]
                nk_ref[s, WINDOW - seq_rows:WINDOW, :] = kn[s * seq_rows:(s + 1) * seq_rows]
                nv_ref[s, WINDOW - seq_rows:WINDOW, :] = vv[s * seq_rows:(s + 1) * seq_rows]

    def post_gate(col0):
        def post(res):
            gate_ref[:, col0:col0 + PROJ_GROUP] = _silu(res)
        return post

    group_post = {C_AG // PROJ_GROUP: post_gate(0), C_AG // PROJ_GROUP + 1: post_gate(PROJ_GROUP),
                  C_GG // PROJ_GROUP: post_gate(ATTN_DIM), C_CG // PROJ_GROUP: post_gate(ATTN_DIM + GLA_VDIM)}

    def compute_group(g):
        c0, c1 = g * PROJ_GROUP, min((g + 1) * PROJ_GROUP, IN_COLS_PACKED)
        res = _dot(hb, w_in_ref[:, c0:c1])
        if g in group_post:
            group_post[g](res)
        else:
            proj_ref[:, c0:c1] = res
        return jnp.where(anchor_on, res[0:1, 0:CONV_DIM], 0.0)

    def proj(c0, width):
        return proj_ref[:, c0:c0 + width]

    compute_group(C_CV // PROJ_GROUP)
    compute_group(C_CGL // PROJ_GROUP)
    u = proj(C_CV, CONV_DIM) * jax.nn.sigmoid(proj(C_CGL, CONV_DIM))
    for s in range(n_seq):
        u_ref[s, CONV_PAD:CONV_PAD + seq_rows, :] = u[s * seq_rows:(s + 1) * seq_rows]
    conv_acc = [jnp.broadcast_to(vec_ref[V_CONVB:V_CONVB + 1, 0:CONV_DIM], (seq_rows, CONV_DIM))] * n_seq
    for rho in range(SUBLANES):
        anchor = compute_group(rho)
        for s in range(n_seq):
            frame_rows = seq_rows + (SUBLANES if rho else 0)
            frame = None
            for j in range(CONV_WIDTH):
                off = CONV_PAD - CONV_HIST + j
                if off % SUBLANES != rho:
                    continue
                term = (convw_ref[j:j + 1, :] + anchor) * u_ref[s, off - rho:off - rho + frame_rows, :]
                frame = term if frame is None else frame + term
            conv_acc[s] = conv_acc[s] + frame[rho:rho + seq_rows]
    compute_group(C_CG // PROJ_GROUP)
    compute_group(C_GLR // PROJ_GROUP)
    cc = conv_acc[0] if n_seq == 1 else jnp.concatenate(conv_acc, axis=0)

    for s in range(n_seq):
        nc_ref[s] = u_ref[s, seq_rows + CONV_PAD - CONV_HIST:seq_rows + CONV_PAD, :]
    if carry:
        u_ref[:, 0:CONV_PAD, :] = u_ref[:, seq_rows:seq_rows + CONV_PAD, :]

    mu = jnp.mean(cc, axis=-1, keepdims=True)
    cen = cc - mu
    var = jnp.mean(cen * cen, axis=-1, keepdims=True)
    ln = cen * lax.rsqrt(var + NORM_EPS) * vec_ref[V_LNG:V_LNG + 1, 0:CONV_DIM] + vec_ref[V_LNB:V_LNB + 1, 0:CONV_DIM]
    cpw = _dot(_silu(ln).astype(_BF), w_pw_ref[...]) + vec_ref[V_BPW:V_BPW + 1, 0:CONV_DIM]
    ycat_ref[:, ATTN_DIM + GLA_VDIM:D_MODEL] = cpw * gate_ref[:, ATTN_DIM + GLA_VDIM:D_MODEL]

    for g in range(ATTN_DIM // PROJ_GROUP):
        post_q(g)(proj(C_AQ + g * PROJ_GROUP, PROJ_GROUP))
    post_kv(proj(C_AK, 2 * KV_DIM))

    rows = ATTN_GROUP * CHUNK
    r_head1 = lax.broadcasted_iota(jnp.int32, (rows, 1), 0) >> CHUNK_SHIFT
    sink_cols = []
    for j in range(ATTN_KV_HEADS):
        sink = jnp.zeros((rows, 1), _F32)
        for r in range(ATTN_GROUP):
            sink = jnp.where(r_head1 == r, sinks_ref[layer, j * ATTN_GROUP + r], sink)
        sink_cols.append(sink * LOG2E)

    for s in range(n_seq):
        for c in range(n_chunk):
            r0 = s * seq_rows + c * CHUNK
            k0 = c * CHUNK
            steady = WINDOW // CHUNK
            variant = jnp.minimum(t * n_chunk + c, steady) if carry else steady
            for j in range(ATTN_KV_HEADS):
                qs = jnp.concatenate(
                    [qlo_ref[r0:r0 + CHUNK, (2 * j) * LANES:(2 * j + 1) * LANES],
                     qhi_ref[r0:r0 + CHUNK, (2 * j) * LANES:(2 * j + 1) * LANES],
                     qlo_ref[r0:r0 + CHUNK, (2 * j + 1) * LANES:(2 * j + 2) * LANES],
                     qhi_ref[r0:r0 + CHUNK, (2 * j + 1) * LANES:(2 * j + 2) * LANES]], axis=0)
                kb = kd_ref[s, k0:k0 + KBAND, j * LANES:(j + 1) * LANES]
                vb = vd_ref[s, k0:k0 + KBAND, 2 * j * LANES:(2 * j + 2) * LANES]
                sc = _dot_nt(qs, kb) - bias_ref[variant, j]
                sink = sink_cols[j]
                m = jnp.maximum(jnp.max(sc, axis=-1, keepdims=True), sink)
                p = jnp.exp2(sc - m)
                pv = _dot(p.astype(_BF), vb)
                o = pv[:, 0:LANES] / (pv[:, LANES:2 * LANES] + jnp.exp2(sink - m))
                for pb in range(2):
                    blk = 2 * j + pb
                    ob = jnp.where(lo_half, o[(2 * pb) * CHUNK:(2 * pb + 1) * CHUNK],
                                   o[(2 * pb + 1) * CHUNK:(2 * pb + 2) * CHUNK])
                    ycat_ref[r0:r0 + CHUNK, blk * LANES:(blk + 1) * LANES] = (
                        ob * gate_ref[r0:r0 + CHUNK, blk * LANES:(blk + 1) * LANES])

    if carry:
        kd_ref[:, 0:WINDOW, :] = kd_ref[:, seq_rows:seq_rows + WINDOW, :]
        vd_ref[:, 0:WINDOW, :] = vd_ref[:, seq_rows:seq_rows + WINDOW, :]

    z = _dot(proj(C_GLR, LANES).astype(_BF), w_up_ref[...]) + vec_ref[V_GLAB:V_GLAB + 1, 0:GLA_KDIM]
    log_a = (jnp.minimum(z, 0.0) - jnp.log(1.0 + jnp.exp(-jnp.abs(z)))) * (1.0 / GLA_TAU)
    tri = tri_ref[...]
    la_hi = log_a.astype(_BF)
    la_lo = (log_a - la_hi.astype(_F32)).astype(_BF)
    bcum = _dot(tri, la_hi) + _dot(tri, la_lo)
    gq = proj(C_GQ, GLA_KDIM) * GLA_DK ** -0.5
    gk = proj(C_GK, GLA_KDIM)
    gv = proj(C_GV, GLA_VDIM)
    gate_g = gate_ref[:, ATTN_DIM:ATTN_DIM + GLA_VDIM]
    gla_g = jnp.concatenate([vec_ref[V_GLAG:V_GLAG + 1, 0:GLA_DV]] * GLA_HEADS, axis=1)

    head_of_lane = lane128 >> GLA_DK_SHIFT
    a_r = lax.broadcasted_iota(jnp.int32, (CHUNK, GLA_HEADS * CHUNK), 0)
    a_c = lax.broadcasted_iota(jnp.int32, (CHUNK, GLA_HEADS * CHUNK), 1)
    causal = (a_c & (CHUNK - 1)) <= a_r
    vblk = lax.broadcasted_iota(jnp.int32, (1, GLA_VDIM), 1) >> GLA_DV_SHIFT
    bd_mask_t = ((lax.broadcasted_iota(jnp.int32, (GLA_VDIM, GLA_KDIM), 0) >> GLA_DV_SHIFT)
                 == (lax.broadcasted_iota(jnp.int32, (GLA_VDIM, GLA_KDIM), 1) >> GLA_DK_SHIFT))

    q_all = (gq * jnp.exp(bcum)).astype(_BF)
    k_all = gk * jnp.exp(-bcum)
    intra, kv_upd, decay_rows = [], [], []
    for s in range(n_seq):
        for c in range(n_chunk):
            r0 = s * seq_rows + c * CHUNK
            bc = bcum[r0:r0 + CHUNK]
            b_last = bc[CHUNK - 1:CHUNK]
            k_end = (gk[r0:r0 + CHUNK] * jnp.exp(b_last - bc)).astype(_BF)
            vf = gv[r0:r0 + CHUNK]
            k_stack = jnp.concatenate(
                [jnp.where(head_of_lane == h, k_all[r0:r0 + CHUNK], 0.0) for h in range(GLA_HEADS)],
                axis=0).astype(_BF)
            v_bd = jnp.concatenate(
                [jnp.where(vblk == h, vf, 0.0) for h in range(GLA_HEADS)], axis=0).astype(_BF)
            a = jnp.where(causal, _dot_nt(q_all[r0:r0 + CHUNK], k_stack), 0.0).astype(_BF)
            intra.append(_dot(a, v_bd))
            kv_upd.append(jnp.where(bd_mask_t, _dot_tn(vf.astype(_BF), k_end), 0.0))
            decay_rows.append(jnp.exp(b_last))
    o_rows = []
    for s in range(n_seq):
        state = s_ref[s]
        for c in range(n_chunk):
            i = s * n_chunk + c
            r0 = i * CHUNK
            o_rows.append(intra[i] + _dot_nt(q_all[r0:r0 + CHUNK], state.astype(_BF)))
            state = decay_rows[i] * state + kv_upd[i]
        s_ref[s] = state
    o_all = jnp.concatenate(o_rows, axis=0)
    ycat_ref[:, ATTN_DIM:ATTN_DIM + GLA_VDIM] = _group_rms_scale(o_all, GLA_DV, ones64) * gla_g * gate_g

    for s in range(n_seq):
        sbd_ref[...] = s_ref[s].T
        for h in range(GLA_HEADS):
            ns_ref[s, h] = sbd_ref[h * GLA_DK:(h + 1) * GLA_DK, h * GLA_DV:(h + 1) * GLA_DV]

    y = x
    for c0, c1 in ((ATTN_DIM + GLA_VDIM, D_MODEL), (0, ATTN_DIM), (ATTN_DIM, ATTN_DIM + GLA_VDIM)):
        y = y + _dot(ycat_ref[:, c0:c1].astype(_BF), w_out_ref[c0:c1, :])
    y_ref[...] = y
    xcur_ref[...] = y
    last_layer = layer == pl.num_programs(2) - 1
    nxt = jnp.where(last_layer, xnext_ref[...], y)
    next_gain = vec_all[jnp.where(last_layer, 0, layer + 1), V_NORM:V_NORM + 1, :]
    hb_ref[...] = normed_bf16(nxt, next_gain)


def _run_stream(x2d, n_seqs, seq_len, hist, sinks, params, *, carry, q_pos0):
    depth = sinks.shape[0]
    if carry:
        assert q_pos0 == 0
        tile, n_seq, n_chunk = PROMPT_TILE, 1, PROMPT_TILE // CHUNK
        grid = (n_seqs, seq_len // tile, depth)
    else:
        assert seq_len == CHUNK and q_pos0 >= WINDOW
        tile, n_seq, n_chunk = SAMPLE_SEQS * CHUNK, SAMPLE_SEQS, 1
        grid = (n_seqs // n_seq, 1, depth)
    n_t = grid[1]
    seq_rows = n_chunk * CHUNK

    def const(shape):
        return pl.BlockSpec(shape, lambda g, t, l: (0,) * len(shape), pipeline_mode=pl.Buffered(1))

    def per_layer_group(shape):
        return pl.BlockSpec((depth, n_seq) + shape, lambda g, t, l: (0, g) + (0,) * len(shape))

    state_shapes = ((WINDOW, KV_DIM), (WINDOW, KV_DIM), (GLA_HEADS, GLA_DK, GLA_DV), (CONV_HIST, CONV_DIM))
    x_spec = pl.BlockSpec((tile, D_MODEL), lambda g, t, l: (g * n_t + t, 0))
    n_tiles = x2d.shape[0] // tile
    xnext_spec = pl.BlockSpec((tile, D_MODEL), lambda g, t, l: (jnp.minimum(g * n_t + t + 1, n_tiles - 1), 0))
    in_specs = [pl.BlockSpec(memory_space=pltpu.SMEM), x_spec, xnext_spec]
    args = [sinks, x2d, x2d]
    if not carry:
        in_specs += [per_layer_group(sh) for sh in state_shapes]
        args += list(hist)
    operands = tuple(params) + _shape_constants(tile)
    in_specs += [const(p.shape) for p in operands]
    args += list(operands)

    out_shape = (jax.ShapeDtypeStruct(x2d.shape, _F32),) + tuple(
        jax.ShapeDtypeStruct((depth, n_seqs) + sh, _F32) for sh in state_shapes)
    out_specs = (x_spec,) + tuple(per_layer_group(sh) for sh in state_shapes)
    scratch = [pltpu.VMEM((depth, n_seq, WINDOW + seq_rows + CHUNK, ATTN_KV_HEADS * LANES), _BF),
               pltpu.VMEM((depth, n_seq, WINDOW + seq_rows + CHUNK, ATTN_KV_HEADS * 2 * LANES), _BF),
               pltpu.VMEM((depth, n_seq, GLA_VDIM, GLA_KDIM), _F32),
               pltpu.VMEM((depth, n_seq, CONV_PAD + seq_rows, CONV_DIM), _F32),
               pltpu.VMEM((tile, ATTN_DIM), _BF),
               pltpu.VMEM((tile, ATTN_DIM), _BF),
               pltpu.VMEM((tile, D_MODEL), _F32),
               pltpu.VMEM((tile, IN_COLS_PACKED), _F32),
               pltpu.VMEM((tile, D_MODEL), _F32),
               pltpu.VMEM((tile, D_MODEL), _BF),
               pltpu.VMEM((GLA_KDIM, GLA_VDIM), _F32),
               pltpu.VMEM((tile, D_MODEL), _F32)]
    body = functools.partial(_stream_kernel, tile=tile, n_seq=n_seq, n_chunk=n_chunk, carry=carry, q_pos0=q_pos0)
    return pl.pallas_call(
        body, grid=grid, in_specs=in_specs, out_specs=out_specs, out_shape=out_shape,
        scratch_shapes=scratch,
        compiler_params=pltpu.CompilerParams(dimension_semantics=("arbitrary", "arbitrary", "arbitrary"),
                                             vmem_limit_bytes=VMEM_LIMIT_BYTES),
        name="mixer_prompt" if carry else "mixer_sample",
    )(*args)


def _pack_params(norm_gain, w_in, q_norm_gain, k_norm_gain, gla_w_gate_up, gla_b_gate, gla_norm_gain,
                 conv_w, conv_b, conv_ln_gain, conv_ln_bias, conv_w_pw, conv_b_pw, w_out):
    depth = w_in.shape[0]
    glr0 = 2048
    w_in_p = jnp.concatenate(
        [w_in[:, :, :glr0], w_in[:, :, glr0 + GLA_LOWRANK:], w_in[:, :, glr0:glr0 + GLA_LOWRANK],
         jnp.zeros((depth, D_MODEL, LANES - GLA_LOWRANK), w_in.dtype)], axis=2).astype(_BF)
    w_up_p = jnp.pad(gla_w_gate_up, ((0, 0), (0, LANES - GLA_LOWRANK), (0, 0))).astype(_BF)

    def row(v, reps=1):
        v = jnp.tile(v.astype(_F32), (1, reps))
        return jnp.pad(v, ((0, 0), (0, D_MODEL - v.shape[1])))[:, None, :]

    vec_rows = [row(norm_gain), row(q_norm_gain, ATTN_HEADS), row(k_norm_gain, ATTN_KV_HEADS),
                row(gla_b_gate), row(gla_norm_gain), row(conv_b), row(conv_ln_gain),
                row(conv_ln_bias), row(conv_b_pw)]
    vecs = jnp.concatenate(vec_rows + [jnp.zeros((depth, VEC_ROWS - len(vec_rows), D_MODEL), _F32)], axis=1)
    convw = jnp.pad(conv_w.astype(_F32), ((0, 0), (0, CONV_PAD - CONV_WIDTH), (0, 0)))
    return w_in_p, w_up_p, conv_w_pw.astype(_BF), w_out.astype(_BF), vecs, convw


def kernel(x_prompt, x_sample, cache_k, cache_v, state_gla, state_conv, norm_gain, w_in, q_norm_gain, k_norm_gain, attn_sinks, gla_w_gate_up, gla_b_gate, gla_norm_gain, conv_w, conv_b, conv_ln_gain, conv_ln_bias, conv_w_pw, conv_b_pw, w_out):
    depth = w_in.shape[0]
    bp, lp, _ = x_prompt.shape
    bs, ls, _ = x_sample.shape
    params = _pack_params(norm_gain, w_in, q_norm_gain, k_norm_gain, gla_w_gate_up, gla_b_gate, gla_norm_gain,
                          conv_w, conv_b, conv_ln_gain, conv_ln_bias, conv_w_pw, conv_b_pw, w_out)
    sinks = jnp.pad(attn_sinks.astype(_F32), ((0, 0), (0, ATTN_HEADS)))
    yp, pk, pv, ps, pc = _run_stream(x_prompt.reshape(bp * lp, D_MODEL), bp, lp, None, sinks, params,
                                     carry=True, q_pos0=0)
    hist = (cache_k.reshape(depth, bs, WINDOW, KV_DIM), cache_v.reshape(depth, bs, WINDOW, KV_DIM),
            state_gla, state_conv)
    ys, sk, sv, ss, sc = _run_stream(x_sample.reshape(bs * ls, D_MODEL), bs, ls, hist, sinks, params,
                                     carry=False, q_pos0=PAST_LEN)
    kv_shape = (WINDOW, ATTN_KV_HEADS, HEAD_DIM)
    return (yp.reshape(bp, lp, D_MODEL), ys.reshape(bs, ls, D_MODEL),
            pk.reshape((depth, bp) + kv_shape), pv.reshape((depth, bp) + kv_shape), ps, pc,
            sk.reshape((depth, bs) + kv_shape), sv.reshape((depth, bs) + kv_shape), ss, sc)
```

```python
import functools
import math

import numpy as np
import jax
import jax.numpy as jnp
from jax import lax
from jax.experimental import pallas as pl
from jax.experimental.pallas import tpu as pltpu

D_MODEL = 1024
CHUNK = 64
ATTN_HEADS = 8
ATTN_KV_HEADS = 2
HEAD_DIM = 64
ATTN_GROUP = ATTN_HEADS // ATTN_KV_HEADS
ATTN_DIM = ATTN_HEADS * HEAD_DIM
KV_DIM = ATTN_KV_HEADS * HEAD_DIM
WINDOW = 128
BAND = WINDOW + CHUNK
KBAND = WINDOW + 2 * CHUNK
GLA_HEADS = 4
GLA_DK = 32
GLA_DV = 64
GLA_KDIM = GLA_HEADS * GLA_DK
GLA_VDIM = GLA_HEADS * GLA_DV
GLA_LOWRANK = 16
GLA_TAU = 16.0
CONV_DIM = 256
CONV_WIDTH = 31
CONV_HIST = CONV_WIDTH - 1
NORM_EPS = 1e-6
NEG_INF = -1e30
PAST_LEN = 4096
LOG2E = math.log2(math.e)

LANES = 128
SUBLANES = 8
CHUNK_SHIFT = CHUNK.bit_length() - 1
GLA_DK_SHIFT = GLA_DK.bit_length() - 1
GLA_DV_SHIFT = GLA_DV.bit_length() - 1
C_AQ, C_AK, C_AV, C_AG = 0, 512, 640, 768
C_GQ, C_GK, C_GV, C_GG = 1280, 1408, 1536, 1792
C_CV, C_CGL, C_CG, C_GLR = 2048, 2304, 2560, 2816
IN_COLS_PACKED = C_GLR + LANES
PROJ_GROUP = 256
CONV_PAD = 32
VEC_ROWS = 16
(V_NORM, V_QG, V_KG, V_GLAB, V_GLAG, V_CONVB, V_LNG, V_LNB, V_BPW) = range(9)

PROMPT_TILE = 512
SAMPLE_SEQS = 4
VMEM_LIMIT_BYTES = 56 * 1024 * 1024

_BF = jnp.bfloat16
_F32 = jnp.float32


def _shape_constants(tile):
    rows = np.arange(ATTN_GROUP * CHUNK)
    dist = np.abs((rows % CHUNK)[:, None] + WINDOW - np.arange(KBAND)[None, :]).astype(np.float64)
    bias = np.stack([
        (2.0 ** (-8.0 * (j * ATTN_GROUP + rows // CHUNK + 1) / ATTN_HEADS))[:, None] * LOG2E * dist
        for j in range(ATTN_KV_HEADS)]).astype(np.float32)
    bias[:, :, BAND:] = -NEG_INF
    variants = []
    for n in range(WINDOW // CHUNK + 1):
        b = bias.copy()
        b[:, :, :max(WINDOW - n * CHUNK, 0)] = -NEG_INF
        variants.append(b)
    bias = np.stack(variants)
    tr = np.arange(tile)
    tri = ((tr[:, None] // CHUNK == tr[None, :] // CHUNK) & (tr[None, :] <= tr[:, None])).astype(np.float32)
    ln = np.arange(LANES) // HEAD_DIM
    ones = (ln[:, None] == ln[None, :]).astype(np.float32)
    return jnp.asarray(bias), jnp.asarray(tri, dtype=_BF), jnp.asarray(ones, dtype=_BF)


def _group_rms_scale(x, group, ones_bd):
    outs = []
    for c0 in range(0, x.shape[1], LANES):
        blk = x[:, c0:c0 + LANES]
        ss = _dot((blk * blk).astype(_BF), ones_bd)
        outs.append(blk * lax.rsqrt(ss * (1.0 / group) + NORM_EPS))
    return outs[0] if len(outs) == 1 else jnp.concatenate(outs, axis=1)


def _silu(x):
    return x * jax.nn.sigmoid(x)


def _dot(a, b):
    return jnp.dot(a, b, preferred_element_type=_F32)


def _dot_nt(a, b):
    return lax.dot_general(a, b, (((1,), (1,)), ((), ())), preferred_element_type=_F32)


def _dot_tn(a, b):
    return lax.dot_general(a, b, (((0,), (0,)), ((), ())), preferred_element_type=_F32)


def _stream_kernel(*refs, tile, n_seq, n_chunk, carry, q_pos0):
    if carry:
        (sinks_ref, x_ref, xnext_ref, *rest) = refs
    else:
        (sinks_ref, x_ref, xnext_ref, hk_all, hv_all, hs_all, hc_all, *rest) = refs
    (w_in_a, w_in_b, w_in_c, w_up_all, w_pw_all, w_out_all, vec_all, convw_all, bias_ref, tri_ref, ones_ref,
     y_ref, nk_all, nv_all, ns_all, nc_all,
     kd_all, vd_all, s_all, u_all, qlo_ref, qhi_ref, ycat_ref, proj_ref, xcur_ref, hb_ref, sbd_ref,
     gate_ref) = rest
    t = pl.program_id(1)
    layer = pl.program_id(2)
    seq_rows = n_chunk * CHUNK
    w_up_ref, w_pw_ref, w_out_ref = (r.at[layer] for r in (w_up_all, w_pw_all, w_out_all))
    vec_ref, convw_ref = vec_all.at[layer], convw_all.at[layer]
    kd_ref, vd_ref, s_ref, u_ref = (r.at[layer] for r in (kd_all, vd_all, s_all, u_all))
    nk_ref, nv_ref, ns_ref, nc_ref = (r.at[layer] for r in (nk_all, nv_all, ns_all, nc_all))
    if not carry:
        hk_ref, hv_ref, hs_ref, hc_ref = (r.at[layer] for r in (hk_all, hv_all, hs_all, hc_all))

    lane128 = lax.broadcasted_iota(jnp.int32, (1, LANES), 1)
    lo_half = lane128 < HEAD_DIM

    def dup_halves(a):
        sw = pltpu.roll(a, HEAD_DIM, 1)
        return jnp.concatenate([jnp.where(lo_half, a, sw), jnp.where(lo_half, sw, a)], axis=1)

    buf_rows = WINDOW + seq_rows + CHUNK
    zeros_blk = jnp.zeros((buf_rows, LANES), _BF)
    ones_blk = jnp.ones((buf_rows, LANES), _BF)
    v_blank = jnp.concatenate([zeros_blk, ones_blk] * ATTN_KV_HEADS, axis=1)

    def store_v(s, row0, vdup_rows):
        for j in range(ATTN_KV_HEADS):
            vd_ref[s, row0:row0 + vdup_rows.shape[0], 2 * j * LANES:(2 * j + 1) * LANES] = (
                vdup_rows[:, j * LANES:(j + 1) * LANES])

    if carry:
        @pl.when(t == 0)
        def _():
            kd_ref[...] = jnp.zeros(kd_ref.shape, _BF)
            for s in range(n_seq):
                vd_ref[s] = v_blank
            s_ref[...] = jnp.zeros(s_ref.shape, _F32)
            u_ref[:, 0:CONV_PAD, :] = jnp.zeros((n_seq, CONV_PAD, CONV_DIM), _F32)
    else:
        for s in range(n_seq):
            kd_ref[s, 0:WINDOW, :] = dup_halves(hk_ref[s]).astype(_BF)
            kd_ref[s, WINDOW + seq_rows:, :] = jnp.zeros((CHUNK, 2 * LANES), _BF)
            vd_ref[s] = v_blank
            store_v(s, 0, dup_halves(hv_ref[s]).astype(_BF))
            sbd_ref[...] = jnp.zeros((GLA_KDIM, GLA_VDIM), _F32)
            for h in range(GLA_HEADS):
                sbd_ref[h * GLA_DK:(h + 1) * GLA_DK, h * GLA_DV:(h + 1) * GLA_DV] = hs_ref[s, h]
            s_ref[s] = sbd_ref[...].T
            u_ref[s, 0:SUBLANES, :] = jnp.zeros((SUBLANES, CONV_DIM), _F32)
            u_ref[s, CONV_PAD - CONV_HIST:CONV_PAD, :] = hc_ref[s]

    def normed_bf16(v, gain_row):
        ms = jnp.mean(v * v, axis=-1, keepdims=True)
        return (v * lax.rsqrt(ms + NORM_EPS) * gain_row).astype(_BF)

    @pl.when(layer == 0)
    def _():
        xcur_ref[...] = x_ref[...]

    @pl.when((pl.program_id(0) == 0) & (t == 0) & (layer == 0))
    def _():
        hb_ref[...] = normed_bf16(x_ref[...], vec_ref[V_NORM:V_NORM + 1, :])
    x = xcur_ref[...]
    hb = hb_ref[...]

    anchor_on = sinks_ref[layer, ATTN_HEADS] != 0.0

    ones64 = ones_ref[...]

    def post_q(g):
        def post(res):
            for b in range(PROJ_GROUP // LANES):
                c0 = g * PROJ_GROUP + b * LANES
                qn = _group_rms_scale(res[:, b * LANES:(b + 1) * LANES], HEAD_DIM, ones64) * (
                    vec_ref[V_QG:V_QG + 1, c0:c0 + LANES] * (LOG2E * HEAD_DIM ** -0.5))
                qlo_ref[:, c0:c0 + LANES] = jnp.where(lo_half, qn, 0.0).astype(_BF)
                qhi_ref[:, c0:c0 + LANES] = jnp.where(lo_half, 0.0, qn).astype(_BF)
        return post

    def post_kv(res):
        kn = _group_rms_scale(res[:, 0:KV_DIM], HEAD_DIM, ones64) * vec_ref[V_KG:V_KG + 1, 0:KV_DIM]
        vv = res[:, KV_DIM:2 * KV_DIM]
        kdup = dup_halves(kn).astype(_BF)
        vdup = dup_halves(vv).astype(_BF)
        for s in range(n_seq):
            kd_ref[s, WINDOW:WINDOW + seq_rows, :] = kdup[s * seq_rows:(s + 1) * seq_rows]
            store_v(s, WINDOW, vdup[s * seq_rows:(s + 1) * seq_rows])
        if carry:
            nk_ref[0] = kn[tile - WINDOW:tile]
            nv_ref[0] = vv[tile - WINDOW:tile]
        else:
            for s in range(n_seq):
                nk_ref[s, 0:WINDOW - seq_rows, :] = hk_ref[s, seq_rows:WINDOW, :]
                nv_ref[s, 0:WINDOW - seq_rows, :] = hv_ref[s, seq_rows:WINDOW, :]
                nk_ref[s, WINDOW - seq_rows:WINDOW, :] = kn[s * seq_rows:(s + 1) * seq_rows]
                nv_ref[s, WINDOW - seq_rows:WINDOW, :] = vv[s * seq_rows:(s + 1) * seq_rows]

    def post_gate(col0):
        def post(res):
            gate_ref[:, col0:col0 + PROJ_GROUP] = _silu(res)
        return post

    group_post = {C_AG // PROJ_GROUP: post_gate(0), C_AG // PROJ_GROUP + 1: post_gate(PROJ_GROUP),
                  C_GG // PROJ_GROUP: post_gate(ATTN_DIM), C_CG // PROJ_GROUP: post_gate(ATTN_DIM + GLA_VDIM)}

    def w_in_cols(c0, c1):
        for ref, start in ((w_in_a, 0), (w_in_b, C_CV), (w_in_c, C_GLR)):
            if start <= c0 and c1 <= start + ref.shape[2]:
                return ref[layer, :, c0 - start:c1 - start]
        raise ValueError((c0, c1))

    def compute_group(g):
        c0, c1 = g * PROJ_GROUP, min((g + 1) * PROJ_GROUP, IN_COLS_PACKED)
        res = _dot(hb, w_in_cols(c0, c1))
        if g in group_post:
            group_post[g](res)
        else:
            proj_ref[:, c0:c1] = res
        return jnp.where(anchor_on, res[0:1, 0:CONV_DIM], 0.0)

    def proj(c0, width):
        return proj_ref[:, c0:c0 + width]

    compute_group(C_CV // PROJ_GROUP)
    compute_group(C_CGL // PROJ_GROUP)
    u = proj(C_CV, CONV_DIM) * jax.nn.sigmoid(proj(C_CGL, CONV_DIM))
    for s in range(n_seq):
        u_ref[s, CONV_PAD:CONV_PAD + seq_rows, :] = u[s * seq_rows:(s + 1) * seq_rows]
    conv_acc = [jnp.broadcast_to(vec_ref[V_CONVB:V_CONVB + 1, 0:CONV_DIM], (seq_rows, CONV_DIM))] * n_seq
    for rho in range(SUBLANES):
        anchor = compute_group(rho)
        for s in range(n_seq):
            frame_rows = seq_rows + (SUBLANES if rho else 0)
            frame = None
            for j in range(CONV_WIDTH):
                off = CONV_PAD - CONV_HIST + j
                if off % SUBLANES != rho:
                    continue
                term = (convw_ref[j:j + 1, :] + anchor) * u_ref[s, off - rho:off - rho + frame_rows, :]
                frame = term if frame is None else frame + term
            conv_acc[s] = conv_acc[s] + frame[rho:rho + seq_rows]
    compute_group(C_CG // PROJ_GROUP)
    compute_group(C_GLR // PROJ_GROUP)
    cc = conv_acc[0] if n_seq == 1 else jnp.concatenate(conv_acc, axis=0)

    for s in range(n_seq):
        nc_ref[s] = u_ref[s, seq_rows + CONV_PAD - CONV_HIST:seq_rows + CONV_PAD, :]
    if carry:
        u_ref[:, 0:CONV_PAD, :] = u_ref[:, seq_rows:seq_rows + CONV_PAD, :]

    mu = jnp.mean(cc, axis=-1, keepdims=True)
    cen = cc - mu
    var = jnp.mean(cen * cen, axis=-1, keepdims=True)
    ln = cen * lax.rsqrt(var + NORM_EPS) * vec_ref[V_LNG:V_LNG + 1, 0:CONV_DIM] + vec_ref[V_LNB:V_LNB + 1, 0:CONV_DIM]
    cpw = _dot(_silu(ln).astype(_BF), w_pw_ref[...]) + vec_ref[V_BPW:V_BPW + 1, 0:CONV_DIM]
    ycat_ref[:, ATTN_DIM + GLA_VDIM:D_MODEL] = cpw * gate_ref[:, ATTN_DIM + GLA_VDIM:D_MODEL]

    for g in range(ATTN_DIM // PROJ_GROUP):
        post_q(g)(proj(C_AQ + g * PROJ_GROUP, PROJ_GROUP))
    post_kv(proj(C_AK, 2 * KV_DIM))

    rows = ATTN_GROUP * CHUNK
    r_head1 = lax.broadcasted_iota(jnp.int32, (rows, 1), 0) >> CHUNK_SHIFT
    sink_cols = []
    for j in range(ATTN_KV_HEADS):
        sink = jnp.zeros((rows, 1), _F32)
        for r in range(ATTN_GROUP):
            sink = jnp.where(r_head1 == r, sinks_ref[layer, j * ATTN_GROUP + r], sink)
        sink_cols.append(sink * LOG2E)

    for s in range(n_seq):
        for c in range(n_chunk):
            r0 = s * seq_rows + c * CHUNK
            k0 = c * CHUNK
            steady = WINDOW // CHUNK
            variant = jnp.minimum(t * n_chunk + c, steady) if carry else steady
            for j in range(ATTN_KV_HEADS):
                qs = jnp.concatenate(
                    [qlo_ref[r0:r0 + CHUNK, (2 * j) * LANES:(2 * j + 1) * LANES],
                     qhi_ref[r0:r0 + CHUNK, (2 * j) * LANES:(2 * j + 1) * LANES],
                     qlo_ref[r0:r0 + CHUNK, (2 * j + 1) * LANES:(2 * j + 2) * LANES],
                     qhi_ref[r0:r0 + CHUNK, (2 * j + 1) * LANES:(2 * j + 2) * LANES]], axis=0)
                kb = kd_ref[s, k0:k0 + KBAND, j * LANES:(j + 1) * LANES]
                vb = vd_ref[s, k0:k0 + KBAND, 2 * j * LANES:(2 * j + 2) * LANES]
                sc = _dot_nt(qs, kb) - bias_ref[variant, j]
                sink = sink_cols[j]
                m = jnp.maximum(jnp.max(sc, axis=-1, keepdims=True), sink)
                p = jnp.exp2(sc - m)
                pv = _dot(p.astype(_BF), vb)
                o = pv[:, 0:LANES] / (pv[:, LANES:2 * LANES] + jnp.exp2(sink - m))
                for pb in range(2):
                    blk = 2 * j + pb
                    ob = jnp.where(lo_half, o[(2 * pb) * CHUNK:(2 * pb + 1) * CHUNK],
                                   o[(2 * pb + 1) * CHUNK:(2 * pb + 2) * CHUNK])
                    ycat_ref[r0:r0 + CHUNK, blk * LANES:(blk + 1) * LANES] = (
                        ob * gate_ref[r0:r0 + CHUNK, blk * LANES:(blk + 1) * LANES])

    if carry:
        kd_ref[:, 0:WINDOW, :] = kd_ref[:, seq_rows:seq_rows + WINDOW, :]
        vd_ref[:, 0:WINDOW, :] = vd_ref[:, seq_rows:seq_rows + WINDOW, :]

    z = _dot(proj(C_GLR, LANES).astype(_BF), w_up_ref[...]) + vec_ref[V_GLAB:V_GLAB + 1, 0:GLA_KDIM]
    log_a = (jnp.minimum(z, 0.0) - jnp.log(1.0 + jnp.exp(-jnp.abs(z)))) * (1.0 / GLA_TAU)
    tri = tri_ref[...]
    la_hi = log_a.astype(_BF)
    la_lo = (log_a - la_hi.astype(_F32)).astype(_BF)
    bcum = _dot(tri, la_hi) + _dot(tri, la_lo)
    gq = proj(C_GQ, GLA_KDIM) * GLA_DK ** -0.5
    gk = proj(C_GK, GLA_KDIM)
    gv = proj(C_GV, GLA_VDIM)
    gate_g = gate_ref[:, ATTN_DIM:ATTN_DIM + GLA_VDIM]
    gla_g = jnp.concatenate([vec_ref[V_GLAG:V_GLAG + 1, 0:GLA_DV]] * GLA_HEADS, axis=1)

    head_of_lane = lane128 >> GLA_DK_SHIFT
    a_r = lax.broadcasted_iota(jnp.int32, (CHUNK, GLA_HEADS * CHUNK), 0)
    a_c = lax.broadcasted_iota(jnp.int32, (CHUNK, GLA_HEADS * CHUNK), 1)
    causal = (a_c & (CHUNK - 1)) <= a_r
    vblk = lax.broadcasted_iota(jnp.int32, (1, GLA_VDIM), 1) >> GLA_DV_SHIFT
    bd_mask_t = ((lax.broadcasted_iota(jnp.int32, (GLA_VDIM, GLA_KDIM), 0) >> GLA_DV_SHIFT)
                 == (lax.broadcasted_iota(jnp.int32, (GLA_VDIM, GLA_KDIM), 1) >> GLA_DK_SHIFT))

    q_all = (gq * jnp.exp(bcum)).astype(_BF)
    k_all = gk * jnp.exp(-bcum)
    intra, kv_upd, decay_rows = [], [], []
    for s in range(n_seq):
        for c in range(n_chunk):
            r0 = s * seq_rows + c * CHUNK
            bc = bcum[r0:r0 + CHUNK]
            b_last = bc[CHUNK - 1:CHUNK]
            k_end = (gk[r0:r0 + CHUNK] * jnp.exp(b_last - bc)).astype(_BF)
            vf = gv[r0:r0 + CHUNK]
            k_stack = jnp.concatenate(
                [jnp.where(head_of_lane == h, k_all[r0:r0 + CHUNK], 0.0) for h in range(GLA_HEADS)],
                axis=0).astype(_BF)
            v_bd = jnp.concatenate(
                [jnp.where(vblk == h, vf, 0.0) for h in range(GLA_HEADS)], axis=0).astype(_BF)
            a = jnp.where(causal, _dot_nt(q_all[r0:r0 + CHUNK], k_stack), 0.0).astype(_BF)
            intra.append(_dot(a, v_bd))
            kv_upd.append(jnp.where(bd_mask_t, _dot_tn(vf.astype(_BF), k_end), 0.0))
            decay_rows.append(jnp.exp(b_last))
    o_rows = []
    for s in range(n_seq):
        state = s_ref[s]
        for c in range(n_chunk):
            i = s * n_chunk + c
            r0 = i * CHUNK
            o_rows.append(intra[i] + _dot_nt(q_all[r0:r0 + CHUNK], state.astype(_BF)))
            state = decay_rows[i] * state + kv_upd[i]
        s_ref[s] = state
    o_all = jnp.concatenate(o_rows, axis=0)
    ycat_ref[:, ATTN_DIM:ATTN_DIM + GLA_VDIM] = _group_rms_scale(o_all, GLA_DV, ones64) * gla_g * gate_g

    for s in range(n_seq):
        sbd_ref[...] = s_ref[s].T
        for h in range(GLA_HEADS):
            ns_ref[s, h] = sbd_ref[h * GLA_DK:(h + 1) * GLA_DK, h * GLA_DV:(h + 1) * GLA_DV]

    y = x
    for c0, c1 in ((ATTN_DIM + GLA_VDIM, D_MODEL), (0, ATTN_DIM), (ATTN_DIM, ATTN_DIM + GLA_VDIM)):
        y = y + _dot(ycat_ref[:, c0:c1].astype(_BF), w_out_ref[c0:c1, :])
    y_ref[...] = y
    xcur_ref[...] = y
    last_layer = layer == pl.num_programs(2) - 1
    nxt = jnp.where(last_layer, xnext_ref[...], y)
    next_gain = vec_all[jnp.where(last_layer, 0, layer + 1), V_NORM:V_NORM + 1, :]
    hb_ref[...] = normed_bf16(nxt, next_gain)


def _run_stream(x2d, n_seqs, seq_len, hist, sinks, params, *, carry, q_pos0):
    depth = sinks.shape[0]
    if carry:
        assert q_pos0 == 0
        tile, n_seq, n_chunk = PROMPT_TILE, 1, PROMPT_TILE // CHUNK
        grid = (n_seqs, seq_len // tile, depth)
    else:
        assert seq_len == CHUNK and q_pos0 >= WINDOW
        tile, n_seq, n_chunk = SAMPLE_SEQS * CHUNK, SAMPLE_SEQS, 1
        grid = (n_seqs // n_seq, 1, depth)
    n_t = grid[1]
    seq_rows = n_chunk * CHUNK

    def const(shape):
        return pl.BlockSpec(shape, lambda g, t, l: (0,) * len(shape), pipeline_mode=pl.Buffered(1))

    def per_layer_group(shape):
        return pl.BlockSpec((depth, n_seq) + shape, lambda g, t, l: (0, g) + (0,) * len(shape))

    state_shapes = ((WINDOW, KV_DIM), (WINDOW, KV_DIM), (GLA_HEADS, GLA_DK, GLA_DV), (CONV_HIST, CONV_DIM))
    x_spec = pl.BlockSpec((tile, D_MODEL), lambda g, t, l: (g * n_t + t, 0))
    n_tiles = x2d.shape[0] // tile
    xnext_spec = pl.BlockSpec((tile, D_MODEL), lambda g, t, l: (jnp.minimum(g * n_t + t + 1, n_tiles - 1), 0))
    in_specs = [pl.BlockSpec(memory_space=pltpu.SMEM), x_spec, xnext_spec]
    args = [sinks, x2d, x2d]
    if not carry:
        in_specs += [per_layer_group(sh) for sh in state_shapes]
        args += list(hist)
    operands = tuple(params) + _shape_constants(tile)
    in_specs += [const(p.shape) for p in operands]
    args += list(operands)

    out_shape = (jax.ShapeDtypeStruct(x2d.shape, _F32),) + tuple(
        jax.ShapeDtypeStruct((depth, n_seqs) + sh, _F32) for sh in state_shapes)
    out_specs = (x_spec,) + tuple(per_layer_group(sh) for sh in state_shapes)
    scratch = [pltpu.VMEM((depth, n_seq, WINDOW + seq_rows + CHUNK, ATTN_KV_HEADS * LANES), _BF),
               pltpu.VMEM((depth, n_seq, WINDOW + seq_rows + CHUNK, ATTN_KV_HEADS * 2 * LANES), _BF),
               pltpu.VMEM((depth, n_seq, GLA_VDIM, GLA_KDIM), _F32),
               pltpu.VMEM((depth, n_seq, CONV_PAD + seq_rows, CONV_DIM), _F32),
               pltpu.VMEM((tile, ATTN_DIM), _BF),
               pltpu.VMEM((tile, ATTN_DIM), _BF),
               pltpu.VMEM((tile, D_MODEL), _F32),
               pltpu.VMEM((tile, IN_COLS_PACKED), _F32),
               pltpu.VMEM((tile, D_MODEL), _F32),
               pltpu.VMEM((tile, D_MODEL), _BF),
               pltpu.VMEM((GLA_KDIM, GLA_VDIM), _F32),
               pltpu.VMEM((tile, D_MODEL), _F32)]
    body = functools.partial(_stream_kernel, tile=tile, n_seq=n_seq, n_chunk=n_chunk, carry=carry, q_pos0=q_pos0)
    return pl.pallas_call(
        body, grid=grid, in_specs=in_specs, out_specs=out_specs, out_shape=out_shape,
        scratch_shapes=scratch,
        compiler_params=pltpu.CompilerParams(dimension_semantics=("arbitrary", "arbitrary", "arbitrary"),
                                             vmem_limit_bytes=VMEM_LIMIT_BYTES),
        name="mixer_prompt" if carry else "mixer_sample",
    )(*args)


def _pack_params(norm_gain, w_in, q_norm_gain, k_norm_gain, gla_w_gate_up, gla_b_gate, gla_norm_gain,
                 conv_w, conv_b, conv_ln_gain, conv_ln_bias, conv_w_pw, conv_b_pw, w_out):
    depth = w_in.shape[0]
    glr0 = 2048
    w_in_a = w_in[:, :, :glr0].astype(_BF)
    w_in_b = w_in[:, :, glr0 + GLA_LOWRANK:].astype(_BF)
    w_in_c = jnp.pad(w_in[:, :, glr0:glr0 + GLA_LOWRANK], ((0, 0), (0, 0), (0, LANES - GLA_LOWRANK))).astype(_BF)
    w_up_p = jnp.pad(gla_w_gate_up, ((0, 0), (0, LANES - GLA_LOWRANK), (0, 0))).astype(_BF)

    def row(v, reps=1):
        v = jnp.tile(v.astype(_F32), (1, reps))
        return jnp.pad(v, ((0, 0), (0, D_MODEL - v.shape[1])))[:, None, :]

    vec_rows = [row(norm_gain), row(q_norm_gain, ATTN_HEADS), row(k_norm_gain, ATTN_KV_HEADS),
                row(gla_b_gate), row(gla_norm_gain), row(conv_b), row(conv_ln_gain),
                row(conv_ln_bias), row(conv_b_pw)]
    vecs = jnp.concatenate(vec_rows + [jnp.zeros((depth, VEC_ROWS - len(vec_rows), D_MODEL), _F32)], axis=1)
    convw = jnp.pad(conv_w.astype(_F32), ((0, 0), (0, CONV_PAD - CONV_WIDTH), (0, 0)))
    return w_in_a, w_in_b, w_in_c, w_up_p, conv_w_pw.astype(_BF), w_out.astype(_BF), vecs, convw


def kernel(x_prompt, x_sample, cache_k, cache_v, state_gla, state_conv, norm_gain, w_in, q_norm_gain, k_norm_gain, attn_sinks, gla_w_gate_up, gla_b_gate, gla_norm_gain, conv_w, conv_b, conv_ln_gain, conv_ln_bias, conv_w_pw, conv_b_pw, w_out):
    depth = w_in.shape[0]
    bp, lp, _ = x_prompt.shape
    bs, ls, _ = x_sample.shape
    params = _pack_params(norm_gain, w_in, q_norm_gain, k_norm_gain, gla_w_gate_up, gla_b_gate, gla_norm_gain,
                          conv_w, conv_b, conv_ln_gain, conv_ln_bias, conv_w_pw, conv_b_pw, w_out)
    sinks = jnp.pad(attn_sinks.astype(_F32), ((0, 0), (0, ATTN_HEADS)))
    yp, pk, pv, ps, pc = _run_stream(x_prompt.reshape(bp * lp, D_MODEL), bp, lp, None, sinks, params,
                                     carry=True, q_pos0=0)
    hist = (cache_k.reshape(depth, bs, WINDOW, KV_DIM), cache_v.reshape(depth, bs, WINDOW, KV_DIM),
            state_gla, state_conv)
    ys, sk, sv, ss, sc = _run_stream(x_sample.reshape(bs * ls, D_MODEL), bs, ls, hist, sinks, params,
                                     carry=False, q_pos0=PAST_LEN)
    kv_shape = (WINDOW, ATTN_KV_HEADS, HEAD_DIM)
    return (yp.reshape(bp, lp, D_MODEL), ys.reshape(bs, ls, D_MODEL),
            pk.reshape((depth, bp) + kv_shape), pv.reshape((depth, bp) + kv_shape), ps, pc,
            sk.reshape((depth, bs) + kv_shape), sv.reshape((depth, bs) + kv_shape), ss, sc)
```

```python
import functools
import math

import numpy as np
import jax
import jax.numpy as jnp
from jax import lax
from jax.experimental import pallas as pl
from jax.experimental.pallas import tpu as pltpu

D_MODEL = 1024
CHUNK = 64
ATTN_HEADS = 8
ATTN_KV_HEADS = 2
HEAD_DIM = 64
ATTN_GROUP = ATTN_HEADS // ATTN_KV_HEADS
ATTN_DIM = ATTN_HEADS * HEAD_DIM
KV_DIM = ATTN_KV_HEADS * HEAD_DIM
WINDOW = 128
BAND = WINDOW + CHUNK
KBAND = WINDOW + 2 * CHUNK
GLA_HEADS = 4
GLA_DK = 32
GLA_DV = 64
GLA_KDIM = GLA_HEADS * GLA_DK
GLA_VDIM = GLA_HEADS * GLA_DV
GLA_LOWRANK = 16
GLA_TAU = 16.0
CONV_DIM = 256
CONV_WIDTH = 31
CONV_HIST = CONV_WIDTH - 1
NORM_EPS = 1e-6
NEG_INF = -1e30
PAST_LEN = 4096
LOG2E = math.log2(math.e)

LANES = 128
SUBLANES = 8
CHUNK_SHIFT = CHUNK.bit_length() - 1
GLA_DK_SHIFT = GLA_DK.bit_length() - 1
GLA_DV_SHIFT = GLA_DV.bit_length() - 1
C_AQ, C_AK, C_AV, C_AG = 0, 512, 640, 768
C_GQ, C_GK, C_GV, C_GG = 1280, 1408, 1536, 1792
C_CV, C_CGL, C_CG, C_GLR = 2048, 2304, 2560, 2816
IN_COLS_PACKED = C_GLR + LANES
PROJ_GROUP = 256
CONV_PAD = 32
GLR_SRC = 2048

PROMPT_TILE = 512
SAMPLE_SEQS = 4
VMEM_LIMIT_BYTES = 56 * 1024 * 1024

_BF = jnp.bfloat16
_F32 = jnp.float32


def _shape_constants(tile):
    rows = np.arange(ATTN_GROUP * CHUNK)
    dist = np.abs((rows % CHUNK)[:, None] + WINDOW - np.arange(KBAND)[None, :]).astype(np.float64)
    bias = np.stack([
        (2.0 ** (-8.0 * (j * ATTN_GROUP + rows // CHUNK + 1) / ATTN_HEADS))[:, None] * LOG2E * dist
        for j in range(ATTN_KV_HEADS)]).astype(np.float32)
    bias[:, :, BAND:] = -NEG_INF
    variants = []
    for n in range(WINDOW // CHUNK + 1):
        b = bias.copy()
        b[:, :, :max(WINDOW - n * CHUNK, 0)] = -NEG_INF
        variants.append(b)
    bias = np.stack(variants)
    tr = np.arange(tile)
    tri = ((tr[:, None] // CHUNK == tr[None, :] // CHUNK) & (tr[None, :] <= tr[:, None])).astype(np.float32)
    ln = np.arange(LANES) // HEAD_DIM
    ones = (ln[:, None] == ln[None, :]).astype(np.float32)
    return jnp.asarray(bias), jnp.asarray(tri, dtype=_BF), jnp.asarray(ones, dtype=_BF)


def _group_rms_scale(x, group, ones_bd):
    outs = []
    for c0 in range(0, x.shape[1], LANES):
        blk = x[:, c0:c0 + LANES]
        ss = _dot((blk * blk).astype(_BF), ones_bd)
        outs.append(blk * lax.rsqrt(ss * (1.0 / group) + NORM_EPS))
    return outs[0] if len(outs) == 1 else jnp.concatenate(outs, axis=1)


def _silu(x):
    return x * jax.nn.sigmoid(x)


def _dot(a, b):
    return jnp.dot(a, b, preferred_element_type=_F32)


def _dot_nt(a, b):
    return lax.dot_general(a, b, (((1,), (1,)), ((), ())), preferred_element_type=_F32)


def _dot_tn(a, b):
    return lax.dot_general(a, b, (((0,), (0,)), ((), ())), preferred_element_type=_F32)


def _stream_kernel(*refs, tile, n_seq, n_chunk, carry, q_pos0):
    if carry:
        (sinks_ref, x_ref, xnext_ref, *rest) = refs
    else:
        (sinks_ref, x_ref, xnext_ref, hk_all, hv_all, hs_all, hc_all, *rest) = refs
    (w_in_a, w_in_b, w_up_all, w_pw_all, w_out_all, norm_g_all, q_g_all, k_g_all, gla_b_all, gla_g_all,
     conv_b_all, ln_g_all, ln_b_all, b_pw_all, convw_all, bias_ref, tri_ref, ones_ref,
     y_ref, nk_all, nv_all, ns_all, nc_all,
     kd_all, vd_all, s_all, u_all, qlo_ref, qhi_ref, ycat_ref, proj_ref, xcur_ref, hb_ref, sbd_ref,
     gate_ref) = rest
    t = pl.program_id(1)
    layer = pl.program_id(2)
    seq_rows = n_chunk * CHUNK
    w_up_ref, w_pw_ref, w_out_ref = (r.at[layer] for r in (w_up_all, w_pw_all, w_out_all))
    convw_ref = convw_all.at[layer]

    def layer_row(ref, which=layer):
        return ref[pl.ds(which, 1), :]

    def lane_tiled(row, width):
        return jnp.concatenate([row] * (width // row.shape[1]), axis=1)

    q_gain = lane_tiled(layer_row(q_g_all), LANES) * (LOG2E * HEAD_DIM ** -0.5)
    k_gain = lane_tiled(layer_row(k_g_all), KV_DIM)
    kd_ref, vd_ref, s_ref, u_ref = (r.at[layer] for r in (kd_all, vd_all, s_all, u_all))
    nk_ref, nv_ref, ns_ref, nc_ref = (r.at[layer] for r in (nk_all, nv_all, ns_all, nc_all))
    if not carry:
        hk_ref, hv_ref, hs_ref, hc_ref = (r.at[layer] for r in (hk_all, hv_all, hs_all, hc_all))

    lane128 = lax.broadcasted_iota(jnp.int32, (1, LANES), 1)
    lo_half = lane128 < HEAD_DIM

    def dup_halves(a):
        sw = pltpu.roll(a, HEAD_DIM, 1)
        return jnp.concatenate([jnp.where(lo_half, a, sw), jnp.where(lo_half, sw, a)], axis=1)

    buf_rows = WINDOW + seq_rows + CHUNK
    zeros_blk = jnp.zeros((buf_rows, LANES), _BF)
    ones_blk = jnp.ones((buf_rows, LANES), _BF)
    v_blank = jnp.concatenate([zeros_blk, ones_blk] * ATTN_KV_HEADS, axis=1)

    def store_v(s, row0, vdup_rows):
        for j in range(ATTN_KV_HEADS):
            vd_ref[s, row0:row0 + vdup_rows.shape[0], 2 * j * LANES:(2 * j + 1) * LANES] = (
                vdup_rows[:, j * LANES:(j + 1) * LANES])

    if carry:
        @pl.when(t == 0)
        def _():
            kd_ref[...] = jnp.zeros(kd_ref.shape, _BF)
            for s in range(n_seq):
                vd_ref[s] = v_blank
            s_ref[...] = jnp.zeros(s_ref.shape, _F32)
            u_ref[:, 0:CONV_PAD, :] = jnp.zeros((n_seq, CONV_PAD, CONV_DIM), _F32)
    else:
        for s in range(n_seq):
            kd_ref[s, 0:WINDOW, :] = dup_halves(hk_ref[s]).astype(_BF)
            kd_ref[s, WINDOW + seq_rows:, :] = jnp.zeros((CHUNK, 2 * LANES), _BF)
            vd_ref[s] = v_blank
            store_v(s, 0, dup_halves(hv_ref[s]).astype(_BF))
            sbd_ref[...] = jnp.zeros((GLA_KDIM, GLA_VDIM), _F32)
            for h in range(GLA_HEADS):
                sbd_ref[h * GLA_DK:(h + 1) * GLA_DK, h * GLA_DV:(h + 1) * GLA_DV] = hs_ref[s, h]
            s_ref[s] = sbd_ref[...].T
            u_ref[s, 0:SUBLANES, :] = jnp.zeros((SUBLANES, CONV_DIM), _F32)
            u_ref[s, CONV_PAD - CONV_HIST:CONV_PAD, :] = hc_ref[s]

    def normed_bf16(v, gain_row):
        ms = jnp.mean(v * v, axis=-1, keepdims=True)
        return (v * lax.rsqrt(ms + NORM_EPS) * gain_row).astype(_BF)

    @pl.when(layer == 0)
    def _():
        xcur_ref[...] = x_ref[...]

    @pl.when((pl.program_id(0) == 0) & (t == 0) & (layer == 0))
    def _():
        hb_ref[...] = normed_bf16(x_ref[...], layer_row(norm_g_all))
    x = xcur_ref[...]
    hb = hb_ref[...]

    anchor_on = sinks_ref[layer, ATTN_HEADS] != 0.0

    ones64 = ones_ref[...]

    def post_q(g):
        def post(res):
            for b in range(PROJ_GROUP // LANES):
                c0 = g * PROJ_GROUP + b * LANES
                qn = _group_rms_scale(res[:, b * LANES:(b + 1) * LANES], HEAD_DIM, ones64) * q_gain
                qlo_ref[:, c0:c0 + LANES] = jnp.where(lo_half, qn, 0.0).astype(_BF)
                qhi_ref[:, c0:c0 + LANES] = jnp.where(lo_half, 0.0, qn).astype(_BF)
        return post

    def post_kv(res):
        kn = _group_rms_scale(res[:, 0:KV_DIM], HEAD_DIM, ones64) * k_gain
        vv = res[:, KV_DIM:2 * KV_DIM]
        kdup = dup_halves(kn).astype(_BF)
        vdup = dup_halves(vv).astype(_BF)
        for s in range(n_seq):
            kd_ref[s, WINDOW:WINDOW + seq_rows, :] = kdup[s * seq_rows:(s + 1) * seq_rows]
            store_v(s, WINDOW, vdup[s * seq_rows:(s + 1) * seq_rows])
        if carry:
            nk_ref[0] = kn[tile - WINDOW:tile]
            nv_ref[0] = vv[tile - WINDOW:tile]
        else:
            for s in range(n_seq):
                nk_ref[s, 0:WINDOW - seq_rows, :] = hk_ref[s, seq_rows:WINDOW, :]
                nv_ref[s, 0:WINDOW - seq_rows, :] = hv_ref[s, seq_rows:WINDOW, :]
                nk_ref[s, WINDOW - seq_rows:WINDOW, :] = kn[s * seq_rows:(s + 1) * seq_rows]
                nv_ref[s, WINDOW - seq_rows:WINDOW, :] = vv[s * seq_rows:(s + 1) * seq_rows]

    def post_gate(col0):
        def post(res):
            gate_ref[:, col0:col0 + PROJ_GROUP] = _silu(res)
        return post

    group_post = {C_AG // PROJ_GROUP: post_gate(0), C_AG // PROJ_GROUP + 1: post_gate(PROJ_GROUP),
                  C_GG // PROJ_GROUP: post_gate(ATTN_DIM), C_CG // PROJ_GROUP: post_gate(ATTN_DIM + GLA_VDIM)}

    def w_in_cols(c0, c1):
        if c1 <= C_CV:
            return w_in_a[layer, :, c0:c1]
        if c1 <= C_GLR:
            return w_in_b[layer, :, c0 - C_CV:c1 - C_CV]
        return w_in_a[layer, :, GLR_SRC:GLR_SRC + (c1 - c0)]

    def compute_group(g):
        c0, c1 = g * PROJ_GROUP, min((g + 1) * PROJ_GROUP, IN_COLS_PACKED)
        res = _dot(hb, w_in_cols(c0, c1))
        if g in group_post:
            group_post[g](res)
        else:
            proj_ref[:, c0:c1] = res
        return jnp.where(anchor_on, res[0:1, 0:CONV_DIM], 0.0)

    def proj(c0, width):
        return proj_ref[:, c0:c0 + width]

    compute_group(C_CV // PROJ_GROUP)
    compute_group(C_CGL // PROJ_GROUP)
    u = proj(C_CV, CONV_DIM) * jax.nn.sigmoid(proj(C_CGL, CONV_DIM))
    for s in range(n_seq):
        u_ref[s, CONV_PAD:CONV_PAD + seq_rows, :] = u[s * seq_rows:(s + 1) * seq_rows]
    conv_acc = [jnp.broadcast_to(layer_row(conv_b_all), (seq_rows, CONV_DIM))] * n_seq
    for rho in range(SUBLANES):
        anchor = compute_group(rho)
        for s in range(n_seq):
            frame_rows = seq_rows + (SUBLANES if rho else 0)
            frame = None
            for j in range(CONV_WIDTH):
                off = CONV_PAD - CONV_HIST + j
                if off % SUBLANES != rho:
                    continue
                term = (convw_ref[j:j + 1, :] + anchor) * u_ref[s, off - rho:off - rho + frame_rows, :]
                frame = term if frame is None else frame + term
            conv_acc[s] = conv_acc[s] + frame[rho:rho + seq_rows]
    compute_group(C_CG // PROJ_GROUP)
    compute_group(C_GLR // PROJ_GROUP)
    cc = conv_acc[0] if n_seq == 1 else jnp.concatenate(conv_acc, axis=0)

    for s in range(n_seq):
        nc_ref[s] = u_ref[s, seq_rows + CONV_PAD - CONV_HIST:seq_rows + CONV_PAD, :]
    if carry:
        u_ref[:, 0:CONV_PAD, :] = u_ref[:, seq_rows:seq_rows + CONV_PAD, :]

    mu = jnp.mean(cc, axis=-1, keepdims=True)
    cen = cc - mu
    var = jnp.mean(cen * cen, axis=-1, keepdims=True)
    ln = cen * lax.rsqrt(var + NORM_EPS) * layer_row(ln_g_all) + layer_row(ln_b_all)
    cpw = _dot(_silu(ln).astype(_BF), w_pw_ref[...]) + layer_row(b_pw_all)
    ycat_ref[:, ATTN_DIM + GLA_VDIM:D_MODEL] = cpw * gate_ref[:, ATTN_DIM + GLA_VDIM:D_MODEL]

    for g in range(ATTN_DIM // PROJ_GROUP):
        post_q(g)(proj(C_AQ + g * PROJ_GROUP, PROJ_GROUP))
    post_kv(proj(C_AK, 2 * KV_DIM))

    rows = ATTN_GROUP * CHUNK
    r_head1 = lax.broadcasted_iota(jnp.int32, (rows, 1), 0) >> CHUNK_SHIFT
    sink_cols = []
    for j in range(ATTN_KV_HEADS):
        sink = jnp.zeros((rows, 1), _F32)
        for r in range(ATTN_GROUP):
            sink = jnp.where(r_head1 == r, sinks_ref[layer, j * ATTN_GROUP + r], sink)
        sink_cols.append(sink * LOG2E)

    for s in range(n_seq):
        for c in range(n_chunk):
            r0 = s * seq_rows + c * CHUNK
            k0 = c * CHUNK
            steady = WINDOW // CHUNK
            variant = jnp.minimum(t * n_chunk + c, steady) if carry else steady
            for j in range(ATTN_KV_HEADS):
                qs = jnp.concatenate(
                    [qlo_ref[r0:r0 + CHUNK, (2 * j) * LANES:(2 * j + 1) * LANES],
                     qhi_ref[r0:r0 + CHUNK, (2 * j) * LANES:(2 * j + 1) * LANES],
                     qlo_ref[r0:r0 + CHUNK, (2 * j + 1) * LANES:(2 * j + 2) * LANES],
                     qhi_ref[r0:r0 + CHUNK, (2 * j + 1) * LANES:(2 * j + 2) * LANES]], axis=0)
                kb = kd_ref[s, k0:k0 + KBAND, j * LANES:(j + 1) * LANES]
                vb = vd_ref[s, k0:k0 + KBAND, 2 * j * LANES:(2 * j + 2) * LANES]
                sc = _dot_nt(qs, kb) - bias_ref[variant, j]
                sink = sink_cols[j]
                m = jnp.maximum(jnp.max(sc, axis=-1, keepdims=True), sink)
                p = jnp.exp2(sc - m)
                pv = _dot(p.astype(_BF), vb)
                o = pv[:, 0:LANES] / (pv[:, LANES:2 * LANES] + jnp.exp2(sink - m))
                for pb in range(2):
                    blk = 2 * j + pb
                    ob = jnp.where(lo_half, o[(2 * pb) * CHUNK:(2 * pb + 1) * CHUNK],
                                   o[(2 * pb + 1) * CHUNK:(2 * pb + 2) * CHUNK])
                    ycat_ref[r0:r0 + CHUNK, blk * LANES:(blk + 1) * LANES] = (
                        ob * gate_ref[r0:r0 + CHUNK, blk * LANES:(blk + 1) * LANES])

    if carry:
        kd_ref[:, 0:WINDOW, :] = kd_ref[:, seq_rows:seq_rows + WINDOW, :]
        vd_ref[:, 0:WINDOW, :] = vd_ref[:, seq_rows:seq_rows + WINDOW, :]

    z = _dot(proj(C_GLR, GLA_LOWRANK).astype(_BF), w_up_ref[...].astype(_BF)) + layer_row(gla_b_all)
    log_a = (jnp.minimum(z, 0.0) - jnp.log(1.0 + jnp.exp(-jnp.abs(z)))) * (1.0 / GLA_TAU)
    tri = tri_ref[...]
    la_hi = log_a.astype(_BF)
    la_lo = (log_a - la_hi.astype(_F32)).astype(_BF)
    bcum = _dot(tri, la_hi) + _dot(tri, la_lo)
    gq = proj(C_GQ, GLA_KDIM) * GLA_DK ** -0.5
    gk = proj(C_GK, GLA_KDIM)
    gv = proj(C_GV, GLA_VDIM)
    gate_g = gate_ref[:, ATTN_DIM:ATTN_DIM + GLA_VDIM]
    gla_g = lane_tiled(layer_row(gla_g_all), GLA_VDIM)

    head_of_lane = lane128 >> GLA_DK_SHIFT
    a_r = lax.broadcasted_iota(jnp.int32, (CHUNK, GLA_HEADS * CHUNK), 0)
    a_c = lax.broadcasted_iota(jnp.int32, (CHUNK, GLA_HEADS * CHUNK), 1)
    causal = (a_c & (CHUNK - 1)) <= a_r
    vblk = lax.broadcasted_iota(jnp.int32, (1, GLA_VDIM), 1) >> GLA_DV_SHIFT
    bd_mask_t = ((lax.broadcasted_iota(jnp.int32, (GLA_VDIM, GLA_KDIM), 0) >> GLA_DV_SHIFT)
                 == (lax.broadcasted_iota(jnp.int32, (GLA_VDIM, GLA_KDIM), 1) >> GLA_DK_SHIFT))

    q_all = (gq * jnp.exp(bcum)).astype(_BF)
    k_all = gk * jnp.exp(-bcum)
    intra, kv_upd, decay_rows = [], [], []
    for s in range(n_seq):
        for c in range(n_chunk):
            r0 = s * seq_rows + c * CHUNK
            bc = bcum[r0:r0 + CHUNK]
            b_last = bc[CHUNK - 1:CHUNK]
            k_end = (gk[r0:r0 + CHUNK] * jnp.exp(b_last - bc)).astype(_BF)
            vf = gv[r0:r0 + CHUNK]
            k_stack = jnp.concatenate(
                [jnp.where(head_of_lane == h, k_all[r0:r0 + CHUNK], 0.0) for h in range(GLA_HEADS)],
                axis=0).astype(_BF)
            v_bd = jnp.concatenate(
                [jnp.where(vblk == h, vf, 0.0) for h in range(GLA_HEADS)], axis=0).astype(_BF)
            a = jnp.where(causal, _dot_nt(q_all[r0:r0 + CHUNK], k_stack), 0.0).astype(_BF)
            intra.append(_dot(a, v_bd))
            kv_upd.append(jnp.where(bd_mask_t, _dot_tn(vf.astype(_BF), k_end), 0.0))
            decay_rows.append(jnp.exp(b_last))
    o_rows = []
    for s in range(n_seq):
        state = s_ref[s]
        for c in range(n_chunk):
            i = s * n_chunk + c
            r0 = i * CHUNK
            o_rows.append(intra[i] + _dot_nt(q_all[r0:r0 + CHUNK], state.astype(_BF)))
            state = decay_rows[i] * state + kv_upd[i]
        s_ref[s] = state
    o_all = jnp.concatenate(o_rows, axis=0)
    ycat_ref[:, ATTN_DIM:ATTN_DIM + GLA_VDIM] = _group_rms_scale(o_all, GLA_DV, ones64) * gla_g * gate_g

    for s in range(n_seq):
        sbd_ref[...] = s_ref[s].T
        for h in range(GLA_HEADS):
            ns_ref[s, h] = sbd_ref[h * GLA_DK:(h + 1) * GLA_DK, h * GLA_DV:(h + 1) * GLA_DV]

    y = x
    for c0, c1 in ((ATTN_DIM + GLA_VDIM, D_MODEL), (0, ATTN_DIM), (ATTN_DIM, ATTN_DIM + GLA_VDIM)):
        y = y + _dot(ycat_ref[:, c0:c1].astype(_BF), w_out_ref[c0:c1, :])
    y_ref[...] = y
    xcur_ref[...] = y
    last_layer = layer == pl.num_programs(2) - 1
    nxt = jnp.where(last_layer, xnext_ref[...], y)
    next_gain = layer_row(norm_g_all, jnp.where(last_layer, 0, layer + 1))
    hb_ref[...] = normed_bf16(nxt, next_gain)


def _run_stream(x2d, n_seqs, seq_len, hist, sinks, params, *, carry, q_pos0):
    depth = sinks.shape[0]
    if carry:
        assert q_pos0 == 0
        tile, n_seq, n_chunk = PROMPT_TILE, 1, PROMPT_TILE // CHUNK
        grid = (n_seqs, seq_len // tile, depth)
    else:
        assert seq_len == CHUNK and q_pos0 >= WINDOW
        tile, n_seq, n_chunk = SAMPLE_SEQS * CHUNK, SAMPLE_SEQS, 1
        grid = (n_seqs // n_seq, 1, depth)
    n_t = grid[1]
    seq_rows = n_chunk * CHUNK

    def const(shape):
        return pl.BlockSpec(shape, lambda g, t, l: (0,) * len(shape), pipeline_mode=pl.Buffered(1))

    def per_layer_group(shape):
        return pl.BlockSpec((depth, n_seq) + shape, lambda g, t, l: (0, g) + (0,) * len(shape))

    state_shapes = ((WINDOW, KV_DIM), (WINDOW, KV_DIM), (GLA_HEADS, GLA_DK, GLA_DV), (CONV_HIST, CONV_DIM))
    x_spec = pl.BlockSpec((tile, D_MODEL), lambda g, t, l: (g * n_t + t, 0))
    n_tiles = x2d.shape[0] // tile
    xnext_spec = pl.BlockSpec((tile, D_MODEL), lambda g, t, l: (jnp.minimum(g * n_t + t + 1, n_tiles - 1), 0))
    in_specs = [pl.BlockSpec(memory_space=pltpu.SMEM), x_spec, xnext_spec]
    args = [sinks, x2d, x2d]
    if not carry:
        in_specs += [per_layer_group(sh) for sh in state_shapes]
        args += list(hist)
    operands = tuple(params) + _shape_constants(tile)
    in_specs += [const(p.shape) for p in operands]
    args += list(operands)

    out_shape = (jax.ShapeDtypeStruct(x2d.shape, _F32),) + tuple(
        jax.ShapeDtypeStruct((depth, n_seqs) + sh, _F32) for sh in state_shapes)
    out_specs = (x_spec,) + tuple(per_layer_group(sh) for sh in state_shapes)
    scratch = [pltpu.VMEM((depth, n_seq, WINDOW + seq_rows + CHUNK, ATTN_KV_HEADS * LANES), _BF),
               pltpu.VMEM((depth, n_seq, WINDOW + seq_rows + CHUNK, ATTN_KV_HEADS * 2 * LANES), _BF),
               pltpu.VMEM((depth, n_seq, GLA_VDIM, GLA_KDIM), _F32),
               pltpu.VMEM((depth, n_seq, CONV_PAD + seq_rows, CONV_DIM), _F32),
               pltpu.VMEM((tile, ATTN_DIM), _BF),
               pltpu.VMEM((tile, ATTN_DIM), _BF),
               pltpu.VMEM((tile, D_MODEL), _F32),
               pltpu.VMEM((tile, IN_COLS_PACKED), _F32),
               pltpu.VMEM((tile, D_MODEL), _F32),
               pltpu.VMEM((tile, D_MODEL), _BF),
               pltpu.VMEM((GLA_KDIM, GLA_VDIM), _F32),
               pltpu.VMEM((tile, D_MODEL), _F32)]
    body = functools.partial(_stream_kernel, tile=tile, n_seq=n_seq, n_chunk=n_chunk, carry=carry, q_pos0=q_pos0)
    return pl.pallas_call(
        body, grid=grid, in_specs=in_specs, out_specs=out_specs, out_shape=out_shape,
        scratch_shapes=scratch,
        compiler_params=pltpu.CompilerParams(dimension_semantics=("arbitrary", "arbitrary", "arbitrary"),
                                             vmem_limit_bytes=VMEM_LIMIT_BYTES),
        name="mixer_prompt" if carry else "mixer_sample",
    )(*args)


def _pack_params(norm_gain, w_in, q_norm_gain, k_norm_gain, gla_w_gate_up, gla_b_gate, gla_norm_gain,
                 conv_w, conv_b, conv_ln_gain, conv_ln_bias, conv_w_pw, conv_b_pw, w_out):
    w_in_a = w_in.astype(_BF)
    w_in_b = w_in[:, :, GLR_SRC + GLA_LOWRANK:].astype(_BF)
    return (w_in_a, w_in_b, gla_w_gate_up, conv_w_pw.astype(_BF), w_out.astype(_BF), norm_gain, q_norm_gain,
            k_norm_gain, gla_b_gate, gla_norm_gain, conv_b, conv_ln_gain, conv_ln_bias, conv_b_pw, conv_w)


def kernel(x_prompt, x_sample, cache_k, cache_v, state_gla, state_conv, norm_gain, w_in, q_norm_gain, k_norm_gain, attn_sinks, gla_w_gate_up, gla_b_gate, gla_norm_gain, conv_w, conv_b, conv_ln_gain, conv_ln_bias, conv_w_pw, conv_b_pw, w_out):
    depth = w_in.shape[0]
    bp, lp, _ = x_prompt.shape
    bs, ls, _ = x_sample.shape
    params = _pack_params(norm_gain, w_in, q_norm_gain, k_norm_gain, gla_w_gate_up, gla_b_gate, gla_norm_gain,
                          conv_w, conv_b, conv_ln_gain, conv_ln_bias, conv_w_pw, conv_b_pw, w_out)
    sinks = jnp.pad(attn_sinks.astype(_F32), ((0, 0), (0, ATTN_HEADS)))
    yp, pk, pv, ps, pc = _run_stream(x_prompt.reshape(bp * lp, D_MODEL), bp, lp, None, sinks, params,
                                     carry=True, q_pos0=0)
    hist = (cache_k.reshape(depth, bs, WINDOW, KV_DIM), cache_v.reshape(depth, bs, WINDOW, KV_DIM),
            state_gla, state_conv)
    ys, sk, sv, ss, sc = _run_stream(x_sample.reshape(bs * ls, D_MODEL), bs, ls, hist, sinks, params,
                                     carry=False, q_pos0=PAST_LEN)
    kv_shape = (WINDOW, ATTN_KV_HEADS, HEAD_DIM)
    return (yp.reshape(bp, lp, D_MODEL), ys.reshape(bs, ls, D_MODEL),
            pk.reshape((depth, bp) + kv_shape), pv.reshape((depth, bp) + kv_shape), ps, pc,
            sk.reshape((depth, bs) + kv_shape), sv.reshape((depth, bs) + kv_shape), ss, sc)
```

```python
import functools
import math

import numpy as np
import jax
import jax.numpy as jnp
from jax import lax
from jax.experimental import pallas as pl
from jax.experimental.pallas import tpu as pltpu

D_MODEL = 1024
CHUNK = 64
ATTN_HEADS = 8
ATTN_KV_HEADS = 2
HEAD_DIM = 64
ATTN_GROUP = ATTN_HEADS // ATTN_KV_HEADS
ATTN_DIM = ATTN_HEADS * HEAD_DIM
KV_DIM = ATTN_KV_HEADS * HEAD_DIM
WINDOW = 128
BAND = WINDOW + CHUNK
KBAND = WINDOW + 2 * CHUNK
GLA_HEADS = 4
GLA_DK = 32
GLA_DV = 64
GLA_KDIM = GLA_HEADS * GLA_DK
GLA_VDIM = GLA_HEADS * GLA_DV
GLA_LOWRANK = 16
GLA_TAU = 16.0
CONV_DIM = 256
CONV_WIDTH = 31
CONV_HIST = CONV_WIDTH - 1
NORM_EPS = 1e-6
NEG_INF = -1e30
PAST_LEN = 4096
LOG2E = math.log2(math.e)

LANES = 128
SUBLANES = 8
CHUNK_SHIFT = CHUNK.bit_length() - 1
GLA_DK_SHIFT = GLA_DK.bit_length() - 1
GLA_DV_SHIFT = GLA_DV.bit_length() - 1
C_AQ, C_AK, C_AV, C_AG = 0, 512, 640, 768
C_GQ, C_GK, C_GV, C_GG = 1280, 1408, 1536, 1792
C_CV, C_CGL, C_CG, C_GLR = 2048, 2304, 2560, 2816
IN_COLS_PACKED = C_GLR + LANES
PROJ_GROUP = 256
CONV_PAD = 32
GLR_SRC = 2048
VEC_LAYOUT = (("norm_gain", D_MODEL), ("q_gain", LANES), ("k_gain", KV_DIM), ("gla_b", GLA_KDIM),
              ("gla_gain", GLA_VDIM), ("conv_b", CONV_DIM), ("ln_gain", CONV_DIM), ("ln_bias", CONV_DIM),
              ("b_pw", CONV_DIM))
VEC_OFFSETS = {name: (sum(n for _, n in VEC_LAYOUT[:k]), VEC_LAYOUT[k][1]) for k, (name, _) in enumerate(VEC_LAYOUT)}

PROMPT_TILE = 512
SAMPLE_SEQS = 4
VMEM_LIMIT_BYTES = 56 * 1024 * 1024

_BF = jnp.bfloat16
_F32 = jnp.float32


def _shape_constants(tile):
    rows = np.arange(ATTN_GROUP * CHUNK)
    dist = np.abs((rows % CHUNK)[:, None] + WINDOW - np.arange(KBAND)[None, :]).astype(np.float64)
    bias = np.stack([
        (2.0 ** (-8.0 * (j * ATTN_GROUP + rows // CHUNK + 1) / ATTN_HEADS))[:, None] * LOG2E * dist
        for j in range(ATTN_KV_HEADS)]).astype(np.float32)
    bias[:, :, BAND:] = -NEG_INF
    variants = []
    for n in range(WINDOW // CHUNK + 1):
        b = bias.copy()
        b[:, :, :max(WINDOW - n * CHUNK, 0)] = -NEG_INF
        variants.append(b)
    bias = np.stack(variants)
    tr = np.arange(tile)
    tri = ((tr[:, None] // CHUNK == tr[None, :] // CHUNK) & (tr[None, :] <= tr[:, None])).astype(np.float32)
    ln = np.arange(LANES) // HEAD_DIM
    ones = (ln[:, None] == ln[None, :]).astype(np.float32)
    triones = np.zeros((tile + LANES, tile), np.float32)
    triones[:tile] = tri
    triones[tile:, :LANES] = ones
    return jnp.asarray(bias), jnp.asarray(triones, dtype=_BF)


def _group_rms_scale(x, group, ones_bd):
    outs = []
    for c0 in range(0, x.shape[1], LANES):
        blk = x[:, c0:c0 + LANES]
        ss = _dot((blk * blk).astype(_BF), ones_bd)
        outs.append(blk * lax.rsqrt(ss * (1.0 / group) + NORM_EPS))
    return outs[0] if len(outs) == 1 else jnp.concatenate(outs, axis=1)


def _silu(x):
    return x * jax.nn.sigmoid(x)


def _dot(a, b):
    return jnp.dot(a, b, preferred_element_type=_F32)


def _dot_nt(a, b):
    return lax.dot_general(a, b, (((1,), (1,)), ((), ())), preferred_element_type=_F32)


def _dot_tn(a, b):
    return lax.dot_general(a, b, (((0,), (0,)), ((), ())), preferred_element_type=_F32)


def _stream_kernel(*refs, tile, n_seq, n_chunk, carry, q_pos0):
    if carry:
        (sinks_ref, x_ref, xnext_ref, *rest) = refs
    else:
        (sinks_ref, x_ref, xnext_ref, hk_all, hv_all, hs_all, hc_all, *rest) = refs
    (w_in_a, w_in_b, w_up_all, w_pw_all, w_out_all, vec_all, convw_all, bias_ref, triones_ref,
     y_ref, nk_all, nv_all, ns_all, nc_all,
     kd_all, vd_all, s_all, u_all, qlo_ref, qhi_ref, ycat_ref, proj_ref, xcur_ref, hb_ref, sbd_ref,
     gate_ref) = rest
    t = pl.program_id(1)
    layer = pl.program_id(2)
    seq_rows = n_chunk * CHUNK
    w_up_ref, w_pw_ref, w_out_ref = (r.at[layer] for r in (w_up_all, w_pw_all, w_out_all))
    convw_ref = convw_all.at[layer]

    def layer_row(name, which=layer):
        off, n = VEC_OFFSETS[name]
        return vec_all[which, :, off:off + n]

    q_gain = layer_row("q_gain") * (LOG2E * HEAD_DIM ** -0.5)
    k_gain = layer_row("k_gain")
    kd_ref, vd_ref, s_ref, u_ref = (r.at[layer] for r in (kd_all, vd_all, s_all, u_all))
    nk_ref, nv_ref, ns_ref, nc_ref = (r.at[layer] for r in (nk_all, nv_all, ns_all, nc_all))
    if not carry:
        hk_ref, hv_ref, hs_ref, hc_ref = (r.at[layer] for r in (hk_all, hv_all, hs_all, hc_all))

    lane128 = lax.broadcasted_iota(jnp.int32, (1, LANES), 1)
    lo_half = lane128 < HEAD_DIM

    def dup_halves(a):
        sw = pltpu.roll(a, HEAD_DIM, 1)
        return jnp.concatenate([jnp.where(lo_half, a, sw), jnp.where(lo_half, sw, a)], axis=1)

    buf_rows = WINDOW + seq_rows + CHUNK
    zeros_blk = jnp.zeros((buf_rows, LANES), _BF)
    ones_blk = jnp.ones((buf_rows, LANES), _BF)
    v_blank = jnp.concatenate([zeros_blk, ones_blk] * ATTN_KV_HEADS, axis=1)

    def store_v(s, row0, vdup_rows):
        for j in range(ATTN_KV_HEADS):
            vd_ref[s, row0:row0 + vdup_rows.shape[0], 2 * j * LANES:(2 * j + 1) * LANES] = (
                vdup_rows[:, j * LANES:(j + 1) * LANES])

    if carry:
        @pl.when(t == 0)
        def _():
            kd_ref[...] = jnp.zeros(kd_ref.shape, _BF)
            for s in range(n_seq):
                vd_ref[s] = v_blank
            s_ref[...] = jnp.zeros(s_ref.shape, _F32)
            u_ref[:, 0:CONV_PAD, :] = jnp.zeros((n_seq, CONV_PAD, CONV_DIM), _F32)
    else:
        for s in range(n_seq):
            kd_ref[s, 0:WINDOW, :] = dup_halves(hk_ref[s]).astype(_BF)
            kd_ref[s, WINDOW + seq_rows:, :] = jnp.zeros((CHUNK, 2 * LANES), _BF)
            vd_ref[s] = v_blank
            store_v(s, 0, dup_halves(hv_ref[s]).astype(_BF))
            sbd_ref[...] = jnp.zeros((GLA_KDIM, GLA_VDIM), _F32)
            for h in range(GLA_HEADS):
                sbd_ref[h * GLA_DK:(h + 1) * GLA_DK, h * GLA_DV:(h + 1) * GLA_DV] = hs_ref[s, h]
            s_ref[s] = sbd_ref[...].T
            u_ref[s, 0:SUBLANES, :] = jnp.zeros((SUBLANES, CONV_DIM), _F32)
            u_ref[s, CONV_PAD - CONV_HIST:CONV_PAD, :] = hc_ref[s]

    def normed_bf16(v, gain_row):
        ms = jnp.mean(v * v, axis=-1, keepdims=True)
        return (v * lax.rsqrt(ms + NORM_EPS) * gain_row).astype(_BF)

    @pl.when(layer == 0)
    def _():
        xcur_ref[...] = x_ref[...]

    @pl.when((pl.program_id(0) == 0) & (t == 0) & (layer == 0))
    def _():
        hb_ref[...] = normed_bf16(x_ref[...], layer_row("norm_gain"))
    x = xcur_ref[...]
    hb = hb_ref[...]

    anchor_on = sinks_ref[layer, ATTN_HEADS] != 0.0

    ones64 = triones_ref[tile:tile + LANES, 0:LANES]

    def post_q(g):
        def post(res):
            for b in range(PROJ_GROUP // LANES):
                c0 = g * PROJ_GROUP + b * LANES
                qn = _group_rms_scale(res[:, b * LANES:(b + 1) * LANES], HEAD_DIM, ones64) * q_gain
                qlo_ref[:, c0:c0 + LANES] = jnp.where(lo_half, qn, 0.0).astype(_BF)
                qhi_ref[:, c0:c0 + LANES] = jnp.where(lo_half, 0.0, qn).astype(_BF)
        return post

    def post_kv(res):
        kn = _group_rms_scale(res[:, 0:KV_DIM], HEAD_DIM, ones64) * k_gain
        vv = res[:, KV_DIM:2 * KV_DIM]
        kdup = dup_halves(kn).astype(_BF)
        vdup = dup_halves(vv).astype(_BF)
        for s in range(n_seq):
            kd_ref[s, WINDOW:WINDOW + seq_rows, :] = kdup[s * seq_rows:(s + 1) * seq_rows]
            store_v(s, WINDOW, vdup[s * seq_rows:(s + 1) * seq_rows])
        if carry:
            nk_ref[0] = kn[tile - WINDOW:tile]
            nv_ref[0] = vv[tile - WINDOW:tile]
        else:
            for s in range(n_seq):
                nk_ref[s, 0:WINDOW - seq_rows, :] = hk_ref[s, seq_rows:WINDOW, :]
                nv_ref[s, 0:WINDOW - seq_rows, :] = hv_ref[s, seq_rows:WINDOW, :]
                nk_ref[s, WINDOW - seq_rows:WINDOW, :] = kn[s * seq_rows:(s + 1) * seq_rows]
                nv_ref[s, WINDOW - seq_rows:WINDOW, :] = vv[s * seq_rows:(s + 1) * seq_rows]

    def post_gate(col0):
        def post(res):
            gate_ref[:, col0:col0 + PROJ_GROUP] = _silu(res)
        return post

    group_post = {C_AG // PROJ_GROUP: post_gate(0), C_AG // PROJ_GROUP + 1: post_gate(PROJ_GROUP),
                  C_GG // PROJ_GROUP: post_gate(ATTN_DIM), C_CG // PROJ_GROUP: post_gate(ATTN_DIM + GLA_VDIM)}

    def w_in_cols(c0, c1):
        if c1 <= C_CV:
            return w_in_a[layer, :, c0:c1]
        if c1 <= C_GLR:
            return w_in_b[layer, :, c0 - C_CV:c1 - C_CV]
        return w_in_a[layer, :, GLR_SRC:GLR_SRC + (c1 - c0)]

    def compute_group(g):
        c0, c1 = g * PROJ_GROUP, min((g + 1) * PROJ_GROUP, IN_COLS_PACKED)
        res = _dot(hb, w_in_cols(c0, c1))
        if g in group_post:
            group_post[g](res)
        else:
            proj_ref[:, c0:c1] = res
        return jnp.where(anchor_on, res[0:1, 0:CONV_DIM], 0.0)

    def proj(c0, width):
        return proj_ref[:, c0:c0 + width]

    compute_group(C_CV // PROJ_GROUP)
    compute_group(C_CGL // PROJ_GROUP)
    u = proj(C_CV, CONV_DIM) * jax.nn.sigmoid(proj(C_CGL, CONV_DIM))
    for s in range(n_seq):
        u_ref[s, CONV_PAD:CONV_PAD + seq_rows, :] = u[s * seq_rows:(s + 1) * seq_rows]
    conv_acc = [jnp.broadcast_to(layer_row("conv_b"), (seq_rows, CONV_DIM))] * n_seq
    for rho in range(SUBLANES):
        anchor = compute_group(rho)
        for s in range(n_seq):
            frame_rows = seq_rows + (SUBLANES if rho else 0)
            frame = None
            for j in range(CONV_WIDTH):
                off = CONV_PAD - CONV_HIST + j
                if off % SUBLANES != rho:
                    continue
                term = (convw_ref[j:j + 1, :] + anchor) * u_ref[s, off - rho:off - rho + frame_rows, :]
                frame = term if frame is None else frame + term
            conv_acc[s] = conv_acc[s] + frame[rho:rho + seq_rows]
    compute_group(C_CG // PROJ_GROUP)
    compute_group(C_GLR // PROJ_GROUP)
    cc = conv_acc[0] if n_seq == 1 else jnp.concatenate(conv_acc, axis=0)

    for s in range(n_seq):
        nc_ref[s] = u_ref[s, seq_rows + CONV_PAD - CONV_HIST:seq_rows + CONV_PAD, :]
    if carry:
        u_ref[:, 0:CONV_PAD, :] = u_ref[:, seq_rows:seq_rows + CONV_PAD, :]

    mu = jnp.mean(cc, axis=-1, keepdims=True)
    cen = cc - mu
    var = jnp.mean(cen * cen, axis=-1, keepdims=True)
    ln = cen * lax.rsqrt(var + NORM_EPS) * layer_row("ln_gain") + layer_row("ln_bias")
    cpw = _dot(_silu(ln).astype(_BF), w_pw_ref[...]) + layer_row("b_pw")
    ycat_ref[:, ATTN_DIM + GLA_VDIM:D_MODEL] = cpw * gate_ref[:, ATTN_DIM + GLA_VDIM:D_MODEL]

    for g in range(ATTN_DIM // PROJ_GROUP):
        post_q(g)(proj(C_AQ + g * PROJ_GROUP, PROJ_GROUP))
    post_kv(proj(C_AK, 2 * KV_DIM))

    rows = ATTN_GROUP * CHUNK
    r_head1 = lax.broadcasted_iota(jnp.int32, (rows, 1), 0) >> CHUNK_SHIFT
    sink_cols = []
    for j in range(ATTN_KV_HEADS):
        sink = jnp.zeros((rows, 1), _F32)
        for r in range(ATTN_GROUP):
            sink = jnp.where(r_head1 == r, sinks_ref[layer, j * ATTN_GROUP + r], sink)
        sink_cols.append(sink * LOG2E)

    for s in range(n_seq):
        for c in range(n_chunk):
            r0 = s * seq_rows + c * CHUNK
            k0 = c * CHUNK
            steady = WINDOW // CHUNK
            variant = jnp.minimum(t * n_chunk + c, steady) if carry else steady
            for j in range(ATTN_KV_HEADS):
                qs = jnp.concatenate(
                    [qlo_ref[r0:r0 + CHUNK, (2 * j) * LANES:(2 * j + 1) * LANES],
                     qhi_ref[r0:r0 + CHUNK, (2 * j) * LANES:(2 * j + 1) * LANES],
                     qlo_ref[r0:r0 + CHUNK, (2 * j + 1) * LANES:(2 * j + 2) * LANES],
                     qhi_ref[r0:r0 + CHUNK, (2 * j + 1) * LANES:(2 * j + 2) * LANES]], axis=0)
                kb = kd_ref[s, k0:k0 + KBAND, j * LANES:(j + 1) * LANES]
                vb = vd_ref[s, k0:k0 + KBAND, 2 * j * LANES:(2 * j + 2) * LANES]
                sc = _dot_nt(qs, kb) - bias_ref[variant, j]
                sink = sink_cols[j]
                m = jnp.maximum(jnp.max(sc, axis=-1, keepdims=True), sink)
                p = jnp.exp2(sc - m)
                pv = _dot(p.astype(_BF), vb)
                o = pv[:, 0:LANES] / (pv[:, LANES:2 * LANES] + jnp.exp2(sink - m))
                for pb in range(2):
                    blk = 2 * j + pb
                    ob = jnp.where(lo_half, o[(2 * pb) * CHUNK:(2 * pb + 1) * CHUNK],
                                   o[(2 * pb + 1) * CHUNK:(2 * pb + 2) * CHUNK])
                    ycat_ref[r0:r0 + CHUNK, blk * LANES:(blk + 1) * LANES] = (
                        ob * gate_ref[r0:r0 + CHUNK, blk * LANES:(blk + 1) * LANES])

    if carry:
        kd_ref[:, 0:WINDOW, :] = kd_ref[:, seq_rows:seq_rows + WINDOW, :]
        vd_ref[:, 0:WINDOW, :] = vd_ref[:, seq_rows:seq_rows + WINDOW, :]

    z = _dot(proj(C_GLR, GLA_LOWRANK).astype(_BF), w_up_ref[...].astype(_BF)) + layer_row("gla_b")
    log_a = (jnp.minimum(z, 0.0) - jnp.log(1.0 + jnp.exp(-jnp.abs(z)))) * (1.0 / GLA_TAU)
    tri = triones_ref[0:tile, :]
    la_hi = log_a.astype(_BF)
    la_lo = (log_a - la_hi.astype(_F32)).astype(_BF)
    bcum = _dot(tri, la_hi) + _dot(tri, la_lo)
    gq = proj(C_GQ, GLA_KDIM) * GLA_DK ** -0.5
    gk = proj(C_GK, GLA_KDIM)
    gv = proj(C_GV, GLA_VDIM)
    gate_g = gate_ref[:, ATTN_DIM:ATTN_DIM + GLA_VDIM]
    gla_g = layer_row("gla_gain")

    head_of_lane = lane128 >> GLA_DK_SHIFT
    a_r = lax.broadcasted_iota(jnp.int32, (CHUNK, GLA_HEADS * CHUNK), 0)
    a_c = lax.broadcasted_iota(jnp.int32, (CHUNK, GLA_HEADS * CHUNK), 1)
    causal = (a_c & (CHUNK - 1)) <= a_r
    vblk = lax.broadcasted_iota(jnp.int32, (1, GLA_VDIM), 1) >> GLA_DV_SHIFT
    bd_mask_t = ((lax.broadcasted_iota(jnp.int32, (GLA_VDIM, GLA_KDIM), 0) >> GLA_DV_SHIFT)
                 == (lax.broadcasted_iota(jnp.int32, (GLA_VDIM, GLA_KDIM), 1) >> GLA_DK_SHIFT))

    q_all = (gq * jnp.exp(bcum)).astype(_BF)
    k_all = gk * jnp.exp(-bcum)
    intra, kv_upd, decay_rows = [], [], []
    for s in range(n_seq):
        for c in range(n_chunk):
            r0 = s * seq_rows + c * CHUNK
            bc = bcum[r0:r0 + CHUNK]
            b_last = bc[CHUNK - 1:CHUNK]
            k_end = (gk[r0:r0 + CHUNK] * jnp.exp(b_last - bc)).astype(_BF)
            vf = gv[r0:r0 + CHUNK]
            k_stack = jnp.concatenate(
                [jnp.where(head_of_lane == h, k_all[r0:r0 + CHUNK], 0.0) for h in range(GLA_HEADS)],
                axis=0).astype(_BF)
            v_bd = jnp.concatenate(
                [jnp.where(vblk == h, vf, 0.0) for h in range(GLA_HEADS)], axis=0).astype(_BF)
            a = jnp.where(causal, _dot_nt(q_all[r0:r0 + CHUNK], k_stack), 0.0).astype(_BF)
            intra.append(_dot(a, v_bd))
            kv_upd.append(jnp.where(bd_mask_t, _dot_tn(vf.astype(_BF), k_end), 0.0))
            decay_rows.append(jnp.exp(b_last))
    o_rows = []
    for s in range(n_seq):
        state = s_ref[s]
        for c in range(n_chunk):
            i = s * n_chunk + c
            r0 = i * CHUNK
            o_rows.append(intra[i] + _dot_nt(q_all[r0:r0 + CHUNK], state.astype(_BF)))
            state = decay_rows[i] * state + kv_upd[i]
        s_ref[s] = state
    o_all = jnp.concatenate(o_rows, axis=0)
    ycat_ref[:, ATTN_DIM:ATTN_DIM + GLA_VDIM] = _group_rms_scale(o_all, GLA_DV, ones64) * gla_g * gate_g

    for s in range(n_seq):
        sbd_ref[...] = s_ref[s].T
        for h in range(GLA_HEADS):
            ns_ref[s, h] = sbd_ref[h * GLA_DK:(h + 1) * GLA_DK, h * GLA_DV:(h + 1) * GLA_DV]

    y = x
    for c0, c1 in ((ATTN_DIM + GLA_VDIM, D_MODEL), (0, ATTN_DIM), (ATTN_DIM, ATTN_DIM + GLA_VDIM)):
        y = y + _dot(ycat_ref[:, c0:c1].astype(_BF), w_out_ref[c0:c1, :])
    y_ref[...] = y
    xcur_ref[...] = y
    last_layer = layer == pl.num_programs(2) - 1
    nxt = jnp.where(last_layer, xnext_ref[...], y)
    next_gain = layer_row("norm_gain", jnp.where(last_layer, 0, layer + 1))
    hb_ref[...] = normed_bf16(nxt, next_gain)


def _run_stream(x2d, n_seqs, seq_len, hist, sinks, params, *, carry, q_pos0):
    depth = sinks.shape[0]
    if carry:
        assert q_pos0 == 0
        tile, n_seq, n_chunk = PROMPT_TILE, 1, PROMPT_TILE // CHUNK
        grid = (n_seqs, seq_len // tile, depth)
    else:
        assert seq_len == CHUNK and q_pos0 >= WINDOW
        tile, n_seq, n_chunk = SAMPLE_SEQS * CHUNK, SAMPLE_SEQS, 1
        grid = (n_seqs // n_seq, 1, depth)
    n_t = grid[1]
    seq_rows = n_chunk * CHUNK

    def const(shape):
        return pl.BlockSpec(shape, lambda g, t, l: (0,) * len(shape), pipeline_mode=pl.Buffered(1))

    def per_layer_group(shape):
        return pl.BlockSpec((depth, n_seq) + shape, lambda g, t, l: (0, g) + (0,) * len(shape))

    state_shapes = ((WINDOW, KV_DIM), (WINDOW, KV_DIM), (GLA_HEADS, GLA_DK, GLA_DV), (CONV_HIST, CONV_DIM))
    x_spec = pl.BlockSpec((tile, D_MODEL), lambda g, t, l: (g * n_t + t, 0))
    n_tiles = x2d.shape[0] // tile
    xnext_spec = pl.BlockSpec((tile, D_MODEL), lambda g, t, l: (jnp.minimum(g * n_t + t + 1, n_tiles - 1), 0))
    in_specs = [pl.BlockSpec(memory_space=pltpu.SMEM), x_spec, xnext_spec]
    args = [sinks, x2d, x2d]
    if not carry:
        in_specs += [per_layer_group(sh) for sh in state_shapes]
        args += list(hist)
    operands = tuple(params) + _shape_constants(tile)
    in_specs += [const(p.shape) for p in operands]
    args += list(operands)

    out_shape = (jax.ShapeDtypeStruct(x2d.shape, _F32),) + tuple(
        jax.ShapeDtypeStruct((depth, n_seqs) + sh, _F32) for sh in state_shapes)
    out_specs = (x_spec,) + tuple(per_layer_group(sh) for sh in state_shapes)
    scratch = [pltpu.VMEM((depth, n_seq, WINDOW + seq_rows + CHUNK, ATTN_KV_HEADS * LANES), _BF),
               pltpu.VMEM((depth, n_seq, WINDOW + seq_rows + CHUNK, ATTN_KV_HEADS * 2 * LANES), _BF),
               pltpu.VMEM((depth, n_seq, GLA_VDIM, GLA_KDIM), _F32),
               pltpu.VMEM((depth, n_seq, CONV_PAD + seq_rows, CONV_DIM), _F32),
               pltpu.VMEM((tile, ATTN_DIM), _BF),
               pltpu.VMEM((tile, ATTN_DIM), _BF),
               pltpu.VMEM((tile, D_MODEL), _F32),
               pltpu.VMEM((tile, IN_COLS_PACKED), _F32),
               pltpu.VMEM((tile, D_MODEL), _F32),
               pltpu.VMEM((tile, D_MODEL), _BF),
               pltpu.VMEM((GLA_KDIM, GLA_VDIM), _F32),
               pltpu.VMEM((tile, D_MODEL), _F32)]
    body = functools.partial(_stream_kernel, tile=tile, n_seq=n_seq, n_chunk=n_chunk, carry=carry, q_pos0=q_pos0)
    return pl.pallas_call(
        body, grid=grid, in_specs=in_specs, out_specs=out_specs, out_shape=out_shape,
        scratch_shapes=scratch,
        compiler_params=pltpu.CompilerParams(dimension_semantics=("arbitrary", "arbitrary", "arbitrary"),
                                             vmem_limit_bytes=VMEM_LIMIT_BYTES),
        name="mixer_prompt" if carry else "mixer_sample",
    )(*args)


def _pack_params(norm_gain, w_in, q_norm_gain, k_norm_gain, gla_w_gate_up, gla_b_gate, gla_norm_gain,
                 conv_w, conv_b, conv_ln_gain, conv_ln_bias, conv_w_pw, conv_b_pw, w_out):
    w_in_a = w_in.astype(_BF)
    w_in_b = w_in[:, :, GLR_SRC + GLA_LOWRANK:].astype(_BF)
    pieces = dict(norm_gain=norm_gain, q_gain=q_norm_gain, k_gain=k_norm_gain, gla_b=gla_b_gate,
                  gla_gain=gla_norm_gain, conv_b=conv_b, ln_gain=conv_ln_gain, ln_bias=conv_ln_bias, b_pw=conv_b_pw)
    vecs = jnp.concatenate([jnp.tile(pieces[name].astype(_F32), (1, n // pieces[name].shape[1]))
                            for name, n in VEC_LAYOUT], axis=1)[:, None, :]
    return w_in_a, w_in_b, gla_w_gate_up, conv_w_pw.astype(_BF), w_out.astype(_BF), vecs, conv_w


def kernel(x_prompt, x_sample, cache_k, cache_v, state_gla, state_conv, norm_gain, w_in, q_norm_gain, k_norm_gain, attn_sinks, gla_w_gate_up, gla_b_gate, gla_norm_gain, conv_w, conv_b, conv_ln_gain, conv_ln_bias, conv_w_pw, conv_b_pw, w_out):
    depth = w_in.shape[0]
    bp, lp, _ = x_prompt.shape
    bs, ls, _ = x_sample.shape
    params = _pack_params(norm_gain, w_in, q_norm_gain, k_norm_gain, gla_w_gate_up, gla_b_gate, gla_norm_gain,
                          conv_w, conv_b, conv_ln_gain, conv_ln_bias, conv_w_pw, conv_b_pw, w_out)
    sinks = jnp.pad(attn_sinks.astype(_F32), ((0, 0), (0, ATTN_HEADS)))
    yp, pk, pv, ps, pc = _run_stream(x_prompt.reshape(bp * lp, D_MODEL), bp, lp, None, sinks, params,
                                     carry=True, q_pos0=0)
    hist = (cache_k.reshape(depth, bs, WINDOW, KV_DIM), cache_v.reshape(depth, bs, WINDOW, KV_DIM),
            state_gla, state_conv)
    ys, sk, sv, ss, sc = _run_stream(x_sample.reshape(bs * ls, D_MODEL), bs, ls, hist, sinks, params,
                                     carry=False, q_pos0=PAST_LEN)
    kv_shape = (WINDOW, ATTN_KV_HEADS, HEAD_DIM)
    return (yp.reshape(bp, lp, D_MODEL), ys.reshape(bs, ls, D_MODEL),
            pk.reshape((depth, bp) + kv_shape), pv.reshape((depth, bp) + kv_shape), ps, pc,
            sk.reshape((depth, bs) + kv_shape), sv.reshape((depth, bs) + kv_shape), ss, sc)
```

```python
import functools
import math

import numpy as np
import jax
import jax.numpy as jnp
from jax import lax
from jax.experimental import pallas as pl
from jax.experimental.pallas import tpu as pltpu

D_MODEL = 1024
CHUNK = 64
ATTN_HEADS = 8
ATTN_KV_HEADS = 2
HEAD_DIM = 64
ATTN_GROUP = ATTN_HEADS // ATTN_KV_HEADS
ATTN_DIM = ATTN_HEADS * HEAD_DIM
KV_DIM = ATTN_KV_HEADS * HEAD_DIM
WINDOW = 128
BAND = WINDOW + CHUNK
KBAND = WINDOW + 2 * CHUNK
GLA_HEADS = 4
GLA_DK = 32
GLA_DV = 64
GLA_KDIM = GLA_HEADS * GLA_DK
GLA_VDIM = GLA_HEADS * GLA_DV
GLA_LOWRANK = 16
GLA_TAU = 16.0
CONV_DIM = 256
CONV_WIDTH = 31
CONV_HIST = CONV_WIDTH - 1
NORM_EPS = 1e-6
NEG_INF = -1e30
PAST_LEN = 4096
LOG2E = math.log2(math.e)

LANES = 128
SUBLANES = 8
CHUNK_SHIFT = CHUNK.bit_length() - 1
GLA_DK_SHIFT = GLA_DK.bit_length() - 1
GLA_DV_SHIFT = GLA_DV.bit_length() - 1
C_AQ, C_AK, C_AV, C_AG = 0, 512, 640, 768
C_GQ, C_GK, C_GV, C_GG = 1280, 1408, 1536, 1792
C_CV, C_CGL, C_CG, C_GLR = 2048, 2304, 2560, 2816
IN_COLS_PACKED = C_GLR + LANES
PROJ_GROUP = 256
CONV_PAD = 32
GLR_SRC = 2048
VEC_LAYOUT = (("norm_gain", D_MODEL), ("q_gain", LANES), ("k_gain", KV_DIM), ("gla_b", GLA_KDIM),
              ("gla_gain", GLA_VDIM), ("conv_b", CONV_DIM), ("ln_gain", CONV_DIM), ("ln_bias", CONV_DIM),
              ("b_pw", CONV_DIM))
VEC_OFFSETS = {name: (sum(n for _, n in VEC_LAYOUT[:k]), VEC_LAYOUT[k][1]) for k, (name, _) in enumerate(VEC_LAYOUT)}

PROMPT_TILE = 512
SAMPLE_SEQS = 4
VMEM_LIMIT_BYTES = 56 * 1024 * 1024

_BF = jnp.bfloat16
_F32 = jnp.float32


def _shape_constants(tile):
    rows = np.arange(ATTN_GROUP * CHUNK)
    dist = np.abs((rows % CHUNK)[:, None] + WINDOW - np.arange(KBAND)[None, :]).astype(np.float64)
    bias = np.stack([
        (2.0 ** (-8.0 * (j * ATTN_GROUP + rows // CHUNK + 1) / ATTN_HEADS))[:, None] * LOG2E * dist
        for j in range(ATTN_KV_HEADS)]).astype(np.float32)
    bias[:, :, BAND:] = -NEG_INF
    variants = []
    for n in range(WINDOW // CHUNK + 1):
        b = bias.copy()
        b[:, :, :max(WINDOW - n * CHUNK, 0)] = -NEG_INF
        variants.append(b)
    bias = np.stack(variants)
    tr = np.arange(tile)
    tri = ((tr[:, None] // CHUNK == tr[None, :] // CHUNK) & (tr[None, :] <= tr[:, None])).astype(np.float32)
    ln = np.arange(LANES) // HEAD_DIM
    ones = (ln[:, None] == ln[None, :]).astype(np.float32)
    triones = np.zeros((tile + LANES, tile), np.float32)
    triones[:tile] = tri
    triones[tile:, :LANES] = ones
    return jnp.asarray(bias), jnp.asarray(triones, dtype=_BF)


def _group_rms_scale(x, group, ones_bd):
    outs = []
    for c0 in range(0, x.shape[1], LANES):
        blk = x[:, c0:c0 + LANES]
        ss = _dot((blk * blk).astype(_BF), ones_bd)
        outs.append(blk * lax.rsqrt(ss * (1.0 / group) + NORM_EPS))
    return outs[0] if len(outs) == 1 else jnp.concatenate(outs, axis=1)


def _silu(x):
    return x * jax.nn.sigmoid(x)


def _dot(a, b):
    return jnp.dot(a, b, preferred_element_type=_F32)


def _dot_nt(a, b):
    return lax.dot_general(a, b, (((1,), (1,)), ((), ())), preferred_element_type=_F32)


def _dot_tn(a, b):
    return lax.dot_general(a, b, (((0,), (0,)), ((), ())), preferred_element_type=_F32)


def _stream_kernel(*refs, tile, n_seq, n_chunk, carry, q_pos0):
    if carry:
        (sinks_ref, x_ref, xnext_ref, *rest) = refs
    else:
        (sinks_ref, x_ref, xnext_ref, hk_all, hv_all, hs_all, hc_all, *rest) = refs
    (w_in_a, w_in_b, w_in_c, w_up_all, w_pw_all, w_out_all, vec_all, convw_all, bias_ref, triones_ref,
     y_ref, nk_all, nv_all, ns_all, nc_all,
     kd_all, vd_all, s_all, u_all, qlo_ref, qhi_ref, ycat_ref, proj_ref, xcur_ref, hb_ref, sbd_ref,
     gate_ref) = rest
    t = pl.program_id(1)
    layer = pl.program_id(2)
    seq_rows = n_chunk * CHUNK
    w_up_ref, w_pw_ref, w_out_ref = (r.at[layer] for r in (w_up_all, w_pw_all, w_out_all))
    convw_ref = convw_all.at[layer]

    def layer_row(name, which=layer):
        off, n = VEC_OFFSETS[name]
        return vec_all[which, :, off:off + n]

    q_gain = layer_row("q_gain") * (LOG2E * HEAD_DIM ** -0.5)
    k_gain = layer_row("k_gain")
    kd_ref, vd_ref, s_ref, u_ref = (r.at[layer] for r in (kd_all, vd_all, s_all, u_all))
    nk_ref, nv_ref, ns_ref, nc_ref = (r.at[layer] for r in (nk_all, nv_all, ns_all, nc_all))
    if not carry:
        hk_ref, hv_ref, hs_ref, hc_ref = (r.at[layer] for r in (hk_all, hv_all, hs_all, hc_all))

    lane128 = lax.broadcasted_iota(jnp.int32, (1, LANES), 1)
    lo_half = lane128 < HEAD_DIM

    def dup_halves(a):
        sw = pltpu.roll(a, HEAD_DIM, 1)
        return jnp.concatenate([jnp.where(lo_half, a, sw), jnp.where(lo_half, sw, a)], axis=1)

    buf_rows = WINDOW + seq_rows + CHUNK
    zeros_blk = jnp.zeros((buf_rows, LANES), _BF)
    ones_blk = jnp.ones((buf_rows, LANES), _BF)
    v_blank = jnp.concatenate([zeros_blk, ones_blk] * ATTN_KV_HEADS, axis=1)

    def store_v(s, row0, vdup_rows):
        for j in range(ATTN_KV_HEADS):
            vd_ref[s, row0:row0 + vdup_rows.shape[0], 2 * j * LANES:(2 * j + 1) * LANES] = (
                vdup_rows[:, j * LANES:(j + 1) * LANES])

    if carry:
        @pl.when(t == 0)
        def _():
            kd_ref[...] = jnp.zeros(kd_ref.shape, _BF)
            for s in range(n_seq):
                vd_ref[s] = v_blank
            s_ref[...] = jnp.zeros(s_ref.shape, _F32)
            u_ref[:, 0:CONV_PAD, :] = jnp.zeros((n_seq, CONV_PAD, CONV_DIM), _F32)
    else:
        for s in range(n_seq):
            kd_ref[s, 0:WINDOW, :] = dup_halves(hk_ref[s]).astype(_BF)
            kd_ref[s, WINDOW + seq_rows:, :] = jnp.zeros((CHUNK, 2 * LANES), _BF)
            vd_ref[s] = v_blank
            store_v(s, 0, dup_halves(hv_ref[s]).astype(_BF))
            sbd_ref[...] = jnp.zeros((GLA_KDIM, GLA_VDIM), _F32)
            for h in range(GLA_HEADS):
                sbd_ref[h * GLA_DK:(h + 1) * GLA_DK, h * GLA_DV:(h + 1) * GLA_DV] = hs_ref[s, h]
            s_ref[s] = sbd_ref[...].T
            u_ref[s, 0:SUBLANES, :] = jnp.zeros((SUBLANES, CONV_DIM), _F32)
            u_ref[s, CONV_PAD - CONV_HIST:CONV_PAD, :] = hc_ref[s]

    def normed_bf16(v, gain_row):
        ms = jnp.mean(v * v, axis=-1, keepdims=True)
        return (v * lax.rsqrt(ms + NORM_EPS) * gain_row).astype(_BF)

    @pl.when(layer == 0)
    def _():
        xcur_ref[...] = x_ref[...]

    @pl.when((pl.program_id(0) == 0) & (t == 0) & (layer == 0))
    def _():
        hb_ref[...] = normed_bf16(x_ref[...], layer_row("norm_gain"))
    x = xcur_ref[...]
    hb = hb_ref[...]

    anchor_on = sinks_ref[layer, ATTN_HEADS] != 0.0

    ones64 = triones_ref[tile:tile + LANES, 0:LANES]

    def post_q(g):
        def post(res):
            for b in range(PROJ_GROUP // LANES):
                c0 = g * PROJ_GROUP + b * LANES
                qn = _group_rms_scale(res[:, b * LANES:(b + 1) * LANES], HEAD_DIM, ones64) * q_gain
                qlo_ref[:, c0:c0 + LANES] = jnp.where(lo_half, qn, 0.0).astype(_BF)
                qhi_ref[:, c0:c0 + LANES] = jnp.where(lo_half, 0.0, qn).astype(_BF)
        return post

    def post_kv(res):
        kn = _group_rms_scale(res[:, 0:KV_DIM], HEAD_DIM, ones64) * k_gain
        vv = res[:, KV_DIM:2 * KV_DIM]
        kdup = dup_halves(kn).astype(_BF)
        vdup = dup_halves(vv).astype(_BF)
        for s in range(n_seq):
            kd_ref[s, WINDOW:WINDOW + seq_rows, :] = kdup[s * seq_rows:(s + 1) * seq_rows]
            store_v(s, WINDOW, vdup[s * seq_rows:(s + 1) * seq_rows])
        if carry:
            nk_ref[0] = kn[tile - WINDOW:tile]
            nv_ref[0] = vv[tile - WINDOW:tile]
        else:
            for s in range(n_seq):
                nk_ref[s, 0:WINDOW - seq_rows, :] = hk_ref[s, seq_rows:WINDOW, :]
                nv_ref[s, 0:WINDOW - seq_rows, :] = hv_ref[s, seq_rows:WINDOW, :]
                nk_ref[s, WINDOW - seq_rows:WINDOW, :] = kn[s * seq_rows:(s + 1) * seq_rows]
                nv_ref[s, WINDOW - seq_rows:WINDOW, :] = vv[s * seq_rows:(s + 1) * seq_rows]

    def post_gate(col0):
        def post(res):
            gate_ref[:, col0:col0 + PROJ_GROUP] = _silu(res)
        return post

    group_post = {C_AG // PROJ_GROUP: post_gate(0), C_AG // PROJ_GROUP + 1: post_gate(PROJ_GROUP),
                  C_GG // PROJ_GROUP: post_gate(ATTN_DIM), C_CG // PROJ_GROUP: post_gate(ATTN_DIM + GLA_VDIM)}

    def w_in_cols(c0, c1):
        if c1 <= C_CV:
            return w_in_a[layer, :, c0:c1]
        if c1 <= C_GLR:
            return w_in_b[layer, :, c0 - C_CV:c1 - C_CV]
        return w_in_c[layer, :, c0 - C_GLR:c1 - C_GLR]

    def compute_group(g):
        c0, c1 = g * PROJ_GROUP, min((g + 1) * PROJ_GROUP, IN_COLS_PACKED)
        res = _dot(hb, w_in_cols(c0, c1))
        if g in group_post:
            group_post[g](res)
        else:
            proj_ref[:, c0:c1] = res
        return jnp.where(anchor_on, res[0:1, 0:CONV_DIM], 0.0)

    def proj(c0, width):
        return proj_ref[:, c0:c0 + width]

    compute_group(C_CV // PROJ_GROUP)
    compute_group(C_CGL // PROJ_GROUP)
    u = proj(C_CV, CONV_DIM) * jax.nn.sigmoid(proj(C_CGL, CONV_DIM))
    for s in range(n_seq):
        u_ref[s, CONV_PAD:CONV_PAD + seq_rows, :] = u[s * seq_rows:(s + 1) * seq_rows]
    conv_acc = [jnp.broadcast_to(layer_row("conv_b"), (seq_rows, CONV_DIM))] * n_seq
    for rho in range(SUBLANES):
        anchor = compute_group(rho)
        for s in range(n_seq):
            frame_rows = seq_rows + (SUBLANES if rho else 0)
            frame = None
            for j in range(CONV_WIDTH):
                off = CONV_PAD - CONV_HIST + j
                if off % SUBLANES != rho:
                    continue
                term = (convw_ref[j:j + 1, :] + anchor) * u_ref[s, off - rho:off - rho + frame_rows, :]
                frame = term if frame is None else frame + term
            conv_acc[s] = conv_acc[s] + frame[rho:rho + seq_rows]
    compute_group(C_CG // PROJ_GROUP)
    compute_group(C_GLR // PROJ_GROUP)
    cc = conv_acc[0] if n_seq == 1 else jnp.concatenate(conv_acc, axis=0)

    for s in range(n_seq):
        nc_ref[s] = u_ref[s, seq_rows + CONV_PAD - CONV_HIST:seq_rows + CONV_PAD, :]
    if carry:
        u_ref[:, 0:CONV_PAD, :] = u_ref[:, seq_rows:seq_rows + CONV_PAD, :]

    mu = jnp.mean(cc, axis=-1, keepdims=True)
    cen = cc - mu
    var = jnp.mean(cen * cen, axis=-1, keepdims=True)
    ln = cen * lax.rsqrt(var + NORM_EPS) * layer_row("ln_gain") + layer_row("ln_bias")
    cpw = _dot(_silu(ln).astype(_BF), w_pw_ref[...]) + layer_row("b_pw")
    ycat_ref[:, ATTN_DIM + GLA_VDIM:D_MODEL] = cpw * gate_ref[:, ATTN_DIM + GLA_VDIM:D_MODEL]

    for g in range(ATTN_DIM // PROJ_GROUP):
        post_q(g)(proj(C_AQ + g * PROJ_GROUP, PROJ_GROUP))
    post_kv(proj(C_AK, 2 * KV_DIM))

    rows = ATTN_GROUP * CHUNK
    r_head1 = lax.broadcasted_iota(jnp.int32, (rows, 1), 0) >> CHUNK_SHIFT
    sink_cols = []
    for j in range(ATTN_KV_HEADS):
        sink = jnp.zeros((rows, 1), _F32)
        for r in range(ATTN_GROUP):
            sink = jnp.where(r_head1 == r, sinks_ref[layer, j * ATTN_GROUP + r], sink)
        sink_cols.append(sink * LOG2E)

    for s in range(n_seq):
        for c in range(n_chunk):
            r0 = s * seq_rows + c * CHUNK
            k0 = c * CHUNK
            steady = WINDOW // CHUNK
            variant = jnp.minimum(t * n_chunk + c, steady) if carry else steady
            for j in range(ATTN_KV_HEADS):
                qs = jnp.concatenate(
                    [qlo_ref[r0:r0 + CHUNK, (2 * j) * LANES:(2 * j + 1) * LANES],
                     qhi_ref[r0:r0 + CHUNK, (2 * j) * LANES:(2 * j + 1) * LANES],
                     qlo_ref[r0:r0 + CHUNK, (2 * j + 1) * LANES:(2 * j + 2) * LANES],
                     qhi_ref[r0:r0 + CHUNK, (2 * j + 1) * LANES:(2 * j + 2) * LANES]], axis=0)
                kb = kd_ref[s, k0:k0 + KBAND, j * LANES:(j + 1) * LANES]
                vb = vd_ref[s, k0:k0 + KBAND, 2 * j * LANES:(2 * j + 2) * LANES]
                sc = _dot_nt(qs, kb) - bias_ref[variant, j]
                sink = sink_cols[j]
                m = jnp.maximum(jnp.max(sc, axis=-1, keepdims=True), sink)
                p = jnp.exp2(sc - m)
                pv = _dot(p.astype(_BF), vb)
                o = pv[:, 0:LANES] / (pv[:, LANES:2 * LANES] + jnp.exp2(sink - m))
                for pb in range(2):
                    blk = 2 * j + pb
                    ob = jnp.where(lo_half, o[(2 * pb) * CHUNK:(2 * pb + 1) * CHUNK],
                                   o[(2 * pb + 1) * CHUNK:(2 * pb + 2) * CHUNK])
                    ycat_ref[r0:r0 + CHUNK, blk * LANES:(blk + 1) * LANES] = (
                        ob * gate_ref[r0:r0 + CHUNK, blk * LANES:(blk + 1) * LANES])

    if carry:
        kd_ref[:, 0:WINDOW, :] = kd_ref[:, seq_rows:seq_rows + WINDOW, :]
        vd_ref[:, 0:WINDOW, :] = vd_ref[:, seq_rows:seq_rows + WINDOW, :]

    w_up = jnp.concatenate([w_up_ref[...], jnp.zeros((LANES - GLA_LOWRANK, GLA_KDIM), _F32)], axis=0).astype(_BF)
    z = _dot(proj(C_GLR, LANES).astype(_BF), w_up) + layer_row("gla_b")
    log_a = (jnp.minimum(z, 0.0) - jnp.log(1.0 + jnp.exp(-jnp.abs(z)))) * (1.0 / GLA_TAU)
    tri = triones_ref[0:tile, :]
    la_hi = log_a.astype(_BF)
    la_lo = (log_a - la_hi.astype(_F32)).astype(_BF)
    bcum = _dot(tri, la_hi) + _dot(tri, la_lo)
    gq = proj(C_GQ, GLA_KDIM) * GLA_DK ** -0.5
    gk = proj(C_GK, GLA_KDIM)
    gv = proj(C_GV, GLA_VDIM)
    gate_g = gate_ref[:, ATTN_DIM:ATTN_DIM + GLA_VDIM]
    gla_g = layer_row("gla_gain")

    head_of_lane = lane128 >> GLA_DK_SHIFT
    a_r = lax.broadcasted_iota(jnp.int32, (CHUNK, GLA_HEADS * CHUNK), 0)
    a_c = lax.broadcasted_iota(jnp.int32, (CHUNK, GLA_HEADS * CHUNK), 1)
    causal = (a_c & (CHUNK - 1)) <= a_r
    vblk = lax.broadcasted_iota(jnp.int32, (1, GLA_VDIM), 1) >> GLA_DV_SHIFT
    bd_mask_t = ((lax.broadcasted_iota(jnp.int32, (GLA_VDIM, GLA_KDIM), 0) >> GLA_DV_SHIFT)
                 == (lax.broadcasted_iota(jnp.int32, (GLA_VDIM, GLA_KDIM), 1) >> GLA_DK_SHIFT))

    q_all = (gq * jnp.exp(bcum)).astype(_BF)
    k_all = gk * jnp.exp(-bcum)
    intra, kv_upd, decay_rows = [], [], []
    for s in range(n_seq):
        for c in range(n_chunk):
            r0 = s * seq_rows + c * CHUNK
            bc = bcum[r0:r0 + CHUNK]
            b_last = bc[CHUNK - 1:CHUNK]
            k_end = (gk[r0:r0 + CHUNK] * jnp.exp(b_last - bc)).astype(_BF)
            vf = gv[r0:r0 + CHUNK]
            k_stack = jnp.concatenate(
                [jnp.where(head_of_lane == h, k_all[r0:r0 + CHUNK], 0.0) for h in range(GLA_HEADS)],
                axis=0).astype(_BF)
            v_bd = jnp.concatenate(
                [jnp.where(vblk == h, vf, 0.0) for h in range(GLA_HEADS)], axis=0).astype(_BF)
            a = jnp.where(causal, _dot_nt(q_all[r0:r0 + CHUNK], k_stack), 0.0).astype(_BF)
            intra.append(_dot(a, v_bd))
            kv_upd.append(jnp.where(bd_mask_t, _dot_tn(vf.astype(_BF), k_end), 0.0))
            decay_rows.append(jnp.exp(b_last))
    o_rows = []
    for s in range(n_seq):
        state = s_ref[s]
        for c in range(n_chunk):
            i = s * n_chunk + c
            r0 = i * CHUNK
            o_rows.append(intra[i] + _dot_nt(q_all[r0:r0 + CHUNK], state.astype(_BF)))
            state = decay_rows[i] * state + kv_upd[i]
        s_ref[s] = state
    o_all = jnp.concatenate(o_rows, axis=0)
    ycat_ref[:, ATTN_DIM:ATTN_DIM + GLA_VDIM] = _group_rms_scale(o_all, GLA_DV, ones64) * gla_g * gate_g

    for s in range(n_seq):
        sbd_ref[...] = s_ref[s].T
        for h in range(GLA_HEADS):
            ns_ref[s, h] = sbd_ref[h * GLA_DK:(h + 1) * GLA_DK, h * GLA_DV:(h + 1) * GLA_DV]

    y = x
    for c0, c1 in ((ATTN_DIM + GLA_VDIM, D_MODEL), (0, ATTN_DIM), (ATTN_DIM, ATTN_DIM + GLA_VDIM)):
        y = y + _dot(ycat_ref[:, c0:c1].astype(_BF), w_out_ref[c0:c1, :])
    y_ref[...] = y
    xcur_ref[...] = y
    last_layer = layer == pl.num_programs(2) - 1
    nxt = jnp.where(last_layer, xnext_ref[...], y)
    next_gain = layer_row("norm_gain", jnp.where(last_layer, 0, layer + 1))
    hb_ref[...] = normed_bf16(nxt, next_gain)


def _run_stream(x2d, n_seqs, seq_len, hist, sinks, params, *, carry, q_pos0):
    depth = sinks.shape[0]
    if carry:
        assert q_pos0 == 0
        tile, n_seq, n_chunk = PROMPT_TILE, 1, PROMPT_TILE // CHUNK
        grid = (n_seqs, seq_len // tile, depth)
    else:
        assert seq_len == CHUNK and q_pos0 >= WINDOW
        tile, n_seq, n_chunk = SAMPLE_SEQS * CHUNK, SAMPLE_SEQS, 1
        grid = (n_seqs // n_seq, 1, depth)
    n_t = grid[1]
    seq_rows = n_chunk * CHUNK

    def const(shape):
        return pl.BlockSpec(shape, lambda g, t, l: (0,) * len(shape), pipeline_mode=pl.Buffered(1))

    def per_layer_group(shape):
        return pl.BlockSpec((depth, n_seq) + shape, lambda g, t, l: (0, g) + (0,) * len(shape))

    state_shapes = ((WINDOW, KV_DIM), (WINDOW, KV_DIM), (GLA_HEADS, GLA_DK, GLA_DV), (CONV_HIST, CONV_DIM))
    x_spec = pl.BlockSpec((tile, D_MODEL), lambda g, t, l: (g * n_t + t, 0))
    n_tiles = x2d.shape[0] // tile
    xnext_spec = pl.BlockSpec((tile, D_MODEL), lambda g, t, l: (jnp.minimum(g * n_t + t + 1, n_tiles - 1), 0))
    in_specs = [pl.BlockSpec(memory_space=pltpu.SMEM), x_spec, xnext_spec]
    args = [sinks, x2d, x2d]
    if not carry:
        in_specs += [per_layer_group(sh) for sh in state_shapes]
        args += list(hist)
    operands = tuple(params) + _shape_constants(tile)
    in_specs += [const(p.shape) for p in operands]
    args += list(operands)

    out_shape = (jax.ShapeDtypeStruct(x2d.shape, _F32),) + tuple(
        jax.ShapeDtypeStruct((depth, n_seqs) + sh, _F32) for sh in state_shapes)
    out_specs = (x_spec,) + tuple(per_layer_group(sh) for sh in state_shapes)
    scratch = [pltpu.VMEM((depth, n_seq, WINDOW + seq_rows + CHUNK, ATTN_KV_HEADS * LANES), _BF),
               pltpu.VMEM((depth, n_seq, WINDOW + seq_rows + CHUNK, ATTN_KV_HEADS * 2 * LANES), _BF),
               pltpu.VMEM((depth, n_seq, GLA_VDIM, GLA_KDIM), _F32),
               pltpu.VMEM((depth, n_seq, CONV_PAD + seq_rows, CONV_DIM), _F32),
               pltpu.VMEM((tile, ATTN_DIM), _BF),
               pltpu.VMEM((tile, ATTN_DIM), _BF),
               pltpu.VMEM((tile, D_MODEL), _F32),
               pltpu.VMEM((tile, IN_COLS_PACKED), _F32),
               pltpu.VMEM((tile, D_MODEL), _F32),
               pltpu.VMEM((tile, D_MODEL), _BF),
               pltpu.VMEM((GLA_KDIM, GLA_VDIM), _F32),
               pltpu.VMEM((tile, D_MODEL), _F32)]
    body = functools.partial(_stream_kernel, tile=tile, n_seq=n_seq, n_chunk=n_chunk, carry=carry, q_pos0=q_pos0)
    return pl.pallas_call(
        body, grid=grid, in_specs=in_specs, out_specs=out_specs, out_shape=out_shape,
        scratch_shapes=scratch,
        compiler_params=pltpu.CompilerParams(dimension_semantics=("arbitrary", "arbitrary", "arbitrary"),
                                             vmem_limit_bytes=VMEM_LIMIT_BYTES),
        name="mixer_prompt" if carry else "mixer_sample",
    )(*args)


def _pack_params(norm_gain, w_in, q_norm_gain, k_norm_gain, gla_w_gate_up, gla_b_gate, gla_norm_gain,
                 conv_w, conv_b, conv_ln_gain, conv_ln_bias, conv_w_pw, conv_b_pw, w_out):
    w_in_a = w_in[:, :, :GLR_SRC].astype(_BF)
    w_in_c = jnp.pad(w_in[:, :, GLR_SRC:GLR_SRC + GLA_LOWRANK],
                     ((0, 0), (0, 0), (0, LANES - GLA_LOWRANK))).astype(_BF)
    w_in_b = w_in[:, :, GLR_SRC + GLA_LOWRANK:].astype(_BF)
    pieces = dict(norm_gain=norm_gain, q_gain=q_norm_gain, k_gain=k_norm_gain, gla_b=gla_b_gate,
                  gla_gain=gla_norm_gain, conv_b=conv_b, ln_gain=conv_ln_gain, ln_bias=conv_ln_bias, b_pw=conv_b_pw)
    vecs = jnp.concatenate([jnp.tile(pieces[name].astype(_F32), (1, n // pieces[name].shape[1]))
                            for name, n in VEC_LAYOUT], axis=1)[:, None, :]
    return w_in_a, w_in_b, w_in_c, gla_w_gate_up, conv_w_pw.astype(_BF), w_out.astype(_BF), vecs, conv_w


def kernel(x_prompt, x_sample, cache_k, cache_v, state_gla, state_conv, norm_gain, w_in, q_norm_gain, k_norm_gain, attn_sinks, gla_w_gate_up, gla_b_gate, gla_norm_gain, conv_w, conv_b, conv_ln_gain, conv_ln_bias, conv_w_pw, conv_b_pw, w_out):
    depth = w_in.shape[0]
    bp, lp, _ = x_prompt.shape
    bs, ls, _ = x_sample.shape
    params = _pack_params(norm_gain, w_in, q_norm_gain, k_norm_gain, gla_w_gate_up, gla_b_gate, gla_norm_gain,
                          conv_w, conv_b, conv_ln_gain, conv_ln_bias, conv_w_pw, conv_b_pw, w_out)
    sinks = jnp.pad(attn_sinks.astype(_F32), ((0, 0), (0, ATTN_HEADS)))
    yp, pk, pv, ps, pc = _run_stream(x_prompt.reshape(bp * lp, D_MODEL), bp, lp, None, sinks, params,
                                     carry=True, q_pos0=0)
    hist = (cache_k.reshape(depth, bs, WINDOW, KV_DIM), cache_v.reshape(depth, bs, WINDOW, KV_DIM),
            state_gla, state_conv)
    ys, sk, sv, ss, sc = _run_stream(x_sample.reshape(bs * ls, D_MODEL), bs, ls, hist, sinks, params,
                                     carry=False, q_pos0=PAST_LEN)
    kv_shape = (WINDOW, ATTN_KV_HEADS, HEAD_DIM)
    return (yp.reshape(bp, lp, D_MODEL), ys.reshape(bs, ls, D_MODEL),
            pk.reshape((depth, bp) + kv_shape), pv.reshape((depth, bp) + kv_shape), ps, pc,
            sk.reshape((depth, bs) + kv_shape), sv.reshape((depth, bs) + kv_shape), ss, sc)
```

```python
import functools
import math

import numpy as np
import jax
import jax.numpy as jnp
from jax import lax
from jax.experimental import pallas as pl
from jax.experimental.pallas import tpu as pltpu

D_MODEL = 1024
CHUNK = 64
ATTN_HEADS = 8
ATTN_KV_HEADS = 2
HEAD_DIM = 64
ATTN_GROUP = ATTN_HEADS // ATTN_KV_HEADS
ATTN_DIM = ATTN_HEADS * HEAD_DIM
KV_DIM = ATTN_KV_HEADS * HEAD_DIM
WINDOW = 128
BAND = WINDOW + CHUNK
KBAND = WINDOW + 2 * CHUNK
GLA_HEADS = 4
GLA_DK = 32
GLA_DV = 64
GLA_KDIM = GLA_HEADS * GLA_DK
GLA_VDIM = GLA_HEADS * GLA_DV
GLA_LOWRANK = 16
GLA_TAU = 16.0
CONV_DIM = 256
CONV_WIDTH = 31
CONV_HIST = CONV_WIDTH - 1
NORM_EPS = 1e-6
NEG_INF = -1e30
PAST_LEN = 4096
LOG2E = math.log2(math.e)

LANES = 128
SUBLANES = 8
CHUNK_SHIFT = CHUNK.bit_length() - 1
GLA_DK_SHIFT = GLA_DK.bit_length() - 1
GLA_DV_SHIFT = GLA_DV.bit_length() - 1
C_AQ, C_AK, C_AV, C_AG = 0, 512, 640, 768
C_GQ, C_GK, C_GV, C_GG = 1280, 1408, 1536, 1792
C_CV, C_CGL, C_CG, C_GLR = 2048, 2304, 2560, 2816
IN_COLS_PACKED = C_GLR + LANES
PROJ_GROUP = 256
CONV_PAD = 32
GLR_SRC = 2048
VEC_LAYOUT = (("norm_gain", D_MODEL), ("q_gain", LANES), ("k_gain", KV_DIM), ("gla_b", GLA_KDIM),
              ("gla_gain", GLA_VDIM), ("conv_b", CONV_DIM), ("ln_gain", CONV_DIM), ("ln_bias", CONV_DIM),
              ("b_pw", CONV_DIM))
VEC_ROW_QUANTUM = 16384 // (SUBLANES * 4)
VEC_OFFSETS = {name: (sum(n for _, n in VEC_LAYOUT[:k]), VEC_LAYOUT[k][1]) for k, (name, _) in enumerate(VEC_LAYOUT)}

PROMPT_TILE = 512
SAMPLE_SEQS = 4
VMEM_LIMIT_BYTES = 56 * 1024 * 1024

_BF = jnp.bfloat16
_F32 = jnp.float32


def _shape_constants(tile):
    rows = np.arange(ATTN_GROUP * CHUNK)
    dist = np.abs((rows % CHUNK)[:, None] + WINDOW - np.arange(KBAND)[None, :]).astype(np.float64)
    bias = np.stack([
        (2.0 ** (-8.0 * (j * ATTN_GROUP + rows // CHUNK + 1) / ATTN_HEADS))[:, None] * LOG2E * dist
        for j in range(ATTN_KV_HEADS)]).astype(np.float32)
    bias[:, :, BAND:] = -NEG_INF
    variants = []
    for n in range(WINDOW // CHUNK + 1):
        b = bias.copy()
        b[:, :, :max(WINDOW - n * CHUNK, 0)] = -NEG_INF
        variants.append(b)
    bias = np.stack(variants)
    tr = np.arange(tile)
    tri = ((tr[:, None] // CHUNK == tr[None, :] // CHUNK) & (tr[None, :] <= tr[:, None])).astype(np.float32)
    ln = np.arange(LANES) // HEAD_DIM
    ones = (ln[:, None] == ln[None, :]).astype(np.float32)
    triones = np.zeros((tile + LANES, tile), np.float32)
    triones[:tile] = tri
    triones[tile:, :LANES] = ones
    return jnp.asarray(bias), jnp.asarray(triones, dtype=_BF)


def _group_rms_scale(x, group, ones_bd):
    outs = []
    for c0 in range(0, x.shape[1], LANES):
        blk = x[:, c0:c0 + LANES]
        ss = _dot((blk * blk).astype(_BF), ones_bd)
        outs.append(blk * lax.rsqrt(ss * (1.0 / group) + NORM_EPS))
    return outs[0] if len(outs) == 1 else jnp.concatenate(outs, axis=1)


def _silu(x):
    return x * jax.nn.sigmoid(x)


def _dot(a, b):
    return jnp.dot(a, b, preferred_element_type=_F32)


def _dot_nt(a, b):
    return lax.dot_general(a, b, (((1,), (1,)), ((), ())), preferred_element_type=_F32)


def _dot_tn(a, b):
    return lax.dot_general(a, b, (((0,), (0,)), ((), ())), preferred_element_type=_F32)


def _stream_kernel(*refs, tile, n_seq, n_chunk, carry, q_pos0):
    if carry:
        (sinks_ref, x_ref, xnext_ref, *rest) = refs
    else:
        (sinks_ref, x_ref, xnext_ref, hk_all, hv_all, hs_all, hc_all, *rest) = refs
    (w_in_a, w_in_b, w_in_c, w_up_all, w_pw_all, w_out_all, vec_all, convw_all, bias_ref, triones_ref,
     y_ref, nk_all, nv_all, ns_all, nc_all,
     kd_all, vd_all, s_all, u_all, qlo_ref, qhi_ref, ycat_ref, proj_ref, xcur_ref, hb_ref, sbd_ref,
     gate_ref) = rest
    t = pl.program_id(1)
    layer = pl.program_id(2)
    seq_rows = n_chunk * CHUNK
    w_up_ref, w_pw_ref, w_out_ref = (r.at[layer] for r in (w_up_all, w_pw_all, w_out_all))
    convw_ref = convw_all.at[layer]

    def layer_row(name, which=layer):
        off, n = VEC_OFFSETS[name]
        return vec_all[which, :, off:off + n]

    q_gain = layer_row("q_gain") * (LOG2E * HEAD_DIM ** -0.5)
    k_gain = layer_row("k_gain")
    kd_ref, vd_ref, s_ref, u_ref = (r.at[layer] for r in (kd_all, vd_all, s_all, u_all))
    nk_ref, nv_ref, ns_ref, nc_ref = (r.at[layer] for r in (nk_all, nv_all, ns_all, nc_all))
    if not carry:
        hk_ref, hv_ref, hs_ref, hc_ref = (r.at[layer] for r in (hk_all, hv_all, hs_all, hc_all))

    lane128 = lax.broadcasted_iota(jnp.int32, (1, LANES), 1)
    lo_half = lane128 < HEAD_DIM

    def dup_halves(a):
        sw = pltpu.roll(a, HEAD_DIM, 1)
        return jnp.concatenate([jnp.where(lo_half, a, sw), jnp.where(lo_half, sw, a)], axis=1)

    buf_rows = WINDOW + seq_rows + CHUNK
    zeros_blk = jnp.zeros((buf_rows, LANES), _BF)
    ones_blk = jnp.ones((buf_rows, LANES), _BF)
    v_blank = jnp.concatenate([zeros_blk, ones_blk] * ATTN_KV_HEADS, axis=1)

    def store_v(s, row0, vdup_rows):
        for j in range(ATTN_KV_HEADS):
            vd_ref[s, row0:row0 + vdup_rows.shape[0], 2 * j * LANES:(2 * j + 1) * LANES] = (
                vdup_rows[:, j * LANES:(j + 1) * LANES])

    if carry:
        @pl.when(t == 0)
        def _():
            kd_ref[...] = jnp.zeros(kd_ref.shape, _BF)
            for s in range(n_seq):
                vd_ref[s] = v_blank
            s_ref[...] = jnp.zeros(s_ref.shape, _F32)
            u_ref[:, 0:CONV_PAD, :] = jnp.zeros((n_seq, CONV_PAD, CONV_DIM), _F32)
    else:
        for s in range(n_seq):
            kd_ref[s, 0:WINDOW, :] = dup_halves(hk_ref[s]).astype(_BF)
            kd_ref[s, WINDOW + seq_rows:, :] = jnp.zeros((CHUNK, 2 * LANES), _BF)
            vd_ref[s] = v_blank
            store_v(s, 0, dup_halves(hv_ref[s]).astype(_BF))
            sbd_ref[...] = jnp.zeros((GLA_KDIM, GLA_VDIM), _F32)
            for h in range(GLA_HEADS):
                sbd_ref[h * GLA_DK:(h + 1) * GLA_DK, h * GLA_DV:(h + 1) * GLA_DV] = hs_ref[s, h]
            s_ref[s] = sbd_ref[...].T
            u_ref[s, 0:SUBLANES, :] = jnp.zeros((SUBLANES, CONV_DIM), _F32)
            u_ref[s, CONV_PAD - CONV_HIST:CONV_PAD, :] = hc_ref[s]

    def normed_bf16(v, gain_row):
        ms = jnp.mean(v * v, axis=-1, keepdims=True)
        return (v * lax.rsqrt(ms + NORM_EPS) * gain_row).astype(_BF)

    @pl.when(layer == 0)
    def _():
        xcur_ref[...] = x_ref[...]

    @pl.when((pl.program_id(0) == 0) & (t == 0) & (layer == 0))
    def _():
        hb_ref[...] = normed_bf16(x_ref[...], layer_row("norm_gain"))
    x = xcur_ref[...]
    hb = hb_ref[...]

    anchor_on = sinks_ref[layer, ATTN_HEADS] != 0.0

    ones64 = triones_ref[tile:tile + LANES, 0:LANES]

    def post_q(g):
        def post(res):
            for b in range(PROJ_GROUP // LANES):
                c0 = g * PROJ_GROUP + b * LANES
                qn = _group_rms_scale(res[:, b * LANES:(b + 1) * LANES], HEAD_DIM, ones64) * q_gain
                qlo_ref[:, c0:c0 + LANES] = jnp.where(lo_half, qn, 0.0).astype(_BF)
                qhi_ref[:, c0:c0 + LANES] = jnp.where(lo_half, 0.0, qn).astype(_BF)
        return post

    def post_kv(res):
        kn = _group_rms_scale(res[:, 0:KV_DIM], HEAD_DIM, ones64) * k_gain
        vv = res[:, KV_DIM:2 * KV_DIM]
        kdup = dup_halves(kn).astype(_BF)
        vdup = dup_halves(vv).astype(_BF)
        for s in range(n_seq):
            kd_ref[s, WINDOW:WINDOW + seq_rows, :] = kdup[s * seq_rows:(s + 1) * seq_rows]
            store_v(s, WINDOW, vdup[s * seq_rows:(s + 1) * seq_rows])
        if carry:
            nk_ref[0] = kn[tile - WINDOW:tile]
            nv_ref[0] = vv[tile - WINDOW:tile]
        else:
            for s in range(n_seq):
                nk_ref[s, 0:WINDOW - seq_rows, :] = hk_ref[s, seq_rows:WINDOW, :]
                nv_ref[s, 0:WINDOW - seq_rows, :] = hv_ref[s, seq_rows:WINDOW, :]
                nk_ref[s, WINDOW - seq_rows:WINDOW, :] = kn[s * seq_rows:(s + 1) * seq_rows]
                nv_ref[s, WINDOW - seq_rows:WINDOW, :] = vv[s * seq_rows:(s + 1) * seq_rows]

    def post_gate(col0):
        def post(res):
            gate_ref[:, col0:col0 + PROJ_GROUP] = _silu(res)
        return post

    group_post = {C_AG // PROJ_GROUP: post_gate(0), C_AG // PROJ_GROUP + 1: post_gate(PROJ_GROUP),
                  C_GG // PROJ_GROUP: post_gate(ATTN_DIM), C_CG // PROJ_GROUP: post_gate(ATTN_DIM + GLA_VDIM)}

    def w_in_cols(c0, c1):
        if c1 <= C_CV:
            return w_in_a[layer, :, c0:c1]
        if c1 <= C_GLR:
            return w_in_b[layer, :, c0 - C_CV:c1 - C_CV]
        return w_in_c[layer, :, c0 - C_GLR:c1 - C_GLR]

    def compute_group(g):
        c0, c1 = g * PROJ_GROUP, min((g + 1) * PROJ_GROUP, IN_COLS_PACKED)
        res = _dot(hb, w_in_cols(c0, c1))
        if g in group_post:
            group_post[g](res)
        else:
            proj_ref[:, c0:c1] = res
        return jnp.where(anchor_on, res[0:1, 0:CONV_DIM], 0.0)

    def proj(c0, width):
        return proj_ref[:, c0:c0 + width]

    compute_group(C_CV // PROJ_GROUP)
    compute_group(C_CGL // PROJ_GROUP)
    u = proj(C_CV, CONV_DIM) * jax.nn.sigmoid(proj(C_CGL, CONV_DIM))
    for s in range(n_seq):
        u_ref[s, CONV_PAD:CONV_PAD + seq_rows, :] = u[s * seq_rows:(s + 1) * seq_rows]
    conv_acc = [jnp.broadcast_to(layer_row("conv_b"), (seq_rows, CONV_DIM))] * n_seq
    for rho in range(SUBLANES):
        anchor = compute_group(rho)
        for s in range(n_seq):
            frame_rows = seq_rows + (SUBLANES if rho else 0)
            frame = None
            for j in range(CONV_WIDTH):
                off = CONV_PAD - CONV_HIST + j
                if off % SUBLANES != rho:
                    continue
                term = (convw_ref[j:j + 1, :] + anchor) * u_ref[s, off - rho:off - rho + frame_rows, :]
                frame = term if frame is None else frame + term
            conv_acc[s] = conv_acc[s] + frame[rho:rho + seq_rows]
    compute_group(C_CG // PROJ_GROUP)
    compute_group(C_GLR // PROJ_GROUP)
    cc = conv_acc[0] if n_seq == 1 else jnp.concatenate(conv_acc, axis=0)

    for s in range(n_seq):
        nc_ref[s] = u_ref[s, seq_rows + CONV_PAD - CONV_HIST:seq_rows + CONV_PAD, :]
    if carry:
        u_ref[:, 0:CONV_PAD, :] = u_ref[:, seq_rows:seq_rows + CONV_PAD, :]

    mu = jnp.mean(cc, axis=-1, keepdims=True)
    cen = cc - mu
    var = jnp.mean(cen * cen, axis=-1, keepdims=True)
    ln = cen * lax.rsqrt(var + NORM_EPS) * layer_row("ln_gain") + layer_row("ln_bias")
    cpw = _dot(_silu(ln).astype(_BF), w_pw_ref[...]) + layer_row("b_pw")
    ycat_ref[:, ATTN_DIM + GLA_VDIM:D_MODEL] = cpw * gate_ref[:, ATTN_DIM + GLA_VDIM:D_MODEL]

    for g in range(ATTN_DIM // PROJ_GROUP):
        post_q(g)(proj(C_AQ + g * PROJ_GROUP, PROJ_GROUP))
    post_kv(proj(C_AK, 2 * KV_DIM))

    rows = ATTN_GROUP * CHUNK
    r_head1 = lax.broadcasted_iota(jnp.int32, (rows, 1), 0) >> CHUNK_SHIFT
    sink_cols = []
    for j in range(ATTN_KV_HEADS):
        sink = jnp.zeros((rows, 1), _F32)
        for r in range(ATTN_GROUP):
            sink = jnp.where(r_head1 == r, sinks_ref[layer, j * ATTN_GROUP + r], sink)
        sink_cols.append(sink * LOG2E)

    for s in range(n_seq):
        for c in range(n_chunk):
            r0 = s * seq_rows + c * CHUNK
            k0 = c * CHUNK
            steady = WINDOW // CHUNK
            variant = jnp.minimum(t * n_chunk + c, steady) if carry else steady
            for j in range(ATTN_KV_HEADS):
                qs = jnp.concatenate(
                    [qlo_ref[r0:r0 + CHUNK, (2 * j) * LANES:(2 * j + 1) * LANES],
                     qhi_ref[r0:r0 + CHUNK, (2 * j) * LANES:(2 * j + 1) * LANES],
                     qlo_ref[r0:r0 + CHUNK, (2 * j + 1) * LANES:(2 * j + 2) * LANES],
                     qhi_ref[r0:r0 + CHUNK, (2 * j + 1) * LANES:(2 * j + 2) * LANES]], axis=0)
                kb = kd_ref[s, k0:k0 + KBAND, j * LANES:(j + 1) * LANES]
                vb = vd_ref[s, k0:k0 + KBAND, 2 * j * LANES:(2 * j + 2) * LANES]
                sc = _dot_nt(qs, kb) - bias_ref[variant, j]
                sink = sink_cols[j]
                m = jnp.maximum(jnp.max(sc, axis=-1, keepdims=True), sink)
                p = jnp.exp2(sc - m)
                pv = _dot(p.astype(_BF), vb)
                o = pv[:, 0:LANES] / (pv[:, LANES:2 * LANES] + jnp.exp2(sink - m))
                for pb in range(2):
                    blk = 2 * j + pb
                    ob = jnp.where(lo_half, o[(2 * pb) * CHUNK:(2 * pb + 1) * CHUNK],
                                   o[(2 * pb + 1) * CHUNK:(2 * pb + 2) * CHUNK])
                    ycat_ref[r0:r0 + CHUNK, blk * LANES:(blk + 1) * LANES] = (
                        ob * gate_ref[r0:r0 + CHUNK, blk * LANES:(blk + 1) * LANES])

    if carry:
        kd_ref[:, 0:WINDOW, :] = kd_ref[:, seq_rows:seq_rows + WINDOW, :]
        vd_ref[:, 0:WINDOW, :] = vd_ref[:, seq_rows:seq_rows + WINDOW, :]

    w_up = jnp.concatenate([w_up_ref[...], jnp.zeros((LANES - GLA_LOWRANK, GLA_KDIM), _F32)], axis=0).astype(_BF)
    z = _dot(proj(C_GLR, LANES).astype(_BF), w_up) + layer_row("gla_b")
    log_a = (jnp.minimum(z, 0.0) - jnp.log(1.0 + jnp.exp(-jnp.abs(z)))) * (1.0 / GLA_TAU)
    tri = triones_ref[0:tile, :]
    la_hi = log_a.astype(_BF)
    la_lo = (log_a - la_hi.astype(_F32)).astype(_BF)
    bcum = _dot(tri, la_hi) + _dot(tri, la_lo)
    gq = proj(C_GQ, GLA_KDIM) * GLA_DK ** -0.5
    gk = proj(C_GK, GLA_KDIM)
    gv = proj(C_GV, GLA_VDIM)
    gate_g = gate_ref[:, ATTN_DIM:ATTN_DIM + GLA_VDIM]
    gla_g = layer_row("gla_gain")

    head_of_lane = lane128 >> GLA_DK_SHIFT
    a_r = lax.broadcasted_iota(jnp.int32, (CHUNK, GLA_HEADS * CHUNK), 0)
    a_c = lax.broadcasted_iota(jnp.int32, (CHUNK, GLA_HEADS * CHUNK), 1)
    causal = (a_c & (CHUNK - 1)) <= a_r
    vblk = lax.broadcasted_iota(jnp.int32, (1, GLA_VDIM), 1) >> GLA_DV_SHIFT
    bd_mask_t = ((lax.broadcasted_iota(jnp.int32, (GLA_VDIM, GLA_KDIM), 0) >> GLA_DV_SHIFT)
                 == (lax.broadcasted_iota(jnp.int32, (GLA_VDIM, GLA_KDIM), 1) >> GLA_DK_SHIFT))

    q_all = (gq * jnp.exp(bcum)).astype(_BF)
    k_all = gk * jnp.exp(-bcum)
    intra, kv_upd, decay_rows = [], [], []
    for s in range(n_seq):
        for c in range(n_chunk):
            r0 = s * seq_rows + c * CHUNK
            bc = bcum[r0:r0 + CHUNK]
            b_last = bc[CHUNK - 1:CHUNK]
            k_end = (gk[r0:r0 + CHUNK] * jnp.exp(b_last - bc)).astype(_BF)
            vf = gv[r0:r0 + CHUNK]
            k_stack = jnp.concatenate(
                [jnp.where(head_of_lane == h, k_all[r0:r0 + CHUNK], 0.0) for h in range(GLA_HEADS)],
                axis=0).astype(_BF)
            v_bd = jnp.concatenate(
                [jnp.where(vblk == h, vf, 0.0) for h in range(GLA_HEADS)], axis=0).astype(_BF)
            a = jnp.where(causal, _dot_nt(q_all[r0:r0 + CHUNK], k_stack), 0.0).astype(_BF)
            intra.append(_dot(a, v_bd))
            kv_upd.append(jnp.where(bd_mask_t, _dot_tn(vf.astype(_BF), k_end), 0.0))
            decay_rows.append(jnp.exp(b_last))
    o_rows = []
    for s in range(n_seq):
        state = s_ref[s]
        for c in range(n_chunk):
            i = s * n_chunk + c
            r0 = i * CHUNK
            o_rows.append(intra[i] + _dot_nt(q_all[r0:r0 + CHUNK], state.astype(_BF)))
            state = decay_rows[i] * state + kv_upd[i]
        s_ref[s] = state
    o_all = jnp.concatenate(o_rows, axis=0)
    ycat_ref[:, ATTN_DIM:ATTN_DIM + GLA_VDIM] = _group_rms_scale(o_all, GLA_DV, ones64) * gla_g * gate_g

    for s in range(n_seq):
        sbd_ref[...] = s_ref[s].T
        for h in range(GLA_HEADS):
            ns_ref[s, h] = sbd_ref[h * GLA_DK:(h + 1) * GLA_DK, h * GLA_DV:(h + 1) * GLA_DV]

    y = x
    for c0, c1 in ((ATTN_DIM + GLA_VDIM, D_MODEL), (0, ATTN_DIM), (ATTN_DIM, ATTN_DIM + GLA_VDIM)):
        y = y + _dot(ycat_ref[:, c0:c1].astype(_BF), w_out_ref[c0:c1, :])
    y_ref[...] = y
    xcur_ref[...] = y
    last_layer = layer == pl.num_programs(2) - 1
    nxt = jnp.where(last_layer, xnext_ref[...], y)
    next_gain = layer_row("norm_gain", jnp.where(last_layer, 0, layer + 1))
    hb_ref[...] = normed_bf16(nxt, next_gain)


def _run_stream(x2d, n_seqs, seq_len, hist, sinks, params, *, carry, q_pos0):
    depth = sinks.shape[0]
    if carry:
        assert q_pos0 == 0
        tile, n_seq, n_chunk = PROMPT_TILE, 1, PROMPT_TILE // CHUNK
        grid = (n_seqs, seq_len // tile, depth)
    else:
        assert seq_len == CHUNK and q_pos0 >= WINDOW
        tile, n_seq, n_chunk = SAMPLE_SEQS * CHUNK, SAMPLE_SEQS, 1
        grid = (n_seqs // n_seq, 1, depth)
    n_t = grid[1]
    seq_rows = n_chunk * CHUNK

    def const(shape):
        return pl.BlockSpec(shape, lambda g, t, l: (0,) * len(shape), pipeline_mode=pl.Buffered(1))

    def per_layer_group(shape):
        return pl.BlockSpec((depth, n_seq) + shape, lambda g, t, l: (0, g) + (0,) * len(shape))

    state_shapes = ((WINDOW, KV_DIM), (WINDOW, KV_DIM), (GLA_HEADS, GLA_DK, GLA_DV), (CONV_HIST, CONV_DIM))
    x_spec = pl.BlockSpec((tile, D_MODEL), lambda g, t, l: (g * n_t + t, 0))
    n_tiles = x2d.shape[0] // tile
    xnext_spec = pl.BlockSpec((tile, D_MODEL), lambda g, t, l: (jnp.minimum(g * n_t + t + 1, n_tiles - 1), 0))
    in_specs = [pl.BlockSpec(memory_space=pltpu.SMEM), x_spec, xnext_spec]
    args = [sinks, x2d, x2d]
    if not carry:
        in_specs += [per_layer_group(sh) for sh in state_shapes]
        args += list(hist)
    operands = tuple(params) + _shape_constants(tile)
    in_specs += [const(p.shape) for p in operands]
    args += list(operands)

    out_shape = (jax.ShapeDtypeStruct(x2d.shape, _F32),) + tuple(
        jax.ShapeDtypeStruct((depth, n_seqs) + sh, _F32) for sh in state_shapes)
    out_specs = (x_spec,) + tuple(per_layer_group(sh) for sh in state_shapes)
    scratch = [pltpu.VMEM((depth, n_seq, WINDOW + seq_rows + CHUNK, ATTN_KV_HEADS * LANES), _BF),
               pltpu.VMEM((depth, n_seq, WINDOW + seq_rows + CHUNK, ATTN_KV_HEADS * 2 * LANES), _BF),
               pltpu.VMEM((depth, n_seq, GLA_VDIM, GLA_KDIM), _F32),
               pltpu.VMEM((depth, n_seq, CONV_PAD + seq_rows, CONV_DIM), _F32),
               pltpu.VMEM((tile, ATTN_DIM), _BF),
               pltpu.VMEM((tile, ATTN_DIM), _BF),
               pltpu.VMEM((tile, D_MODEL), _F32),
               pltpu.VMEM((tile, IN_COLS_PACKED), _F32),
               pltpu.VMEM((tile, D_MODEL), _F32),
               pltpu.VMEM((tile, D_MODEL), _BF),
               pltpu.VMEM((GLA_KDIM, GLA_VDIM), _F32),
               pltpu.VMEM((tile, D_MODEL), _F32)]
    body = functools.partial(_stream_kernel, tile=tile, n_seq=n_seq, n_chunk=n_chunk, carry=carry, q_pos0=q_pos0)
    return pl.pallas_call(
        body, grid=grid, in_specs=in_specs, out_specs=out_specs, out_shape=out_shape,
        scratch_shapes=scratch,
        compiler_params=pltpu.CompilerParams(dimension_semantics=("arbitrary", "arbitrary", "arbitrary"),
                                             vmem_limit_bytes=VMEM_LIMIT_BYTES),
        name="mixer_prompt" if carry else "mixer_sample",
    )(*args)


def _pack_params(norm_gain, w_in, q_norm_gain, k_norm_gain, gla_w_gate_up, gla_b_gate, gla_norm_gain,
                 conv_w, conv_b, conv_ln_gain, conv_ln_bias, conv_w_pw, conv_b_pw, w_out):
    w_in_a = w_in[:, :, :GLR_SRC].astype(_BF)
    w_in_c = jnp.pad(w_in[:, :, GLR_SRC:GLR_SRC + GLA_LOWRANK],
                     ((0, 0), (0, 0), (0, LANES - GLA_LOWRANK))).astype(_BF)
    w_in_b = w_in[:, :, GLR_SRC + GLA_LOWRANK:].astype(_BF)
    pieces = dict(norm_gain=norm_gain, q_gain=q_norm_gain, k_gain=k_norm_gain, gla_b=gla_b_gate,
                  gla_gain=gla_norm_gain, conv_b=conv_b, ln_gain=conv_ln_gain, ln_bias=conv_ln_bias, b_pw=conv_b_pw)
    vecs = jnp.concatenate([jnp.tile(pieces[name].astype(_F32), (1, n // pieces[name].shape[1]))
                            for name, n in VEC_LAYOUT], axis=1)
    row_len = -(-vecs.shape[1] // VEC_ROW_QUANTUM) * VEC_ROW_QUANTUM
    vecs = jnp.pad(vecs, ((0, 0), (0, row_len - vecs.shape[1])))[:, None, :]
    return w_in_a, w_in_b, w_in_c, gla_w_gate_up, conv_w_pw.astype(_BF), w_out.astype(_BF), vecs, conv_w


def kernel(x_prompt, x_sample, cache_k, cache_v, state_gla, state_conv, norm_gain, w_in, q_norm_gain, k_norm_gain, attn_sinks, gla_w_gate_up, gla_b_gate, gla_norm_gain, conv_w, conv_b, conv_ln_gain, conv_ln_bias, conv_w_pw, conv_b_pw, w_out):
    depth = w_in.shape[0]
    bp, lp, _ = x_prompt.shape
    bs, ls, _ = x_sample.shape
    params = _pack_params(norm_gain, w_in, q_norm_gain, k_norm_gain, gla_w_gate_up, gla_b_gate, gla_norm_gain,
                          conv_w, conv_b, conv_ln_gain, conv_ln_bias, conv_w_pw, conv_b_pw, w_out)
    sinks = jnp.pad(attn_sinks.astype(_F32), ((0, 0), (0, ATTN_HEADS)))
    yp, pk, pv, ps, pc = _run_stream(x_prompt.reshape(bp * lp, D_MODEL), bp, lp, None, sinks, params,
                                     carry=True, q_pos0=0)
    hist = (cache_k.reshape(depth, bs, WINDOW, KV_DIM), cache_v.reshape(depth, bs, WINDOW, KV_DIM),
            state_gla, state_conv)
    ys, sk, sv, ss, sc = _run_stream(x_sample.reshape(bs * ls, D_MODEL), bs, ls, hist, sinks, params,
                                     carry=False, q_pos0=PAST_LEN)
    kv_shape = (WINDOW, ATTN_KV_HEADS, HEAD_DIM)
    return (yp.reshape(bp, lp, D_MODEL), ys.reshape(bs, ls, D_MODEL),
            pk.reshape((depth, bp) + kv_shape), pv.reshape((depth, bp) + kv_shape), ps, pc,
            sk.reshape((depth, bs) + kv_shape), sv.reshape((depth, bs) + kv_shape), ss, sc)
```

```python
import functools
import math

import numpy as np
import jax
import jax.numpy as jnp
from jax import lax
from jax.experimental import pallas as pl
from jax.experimental.pallas import tpu as pltpu

D_MODEL = 1024
CHUNK = 64
ATTN_HEADS = 8
ATTN_KV_HEADS = 2
HEAD_DIM = 64
ATTN_GROUP = ATTN_HEADS // ATTN_KV_HEADS
ATTN_DIM = ATTN_HEADS * HEAD_DIM
KV_DIM = ATTN_KV_HEADS * HEAD_DIM
WINDOW = 128
BAND = WINDOW + CHUNK
KBAND = WINDOW + 2 * CHUNK
GLA_HEADS = 4
GLA_DK = 32
GLA_DV = 64
GLA_KDIM = GLA_HEADS * GLA_DK
GLA_VDIM = GLA_HEADS * GLA_DV
GLA_LOWRANK = 16
GLA_TAU = 16.0
CONV_DIM = 256
CONV_WIDTH = 31
CONV_HIST = CONV_WIDTH - 1
NORM_EPS = 1e-6
NEG_INF = -1e30
PAST_LEN = 4096
LOG2E = math.log2(math.e)

LANES = 128
SUBLANES = 8
CHUNK_SHIFT = CHUNK.bit_length() - 1
GLA_DK_SHIFT = GLA_DK.bit_length() - 1
GLA_DV_SHIFT = GLA_DV.bit_length() - 1
C_AQ, C_AK, C_AV, C_AG = 0, 512, 640, 768
C_GQ, C_GK, C_GV, C_GG = 1280, 1408, 1536, 1792
C_CV, C_CGL, C_CG, C_GLR = 2048, 2304, 2560, 2816
IN_COLS_PACKED = C_GLR + LANES
PROJ_GROUP = 256
CONV_PAD = 32
GLR_SRC = 2048
VEC_LAYOUT = (("norm_gain", D_MODEL), ("q_gain", LANES), ("k_gain", KV_DIM), ("gla_b", GLA_KDIM),
              ("gla_gain", GLA_VDIM), ("conv_b", CONV_DIM), ("ln_gain", CONV_DIM), ("ln_bias", CONV_DIM),
              ("b_pw", CONV_DIM))
VEC_ROWS = 16
VEC_ROW_OF = {name: (k, n) for k, (name, n) in enumerate(VEC_LAYOUT)}

PROMPT_TILE = 512
SAMPLE_SEQS = 4
VMEM_LIMIT_BYTES = 56 * 1024 * 1024

_BF = jnp.bfloat16
_F32 = jnp.float32


def _shape_constants(tile):
    rows = np.arange(ATTN_GROUP * CHUNK)
    dist = np.abs((rows % CHUNK)[:, None] + WINDOW - np.arange(KBAND)[None, :]).astype(np.float64)
    bias = np.stack([
        (2.0 ** (-8.0 * (j * ATTN_GROUP + rows // CHUNK + 1) / ATTN_HEADS))[:, None] * LOG2E * dist
        for j in range(ATTN_KV_HEADS)]).astype(np.float32)
    bias[:, :, BAND:] = -NEG_INF
    variants = []
    for n in range(WINDOW // CHUNK + 1):
        b = bias.copy()
        b[:, :, :max(WINDOW - n * CHUNK, 0)] = -NEG_INF
        variants.append(b)
    bias = np.stack(variants)
    tr = np.arange(tile)
    tri = ((tr[:, None] // CHUNK == tr[None, :] // CHUNK) & (tr[None, :] <= tr[:, None])).astype(np.float32)
    ln = np.arange(LANES) // HEAD_DIM
    ones = (ln[:, None] == ln[None, :]).astype(np.float32)
    return jnp.asarray(bias), jnp.asarray(tri, dtype=_BF), jnp.asarray(ones, dtype=_BF)


def _group_rms_scale(x, group, ones_bd):
    outs = []
    for c0 in range(0, x.shape[1], LANES):
        blk = x[:, c0:c0 + LANES]
        ss = _dot((blk * blk).astype(_BF), ones_bd)
        outs.append(blk * lax.rsqrt(ss * (1.0 / group) + NORM_EPS))
    return outs[0] if len(outs) == 1 else jnp.concatenate(outs, axis=1)


def _silu(x):
    return x * jax.nn.sigmoid(x)


def _dot(a, b):
    return jnp.dot(a, b, preferred_element_type=_F32)


def _dot_nt(a, b):
    return lax.dot_general(a, b, (((1,), (1,)), ((), ())), preferred_element_type=_F32)


def _dot_tn(a, b):
    return lax.dot_general(a, b, (((0,), (0,)), ((), ())), preferred_element_type=_F32)


def _stream_kernel(*refs, tile, n_seq, n_chunk, carry, q_pos0):
    if carry:
        (sinks_ref, x_ref, xnext_ref, *rest) = refs
    else:
        (sinks_ref, x_ref, xnext_ref, hk_all, hv_all, hs_all, hc_all, *rest) = refs
    (w_in_a, w_in_b, w_in_c, w_up_all, w_pw_all, w_out_all, vec_all, convw_all, bias_ref, tri_ref, ones_ref,
     y_ref, nk_all, nv_all, ns_all, nc_all,
     kd_all, vd_all, s_all, u_all, qlo_ref, qhi_ref, ycat_ref, proj_ref, xcur_ref, hb_ref, sbd_ref,
     gate_ref) = rest
    t = pl.program_id(1)
    layer = pl.program_id(2)
    seq_rows = n_chunk * CHUNK
    w_up_ref, w_pw_ref, w_out_ref = (r.at[layer] for r in (w_up_all, w_pw_all, w_out_all))
    convw_ref = convw_all.at[layer]

    def layer_row(name, which=layer):
        r, n = VEC_ROW_OF[name]
        return vec_all[which, r:r + 1, 0:n]

    q_gain = layer_row("q_gain") * (LOG2E * HEAD_DIM ** -0.5)
    k_gain = layer_row("k_gain")
    kd_ref, vd_ref, s_ref, u_ref = (r.at[layer] for r in (kd_all, vd_all, s_all, u_all))
    nk_ref, nv_ref, ns_ref, nc_ref = (r.at[layer] for r in (nk_all, nv_all, ns_all, nc_all))
    if not carry:
        hk_ref, hv_ref, hs_ref, hc_ref = (r.at[layer] for r in (hk_all, hv_all, hs_all, hc_all))

    lane128 = lax.broadcasted_iota(jnp.int32, (1, LANES), 1)
    lo_half = lane128 < HEAD_DIM

    def dup_halves(a):
        sw = pltpu.roll(a, HEAD_DIM, 1)
        return jnp.concatenate([jnp.where(lo_half, a, sw), jnp.where(lo_half, sw, a)], axis=1)

    buf_rows = WINDOW + seq_rows + CHUNK
    zeros_blk = jnp.zeros((buf_rows, LANES), _BF)
    ones_blk = jnp.ones((buf_rows, LANES), _BF)
    v_blank = jnp.concatenate([zeros_blk, ones_blk] * ATTN_KV_HEADS, axis=1)

    def store_v(s, row0, vdup_rows):
        for j in range(ATTN_KV_HEADS):
            vd_ref[s, row0:row0 + vdup_rows.shape[0], 2 * j * LANES:(2 * j + 1) * LANES] = (
                vdup_rows[:, j * LANES:(j + 1) * LANES])

    if carry:
        @pl.when(t == 0)
        def _():
            kd_ref[...] = jnp.zeros(kd_ref.shape, _BF)
            for s in range(n_seq):
                vd_ref[s] = v_blank
            s_ref[...] = jnp.zeros(s_ref.shape, _F32)
            u_ref[:, 0:CONV_PAD, :] = jnp.zeros((n_seq, CONV_PAD, CONV_DIM), _F32)
    else:
        for s in range(n_seq):
            kd_ref[s, 0:WINDOW, :] = dup_halves(hk_ref[s]).astype(_BF)
            kd_ref[s, WINDOW + seq_rows:, :] = jnp.zeros((CHUNK, 2 * LANES), _BF)
            vd_ref[s] = v_blank
            store_v(s, 0, dup_halves(hv_ref[s]).astype(_BF))
            sbd_ref[...] = jnp.zeros((GLA_KDIM, GLA_VDIM), _F32)
            for h in range(GLA_HEADS):
                sbd_ref[h * GLA_DK:(h + 1) * GLA_DK, h * GLA_DV:(h + 1) * GLA_DV] = hs_ref[s, h]
            s_ref[s] = sbd_ref[...].T
            u_ref[s, 0:SUBLANES, :] = jnp.zeros((SUBLANES, CONV_DIM), _F32)
            u_ref[s, CONV_PAD - CONV_HIST:CONV_PAD, :] = hc_ref[s]

    def normed_bf16(v, gain_row):
        ms = jnp.mean(v * v, axis=-1, keepdims=True)
        return (v * lax.rsqrt(ms + NORM_EPS) * gain_row).astype(_BF)

    @pl.when(layer == 0)
    def _():
        xcur_ref[...] = x_ref[...]

    @pl.when((pl.program_id(0) == 0) & (t == 0) & (layer == 0))
    def _():
        hb_ref[...] = normed_bf16(x_ref[...], layer_row("norm_gain"))
    x = xcur_ref[...]
    hb = hb_ref[...]

    anchor_on = sinks_ref[layer, ATTN_HEADS] != 0.0

    ones64 = ones_ref[...]

    def post_q(g):
        def post(res):
            for b in range(PROJ_GROUP // LANES):
                c0 = g * PROJ_GROUP + b * LANES
                qn = _group_rms_scale(res[:, b * LANES:(b + 1) * LANES], HEAD_DIM, ones64) * q_gain
                qlo_ref[:, c0:c0 + LANES] = jnp.where(lo_half, qn, 0.0).astype(_BF)
                qhi_ref[:, c0:c0 + LANES] = jnp.where(lo_half, 0.0, qn).astype(_BF)
        return post

    def post_kv(res):
        kn = _group_rms_scale(res[:, 0:KV_DIM], HEAD_DIM, ones64) * k_gain
        vv = res[:, KV_DIM:2 * KV_DIM]
        kdup = dup_halves(kn).astype(_BF)
        vdup = dup_halves(vv).astype(_BF)
        for s in range(n_seq):
            kd_ref[s, WINDOW:WINDOW + seq_rows, :] = kdup[s * seq_rows:(s + 1) * seq_rows]
            store_v(s, WINDOW, vdup[s * seq_rows:(s + 1) * seq_rows])
        if carry:
            nk_ref[0] = kn[tile - WINDOW:tile]
            nv_ref[0] = vv[tile - WINDOW:tile]
        else:
            for s in range(n_seq):
                nk_ref[s, 0:WINDOW - seq_rows, :] = hk_ref[s, seq_rows:WINDOW, :]
                nv_ref[s, 0:WINDOW - seq_rows, :] = hv_ref[s, seq_rows:WINDOW, :]
                nk_ref[s, WINDOW - seq_rows:WINDOW, :] = kn[s * seq_rows:(s + 1) * seq_rows]
                nv_ref[s, WINDOW - seq_rows:WINDOW, :] = vv[s * seq_rows:(s + 1) * seq_rows]

    def post_gate(col0):
        def post(res):
            gate_ref[:, col0:col0 + PROJ_GROUP] = _silu(res)
        return post

    group_post = {C_AG // PROJ_GROUP: post_gate(0), C_AG // PROJ_GROUP + 1: post_gate(PROJ_GROUP),
                  C_GG // PROJ_GROUP: post_gate(ATTN_DIM), C_CG // PROJ_GROUP: post_gate(ATTN_DIM + GLA_VDIM)}

    def w_in_cols(c0, c1):
        if c1 <= C_CV:
            return w_in_a[layer, :, c0:c1]
        if c1 <= C_GLR:
            return w_in_b[layer, :, c0 - C_CV:c1 - C_CV]
        return w_in_c[layer, :, c0 - C_GLR:c1 - C_GLR]

    def compute_group(g):
        c0, c1 = g * PROJ_GROUP, min((g + 1) * PROJ_GROUP, IN_COLS_PACKED)
        res = _dot(hb, w_in_cols(c0, c1))
        if g in group_post:
            group_post[g](res)
        else:
            proj_ref[:, c0:c1] = res
        return jnp.where(anchor_on, res[0:1, 0:CONV_DIM], 0.0)

    def proj(c0, width):
        return proj_ref[:, c0:c0 + width]

    compute_group(C_CV // PROJ_GROUP)
    compute_group(C_CGL // PROJ_GROUP)
    u = proj(C_CV, CONV_DIM) * jax.nn.sigmoid(proj(C_CGL, CONV_DIM))
    for s in range(n_seq):
        u_ref[s, CONV_PAD:CONV_PAD + seq_rows, :] = u[s * seq_rows:(s + 1) * seq_rows]
    conv_acc = [jnp.broadcast_to(layer_row("conv_b"), (seq_rows, CONV_DIM))] * n_seq
    for rho in range(SUBLANES):
        anchor = compute_group(rho)
        for s in range(n_seq):
            frame_rows = seq_rows + (SUBLANES if rho else 0)
            frame = None
            for j in range(CONV_WIDTH):
                off = CONV_PAD - CONV_HIST + j
                if off % SUBLANES != rho:
                    continue
                term = (convw_ref[j:j + 1, :] + anchor) * u_ref[s, off - rho:off - rho + frame_rows, :]
                frame = term if frame is None else frame + term
            conv_acc[s] = conv_acc[s] + frame[rho:rho + seq_rows]
    compute_group(C_CG // PROJ_GROUP)
    compute_group(C_GLR // PROJ_GROUP)
    cc = conv_acc[0] if n_seq == 1 else jnp.concatenate(conv_acc, axis=0)

    for s in range(n_seq):
        nc_ref[s] = u_ref[s, seq_rows + CONV_PAD - CONV_HIST:seq_rows + CONV_PAD, :]
    if carry:
        u_ref[:, 0:CONV_PAD, :] = u_ref[:, seq_rows:seq_rows + CONV_PAD, :]

    mu = jnp.mean(cc, axis=-1, keepdims=True)
    cen = cc - mu
    var = jnp.mean(cen * cen, axis=-1, keepdims=True)
    ln = cen * lax.rsqrt(var + NORM_EPS) * layer_row("ln_gain") + layer_row("ln_bias")
    cpw = _dot(_silu(ln).astype(_BF), w_pw_ref[...]) + layer_row("b_pw")
    ycat_ref[:, ATTN_DIM + GLA_VDIM:D_MODEL] = cpw * gate_ref[:, ATTN_DIM + GLA_VDIM:D_MODEL]

    for g in range(ATTN_DIM // PROJ_GROUP):
        post_q(g)(proj(C_AQ + g * PROJ_GROUP, PROJ_GROUP))
    post_kv(proj(C_AK, 2 * KV_DIM))

    rows = ATTN_GROUP * CHUNK
    r_head1 = lax.broadcasted_iota(jnp.int32, (rows, 1), 0) >> CHUNK_SHIFT
    sink_cols = []
    for j in range(ATTN_KV_HEADS):
        sink = jnp.zeros((rows, 1), _F32)
        for r in range(ATTN_GROUP):
            sink = jnp.where(r_head1 == r, sinks_ref[layer, j * ATTN_GROUP + r], sink)
        sink_cols.append(sink * LOG2E)

    for s in range(n_seq):
        for c in range(n_chunk):
            r0 = s * seq_rows + c * CHUNK
            k0 = c * CHUNK
            steady = WINDOW // CHUNK
            variant = jnp.minimum(t * n_chunk + c, steady) if carry else steady
            for j in range(ATTN_KV_HEADS):
                qs = jnp.concatenate(
                    [qlo_ref[r0:r0 + CHUNK, (2 * j) * LANES:(2 * j + 1) * LANES],
                     qhi_ref[r0:r0 + CHUNK, (2 * j) * LANES:(2 * j + 1) * LANES],
                     qlo_ref[r0:r0 + CHUNK, (2 * j + 1) * LANES:(2 * j + 2) * LANES],
                     qhi_ref[r0:r0 + CHUNK, (2 * j + 1) * LANES:(2 * j + 2) * LANES]], axis=0)
                kb = kd_ref[s, k0:k0 + KBAND, j * LANES:(j + 1) * LANES]
                vb = vd_ref[s, k0:k0 + KBAND, 2 * j * LANES:(2 * j + 2) * LANES]
                sc = _dot_nt(qs, kb) - bias_ref[variant, j]
                sink = sink_cols[j]
                m = jnp.maximum(jnp.max(sc, axis=-1, keepdims=True), sink)
                p = jnp.exp2(sc - m)
                pv = _dot(p.astype(_BF), vb)
                o = pv[:, 0:LANES] / (pv[:, LANES:2 * LANES] + jnp.exp2(sink - m))
                for pb in range(2):
                    blk = 2 * j + pb
                    ob = jnp.where(lo_half, o[(2 * pb) * CHUNK:(2 * pb + 1) * CHUNK],
                                   o[(2 * pb + 1) * CHUNK:(2 * pb + 2) * CHUNK])
                    ycat_ref[r0:r0 + CHUNK, blk * LANES:(blk + 1) * LANES] = (
                        ob * gate_ref[r0:r0 + CHUNK, blk * LANES:(blk + 1) * LANES])

    if carry:
        kd_ref[:, 0:WINDOW, :] = kd_ref[:, seq_rows:seq_rows + WINDOW, :]
        vd_ref[:, 0:WINDOW, :] = vd_ref[:, seq_rows:seq_rows + WINDOW, :]

    z = _dot(proj(C_GLR, LANES).astype(_BF), w_up_ref[...]) + layer_row("gla_b")
    log_a = (jnp.minimum(z, 0.0) - jnp.log(1.0 + jnp.exp(-jnp.abs(z)))) * (1.0 / GLA_TAU)
    tri = tri_ref[...]
    la_hi = log_a.astype(_BF)
    la_lo = (log_a - la_hi.astype(_F32)).astype(_BF)
    bcum = _dot(tri, la_hi) + _dot(tri, la_lo)
    gq = proj(C_GQ, GLA_KDIM) * GLA_DK ** -0.5
    gk = proj(C_GK, GLA_KDIM)
    gv = proj(C_GV, GLA_VDIM)
    gate_g = gate_ref[:, ATTN_DIM:ATTN_DIM + GLA_VDIM]
    gla_g = layer_row("gla_gain")

    head_of_lane = lane128 >> GLA_DK_SHIFT
    a_r = lax.broadcasted_iota(jnp.int32, (CHUNK, GLA_HEADS * CHUNK), 0)
    a_c = lax.broadcasted_iota(jnp.int32, (CHUNK, GLA_HEADS * CHUNK), 1)
    causal = (a_c & (CHUNK - 1)) <= a_r
    vblk = lax.broadcasted_iota(jnp.int32, (1, GLA_VDIM), 1) >> GLA_DV_SHIFT
    bd_mask_t = ((lax.broadcasted_iota(jnp.int32, (GLA_VDIM, GLA_KDIM), 0) >> GLA_DV_SHIFT)
                 == (lax.broadcasted_iota(jnp.int32, (GLA_VDIM, GLA_KDIM), 1) >> GLA_DK_SHIFT))

    q_all = (gq * jnp.exp(bcum)).astype(_BF)
    k_all = gk * jnp.exp(-bcum)
    intra, kv_upd, decay_rows = [], [], []
    for s in range(n_seq):
        for c in range(n_chunk):
            r0 = s * seq_rows + c * CHUNK
            bc = bcum[r0:r0 + CHUNK]
            b_last = bc[CHUNK - 1:CHUNK]
            k_end = (gk[r0:r0 + CHUNK] * jnp.exp(b_last - bc)).astype(_BF)
            vf = gv[r0:r0 + CHUNK]
            k_stack = jnp.concatenate(
                [jnp.where(head_of_lane == h, k_all[r0:r0 + CHUNK], 0.0) for h in range(GLA_HEADS)],
                axis=0).astype(_BF)
            v_bd = jnp.concatenate(
                [jnp.where(vblk == h, vf, 0.0) for h in range(GLA_HEADS)], axis=0).astype(_BF)
            a = jnp.where(causal, _dot_nt(q_all[r0:r0 + CHUNK], k_stack), 0.0).astype(_BF)
            intra.append(_dot(a, v_bd))
            kv_upd.append(jnp.where(bd_mask_t, _dot_tn(vf.astype(_BF), k_end), 0.0))
            decay_rows.append(jnp.exp(b_last))
    o_rows = []
    for s in range(n_seq):
        state = s_ref[s]
        for c in range(n_chunk):
            i = s * n_chunk + c
            r0 = i * CHUNK
            o_rows.append(intra[i] + _dot_nt(q_all[r0:r0 + CHUNK], state.astype(_BF)))
            state = decay_rows[i] * state + kv_upd[i]
        s_ref[s] = state
    o_all = jnp.concatenate(o_rows, axis=0)
    ycat_ref[:, ATTN_DIM:ATTN_DIM + GLA_VDIM] = _group_rms_scale(o_all, GLA_DV, ones64) * gla_g * gate_g

    for s in range(n_seq):
        sbd_ref[...] = s_ref[s].T
        for h in range(GLA_HEADS):
            ns_ref[s, h] = sbd_ref[h * GLA_DK:(h + 1) * GLA_DK, h * GLA_DV:(h + 1) * GLA_DV]

    y = x
    for c0, c1 in ((ATTN_DIM + GLA_VDIM, D_MODEL), (0, ATTN_DIM), (ATTN_DIM, ATTN_DIM + GLA_VDIM)):
        y = y + _dot(ycat_ref[:, c0:c1].astype(_BF), w_out_ref[c0:c1, :])
    y_ref[...] = y
    xcur_ref[...] = y
    last_layer = layer == pl.num_programs(2) - 1
    nxt = jnp.where(last_layer, xnext_ref[...], y)
    next_gain = layer_row("norm_gain", jnp.where(last_layer, 0, layer + 1))
    hb_ref[...] = normed_bf16(nxt, next_gain)


def _run_stream(x2d, n_seqs, seq_len, hist, sinks, params, *, carry, q_pos0):
    depth = sinks.shape[0]
    if carry:
        assert q_pos0 == 0
        tile, n_seq, n_chunk = PROMPT_TILE, 1, PROMPT_TILE // CHUNK
        grid = (n_seqs, seq_len // tile, depth)
    else:
        assert seq_len == CHUNK and q_pos0 >= WINDOW
        tile, n_seq, n_chunk = SAMPLE_SEQS * CHUNK, SAMPLE_SEQS, 1
        grid = (n_seqs // n_seq, 1, depth)
    n_t = grid[1]
    seq_rows = n_chunk * CHUNK

    def const(shape):
        return pl.BlockSpec(shape, lambda g, t, l: (0,) * len(shape), pipeline_mode=pl.Buffered(1))

    def per_layer_group(shape):
        return pl.BlockSpec((depth, n_seq) + shape, lambda g, t, l: (0, g) + (0,) * len(shape))

    state_shapes = ((WINDOW, KV_DIM), (WINDOW, KV_DIM), (GLA_HEADS, GLA_DK, GLA_DV), (CONV_HIST, CONV_DIM))
    x_spec = pl.BlockSpec((tile, D_MODEL), lambda g, t, l: (g * n_t + t, 0))
    n_tiles = x2d.shape[0] // tile
    xnext_spec = pl.BlockSpec((tile, D_MODEL), lambda g, t, l: (jnp.minimum(g * n_t + t + 1, n_tiles - 1), 0))
    in_specs = [pl.BlockSpec(memory_space=pltpu.SMEM), x_spec, xnext_spec]
    args = [sinks, x2d, x2d]
    if not carry:
        in_specs += [per_layer_group(sh) for sh in state_shapes]
        args += list(hist)
    operands = tuple(params) + _shape_constants(tile)
    in_specs += [const(p.shape) for p in operands]
    args += list(operands)

    out_shape = (jax.ShapeDtypeStruct(x2d.shape, _F32),) + tuple(
        jax.ShapeDtypeStruct((depth, n_seqs) + sh, _F32) for sh in state_shapes)
    out_specs = (x_spec,) + tuple(per_layer_group(sh) for sh in state_shapes)
    scratch = [pltpu.VMEM((depth, n_seq, WINDOW + seq_rows + CHUNK, ATTN_KV_HEADS * LANES), _BF),
               pltpu.VMEM((depth, n_seq, WINDOW + seq_rows + CHUNK, ATTN_KV_HEADS * 2 * LANES), _BF),
               pltpu.VMEM((depth, n_seq, GLA_VDIM, GLA_KDIM), _F32),
               pltpu.VMEM((depth, n_seq, CONV_PAD + seq_rows, CONV_DIM), _F32),
               pltpu.VMEM((tile, ATTN_DIM), _BF),
               pltpu.VMEM((tile, ATTN_DIM), _BF),
               pltpu.VMEM((tile, D_MODEL), _F32),
               pltpu.VMEM((tile, IN_COLS_PACKED), _F32),
               pltpu.VMEM((tile, D_MODEL), _F32),
               pltpu.VMEM((tile, D_MODEL), _BF),
               pltpu.VMEM((GLA_KDIM, GLA_VDIM), _F32),
               pltpu.VMEM((tile, D_MODEL), _F32)]
    body = functools.partial(_stream_kernel, tile=tile, n_seq=n_seq, n_chunk=n_chunk, carry=carry, q_pos0=q_pos0)
    return pl.pallas_call(
        body, grid=grid, in_specs=in_specs, out_specs=out_specs, out_shape=out_shape,
        scratch_shapes=scratch,
        compiler_params=pltpu.CompilerParams(dimension_semantics=("arbitrary", "arbitrary", "arbitrary"),
                                             vmem_limit_bytes=VMEM_LIMIT_BYTES),
        name="mixer_prompt" if carry else "mixer_sample",
    )(*args)


def _pack_params(norm_gain, w_in, q_norm_gain, k_norm_gain, gla_w_gate_up, gla_b_gate, gla_norm_gain,
                 conv_w, conv_b, conv_ln_gain, conv_ln_bias, conv_w_pw, conv_b_pw, w_out):
    w_in_a = w_in[:, :, :GLR_SRC].astype(_BF)
    w_in_c = jnp.pad(w_in[:, :, GLR_SRC:GLR_SRC + GLA_LOWRANK],
                     ((0, 0), (0, 0), (0, LANES - GLA_LOWRANK))).astype(_BF)
    w_in_b = w_in[:, :, GLR_SRC + GLA_LOWRANK:].astype(_BF)
    pieces = dict(norm_gain=norm_gain, q_gain=q_norm_gain, k_gain=k_norm_gain, gla_b=gla_b_gate,
                  gla_gain=gla_norm_gain, conv_b=conv_b, ln_gain=conv_ln_gain, ln_bias=conv_ln_bias, b_pw=conv_b_pw)
    depth = w_in.shape[0]
    parts = []
    for name, n in VEC_LAYOUT:
        parts += [pieces[name].astype(_F32)] * (n // pieces[name].shape[1])
        parts.append(jnp.zeros((depth, D_MODEL - n), _F32))
    parts.append(jnp.zeros((depth, (VEC_ROWS - len(VEC_LAYOUT)) * D_MODEL), _F32))
    vecs = jnp.concatenate(parts, axis=1).reshape(depth, VEC_ROWS, D_MODEL)
    w_up_p = jnp.pad(gla_w_gate_up, ((0, 0), (0, LANES - GLA_LOWRANK), (0, 0))).astype(_BF)
    convw = jnp.pad(conv_w.astype(_F32), ((0, 0), (0, CONV_PAD - CONV_WIDTH), (0, 0)))
    return w_in_a, w_in_b, w_in_c, w_up_p, conv_w_pw.astype(_BF), w_out.astype(_BF), vecs, convw


def kernel(x_prompt, x_sample, cache_k, cache_v, state_gla, state_conv, norm_gain, w_in, q_norm_gain, k_norm_gain, attn_sinks, gla_w_gate_up, gla_b_gate, gla_norm_gain, conv_w, conv_b, conv_ln_gain, conv_ln_bias, conv_w_pw, conv_b_pw, w_out):
    depth = w_in.shape[0]
    bp, lp, _ = x_prompt.shape
    bs, ls, _ = x_sample.shape
    params = _pack_params(norm_gain, w_in, q_norm_gain, k_norm_gain, gla_w_gate_up, gla_b_gate, gla_norm_gain,
                          conv_w, conv_b, conv_ln_gain, conv_ln_bias, conv_w_pw, conv_b_pw, w_out)
    sinks = jnp.pad(attn_sinks.astype(_F32), ((0, 0), (0, ATTN_HEADS)))
    yp, pk, pv, ps, pc = _run_stream(x_prompt.reshape(bp * lp, D_MODEL), bp, lp, None, sinks, params,
                                     carry=True, q_pos0=0)
    hist = (cache_k.reshape(depth, bs, WINDOW, KV_DIM), cache_v.reshape(depth, bs, WINDOW, KV_DIM),
            state_gla, state_conv)
    ys, sk, sv, ss, sc = _run_stream(x_sample.reshape(bs * ls, D_MODEL), bs, ls, hist, sinks, params,
                                     carry=False, q_pos0=PAST_LEN)
    kv_shape = (WINDOW, ATTN_KV_HEADS, HEAD_DIM)
    return (yp.reshape(bp, lp, D_MODEL), ys.reshape(bs, ls, D_MODEL),
            pk.reshape((depth, bp) + kv_shape), pv.reshape((depth, bp) + kv_shape), ps, pc,
            sk.reshape((depth, bs) + kv_shape), sv.reshape((depth, bs) + kv_shape), ss, sc)
```

```python
import functools
import math

import numpy as np
import jax
import jax.numpy as jnp
from jax import lax
from jax.experimental import pallas as pl
from jax.experimental.pallas import tpu as pltpu

D_MODEL = 1024
CHUNK = 64
ATTN_HEADS = 8
ATTN_KV_HEADS = 2
HEAD_DIM = 64
ATTN_GROUP = ATTN_HEADS // ATTN_KV_HEADS
ATTN_DIM = ATTN_HEADS * HEAD_DIM
KV_DIM = ATTN_KV_HEADS * HEAD_DIM
WINDOW = 128
BAND = WINDOW + CHUNK
KBAND = WINDOW + 2 * CHUNK
GLA_HEADS = 4
GLA_DK = 32
GLA_DV = 64
GLA_KDIM = GLA_HEADS * GLA_DK
GLA_VDIM = GLA_HEADS * GLA_DV
GLA_LOWRANK = 16
GLA_TAU = 16.0
CONV_DIM = 256
CONV_WIDTH = 31
CONV_HIST = CONV_WIDTH - 1
NORM_EPS = 1e-6
NEG_INF = -1e30
PAST_LEN = 4096
LOG2E = math.log2(math.e)

LANES = 128
SUBLANES = 8
CHUNK_SHIFT = CHUNK.bit_length() - 1
GLA_DK_SHIFT = GLA_DK.bit_length() - 1
GLA_DV_SHIFT = GLA_DV.bit_length() - 1
C_AQ, C_AK, C_AV, C_AG = 0, 512, 640, 768
C_GQ, C_GK, C_GV, C_GG = 1280, 1408, 1536, 1792
C_CV, C_CGL, C_CG, C_GLR = 2048, 2304, 2560, 2816
IN_COLS_PACKED = C_GLR + LANES
PROJ_GROUP = 256
CONV_PAD = 32
GLR_SRC = 2048
VEC_LAYOUT = (("norm_gain", D_MODEL), ("q_gain", LANES), ("k_gain", KV_DIM), ("gla_b", GLA_KDIM),
              ("gla_gain", GLA_VDIM), ("conv_b", CONV_DIM), ("ln_gain", CONV_DIM), ("ln_bias", CONV_DIM),
              ("b_pw", CONV_DIM))
VEC_ROWS = 16
VEC_ROW_OF = {name: (k, n) for k, (name, n) in enumerate(VEC_LAYOUT)}

PROMPT_TILE = 512
SAMPLE_SEQS = 4
VMEM_LIMIT_BYTES = 56 * 1024 * 1024

_BF = jnp.bfloat16
_F32 = jnp.float32


def _shape_constants(tile):
    rows = np.arange(ATTN_GROUP * CHUNK)
    dist = np.abs((rows % CHUNK)[:, None] + WINDOW - np.arange(KBAND)[None, :]).astype(np.float64)
    bias = np.stack([
        (2.0 ** (-8.0 * (j * ATTN_GROUP + rows // CHUNK + 1) / ATTN_HEADS))[:, None] * LOG2E * dist
        for j in range(ATTN_KV_HEADS)]).astype(np.float32)
    bias[:, :, BAND:] = -NEG_INF
    variants = []
    for n in range(WINDOW // CHUNK + 1):
        b = bias.copy()
        b[:, :, :max(WINDOW - n * CHUNK, 0)] = -NEG_INF
        variants.append(b)
    bias = np.stack(variants)
    tr = np.arange(tile)
    tri = ((tr[:, None] // CHUNK == tr[None, :] // CHUNK) & (tr[None, :] <= tr[:, None])).astype(np.float32)
    ln = np.arange(LANES) // HEAD_DIM
    ones = (ln[:, None] == ln[None, :]).astype(np.float32)
    return jnp.asarray(bias), jnp.asarray(tri, dtype=_BF), jnp.asarray(ones, dtype=_BF)


def _group_rms_scale(x, group, ones_bd):
    outs = []
    for c0 in range(0, x.shape[1], LANES):
        blk = x[:, c0:c0 + LANES]
        ss = _dot((blk * blk).astype(_BF), ones_bd)
        outs.append(blk * lax.rsqrt(ss * (1.0 / group) + NORM_EPS))
    return outs[0] if len(outs) == 1 else jnp.concatenate(outs, axis=1)


def _silu(x):
    return x * jax.nn.sigmoid(x)


def _dot(a, b):
    return jnp.dot(a, b, preferred_element_type=_F32)


def _dot_nt(a, b):
    return lax.dot_general(a, b, (((1,), (1,)), ((), ())), preferred_element_type=_F32)


def _dot_tn(a, b):
    return lax.dot_general(a, b, (((0,), (0,)), ((), ())), preferred_element_type=_F32)


def _stream_kernel(*refs, tile, n_seq, n_chunk, carry, q_pos0):
    if carry:
        (sinks_ref, x_ref, xnext_ref, *rest) = refs
    else:
        (sinks_ref, x_ref, xnext_ref, hk_all, hv_all, hs_all, hc_all, *rest) = refs
    (w_in_a, w_in_b, w_in_c, w_up_all, w_pw_all, w_out_all, vec_all, convw_all, bias_ref, tri_ref, ones_ref,
     y_ref, nk_all, nv_all, ns_all, nc_all,
     kd_all, vd_all, s_all, u_all, qlo_ref, qhi_ref, ycat_ref, proj_ref, xcur_ref, hb_ref, sbd_ref,
     gate_ref) = rest
    t = pl.program_id(1)
    layer = pl.program_id(2)
    seq_rows = n_chunk * CHUNK
    w_up_ref, w_pw_ref, w_out_ref = (r.at[layer] for r in (w_up_all, w_pw_all, w_out_all))
    convw_ref = convw_all.at[layer]

    def layer_row(name, which=layer):
        r, n = VEC_ROW_OF[name]
        return vec_all[which, r:r + 1, 0:n]

    q_gain = layer_row("q_gain") * (LOG2E * HEAD_DIM ** -0.5)
    k_gain = layer_row("k_gain")
    kd_ref, vd_ref, s_ref, u_ref = (r.at[layer] for r in (kd_all, vd_all, s_all, u_all))
    nk_ref, nv_ref, ns_ref, nc_ref = (r.at[layer] for r in (nk_all, nv_all, ns_all, nc_all))
    if not carry:
        hk_ref, hv_ref, hs_ref, hc_ref = (r.at[layer] for r in (hk_all, hv_all, hs_all, hc_all))

    lane128 = lax.broadcasted_iota(jnp.int32, (1, LANES), 1)
    lo_half = lane128 < HEAD_DIM

    def dup_halves(a):
        sw = pltpu.roll(a, HEAD_DIM, 1)
        return jnp.concatenate([jnp.where(lo_half, a, sw), jnp.where(lo_half, sw, a)], axis=1)

    buf_rows = WINDOW + seq_rows + CHUNK
    zeros_blk = jnp.zeros((buf_rows, LANES), _BF)
    ones_blk = jnp.ones((buf_rows, LANES), _BF)
    v_blank = jnp.concatenate([zeros_blk, ones_blk] * ATTN_KV_HEADS, axis=1)

    def store_v(s, row0, vdup_rows):
        for j in range(ATTN_KV_HEADS):
            vd_ref[s, row0:row0 + vdup_rows.shape[0], 2 * j * LANES:(2 * j + 1) * LANES] = (
                vdup_rows[:, j * LANES:(j + 1) * LANES])

    if carry:
        @pl.when(t == 0)
        def _():
            kd_ref[...] = jnp.zeros(kd_ref.shape, _BF)
            for s in range(n_seq):
                vd_ref[s] = v_blank
            s_ref[...] = jnp.zeros(s_ref.shape, _F32)
            u_ref[:, 0:CONV_PAD, :] = jnp.zeros((n_seq, CONV_PAD, CONV_DIM), _F32)
    else:
        for s in range(n_seq):
            kd_ref[s, 0:WINDOW, :] = dup_halves(hk_ref[s]).astype(_BF)
            kd_ref[s, WINDOW + seq_rows:, :] = jnp.zeros((CHUNK, 2 * LANES), _BF)
            vd_ref[s] = v_blank
            store_v(s, 0, dup_halves(hv_ref[s]).astype(_BF))
            sbd_ref[...] = jnp.zeros((GLA_KDIM, GLA_VDIM), _F32)
            for h in range(GLA_HEADS):
                sbd_ref[h * GLA_DK:(h + 1) * GLA_DK, h * GLA_DV:(h + 1) * GLA_DV] = hs_ref[s, h]
            s_ref[s] = sbd_ref[...].T
            u_ref[s, 0:SUBLANES, :] = jnp.zeros((SUBLANES, CONV_DIM), _F32)
            u_ref[s, CONV_PAD - CONV_HIST:CONV_PAD, :] = hc_ref[s]

    def normed_bf16(v, gain_row):
        ms = jnp.mean(v * v, axis=-1, keepdims=True)
        return (v * lax.rsqrt(ms + NORM_EPS) * gain_row).astype(_BF)

    @pl.when(layer == 0)
    def _():
        xcur_ref[...] = x_ref[...]

    @pl.when((pl.program_id(0) == 0) & (t == 0) & (layer == 0))
    def _():
        hb_ref[...] = normed_bf16(x_ref[...], layer_row("norm_gain"))
    x = xcur_ref[...]
    hb = hb_ref[...]

    anchor_on = sinks_ref[layer, ATTN_HEADS] != 0.0

    ones64 = ones_ref[...]

    def post_q(g):
        def post(res):
            for b in range(PROJ_GROUP // LANES):
                c0 = g * PROJ_GROUP + b * LANES
                qn = _group_rms_scale(res[:, b * LANES:(b + 1) * LANES], HEAD_DIM, ones64) * q_gain
                qlo_ref[:, c0:c0 + LANES] = jnp.where(lo_half, qn, 0.0).astype(_BF)
                qhi_ref[:, c0:c0 + LANES] = jnp.where(lo_half, 0.0, qn).astype(_BF)
        return post

    def post_kv(res):
        kn = _group_rms_scale(res[:, 0:KV_DIM], HEAD_DIM, ones64) * k_gain
        vv = res[:, KV_DIM:2 * KV_DIM]
        kdup = dup_halves(kn).astype(_BF)
        vdup = dup_halves(vv).astype(_BF)
        for s in range(n_seq):
            kd_ref[s, WINDOW:WINDOW + seq_rows, :] = kdup[s * seq_rows:(s + 1) * seq_rows]
            store_v(s, WINDOW, vdup[s * seq_rows:(s + 1) * seq_rows])
        if carry:
            nk_ref[0] = kn[tile - WINDOW:tile]
            nv_ref[0] = vv[tile - WINDOW:tile]
        else:
            for s in range(n_seq):
                nk_ref[s, 0:WINDOW - seq_rows, :] = hk_ref[s, seq_rows:WINDOW, :]
                nv_ref[s, 0:WINDOW - seq_rows, :] = hv_ref[s, seq_rows:WINDOW, :]
                nk_ref[s, WINDOW - seq_rows:WINDOW, :] = kn[s * seq_rows:(s + 1) * seq_rows]
                nv_ref[s, WINDOW - seq_rows:WINDOW, :] = vv[s * seq_rows:(s + 1) * seq_rows]

    def post_gate(col0):
        def post(res):
            gate_ref[:, col0:col0 + PROJ_GROUP] = _silu(res)
        return post

    group_post = {C_AG // PROJ_GROUP: post_gate(0), C_AG // PROJ_GROUP + 1: post_gate(PROJ_GROUP),
                  C_GG // PROJ_GROUP: post_gate(ATTN_DIM), C_CG // PROJ_GROUP: post_gate(ATTN_DIM + GLA_VDIM)}

    def w_in_cols(c0, c1):
        if c1 <= C_CV:
            return w_in_a[layer, :, c0:c1]
        if c1 <= C_GLR:
            return w_in_b[layer, :, c0 - C_CV:c1 - C_CV]
        return w_in_c[layer, :, c0 - C_GLR:c1 - C_GLR]

    def compute_group(g):
        c0, c1 = g * PROJ_GROUP, min((g + 1) * PROJ_GROUP, IN_COLS_PACKED)
        res = _dot(hb, w_in_cols(c0, c1))
        if g in group_post:
            group_post[g](res)
        else:
            proj_ref[:, c0:c1] = res
        return jnp.where(anchor_on, res[0:1, 0:CONV_DIM], 0.0)

    def proj(c0, width):
        return proj_ref[:, c0:c0 + width]

    compute_group(C_CV // PROJ_GROUP)
    compute_group(C_CGL // PROJ_GROUP)
    u = proj(C_CV, CONV_DIM) * jax.nn.sigmoid(proj(C_CGL, CONV_DIM))
    for s in range(n_seq):
        u_ref[s, CONV_PAD:CONV_PAD + seq_rows, :] = u[s * seq_rows:(s + 1) * seq_rows]
    conv_acc = [jnp.broadcast_to(layer_row("conv_b"), (seq_rows, CONV_DIM))] * n_seq
    for rho in range(SUBLANES):
        anchor = compute_group(rho)
        for s in range(n_seq):
            frame_rows = seq_rows + (SUBLANES if rho else 0)
            frame = None
            for j in range(CONV_WIDTH):
                off = CONV_PAD - CONV_HIST + j
                if off % SUBLANES != rho:
                    continue
                term = (convw_ref[j:j + 1, :] + anchor) * u_ref[s, off - rho:off - rho + frame_rows, :]
                frame = term if frame is None else frame + term
            conv_acc[s] = conv_acc[s] + frame[rho:rho + seq_rows]
    compute_group(C_CG // PROJ_GROUP)
    compute_group(C_GLR // PROJ_GROUP)
    cc = conv_acc[0] if n_seq == 1 else jnp.concatenate(conv_acc, axis=0)

    for s in range(n_seq):
        nc_ref[s] = u_ref[s, seq_rows + CONV_PAD - CONV_HIST:seq_rows + CONV_PAD, :]
    if carry:
        u_ref[:, 0:CONV_PAD, :] = u_ref[:, seq_rows:seq_rows + CONV_PAD, :]

    mu = jnp.mean(cc, axis=-1, keepdims=True)
    cen = cc - mu
    var = jnp.mean(cen * cen, axis=-1, keepdims=True)
    ln = cen * lax.rsqrt(var + NORM_EPS) * layer_row("ln_gain") + layer_row("ln_bias")
    cpw = _dot(_silu(ln).astype(_BF), w_pw_ref[...]) + layer_row("b_pw")
    ycat_ref[:, ATTN_DIM + GLA_VDIM:D_MODEL] = cpw * gate_ref[:, ATTN_DIM + GLA_VDIM:D_MODEL]

    for g in range(ATTN_DIM // PROJ_GROUP):
        post_q(g)(proj(C_AQ + g * PROJ_GROUP, PROJ_GROUP))
    post_kv(proj(C_AK, 2 * KV_DIM))

    rows = ATTN_GROUP * CHUNK
    r_head1 = lax.broadcasted_iota(jnp.int32, (rows, 1), 0) >> CHUNK_SHIFT
    sink_cols = []
    for j in range(ATTN_KV_HEADS):
        sink = jnp.zeros((rows, 1), _F32)
        for r in range(ATTN_GROUP):
            sink = jnp.where(r_head1 == r, sinks_ref[layer, j * ATTN_GROUP + r], sink)
        sink_cols.append(sink * LOG2E)

    for s in range(n_seq):
        for c in range(n_chunk):
            r0 = s * seq_rows + c * CHUNK
            k0 = c * CHUNK
            steady = WINDOW // CHUNK
            variant = jnp.minimum(t * n_chunk + c, steady) if carry else steady
            for j in range(ATTN_KV_HEADS):
                qs = jnp.concatenate(
                    [qlo_ref[r0:r0 + CHUNK, (2 * j) * LANES:(2 * j + 1) * LANES],
                     qhi_ref[r0:r0 + CHUNK, (2 * j) * LANES:(2 * j + 1) * LANES],
                     qlo_ref[r0:r0 + CHUNK, (2 * j + 1) * LANES:(2 * j + 2) * LANES],
                     qhi_ref[r0:r0 + CHUNK, (2 * j + 1) * LANES:(2 * j + 2) * LANES]], axis=0)
                kb = kd_ref[s, k0:k0 + KBAND, j * LANES:(j + 1) * LANES]
                vb = vd_ref[s, k0:k0 + KBAND, 2 * j * LANES:(2 * j + 2) * LANES]
                sc = _dot_nt(qs, kb) - bias_ref[variant, j]
                sink = sink_cols[j]
                m = jnp.maximum(jnp.max(sc, axis=-1, keepdims=True), sink)
                p = jnp.exp2(sc - m)
                pv = _dot(p.astype(_BF), vb)
                o = pv[:, 0:LANES] / (pv[:, LANES:2 * LANES] + jnp.exp2(sink - m))
                for pb in range(2):
                    blk = 2 * j + pb
                    ob = jnp.where(lo_half, o[(2 * pb) * CHUNK:(2 * pb + 1) * CHUNK],
                                   o[(2 * pb + 1) * CHUNK:(2 * pb + 2) * CHUNK])
                    ycat_ref[r0:r0 + CHUNK, blk * LANES:(blk + 1) * LANES] = (
                        ob * gate_ref[r0:r0 + CHUNK, blk * LANES:(blk + 1) * LANES])

    if carry:
        kd_ref[:, 0:WINDOW, :] = kd_ref[:, seq_rows:seq_rows + WINDOW, :]
        vd_ref[:, 0:WINDOW, :] = vd_ref[:, seq_rows:seq_rows + WINDOW, :]

    z = _dot(proj(C_GLR, LANES).astype(_BF), w_up_ref[...]) + layer_row("gla_b")
    log_a = (jnp.minimum(z, 0.0) - jnp.log(1.0 + jnp.exp(-jnp.abs(z)))) * (1.0 / GLA_TAU)
    tri = tri_ref[...]
    la_hi = log_a.astype(_BF)
    la_lo = (log_a - la_hi.astype(_F32)).astype(_BF)
    bcum = _dot(tri, la_hi) + _dot(tri, la_lo)
    gq = proj(C_GQ, GLA_KDIM) * GLA_DK ** -0.5
    gk = proj(C_GK, GLA_KDIM)
    gv = proj(C_GV, GLA_VDIM)
    gate_g = gate_ref[:, ATTN_DIM:ATTN_DIM + GLA_VDIM]
    gla_g = layer_row("gla_gain")

    head_of_lane = lane128 >> GLA_DK_SHIFT
    a_r = lax.broadcasted_iota(jnp.int32, (CHUNK, GLA_HEADS * CHUNK), 0)
    a_c = lax.broadcasted_iota(jnp.int32, (CHUNK, GLA_HEADS * CHUNK), 1)
    causal = (a_c & (CHUNK - 1)) <= a_r
    vblk = lax.broadcasted_iota(jnp.int32, (1, GLA_VDIM), 1) >> GLA_DV_SHIFT
    bd_mask_t = ((lax.broadcasted_iota(jnp.int32, (GLA_VDIM, GLA_KDIM), 0) >> GLA_DV_SHIFT)
                 == (lax.broadcasted_iota(jnp.int32, (GLA_VDIM, GLA_KDIM), 1) >> GLA_DK_SHIFT))

    q_all = (gq * jnp.exp(bcum)).astype(_BF)
    k_all = gk * jnp.exp(-bcum)
    intra, kv_upd, decay_rows = [], [], []
    for s in range(n_seq):
        for c in range(n_chunk):
            r0 = s * seq_rows + c * CHUNK
            bc = bcum[r0:r0 + CHUNK]
            b_last = bc[CHUNK - 1:CHUNK]
            k_end = (gk[r0:r0 + CHUNK] * jnp.exp(b_last - bc)).astype(_BF)
            vf = gv[r0:r0 + CHUNK]
            k_stack = jnp.concatenate(
                [jnp.where(head_of_lane == h, k_all[r0:r0 + CHUNK], 0.0) for h in range(GLA_HEADS)],
                axis=0).astype(_BF)
            v_bd = jnp.concatenate(
                [jnp.where(vblk == h, vf, 0.0) for h in range(GLA_HEADS)], axis=0).astype(_BF)
            a = jnp.where(causal, _dot_nt(q_all[r0:r0 + CHUNK], k_stack), 0.0).astype(_BF)
            intra.append(_dot(a, v_bd))
            kv_upd.append(jnp.where(bd_mask_t, _dot_tn(vf.astype(_BF), k_end), 0.0))
            decay_rows.append(jnp.exp(b_last))
    o_rows = []
    for s in range(n_seq):
        state = s_ref[s]
        for c in range(n_chunk):
            i = s * n_chunk + c
            r0 = i * CHUNK
            o_rows.append(intra[i] + _dot_nt(q_all[r0:r0 + CHUNK], state.astype(_BF)))
            state = decay_rows[i] * state + kv_upd[i]
        s_ref[s] = state
    o_all = jnp.concatenate(o_rows, axis=0)
    ycat_ref[:, ATTN_DIM:ATTN_DIM + GLA_VDIM] = _group_rms_scale(o_all, GLA_DV, ones64) * gla_g * gate_g

    for s in range(n_seq):
        sbd_ref[...] = s_ref[s].T
        for h in range(GLA_HEADS):
            ns_ref[s, h] = sbd_ref[h * GLA_DK:(h + 1) * GLA_DK, h * GLA_DV:(h + 1) * GLA_DV]

    y = x
    for c0, c1 in ((ATTN_DIM + GLA_VDIM, D_MODEL), (0, ATTN_DIM), (ATTN_DIM, ATTN_DIM + GLA_VDIM)):
        y = y + _dot(ycat_ref[:, c0:c1].astype(_BF), w_out_ref[c0:c1, :])
    y_ref[...] = y
    xcur_ref[...] = y
    last_layer = layer == pl.num_programs(2) - 1
    nxt = jnp.where(last_layer, xnext_ref[...], y)
    next_gain = layer_row("norm_gain", jnp.where(last_layer, 0, layer + 1))
    hb_ref[...] = normed_bf16(nxt, next_gain)


def _run_stream(x2d, n_seqs, seq_len, hist, sinks, params, *, carry, q_pos0):
    depth = sinks.shape[0]
    if carry:
        assert q_pos0 == 0
        tile, n_seq, n_chunk = PROMPT_TILE, 1, PROMPT_TILE // CHUNK
        grid = (n_seqs, seq_len // tile, depth)
    else:
        assert seq_len == CHUNK and q_pos0 >= WINDOW
        tile, n_seq, n_chunk = SAMPLE_SEQS * CHUNK, SAMPLE_SEQS, 1
        grid = (n_seqs // n_seq, 1, depth)
    n_t = grid[1]
    seq_rows = n_chunk * CHUNK

    def const(shape):
        return pl.BlockSpec(shape, lambda g, t, l: (0,) * len(shape), pipeline_mode=pl.Buffered(1))

    def per_layer_group(shape):
        return pl.BlockSpec((depth, n_seq) + shape, lambda g, t, l: (0, g) + (0,) * len(shape))

    state_shapes = ((WINDOW, KV_DIM), (WINDOW, KV_DIM), (GLA_HEADS, GLA_DK, GLA_DV), (CONV_HIST, CONV_DIM))
    x_spec = pl.BlockSpec((tile, D_MODEL), lambda g, t, l: (g * n_t + t, 0))
    n_tiles = x2d.shape[0] // tile
    xnext_spec = pl.BlockSpec((tile, D_MODEL), lambda g, t, l: (jnp.minimum(g * n_t + t + 1, n_tiles - 1), 0))
    in_specs = [pl.BlockSpec(memory_space=pltpu.SMEM), x_spec, xnext_spec]
    args = [sinks, x2d, x2d]
    if not carry:
        in_specs += [per_layer_group(sh) for sh in state_shapes]
        args += list(hist)
    operands = tuple(params) + _shape_constants(tile)
    in_specs += [const(p.shape) for p in operands]
    in_specs[len(args)] = const(operands[0].shape[:2] + (GLR_SRC,))
    args += list(operands)

    out_shape = (jax.ShapeDtypeStruct(x2d.shape, _F32),) + tuple(
        jax.ShapeDtypeStruct((depth, n_seqs) + sh, _F32) for sh in state_shapes)
    out_specs = (x_spec,) + tuple(per_layer_group(sh) for sh in state_shapes)
    scratch = [pltpu.VMEM((depth, n_seq, WINDOW + seq_rows + CHUNK, ATTN_KV_HEADS * LANES), _BF),
               pltpu.VMEM((depth, n_seq, WINDOW + seq_rows + CHUNK, ATTN_KV_HEADS * 2 * LANES), _BF),
               pltpu.VMEM((depth, n_seq, GLA_VDIM, GLA_KDIM), _F32),
               pltpu.VMEM((depth, n_seq, CONV_PAD + seq_rows, CONV_DIM), _F32),
               pltpu.VMEM((tile, ATTN_DIM), _BF),
               pltpu.VMEM((tile, ATTN_DIM), _BF),
               pltpu.VMEM((tile, D_MODEL), _F32),
               pltpu.VMEM((tile, IN_COLS_PACKED), _F32),
               pltpu.VMEM((tile, D_MODEL), _F32),
               pltpu.VMEM((tile, D_MODEL), _BF),
               pltpu.VMEM((GLA_KDIM, GLA_VDIM), _F32),
               pltpu.VMEM((tile, D_MODEL), _F32)]
    body = functools.partial(_stream_kernel, tile=tile, n_seq=n_seq, n_chunk=n_chunk, carry=carry, q_pos0=q_pos0)
    return pl.pallas_call(
        body, grid=grid, in_specs=in_specs, out_specs=out_specs, out_shape=out_shape,
        scratch_shapes=scratch,
        compiler_params=pltpu.CompilerParams(dimension_semantics=("arbitrary", "arbitrary", "arbitrary"),
                                             vmem_limit_bytes=VMEM_LIMIT_BYTES),
        name="mixer_prompt" if carry else "mixer_sample",
    )(*args)


def _pack_params(norm_gain, w_in, q_norm_gain, k_norm_gain, gla_w_gate_up, gla_b_gate, gla_norm_gain,
                 conv_w, conv_b, conv_ln_gain, conv_ln_bias, conv_w_pw, conv_b_pw, w_out):
    w_in_a = w_in.astype(_BF)
    w_in_c = jnp.pad(w_in_a[:, :, GLR_SRC:GLR_SRC + GLA_LOWRANK], ((0, 0), (0, 0), (0, LANES - GLA_LOWRANK)))
    w_in_b = w_in_a[:, :, GLR_SRC + GLA_LOWRANK:]
    pieces = dict(norm_gain=norm_gain, q_gain=q_norm_gain, k_gain=k_norm_gain, gla_b=gla_b_gate,
                  gla_gain=gla_norm_gain, conv_b=conv_b, ln_gain=conv_ln_gain, ln_bias=conv_ln_bias, b_pw=conv_b_pw)
    depth = w_in.shape[0]
    parts = []
    for name, n in VEC_LAYOUT:
        parts += [pieces[name].astype(_F32)] * (n // pieces[name].shape[1])
        parts.append(jnp.zeros((depth, D_MODEL - n), _F32))
    parts.append(jnp.zeros((depth, (VEC_ROWS - len(VEC_LAYOUT)) * D_MODEL), _F32))
    vecs = jnp.concatenate(parts, axis=1).reshape(depth, VEC_ROWS, D_MODEL)
    w_up_p = jnp.pad(gla_w_gate_up, ((0, 0), (0, LANES - GLA_LOWRANK), (0, 0))).astype(_BF)
    convw = jnp.pad(conv_w.astype(_F32), ((0, 0), (0, CONV_PAD - CONV_WIDTH), (0, 0)))
    return w_in_a, w_in_b, w_in_c, w_up_p, conv_w_pw.astype(_BF), w_out.astype(_BF), vecs, convw


def kernel(x_prompt, x_sample, cache_k, cache_v, state_gla, state_conv, norm_gain, w_in, q_norm_gain, k_norm_gain, attn_sinks, gla_w_gate_up, gla_b_gate, gla_norm_gain, conv_w, conv_b, conv_ln_gain, conv_ln_bias, conv_w_pw, conv_b_pw, w_out):
    depth = w_in.shape[0]
    bp, lp, _ = x_prompt.shape
    bs, ls, _ = x_sample.shape
    params = _pack_params(norm_gain, w_in, q_norm_gain, k_norm_gain, gla_w_gate_up, gla_b_gate, gla_norm_gain,
                          conv_w, conv_b, conv_ln_gain, conv_ln_bias, conv_w_pw, conv_b_pw, w_out)
    sinks = jnp.pad(attn_sinks.astype(_F32), ((0, 0), (0, ATTN_HEADS)))
    yp, pk, pv, ps, pc = _run_stream(x_prompt.reshape(bp * lp, D_MODEL), bp, lp, None, sinks, params,
                                     carry=True, q_pos0=0)
    hist = (cache_k.reshape(depth, bs, WINDOW, KV_DIM), cache_v.reshape(depth, bs, WINDOW, KV_DIM),
            state_gla, state_conv)
    ys, sk, sv, ss, sc = _run_stream(x_sample.reshape(bs * ls, D_MODEL), bs, ls, hist, sinks, params,
                                     carry=False, q_pos0=PAST_LEN)
    kv_shape = (WINDOW, ATTN_KV_HEADS, HEAD_DIM)
    return (yp.reshape(bp, lp, D_MODEL), ys.reshape(bs, ls, D_MODEL),
            pk.reshape((depth, bp) + kv_shape), pv.reshape((depth, bp) + kv_shape), ps, pc,
            sk.reshape((depth, bs) + kv_shape), sv.reshape((depth, bs) + kv_shape), ss, sc)
```

```python
import functools
import math

import numpy as np
import jax
import jax.numpy as jnp
from jax import lax
from jax.experimental import pallas as pl
from jax.experimental.pallas import tpu as pltpu

D_MODEL = 1024
CHUNK = 64
ATTN_HEADS = 8
ATTN_KV_HEADS = 2
HEAD_DIM = 64
ATTN_GROUP = ATTN_HEADS // ATTN_KV_HEADS
ATTN_DIM = ATTN_HEADS * HEAD_DIM
KV_DIM = ATTN_KV_HEADS * HEAD_DIM
WINDOW = 128
BAND = WINDOW + CHUNK
KBAND = WINDOW + 2 * CHUNK
GLA_HEADS = 4
GLA_DK = 32
GLA_DV = 64
GLA_KDIM = GLA_HEADS * GLA_DK
GLA_VDIM = GLA_HEADS * GLA_DV
GLA_LOWRANK = 16
GLA_TAU = 16.0
CONV_DIM = 256
CONV_WIDTH = 31
CONV_HIST = CONV_WIDTH - 1
NORM_EPS = 1e-6
NEG_INF = -1e30
PAST_LEN = 4096
LOG2E = math.log2(math.e)

LANES = 128
SUBLANES = 8
CHUNK_SHIFT = CHUNK.bit_length() - 1
GLA_DK_SHIFT = GLA_DK.bit_length() - 1
GLA_DV_SHIFT = GLA_DV.bit_length() - 1
C_AQ, C_AK, C_AV, C_AG = 0, 512, 640, 768
C_GQ, C_GK, C_GV, C_GG = 1280, 1408, 1536, 1792
C_CV, C_CGL, C_CG, C_GLR = 2048, 2304, 2560, 2816
IN_COLS_PACKED = C_GLR + LANES
PROJ_GROUP = 256
CONV_PAD = 32
GLR_SRC = 2048
VEC_LAYOUT = (("norm_gain", D_MODEL), ("q_gain", LANES), ("k_gain", KV_DIM), ("gla_b", GLA_KDIM),
              ("gla_gain", GLA_VDIM), ("conv_b", CONV_DIM), ("ln_gain", CONV_DIM), ("ln_bias", CONV_DIM),
              ("b_pw", CONV_DIM))
VEC_ROWS = 16
VEC_ROW_OF = {name: (k, n) for k, (name, n) in enumerate(VEC_LAYOUT)}

CAST_BLOCK_ROWS = 256
PROMPT_TILE = 512
SAMPLE_SEQS = 4
VMEM_LIMIT_BYTES = 56 * 1024 * 1024

_BF = jnp.bfloat16
_F32 = jnp.float32


def _shape_constants(tile):
    rows = np.arange(ATTN_GROUP * CHUNK)
    dist = np.abs((rows % CHUNK)[:, None] + WINDOW - np.arange(KBAND)[None, :]).astype(np.float64)
    bias = np.stack([
        (2.0 ** (-8.0 * (j * ATTN_GROUP + rows // CHUNK + 1) / ATTN_HEADS))[:, None] * LOG2E * dist
        for j in range(ATTN_KV_HEADS)]).astype(np.float32)
    bias[:, :, BAND:] = -NEG_INF
    variants = []
    for n in range(WINDOW // CHUNK + 1):
        b = bias.copy()
        b[:, :, :max(WINDOW - n * CHUNK, 0)] = -NEG_INF
        variants.append(b)
    bias = np.stack(variants)
    tr = np.arange(tile)
    tri = ((tr[:, None] // CHUNK == tr[None, :] // CHUNK) & (tr[None, :] <= tr[:, None])).astype(np.float32)
    ln = np.arange(LANES) // HEAD_DIM
    ones = (ln[:, None] == ln[None, :]).astype(np.float32)
    return jnp.asarray(bias), jnp.asarray(tri, dtype=_BF), jnp.asarray(ones, dtype=_BF)


def _group_rms_scale(x, group, ones_bd):
    outs = []
    for c0 in range(0, x.shape[1], LANES):
        blk = x[:, c0:c0 + LANES]
        ss = _dot((blk * blk).astype(_BF), ones_bd)
        outs.append(blk * lax.rsqrt(ss * (1.0 / group) + NORM_EPS))
    return outs[0] if len(outs) == 1 else jnp.concatenate(outs, axis=1)


def _silu(x):
    return x * jax.nn.sigmoid(x)


def _dot(a, b):
    return jnp.dot(a, b, preferred_element_type=_F32)


def _dot_nt(a, b):
    return lax.dot_general(a, b, (((1,), (1,)), ((), ())), preferred_element_type=_F32)


def _dot_tn(a, b):
    return lax.dot_general(a, b, (((0,), (0,)), ((), ())), preferred_element_type=_F32)


def _stream_kernel(*refs, tile, n_seq, n_chunk, carry, q_pos0):
    if carry:
        (sinks_ref, x_ref, xnext_ref, *rest) = refs
    else:
        (sinks_ref, x_ref, xnext_ref, hk_all, hv_all, hs_all, hc_all, *rest) = refs
    (w_in_a, w_in_b, w_in_c, w_up_all, w_pw_all, w_out_all, vec_all, convw_all, bias_ref, tri_ref, ones_ref,
     y_ref, nk_all, nv_all, ns_all, nc_all,
     kd_all, vd_all, s_all, u_all, qlo_ref, qhi_ref, ycat_ref, proj_ref, xcur_ref, hb_ref, sbd_ref,
     gate_ref) = rest
    t = pl.program_id(1)
    layer = pl.program_id(2)
    seq_rows = n_chunk * CHUNK
    w_up_ref, w_pw_ref, w_out_ref = (r.at[layer] for r in (w_up_all, w_pw_all, w_out_all))
    convw_ref = convw_all.at[layer]

    def layer_row(name, which=layer):
        r, n = VEC_ROW_OF[name]
        return vec_all[which, r:r + 1, 0:n]

    q_gain = layer_row("q_gain") * (LOG2E * HEAD_DIM ** -0.5)
    k_gain = layer_row("k_gain")
    kd_ref, vd_ref, s_ref, u_ref = (r.at[layer] for r in (kd_all, vd_all, s_all, u_all))
    nk_ref, nv_ref, ns_ref, nc_ref = (r.at[layer] for r in (nk_all, nv_all, ns_all, nc_all))
    if not carry:
        hk_ref, hv_ref, hs_ref, hc_ref = (r.at[layer] for r in (hk_all, hv_all, hs_all, hc_all))

    lane128 = lax.broadcasted_iota(jnp.int32, (1, LANES), 1)
    lo_half = lane128 < HEAD_DIM

    def dup_halves(a):
        sw = pltpu.roll(a, HEAD_DIM, 1)
        return jnp.concatenate([jnp.where(lo_half, a, sw), jnp.where(lo_half, sw, a)], axis=1)

    buf_rows = WINDOW + seq_rows + CHUNK
    zeros_blk = jnp.zeros((buf_rows, LANES), _BF)
    ones_blk = jnp.ones((buf_rows, LANES), _BF)
    v_blank = jnp.concatenate([zeros_blk, ones_blk] * ATTN_KV_HEADS, axis=1)

    def store_v(s, row0, vdup_rows):
        for j in range(ATTN_KV_HEADS):
            vd_ref[s, row0:row0 + vdup_rows.shape[0], 2 * j * LANES:(2 * j + 1) * LANES] = (
                vdup_rows[:, j * LANES:(j + 1) * LANES])

    if carry:
        @pl.when(t == 0)
        def _():
            kd_ref[...] = jnp.zeros(kd_ref.shape, _BF)
            for s in range(n_seq):
                vd_ref[s] = v_blank
            s_ref[...] = jnp.zeros(s_ref.shape, _F32)
            u_ref[:, 0:CONV_PAD, :] = jnp.zeros((n_seq, CONV_PAD, CONV_DIM), _F32)
    else:
        for s in range(n_seq):
            kd_ref[s, 0:WINDOW, :] = dup_halves(hk_ref[s]).astype(_BF)
            kd_ref[s, WINDOW + seq_rows:, :] = jnp.zeros((CHUNK, 2 * LANES), _BF)
            vd_ref[s] = v_blank
            store_v(s, 0, dup_halves(hv_ref[s]).astype(_BF))
            sbd_ref[...] = jnp.zeros((GLA_KDIM, GLA_VDIM), _F32)
            for h in range(GLA_HEADS):
                sbd_ref[h * GLA_DK:(h + 1) * GLA_DK, h * GLA_DV:(h + 1) * GLA_DV] = hs_ref[s, h]
            s_ref[s] = sbd_ref[...].T
            u_ref[s, 0:SUBLANES, :] = jnp.zeros((SUBLANES, CONV_DIM), _F32)
            u_ref[s, CONV_PAD - CONV_HIST:CONV_PAD, :] = hc_ref[s]

    def normed_bf16(v, gain_row):
        ms = jnp.mean(v * v, axis=-1, keepdims=True)
        return (v * lax.rsqrt(ms + NORM_EPS) * gain_row).astype(_BF)

    @pl.when(layer == 0)
    def _():
        xcur_ref[...] = x_ref[...]

    @pl.when((pl.program_id(0) == 0) & (t == 0) & (layer == 0))
    def _():
        hb_ref[...] = normed_bf16(x_ref[...], layer_row("norm_gain"))
    x = xcur_ref[...]
    hb = hb_ref[...]

    anchor_on = sinks_ref[layer, ATTN_HEADS] != 0.0

    ones64 = ones_ref[...]

    def post_q(g):
        def post(res):
            for b in range(PROJ_GROUP // LANES):
                c0 = g * PROJ_GROUP + b * LANES
                qn = _group_rms_scale(res[:, b * LANES:(b + 1) * LANES], HEAD_DIM, ones64) * q_gain
                qlo_ref[:, c0:c0 + LANES] = jnp.where(lo_half, qn, 0.0).astype(_BF)
                qhi_ref[:, c0:c0 + LANES] = jnp.where(lo_half, 0.0, qn).astype(_BF)
        return post

    def post_kv(res):
        kn = _group_rms_scale(res[:, 0:KV_DIM], HEAD_DIM, ones64) * k_gain
        vv = res[:, KV_DIM:2 * KV_DIM]
        kdup = dup_halves(kn).astype(_BF)
        vdup = dup_halves(vv).astype(_BF)
        for s in range(n_seq):
            kd_ref[s, WINDOW:WINDOW + seq_rows, :] = kdup[s * seq_rows:(s + 1) * seq_rows]
            store_v(s, WINDOW, vdup[s * seq_rows:(s + 1) * seq_rows])
        if carry:
            nk_ref[0] = kn[tile - WINDOW:tile]
            nv_ref[0] = vv[tile - WINDOW:tile]
        else:
            for s in range(n_seq):
                nk_ref[s, 0:WINDOW - seq_rows, :] = hk_ref[s, seq_rows:WINDOW, :]
                nv_ref[s, 0:WINDOW - seq_rows, :] = hv_ref[s, seq_rows:WINDOW, :]
                nk_ref[s, WINDOW - seq_rows:WINDOW, :] = kn[s * seq_rows:(s + 1) * seq_rows]
                nv_ref[s, WINDOW - seq_rows:WINDOW, :] = vv[s * seq_rows:(s + 1) * seq_rows]

    def post_gate(col0):
        def post(res):
            gate_ref[:, col0:col0 + PROJ_GROUP] = _silu(res)
        return post

    group_post = {C_AG // PROJ_GROUP: post_gate(0), C_AG // PROJ_GROUP + 1: post_gate(PROJ_GROUP),
                  C_GG // PROJ_GROUP: post_gate(ATTN_DIM), C_CG // PROJ_GROUP: post_gate(ATTN_DIM + GLA_VDIM)}

    def w_in_cols(c0, c1):
        if c1 <= C_CV:
            return w_in_a[layer, :, c0:c1]
        if c1 <= C_GLR:
            return w_in_b[layer, :, c0 - C_CV:c1 - C_CV]
        return w_in_c[layer, :, c0 - C_GLR:c1 - C_GLR]

    def compute_group(g):
        c0, c1 = g * PROJ_GROUP, min((g + 1) * PROJ_GROUP, IN_COLS_PACKED)
        res = _dot(hb, w_in_cols(c0, c1))
        if g in group_post:
            group_post[g](res)
        else:
            proj_ref[:, c0:c1] = res
        return jnp.where(anchor_on, res[0:1, 0:CONV_DIM], 0.0)

    def proj(c0, width):
        return proj_ref[:, c0:c0 + width]

    compute_group(C_CV // PROJ_GROUP)
    compute_group(C_CGL // PROJ_GROUP)
    u = proj(C_CV, CONV_DIM) * jax.nn.sigmoid(proj(C_CGL, CONV_DIM))
    for s in range(n_seq):
        u_ref[s, CONV_PAD:CONV_PAD + seq_rows, :] = u[s * seq_rows:(s + 1) * seq_rows]
    conv_acc = [jnp.broadcast_to(layer_row("conv_b"), (seq_rows, CONV_DIM))] * n_seq
    for rho in range(SUBLANES):
        anchor = compute_group(rho)
        for s in range(n_seq):
            frame_rows = seq_rows + (SUBLANES if rho else 0)
            frame = None
            for j in range(CONV_WIDTH):
                off = CONV_PAD - CONV_HIST + j
                if off % SUBLANES != rho:
                    continue
                term = (convw_ref[j:j + 1, :] + anchor) * u_ref[s, off - rho:off - rho + frame_rows, :]
                frame = term if frame is None else frame + term
            conv_acc[s] = conv_acc[s] + frame[rho:rho + seq_rows]
    compute_group(C_CG // PROJ_GROUP)
    compute_group(C_GLR // PROJ_GROUP)
    cc = conv_acc[0] if n_seq == 1 else jnp.concatenate(conv_acc, axis=0)

    for s in range(n_seq):
        nc_ref[s] = u_ref[s, seq_rows + CONV_PAD - CONV_HIST:seq_rows + CONV_PAD, :]
    if carry:
        u_ref[:, 0:CONV_PAD, :] = u_ref[:, seq_rows:seq_rows + CONV_PAD, :]

    mu = jnp.mean(cc, axis=-1, keepdims=True)
    cen = cc - mu
    var = jnp.mean(cen * cen, axis=-1, keepdims=True)
    ln = cen * lax.rsqrt(var + NORM_EPS) * layer_row("ln_gain") + layer_row("ln_bias")
    cpw = _dot(_silu(ln).astype(_BF), w_pw_ref[...]) + layer_row("b_pw")
    ycat_ref[:, ATTN_DIM + GLA_VDIM:D_MODEL] = cpw * gate_ref[:, ATTN_DIM + GLA_VDIM:D_MODEL]

    for g in range(ATTN_DIM // PROJ_GROUP):
        post_q(g)(proj(C_AQ + g * PROJ_GROUP, PROJ_GROUP))
    post_kv(proj(C_AK, 2 * KV_DIM))

    rows = ATTN_GROUP * CHUNK
    r_head1 = lax.broadcasted_iota(jnp.int32, (rows, 1), 0) >> CHUNK_SHIFT
    sink_cols = []
    for j in range(ATTN_KV_HEADS):
        sink = jnp.zeros((rows, 1), _F32)
        for r in range(ATTN_GROUP):
            sink = jnp.where(r_head1 == r, sinks_ref[layer, j * ATTN_GROUP + r], sink)
        sink_cols.append(sink * LOG2E)

    for s in range(n_seq):
        for c in range(n_chunk):
            r0 = s * seq_rows + c * CHUNK
            k0 = c * CHUNK
            steady = WINDOW // CHUNK
            variant = jnp.minimum(t * n_chunk + c, steady) if carry else steady
            for j in range(ATTN_KV_HEADS):
                qs = jnp.concatenate(
                    [qlo_ref[r0:r0 + CHUNK, (2 * j) * LANES:(2 * j + 1) * LANES],
                     qhi_ref[r0:r0 + CHUNK, (2 * j) * LANES:(2 * j + 1) * LANES],
                     qlo_ref[r0:r0 + CHUNK, (2 * j + 1) * LANES:(2 * j + 2) * LANES],
                     qhi_ref[r0:r0 + CHUNK, (2 * j + 1) * LANES:(2 * j + 2) * LANES]], axis=0)
                kb = kd_ref[s, k0:k0 + KBAND, j * LANES:(j + 1) * LANES]
                vb = vd_ref[s, k0:k0 + KBAND, 2 * j * LANES:(2 * j + 2) * LANES]
                sc = _dot_nt(qs, kb) - bias_ref[variant, j]
                sink = sink_cols[j]
                m = jnp.maximum(jnp.max(sc, axis=-1, keepdims=True), sink)
                p = jnp.exp2(sc - m)
                pv = _dot(p.astype(_BF), vb)
                o = pv[:, 0:LANES] / (pv[:, LANES:2 * LANES] + jnp.exp2(sink - m))
                for pb in range(2):
                    blk = 2 * j + pb
                    ob = jnp.where(lo_half, o[(2 * pb) * CHUNK:(2 * pb + 1) * CHUNK],
                                   o[(2 * pb + 1) * CHUNK:(2 * pb + 2) * CHUNK])
                    ycat_ref[r0:r0 + CHUNK, blk * LANES:(blk + 1) * LANES] = (
                        ob * gate_ref[r0:r0 + CHUNK, blk * LANES:(blk + 1) * LANES])

    if carry:
        kd_ref[:, 0:WINDOW, :] = kd_ref[:, seq_rows:seq_rows + WINDOW, :]
        vd_ref[:, 0:WINDOW, :] = vd_ref[:, seq_rows:seq_rows + WINDOW, :]

    z = _dot(proj(C_GLR, LANES).astype(_BF), w_up_ref[...]) + layer_row("gla_b")
    log_a = (jnp.minimum(z, 0.0) - jnp.log(1.0 + jnp.exp(-jnp.abs(z)))) * (1.0 / GLA_TAU)
    tri = tri_ref[...]
    la_hi = log_a.astype(_BF)
    la_lo = (log_a - la_hi.astype(_F32)).astype(_BF)
    bcum = _dot(tri, la_hi) + _dot(tri, la_lo)
    gq = proj(C_GQ, GLA_KDIM) * GLA_DK ** -0.5
    gk = proj(C_GK, GLA_KDIM)
    gv = proj(C_GV, GLA_VDIM)
    gate_g = gate_ref[:, ATTN_DIM:ATTN_DIM + GLA_VDIM]
    gla_g = layer_row("gla_gain")

    head_of_lane = lane128 >> GLA_DK_SHIFT
    a_r = lax.broadcasted_iota(jnp.int32, (CHUNK, GLA_HEADS * CHUNK), 0)
    a_c = lax.broadcasted_iota(jnp.int32, (CHUNK, GLA_HEADS * CHUNK), 1)
    causal = (a_c & (CHUNK - 1)) <= a_r
    vblk = lax.broadcasted_iota(jnp.int32, (1, GLA_VDIM), 1) >> GLA_DV_SHIFT
    bd_mask_t = ((lax.broadcasted_iota(jnp.int32, (GLA_VDIM, GLA_KDIM), 0) >> GLA_DV_SHIFT)
                 == (lax.broadcasted_iota(jnp.int32, (GLA_VDIM, GLA_KDIM), 1) >> GLA_DK_SHIFT))

    q_all = (gq * jnp.exp(bcum)).astype(_BF)
    k_all = gk * jnp.exp(-bcum)
    intra, kv_upd, decay_rows = [], [], []
    for s in range(n_seq):
        for c in range(n_chunk):
            r0 = s * seq_rows + c * CHUNK
            bc = bcum[r0:r0 + CHUNK]
            b_last = bc[CHUNK - 1:CHUNK]
            k_end = (gk[r0:r0 + CHUNK] * jnp.exp(b_last - bc)).astype(_BF)
            vf = gv[r0:r0 + CHUNK]
            k_stack = jnp.concatenate(
                [jnp.where(head_of_lane == h, k_all[r0:r0 + CHUNK], 0.0) for h in range(GLA_HEADS)],
                axis=0).astype(_BF)
            v_bd = jnp.concatenate(
                [jnp.where(vblk == h, vf, 0.0) for h in range(GLA_HEADS)], axis=0).astype(_BF)
            a = jnp.where(causal, _dot_nt(q_all[r0:r0 + CHUNK], k_stack), 0.0).astype(_BF)
            intra.append(_dot(a, v_bd))
            kv_upd.append(jnp.where(bd_mask_t, _dot_tn(vf.astype(_BF), k_end), 0.0))
            decay_rows.append(jnp.exp(b_last))
    o_rows = []
    for s in range(n_seq):
        state = s_ref[s]
        for c in range(n_chunk):
            i = s * n_chunk + c
            r0 = i * CHUNK
            o_rows.append(intra[i] + _dot_nt(q_all[r0:r0 + CHUNK], state.astype(_BF)))
            state = decay_rows[i] * state + kv_upd[i]
        s_ref[s] = state
    o_all = jnp.concatenate(o_rows, axis=0)
    ycat_ref[:, ATTN_DIM:ATTN_DIM + GLA_VDIM] = _group_rms_scale(o_all, GLA_DV, ones64) * gla_g * gate_g

    for s in range(n_seq):
        sbd_ref[...] = s_ref[s].T
        for h in range(GLA_HEADS):
            ns_ref[s, h] = sbd_ref[h * GLA_DK:(h + 1) * GLA_DK, h * GLA_DV:(h + 1) * GLA_DV]

    y = x
    for c0, c1 in ((ATTN_DIM + GLA_VDIM, D_MODEL), (0, ATTN_DIM), (ATTN_DIM, ATTN_DIM + GLA_VDIM)):
        y = y + _dot(ycat_ref[:, c0:c1].astype(_BF), w_out_ref[c0:c1, :])
    y_ref[...] = y
    xcur_ref[...] = y
    last_layer = layer == pl.num_programs(2) - 1
    nxt = jnp.where(last_layer, xnext_ref[...], y)
    next_gain = layer_row("norm_gain", jnp.where(last_layer, 0, layer + 1))
    hb_ref[...] = normed_bf16(nxt, next_gain)


def _run_stream(x2d, n_seqs, seq_len, hist, sinks, params, *, carry, q_pos0):
    depth = sinks.shape[0]
    if carry:
        assert q_pos0 == 0
        tile, n_seq, n_chunk = PROMPT_TILE, 1, PROMPT_TILE // CHUNK
        grid = (n_seqs, seq_len // tile, depth)
    else:
        assert seq_len == CHUNK and q_pos0 >= WINDOW
        tile, n_seq, n_chunk = SAMPLE_SEQS * CHUNK, SAMPLE_SEQS, 1
        grid = (n_seqs // n_seq, 1, depth)
    n_t = grid[1]
    seq_rows = n_chunk * CHUNK

    def const(shape):
        return pl.BlockSpec(shape, lambda g, t, l: (0,) * len(shape), pipeline_mode=pl.Buffered(1))

    def per_layer_group(shape):
        return pl.BlockSpec((depth, n_seq) + shape, lambda g, t, l: (0, g) + (0,) * len(shape))

    state_shapes = ((WINDOW, KV_DIM), (WINDOW, KV_DIM), (GLA_HEADS, GLA_DK, GLA_DV), (CONV_HIST, CONV_DIM))
    x_spec = pl.BlockSpec((tile, D_MODEL), lambda g, t, l: (g * n_t + t, 0))
    n_tiles = x2d.shape[0] // tile
    xnext_spec = pl.BlockSpec((tile, D_MODEL), lambda g, t, l: (jnp.minimum(g * n_t + t + 1, n_tiles - 1), 0))
    in_specs = [pl.BlockSpec(memory_space=pltpu.SMEM), x_spec, xnext_spec]
    args = [sinks, x2d, x2d]
    if not carry:
        in_specs += [per_layer_group(sh) for sh in state_shapes]
        args += list(hist)
    operands = tuple(params) + _shape_constants(tile)
    in_specs += [const(p.shape) for p in operands]
    args += list(operands)

    out_shape = (jax.ShapeDtypeStruct(x2d.shape, _F32),) + tuple(
        jax.ShapeDtypeStruct((depth, n_seqs) + sh, _F32) for sh in state_shapes)
    out_specs = (x_spec,) + tuple(per_layer_group(sh) for sh in state_shapes)
    scratch = [pltpu.VMEM((depth, n_seq, WINDOW + seq_rows + CHUNK, ATTN_KV_HEADS * LANES), _BF),
               pltpu.VMEM((depth, n_seq, WINDOW + seq_rows + CHUNK, ATTN_KV_HEADS * 2 * LANES), _BF),
               pltpu.VMEM((depth, n_seq, GLA_VDIM, GLA_KDIM), _F32),
               pltpu.VMEM((depth, n_seq, CONV_PAD + seq_rows, CONV_DIM), _F32),
               pltpu.VMEM((tile, ATTN_DIM), _BF),
               pltpu.VMEM((tile, ATTN_DIM), _BF),
               pltpu.VMEM((tile, D_MODEL), _F32),
               pltpu.VMEM((tile, IN_COLS_PACKED), _F32),
               pltpu.VMEM((tile, D_MODEL), _F32),
               pltpu.VMEM((tile, D_MODEL), _BF),
               pltpu.VMEM((GLA_KDIM, GLA_VDIM), _F32),
               pltpu.VMEM((tile, D_MODEL), _F32)]
    body = functools.partial(_stream_kernel, tile=tile, n_seq=n_seq, n_chunk=n_chunk, carry=carry, q_pos0=q_pos0)
    return pl.pallas_call(
        body, grid=grid, in_specs=in_specs, out_specs=out_specs, out_shape=out_shape,
        scratch_shapes=scratch,
        compiler_params=pltpu.CompilerParams(dimension_semantics=("arbitrary", "arbitrary", "arbitrary"),
                                             vmem_limit_bytes=VMEM_LIMIT_BYTES),
        name="mixer_prompt" if carry else "mixer_sample",
    )(*args)


def _cast_kernel(w_ref, o_ref):
    o_ref[...] = w_ref[...].astype(o_ref.dtype)


def _cast_leading_columns(w, n_cols):
    depth, rows, _ = w.shape
    block = (1, CAST_BLOCK_ROWS, n_cols)
    spec = pl.BlockSpec(block, lambda l, r: (l, r, 0))
    return pl.pallas_call(
        _cast_kernel, grid=(depth, rows // CAST_BLOCK_ROWS), in_specs=[spec], out_specs=spec,
        out_shape=jax.ShapeDtypeStruct((depth, rows, n_cols), _BF), name="cast_w_in",
    )(w)


def _pack_params(norm_gain, w_in, q_norm_gain, k_norm_gain, gla_w_gate_up, gla_b_gate, gla_norm_gain,
                 conv_w, conv_b, conv_ln_gain, conv_ln_bias, conv_w_pw, conv_b_pw, w_out):
    w_in_a = _cast_leading_columns(w_in, GLR_SRC)
    w_in_c = jnp.pad(w_in[:, :, GLR_SRC:GLR_SRC + GLA_LOWRANK],
                     ((0, 0), (0, 0), (0, LANES - GLA_LOWRANK))).astype(_BF)
    w_in_b = w_in[:, :, GLR_SRC + GLA_LOWRANK:].astype(_BF)
    pieces = dict(norm_gain=norm_gain, q_gain=q_norm_gain, k_gain=k_norm_gain, gla_b=gla_b_gate,
                  gla_gain=gla_norm_gain, conv_b=conv_b, ln_gain=conv_ln_gain, ln_bias=conv_ln_bias, b_pw=conv_b_pw)
    depth = w_in.shape[0]
    parts = []
    for name, n in VEC_LAYOUT:
        parts += [pieces[name].astype(_F32)] * (n // pieces[name].shape[1])
        parts.append(jnp.zeros((depth, D_MODEL - n), _F32))
    parts.append(jnp.zeros((depth, (VEC_ROWS - len(VEC_LAYOUT)) * D_MODEL), _F32))
    vecs = jnp.concatenate(parts, axis=1).reshape(depth, VEC_ROWS, D_MODEL)
    w_up_p = jnp.pad(gla_w_gate_up, ((0, 0), (0, LANES - GLA_LOWRANK), (0, 0))).astype(_BF)
    convw = jnp.pad(conv_w.astype(_F32), ((0, 0), (0, CONV_PAD - CONV_WIDTH), (0, 0)))
    return w_in_a, w_in_b, w_in_c, w_up_p, conv_w_pw.astype(_BF), w_out.astype(_BF), vecs, convw


def kernel(x_prompt, x_sample, cache_k, cache_v, state_gla, state_conv, norm_gain, w_in, q_norm_gain, k_norm_gain, attn_sinks, gla_w_gate_up, gla_b_gate, gla_norm_gain, conv_w, conv_b, conv_ln_gain, conv_ln_bias, conv_w_pw, conv_b_pw, w_out):
    depth = w_in.shape[0]
    bp, lp, _ = x_prompt.shape
    bs, ls, _ = x_sample.shape
    params = _pack_params(norm_gain, w_in, q_norm_gain, k_norm_gain, gla_w_gate_up, gla_b_gate, gla_norm_gain,
                          conv_w, conv_b, conv_ln_gain, conv_ln_bias, conv_w_pw, conv_b_pw, w_out)
    sinks = jnp.pad(attn_sinks.astype(_F32), ((0, 0), (0, ATTN_HEADS)))
    yp, pk, pv, ps, pc = _run_stream(x_prompt.reshape(bp * lp, D_MODEL), bp, lp, None, sinks, params,
                                     carry=True, q_pos0=0)
    hist = (cache_k.reshape(depth, bs, WINDOW, KV_DIM), cache_v.reshape(depth, bs, WINDOW, KV_DIM),
            state_gla, state_conv)
    ys, sk, sv, ss, sc = _run_stream(x_sample.reshape(bs * ls, D_MODEL), bs, ls, hist, sinks, params,
                                     carry=False, q_pos0=PAST_LEN)
    kv_shape = (WINDOW, ATTN_KV_HEADS, HEAD_DIM)
    return (yp.reshape(bp, lp, D_MODEL), ys.reshape(bs, ls, D_MODEL),
            pk.reshape((depth, bp) + kv_shape), pv.reshape((depth, bp) + kv_shape), ps, pc,
            sk.reshape((depth, bs) + kv_shape), sv.reshape((depth, bs) + kv_shape), ss, sc)
```

```python
import functools
import math

import numpy as np
import jax
import jax.numpy as jnp
from jax import lax
from jax.experimental import pallas as pl
from jax.experimental.pallas import tpu as pltpu

D_MODEL = 1024
CHUNK = 64
ATTN_HEADS = 8
ATTN_KV_HEADS = 2
HEAD_DIM = 64
ATTN_GROUP = ATTN_HEADS // ATTN_KV_HEADS
ATTN_DIM = ATTN_HEADS * HEAD_DIM
KV_DIM = ATTN_KV_HEADS * HEAD_DIM
WINDOW = 128
BAND = WINDOW + CHUNK
KBAND = WINDOW + 2 * CHUNK
GLA_HEADS = 4
GLA_DK = 32
GLA_DV = 64
GLA_KDIM = GLA_HEADS * GLA_DK
GLA_VDIM = GLA_HEADS * GLA_DV
GLA_LOWRANK = 16
GLA_TAU = 16.0
CONV_DIM = 256
CONV_WIDTH = 31
CONV_HIST = CONV_WIDTH - 1
NORM_EPS = 1e-6
NEG_INF = -1e30
PAST_LEN = 4096
LOG2E = math.log2(math.e)

LANES = 128
SUBLANES = 8
CHUNK_SHIFT = CHUNK.bit_length() - 1
GLA_DK_SHIFT = GLA_DK.bit_length() - 1
GLA_DV_SHIFT = GLA_DV.bit_length() - 1
C_AQ, C_AK, C_AV, C_AG = 0, 512, 640, 768
C_GQ, C_GK, C_GV, C_GG = 1280, 1408, 1536, 1792
C_CV, C_CGL, C_CG, C_GLR = 2048, 2304, 2560, 2816
IN_COLS_PACKED = C_GLR + LANES
PROJ_GROUP = 256
CONV_PAD = 32
GLR_SRC = 2048
VEC_LAYOUT = (("norm_gain", D_MODEL), ("q_gain", LANES), ("k_gain", KV_DIM), ("gla_b", GLA_KDIM),
              ("gla_gain", GLA_VDIM), ("conv_b", CONV_DIM), ("ln_gain", CONV_DIM), ("ln_bias", CONV_DIM),
              ("b_pw", CONV_DIM))
VEC_ROWS = 16
VEC_ROW_OF = {name: (k, n) for k, (name, n) in enumerate(VEC_LAYOUT)}

OUT_ROW_BLOCK = 256
PROMPT_TILE = 512
SAMPLE_SEQS = 4
VMEM_LIMIT_BYTES = 56 * 1024 * 1024

_BF = jnp.bfloat16
_F32 = jnp.float32


def _shape_constants(tile):
    rows = np.arange(ATTN_GROUP * CHUNK)
    dist = np.abs((rows % CHUNK)[:, None] + WINDOW - np.arange(KBAND)[None, :]).astype(np.float64)
    bias = np.stack([
        (2.0 ** (-8.0 * (j * ATTN_GROUP + rows // CHUNK + 1) / ATTN_HEADS))[:, None] * LOG2E * dist
        for j in range(ATTN_KV_HEADS)]).astype(np.float32)
    bias[:, :, BAND:] = -NEG_INF
    variants = []
    for n in range(WINDOW // CHUNK + 1):
        b = bias.copy()
        b[:, :, :max(WINDOW - n * CHUNK, 0)] = -NEG_INF
        variants.append(b)
    bias = np.stack(variants)
    tr = np.arange(tile)
    tri = ((tr[:, None] // CHUNK == tr[None, :] // CHUNK) & (tr[None, :] <= tr[:, None])).astype(np.float32)
    ln = np.arange(LANES) // HEAD_DIM
    ones = (ln[:, None] == ln[None, :]).astype(np.float32)
    return jnp.asarray(bias), jnp.asarray(tri, dtype=_BF), jnp.asarray(ones, dtype=_BF)


def _group_rms_scale(x, group, ones_bd):
    outs = []
    for c0 in range(0, x.shape[1], LANES):
        blk = x[:, c0:c0 + LANES]
        ss = _dot((blk * blk).astype(_BF), ones_bd)
        outs.append(blk * lax.rsqrt(ss * (1.0 / group) + NORM_EPS))
    return outs[0] if len(outs) == 1 else jnp.concatenate(outs, axis=1)


def _silu(x):
    return x * jax.nn.sigmoid(x)


def _dot(a, b):
    return jnp.dot(a, b, preferred_element_type=_F32)


def _dot_nt(a, b):
    return lax.dot_general(a, b, (((1,), (1,)), ((), ())), preferred_element_type=_F32)


def _dot_tn(a, b):
    return lax.dot_general(a, b, (((0,), (0,)), ((), ())), preferred_element_type=_F32)


def _stream_kernel(*refs, tile, n_seq, n_chunk, carry, q_pos0):
    if carry:
        (sinks_ref, x_ref, xnext_ref, *rest) = refs
    else:
        (sinks_ref, x_ref, xnext_ref, hk_all, hv_all, hs_all, hc_all, *rest) = refs
    (w_in_a, w_in_b, w_in_c, w_up_all, w_pw_all, w_out_all, vec_all, convw_all, bias_ref, tri_ref, ones_ref,
     y_ref, nk_all, nv_all, ns_all, nc_all,
     kd_all, vd_all, s_all, u_all, qlo_ref, qhi_ref, ycat_ref, proj_ref, xcur_ref, hb_ref, sbd_ref,
     gate_ref) = rest
    t = pl.program_id(1)
    layer = pl.program_id(2)
    seq_rows = n_chunk * CHUNK
    w_up_ref, w_pw_ref, w_out_ref = (r.at[layer] for r in (w_up_all, w_pw_all, w_out_all))
    convw_ref = convw_all.at[layer]

    def layer_row(name, which=layer):
        r, n = VEC_ROW_OF[name]
        return vec_all[which, r:r + 1, 0:n]

    q_gain = layer_row("q_gain") * (LOG2E * HEAD_DIM ** -0.5)
    k_gain = layer_row("k_gain")
    kd_ref, vd_ref, s_ref, u_ref = (r.at[layer] for r in (kd_all, vd_all, s_all, u_all))
    nk_ref, nv_ref, ns_ref, nc_ref = (r.at[layer] for r in (nk_all, nv_all, ns_all, nc_all))
    if not carry:
        hk_ref, hv_ref, hs_ref, hc_ref = (r.at[layer] for r in (hk_all, hv_all, hs_all, hc_all))

    lane128 = lax.broadcasted_iota(jnp.int32, (1, LANES), 1)
    lo_half = lane128 < HEAD_DIM

    def dup_halves(a):
        sw = pltpu.roll(a, HEAD_DIM, 1)
        return jnp.concatenate([jnp.where(lo_half, a, sw), jnp.where(lo_half, sw, a)], axis=1)

    buf_rows = WINDOW + seq_rows + CHUNK
    zeros_blk = jnp.zeros((buf_rows, LANES), _BF)
    ones_blk = jnp.ones((buf_rows, LANES), _BF)
    v_blank = jnp.concatenate([zeros_blk, ones_blk] * ATTN_KV_HEADS, axis=1)

    def store_v(s, row0, vdup_rows):
        for j in range(ATTN_KV_HEADS):
            vd_ref[s, row0:row0 + vdup_rows.shape[0], 2 * j * LANES:(2 * j + 1) * LANES] = (
                vdup_rows[:, j * LANES:(j + 1) * LANES])

    if carry:
        @pl.when(t == 0)
        def _():
            kd_ref[...] = jnp.zeros(kd_ref.shape, _BF)
            for s in range(n_seq):
                vd_ref[s] = v_blank
            s_ref[...] = jnp.zeros(s_ref.shape, _F32)
            u_ref[:, 0:CONV_PAD, :] = jnp.zeros((n_seq, CONV_PAD, CONV_DIM), _F32)
    else:
        for s in range(n_seq):
            kd_ref[s, 0:WINDOW, :] = dup_halves(hk_ref[s]).astype(_BF)
            kd_ref[s, WINDOW + seq_rows:, :] = jnp.zeros((CHUNK, 2 * LANES), _BF)
            vd_ref[s] = v_blank
            store_v(s, 0, dup_halves(hv_ref[s]).astype(_BF))
            sbd_ref[...] = jnp.zeros((GLA_KDIM, GLA_VDIM), _F32)
            for h in range(GLA_HEADS):
                sbd_ref[h * GLA_DK:(h + 1) * GLA_DK, h * GLA_DV:(h + 1) * GLA_DV] = hs_ref[s, h]
            s_ref[s] = sbd_ref[...].T
            u_ref[s, 0:SUBLANES, :] = jnp.zeros((SUBLANES, CONV_DIM), _F32)
            u_ref[s, CONV_PAD - CONV_HIST:CONV_PAD, :] = hc_ref[s]

    def normed_bf16(v, gain_row):
        ms = jnp.mean(v * v, axis=-1, keepdims=True)
        return (v * lax.rsqrt(ms + NORM_EPS) * gain_row).astype(_BF)

    @pl.when(layer == 0)
    def _():
        xcur_ref[...] = x_ref[...]

    @pl.when((pl.program_id(0) == 0) & (t == 0) & (layer == 0))
    def _():
        hb_ref[...] = normed_bf16(x_ref[...], layer_row("norm_gain"))
    x = xcur_ref[...]
    hb = hb_ref[...]

    anchor_on = sinks_ref[layer, ATTN_HEADS] != 0.0

    ones64 = ones_ref[...]

    def post_q(g):
        def post(res):
            for b in range(PROJ_GROUP // LANES):
                c0 = g * PROJ_GROUP + b * LANES
                qn = _group_rms_scale(res[:, b * LANES:(b + 1) * LANES], HEAD_DIM, ones64) * q_gain
                qlo_ref[:, c0:c0 + LANES] = jnp.where(lo_half, qn, 0.0).astype(_BF)
                qhi_ref[:, c0:c0 + LANES] = jnp.where(lo_half, 0.0, qn).astype(_BF)
        return post

    def post_kv(res):
        kn = _group_rms_scale(res[:, 0:KV_DIM], HEAD_DIM, ones64) * k_gain
        vv = res[:, KV_DIM:2 * KV_DIM]
        kdup = dup_halves(kn).astype(_BF)
        vdup = dup_halves(vv).astype(_BF)
        for s in range(n_seq):
            kd_ref[s, WINDOW:WINDOW + seq_rows, :] = kdup[s * seq_rows:(s + 1) * seq_rows]
            store_v(s, WINDOW, vdup[s * seq_rows:(s + 1) * seq_rows])
        if carry:
            nk_ref[0] = kn[tile - WINDOW:tile]
            nv_ref[0] = vv[tile - WINDOW:tile]
        else:
            for s in range(n_seq):
                nk_ref[s, 0:WINDOW - seq_rows, :] = hk_ref[s, seq_rows:WINDOW, :]
                nv_ref[s, 0:WINDOW - seq_rows, :] = hv_ref[s, seq_rows:WINDOW, :]
                nk_ref[s, WINDOW - seq_rows:WINDOW, :] = kn[s * seq_rows:(s + 1) * seq_rows]
                nv_ref[s, WINDOW - seq_rows:WINDOW, :] = vv[s * seq_rows:(s + 1) * seq_rows]

    def post_gate(col0):
        def post(res):
            gate_ref[:, col0:col0 + PROJ_GROUP] = _silu(res)
        return post

    group_post = {C_AG // PROJ_GROUP: post_gate(0), C_AG // PROJ_GROUP + 1: post_gate(PROJ_GROUP),
                  C_GG // PROJ_GROUP: post_gate(ATTN_DIM), C_CG // PROJ_GROUP: post_gate(ATTN_DIM + GLA_VDIM)}

    def w_in_cols(c0, c1):
        if c1 <= C_CV:
            return w_in_a[layer, :, c0:c1]
        if c1 <= C_GLR:
            return w_in_b[layer, :, c0 - C_CV:c1 - C_CV]
        return w_in_c[layer, :, c0 - C_GLR:c1 - C_GLR]

    def compute_group(g):
        c0, c1 = g * PROJ_GROUP, min((g + 1) * PROJ_GROUP, IN_COLS_PACKED)
        res = _dot(hb, w_in_cols(c0, c1))
        if g in group_post:
            group_post[g](res)
        else:
            proj_ref[:, c0:c1] = res
        return jnp.where(anchor_on, res[0:1, 0:CONV_DIM], 0.0)

    def proj(c0, width):
        return proj_ref[:, c0:c0 + width]

    compute_group(C_CV // PROJ_GROUP)
    compute_group(C_CGL // PROJ_GROUP)
    u = proj(C_CV, CONV_DIM) * jax.nn.sigmoid(proj(C_CGL, CONV_DIM))
    for s in range(n_seq):
        u_ref[s, CONV_PAD:CONV_PAD + seq_rows, :] = u[s * seq_rows:(s + 1) * seq_rows]
    conv_acc = [jnp.broadcast_to(layer_row("conv_b"), (seq_rows, CONV_DIM))] * n_seq
    for rho in range(SUBLANES):
        anchor = compute_group(rho)
        for s in range(n_seq):
            frame_rows = seq_rows + (SUBLANES if rho else 0)
            frame = None
            for j in range(CONV_WIDTH):
                off = CONV_PAD - CONV_HIST + j
                if off % SUBLANES != rho:
                    continue
                term = (convw_ref[j:j + 1, :] + anchor) * u_ref[s, off - rho:off - rho + frame_rows, :]
                frame = term if frame is None else frame + term
            conv_acc[s] = conv_acc[s] + frame[rho:rho + seq_rows]
    compute_group(C_CG // PROJ_GROUP)
    compute_group(C_GLR // PROJ_GROUP)
    cc = conv_acc[0] if n_seq == 1 else jnp.concatenate(conv_acc, axis=0)

    for s in range(n_seq):
        nc_ref[s] = u_ref[s, seq_rows + CONV_PAD - CONV_HIST:seq_rows + CONV_PAD, :]
    if carry:
        u_ref[:, 0:CONV_PAD, :] = u_ref[:, seq_rows:seq_rows + CONV_PAD, :]

    mu = jnp.mean(cc, axis=-1, keepdims=True)
    cen = cc - mu
    var = jnp.mean(cen * cen, axis=-1, keepdims=True)
    ln = cen * lax.rsqrt(var + NORM_EPS) * layer_row("ln_gain") + layer_row("ln_bias")
    cpw = _dot(_silu(ln).astype(_BF), w_pw_ref[...]) + layer_row("b_pw")
    ycat_ref[:, ATTN_DIM + GLA_VDIM:D_MODEL] = cpw * gate_ref[:, ATTN_DIM + GLA_VDIM:D_MODEL]

    for g in range(ATTN_DIM // PROJ_GROUP):
        post_q(g)(proj(C_AQ + g * PROJ_GROUP, PROJ_GROUP))
    post_kv(proj(C_AK, 2 * KV_DIM))

    rows = ATTN_GROUP * CHUNK
    r_head1 = lax.broadcasted_iota(jnp.int32, (rows, 1), 0) >> CHUNK_SHIFT
    sink_cols = []
    for j in range(ATTN_KV_HEADS):
        sink = jnp.zeros((rows, 1), _F32)
        for r in range(ATTN_GROUP):
            sink = jnp.where(r_head1 == r, sinks_ref[layer, j * ATTN_GROUP + r], sink)
        sink_cols.append(sink * LOG2E)

    for s in range(n_seq):
        for c in range(n_chunk):
            r0 = s * seq_rows + c * CHUNK
            k0 = c * CHUNK
            steady = WINDOW // CHUNK
            variant = jnp.minimum(t * n_chunk + c, steady) if carry else steady
            for j in range(ATTN_KV_HEADS):
                qs = jnp.concatenate(
                    [qlo_ref[r0:r0 + CHUNK, (2 * j) * LANES:(2 * j + 1) * LANES],
                     qhi_ref[r0:r0 + CHUNK, (2 * j) * LANES:(2 * j + 1) * LANES],
                     qlo_ref[r0:r0 + CHUNK, (2 * j + 1) * LANES:(2 * j + 2) * LANES],
                     qhi_ref[r0:r0 + CHUNK, (2 * j + 1) * LANES:(2 * j + 2) * LANES]], axis=0)
                kb = kd_ref[s, k0:k0 + KBAND, j * LANES:(j + 1) * LANES]
                vb = vd_ref[s, k0:k0 + KBAND, 2 * j * LANES:(2 * j + 2) * LANES]
                sc = _dot_nt(qs, kb) - bias_ref[variant, j]
                sink = sink_cols[j]
                m = jnp.maximum(jnp.max(sc, axis=-1, keepdims=True), sink)
                p = jnp.exp2(sc - m)
                pv = _dot(p.astype(_BF), vb)
                o = pv[:, 0:LANES] / (pv[:, LANES:2 * LANES] + jnp.exp2(sink - m))
                for pb in range(2):
                    blk = 2 * j + pb
                    ob = jnp.where(lo_half, o[(2 * pb) * CHUNK:(2 * pb + 1) * CHUNK],
                                   o[(2 * pb + 1) * CHUNK:(2 * pb + 2) * CHUNK])
                    ycat_ref[r0:r0 + CHUNK, blk * LANES:(blk + 1) * LANES] = (
                        ob * gate_ref[r0:r0 + CHUNK, blk * LANES:(blk + 1) * LANES])

    if carry:
        kd_ref[:, 0:WINDOW, :] = kd_ref[:, seq_rows:seq_rows + WINDOW, :]
        vd_ref[:, 0:WINDOW, :] = vd_ref[:, seq_rows:seq_rows + WINDOW, :]

    z = _dot(proj(C_GLR, LANES).astype(_BF), w_up_ref[...]) + layer_row("gla_b")
    log_a = (jnp.minimum(z, 0.0) - jnp.log(1.0 + jnp.exp(-jnp.abs(z)))) * (1.0 / GLA_TAU)
    tri = tri_ref[...]
    la_hi = log_a.astype(_BF)
    la_lo = (log_a - la_hi.astype(_F32)).astype(_BF)
    bcum = _dot(tri, la_hi) + _dot(tri, la_lo)
    gq = proj(C_GQ, GLA_KDIM) * GLA_DK ** -0.5
    gk = proj(C_GK, GLA_KDIM)
    gv = proj(C_GV, GLA_VDIM)
    gate_g = gate_ref[:, ATTN_DIM:ATTN_DIM + GLA_VDIM]
    gla_g = layer_row("gla_gain")

    head_of_lane = lane128 >> GLA_DK_SHIFT
    a_r = lax.broadcasted_iota(jnp.int32, (CHUNK, GLA_HEADS * CHUNK), 0)
    a_c = lax.broadcasted_iota(jnp.int32, (CHUNK, GLA_HEADS * CHUNK), 1)
    causal = (a_c & (CHUNK - 1)) <= a_r
    vblk = lax.broadcasted_iota(jnp.int32, (1, GLA_VDIM), 1) >> GLA_DV_SHIFT
    bd_mask_t = ((lax.broadcasted_iota(jnp.int32, (GLA_VDIM, GLA_KDIM), 0) >> GLA_DV_SHIFT)
                 == (lax.broadcasted_iota(jnp.int32, (GLA_VDIM, GLA_KDIM), 1) >> GLA_DK_SHIFT))

    q_all = (gq * jnp.exp(bcum)).astype(_BF)
    k_all = gk * jnp.exp(-bcum)
    intra, kv_upd, decay_rows = [], [], []
    for s in range(n_seq):
        for c in range(n_chunk):
            r0 = s * seq_rows + c * CHUNK
            bc = bcum[r0:r0 + CHUNK]
            b_last = bc[CHUNK - 1:CHUNK]
            k_end = (gk[r0:r0 + CHUNK] * jnp.exp(b_last - bc)).astype(_BF)
            vf = gv[r0:r0 + CHUNK]
            k_stack = jnp.concatenate(
                [jnp.where(head_of_lane == h, k_all[r0:r0 + CHUNK], 0.0) for h in range(GLA_HEADS)],
                axis=0).astype(_BF)
            v_bd = jnp.concatenate(
                [jnp.where(vblk == h, vf, 0.0) for h in range(GLA_HEADS)], axis=0).astype(_BF)
            a = jnp.where(causal, _dot_nt(q_all[r0:r0 + CHUNK], k_stack), 0.0).astype(_BF)
            intra.append(_dot(a, v_bd))
            kv_upd.append(jnp.where(bd_mask_t, _dot_tn(vf.astype(_BF), k_end), 0.0))
            decay_rows.append(jnp.exp(b_last))
    o_rows = []
    for s in range(n_seq):
        state = s_ref[s]
        for c in range(n_chunk):
            i = s * n_chunk + c
            r0 = i * CHUNK
            o_rows.append(intra[i] + _dot_nt(q_all[r0:r0 + CHUNK], state.astype(_BF)))
            state = decay_rows[i] * state + kv_upd[i]
        s_ref[s] = state
    o_all = jnp.concatenate(o_rows, axis=0)
    ycat_ref[:, ATTN_DIM:ATTN_DIM + GLA_VDIM] = _group_rms_scale(o_all, GLA_DV, ones64) * gla_g * gate_g

    for s in range(n_seq):
        sbd_ref[...] = s_ref[s].T
        for h in range(GLA_HEADS):
            ns_ref[s, h] = sbd_ref[h * GLA_DK:(h + 1) * GLA_DK, h * GLA_DV:(h + 1) * GLA_DV]

    last_layer = layer == pl.num_programs(2) - 1
    next_gain = layer_row("norm_gain", jnp.where(last_layer, 0, layer + 1))
    for r0 in range(0, tile, OUT_ROW_BLOCK):
        r1 = r0 + OUT_ROW_BLOCK
        y = x[r0:r1] + _dot(ycat_ref[r0:r1, :].astype(_BF), w_out_ref[...])
        y_ref[r0:r1, :] = y
        xcur_ref[r0:r1, :] = y
        hb_ref[r0:r1, :] = normed_bf16(jnp.where(last_layer, xnext_ref[r0:r1, :], y), next_gain)


def _run_stream(x2d, n_seqs, seq_len, hist, sinks, params, *, carry, q_pos0):
    depth = sinks.shape[0]
    if carry:
        assert q_pos0 == 0
        tile, n_seq, n_chunk = PROMPT_TILE, 1, PROMPT_TILE // CHUNK
        grid = (n_seqs, seq_len // tile, depth)
    else:
        assert seq_len == CHUNK and q_pos0 >= WINDOW
        tile, n_seq, n_chunk = SAMPLE_SEQS * CHUNK, SAMPLE_SEQS, 1
        grid = (n_seqs // n_seq, 1, depth)
    n_t = grid[1]
    seq_rows = n_chunk * CHUNK

    def const(shape):
        return pl.BlockSpec(shape, lambda g, t, l: (0,) * len(shape), pipeline_mode=pl.Buffered(1))

    def per_layer_group(shape):
        return pl.BlockSpec((depth, n_seq) + shape, lambda g, t, l: (0, g) + (0,) * len(shape))

    state_shapes = ((WINDOW, KV_DIM), (WINDOW, KV_DIM), (GLA_HEADS, GLA_DK, GLA_DV), (CONV_HIST, CONV_DIM))
    x_spec = pl.BlockSpec((tile, D_MODEL), lambda g, t, l: (g * n_t + t, 0))
    n_tiles = x2d.shape[0] // tile
    xnext_spec = pl.BlockSpec((tile, D_MODEL), lambda g, t, l: (jnp.minimum(g * n_t + t + 1, n_tiles - 1), 0))
    in_specs = [pl.BlockSpec(memory_space=pltpu.SMEM), x_spec, xnext_spec]
    args = [sinks, x2d, x2d]
    if not carry:
        in_specs += [per_layer_group(sh) for sh in state_shapes]
        args += list(hist)
    operands = tuple(params) + _shape_constants(tile)
    in_specs += [const(p.shape) for p in operands]
    in_specs[len(args)] = const(operands[0].shape[:2] + (GLR_SRC,))
    args += list(operands)

    out_shape = (jax.ShapeDtypeStruct(x2d.shape, _F32),) + tuple(
        jax.ShapeDtypeStruct((depth, n_seqs) + sh, _F32) for sh in state_shapes)
    out_specs = (x_spec,) + tuple(per_layer_group(sh) for sh in state_shapes)
    scratch = [pltpu.VMEM((depth, n_seq, WINDOW + seq_rows + CHUNK, ATTN_KV_HEADS * LANES), _BF),
               pltpu.VMEM((depth, n_seq, WINDOW + seq_rows + CHUNK, ATTN_KV_HEADS * 2 * LANES), _BF),
               pltpu.VMEM((depth, n_seq, GLA_VDIM, GLA_KDIM), _F32),
               pltpu.VMEM((depth, n_seq, CONV_PAD + seq_rows, CONV_DIM), _F32),
               pltpu.VMEM((tile, ATTN_DIM), _BF),
               pltpu.VMEM((tile, ATTN_DIM), _BF),
               pltpu.VMEM((tile, D_MODEL), _F32),
               pltpu.VMEM((tile, IN_COLS_PACKED), _F32),
               pltpu.VMEM((tile, D_MODEL), _F32),
               pltpu.VMEM((tile, D_MODEL), _BF),
               pltpu.VMEM((GLA_KDIM, GLA_VDIM), _F32),
               pltpu.VMEM((tile, D_MODEL), _F32)]
    body = functools.partial(_stream_kernel, tile=tile, n_seq=n_seq, n_chunk=n_chunk, carry=carry, q_pos0=q_pos0)
    return pl.pallas_call(
        body, grid=grid, in_specs=in_specs, out_specs=out_specs, out_shape=out_shape,
        scratch_shapes=scratch,
        compiler_params=pltpu.CompilerParams(dimension_semantics=("arbitrary", "arbitrary", "arbitrary"),
                                             vmem_limit_bytes=VMEM_LIMIT_BYTES),
        name="mixer_prompt" if carry else "mixer_sample",
    )(*args)


def _pack_params(norm_gain, w_in, q_norm_gain, k_norm_gain, gla_w_gate_up, gla_b_gate, gla_norm_gain,
                 conv_w, conv_b, conv_ln_gain, conv_ln_bias, conv_w_pw, conv_b_pw, w_out):
    w_in_a = w_in.astype(_BF)
    w_in_c = jnp.pad(w_in_a[:, :, GLR_SRC:GLR_SRC + GLA_LOWRANK], ((0, 0), (0, 0), (0, LANES - GLA_LOWRANK)))
    w_in_b = w_in_a[:, :, GLR_SRC + GLA_LOWRANK:]
    pieces = dict(norm_gain=norm_gain, q_gain=q_norm_gain, k_gain=k_norm_gain, gla_b=gla_b_gate,
                  gla_gain=gla_norm_gain, conv_b=conv_b, ln_gain=conv_ln_gain, ln_bias=conv_ln_bias, b_pw=conv_b_pw)
    depth = w_in.shape[0]
    parts = []
    for name, n in VEC_LAYOUT:
        parts += [pieces[name].astype(_F32)] * (n // pieces[name].shape[1])
        parts.append(jnp.zeros((depth, D_MODEL - n), _F32))
    parts.append(jnp.zeros((depth, (VEC_ROWS - len(VEC_LAYOUT)) * D_MODEL), _F32))
    vecs = jnp.concatenate(parts, axis=1).reshape(depth, VEC_ROWS, D_MODEL)
    w_up_p = jnp.pad(gla_w_gate_up, ((0, 0), (0, LANES - GLA_LOWRANK), (0, 0))).astype(_BF)
    convw = jnp.pad(conv_w.astype(_F32), ((0, 0), (0, CONV_PAD - CONV_WIDTH), (0, 0)))
    return w_in_a, w_in_b, w_in_c, w_up_p, conv_w_pw.astype(_BF), w_out.astype(_BF), vecs, convw


def kernel(x_prompt, x_sample, cache_k, cache_v, state_gla, state_conv, norm_gain, w_in, q_norm_gain, k_norm_gain, attn_sinks, gla_w_gate_up, gla_b_gate, gla_norm_gain, conv_w, conv_b, conv_ln_gain, conv_ln_bias, conv_w_pw, conv_b_pw, w_out):
    depth = w_in.shape[0]
    bp, lp, _ = x_prompt.shape
    bs, ls, _ = x_sample.shape
    params = _pack_params(norm_gain, w_in, q_norm_gain, k_norm_gain, gla_w_gate_up, gla_b_gate, gla_norm_gain,
                          conv_w, conv_b, conv_ln_gain, conv_ln_bias, conv_w_pw, conv_b_pw, w_out)
    sinks = jnp.pad(attn_sinks.astype(_F32), ((0, 0), (0, ATTN_HEADS)))
    yp, pk, pv, ps, pc = _run_stream(x_prompt.reshape(bp * lp, D_MODEL), bp, lp, None, sinks, params,
                                     carry=True, q_pos0=0)
    hist = (cache_k.reshape(depth, bs, WINDOW, KV_DIM), cache_v.reshape(depth, bs, WINDOW, KV_DIM),
            state_gla, state_conv)
    ys, sk, sv, ss, sc = _run_stream(x_sample.reshape(bs * ls, D_MODEL), bs, ls, hist, sinks, params,
                                     carry=False, q_pos0=PAST_LEN)
    kv_shape = (WINDOW, ATTN_KV_HEADS, HEAD_DIM)
    return (yp.reshape(bp, lp, D_MODEL), ys.reshape(bs, ls, D_MODEL),
            pk.reshape((depth, bp) + kv_shape), pv.reshape((depth, bp) + kv_shape), ps, pc,
            sk.reshape((depth, bs) + kv_shape), sv.reshape((depth, bs) + kv_shape), ss, sc)
```

```python
import functools
import math

import numpy as np
import jax
import jax.numpy as jnp
from jax import lax
from jax.experimental import pallas as pl
from jax.experimental.pallas import tpu as pltpu

D_MODEL = 1024
CHUNK = 64
ATTN_HEADS = 8
ATTN_KV_HEADS = 2
HEAD_DIM = 64
ATTN_GROUP = ATTN_HEADS // ATTN_KV_HEADS
ATTN_DIM = ATTN_HEADS * HEAD_DIM
KV_DIM = ATTN_KV_HEADS * HEAD_DIM
WINDOW = 128
BAND = WINDOW + CHUNK
KBAND = WINDOW + 2 * CHUNK
GLA_HEADS = 4
GLA_DK = 32
GLA_DV = 64
GLA_KDIM = GLA_HEADS * GLA_DK
GLA_VDIM = GLA_HEADS * GLA_DV
GLA_LOWRANK = 16
GLA_TAU = 16.0
CONV_DIM = 256
CONV_WIDTH = 31
CONV_HIST = CONV_WIDTH - 1
NORM_EPS = 1e-6
NEG_INF = -1e30
PAST_LEN = 4096
LOG2E = math.log2(math.e)

LANES = 128
SUBLANES = 8
CHUNK_SHIFT = CHUNK.bit_length() - 1
GLA_DK_SHIFT = GLA_DK.bit_length() - 1
GLA_DV_SHIFT = GLA_DV.bit_length() - 1
C_AQ, C_AK, C_AV, C_AG = 0, 512, 640, 768
C_GQ, C_GK, C_GV, C_GG = 1280, 1408, 1536, 1792
C_CV, C_CGL, C_CG, C_GLR = 2048, 2304, 2560, 2816
IN_COLS_PACKED = C_GLR + LANES
PROJ_GROUP = 256
CONV_PAD = 32
GLR_SRC = 2048
VEC_LAYOUT = (("norm_gain", D_MODEL), ("q_gain", LANES), ("k_gain", KV_DIM), ("gla_b", GLA_KDIM),
              ("gla_gain", GLA_VDIM), ("conv_b", CONV_DIM), ("ln_gain", CONV_DIM), ("ln_bias", CONV_DIM),
              ("b_pw", CONV_DIM))
VEC_ROWS = 16
VEC_ROW_OF = {name: (k, n) for k, (name, n) in enumerate(VEC_LAYOUT)}

OUT_ROW_BLOCK = 256
PROMPT_TILE = 512
SAMPLE_SEQS = 4
VMEM_LIMIT_BYTES = 56 * 1024 * 1024

_BF = jnp.bfloat16
_F32 = jnp.float32


def _shape_constants(tile):
    rows = np.arange(ATTN_GROUP * CHUNK)
    dist = np.abs((rows % CHUNK)[:, None] + WINDOW - np.arange(KBAND)[None, :]).astype(np.float64)
    bias = np.stack([
        (2.0 ** (-8.0 * (j * ATTN_GROUP + rows // CHUNK + 1) / ATTN_HEADS))[:, None] * LOG2E * dist
        for j in range(ATTN_KV_HEADS)]).astype(np.float32)
    bias[:, :, BAND:] = -NEG_INF
    variants = []
    for n in range(WINDOW // CHUNK + 1):
        b = bias.copy()
        b[:, :, :max(WINDOW - n * CHUNK, 0)] = -NEG_INF
        variants.append(b)
    bias = np.stack(variants)
    tr = np.arange(tile)
    tri = ((tr[:, None] // CHUNK == tr[None, :] // CHUNK) & (tr[None, :] <= tr[:, None])).astype(np.float32)
    ln = np.arange(LANES) // HEAD_DIM
    ones = (ln[:, None] == ln[None, :]).astype(np.float32)
    return jnp.asarray(bias), jnp.asarray(tri, dtype=_BF), jnp.asarray(ones, dtype=_BF)


def _group_rms_scale(x, group, ones_bd):
    outs = []
    for c0 in range(0, x.shape[1], LANES):
        blk = x[:, c0:c0 + LANES]
        ss = _dot((blk * blk).astype(_BF), ones_bd)
        outs.append(blk * lax.rsqrt(ss * (1.0 / group) + NORM_EPS))
    return outs[0] if len(outs) == 1 else jnp.concatenate(outs, axis=1)


def _silu(x):
    return x * jax.nn.sigmoid(x)


def _dot(a, b):
    return jnp.dot(a, b, preferred_element_type=_F32)


def _dot_nt(a, b):
    return lax.dot_general(a, b, (((1,), (1,)), ((), ())), preferred_element_type=_F32)


def _dot_tn(a, b):
    return lax.dot_general(a, b, (((0,), (0,)), ((), ())), preferred_element_type=_F32)


def _stream_kernel(*refs, tile, n_seq, n_chunk, carry, q_pos0):
    if carry:
        (sinks_ref, x_ref, xnext_ref, *rest) = refs
    else:
        (sinks_ref, x_ref, xnext_ref, hk_all, hv_all, hs_all, hc_all, *rest) = refs
    (w_in_a, w_in_b, w_in_c, w_up_all, w_pw_all, w_out_all, vec_all, convw_all, bias_ref, tri_ref, ones_ref,
     y_ref, nk_all, nv_all, ns_all, nc_all,
     kd_all, vd_all, s_all, u_all, qlo_ref, qhi_ref, ycat_ref, proj_ref, xcur_ref, hb_ref, sbd_ref,
     gate_ref) = rest
    t = pl.program_id(1)
    layer = pl.program_id(2)
    seq_rows = n_chunk * CHUNK
    w_up_ref, w_pw_ref, w_out_ref = (r.at[layer] for r in (w_up_all, w_pw_all, w_out_all))
    convw_ref = convw_all.at[layer]

    def layer_row(name, which=layer):
        r, n = VEC_ROW_OF[name]
        return vec_all[which, r:r + 1, 0:n]

    q_gain = layer_row("q_gain") * (LOG2E * HEAD_DIM ** -0.5)
    k_gain = layer_row("k_gain")
    kd_ref, vd_ref, s_ref, u_ref = (r.at[layer] for r in (kd_all, vd_all, s_all, u_all))
    nk_ref, nv_ref, ns_ref, nc_ref = (r.at[layer] for r in (nk_all, nv_all, ns_all, nc_all))
    if not carry:
        hk_ref, hv_ref, hs_ref, hc_ref = (r.at[layer] for r in (hk_all, hv_all, hs_all, hc_all))

    lane128 = lax.broadcasted_iota(jnp.int32, (1, LANES), 1)
    lo_half = lane128 < HEAD_DIM

    def dup_halves(a):
        sw = pltpu.roll(a, HEAD_DIM, 1)
        return jnp.concatenate([jnp.where(lo_half, a, sw), jnp.where(lo_half, sw, a)], axis=1)

    buf_rows = WINDOW + seq_rows + CHUNK
    zeros_blk = jnp.zeros((buf_rows, LANES), _BF)
    ones_blk = jnp.ones((buf_rows, LANES), _BF)
    v_blank = jnp.concatenate([zeros_blk, ones_blk] * ATTN_KV_HEADS, axis=1)

    def store_v(s, row0, vdup_rows):
        for j in range(ATTN_KV_HEADS):
            vd_ref[s, row0:row0 + vdup_rows.shape[0], 2 * j * LANES:(2 * j + 1) * LANES] = (
                vdup_rows[:, j * LANES:(j + 1) * LANES])

    if carry:
        @pl.when(t == 0)
        def _():
            kd_ref[...] = jnp.zeros(kd_ref.shape, _BF)
            for s in range(n_seq):
                vd_ref[s] = v_blank
            s_ref[...] = jnp.zeros(s_ref.shape, _F32)
            u_ref[:, 0:CONV_PAD, :] = jnp.zeros((n_seq, CONV_PAD, CONV_DIM), _F32)
    else:
        for s in range(n_seq):
            kd_ref[s, 0:WINDOW, :] = dup_halves(hk_ref[s]).astype(_BF)
            kd_ref[s, WINDOW + seq_rows:, :] = jnp.zeros((CHUNK, 2 * LANES), _BF)
            vd_ref[s] = v_blank
            store_v(s, 0, dup_halves(hv_ref[s]).astype(_BF))
            sbd_ref[...] = jnp.zeros((GLA_KDIM, GLA_VDIM), _F32)
            for h in range(GLA_HEADS):
                sbd_ref[h * GLA_DK:(h + 1) * GLA_DK, h * GLA_DV:(h + 1) * GLA_DV] = hs_ref[s, h]
            s_ref[s] = sbd_ref[...].T
            u_ref[s, 0:SUBLANES, :] = jnp.zeros((SUBLANES, CONV_DIM), _F32)
            u_ref[s, CONV_PAD - CONV_HIST:CONV_PAD, :] = hc_ref[s]

    def normed_bf16(v, gain_row):
        ms = jnp.mean(v * v, axis=-1, keepdims=True)
        return (v * lax.rsqrt(ms + NORM_EPS) * gain_row).astype(_BF)

    @pl.when(layer == 0)
    def _():
        xcur_ref[...] = x_ref[...]

    @pl.when((pl.program_id(0) == 0) & (t == 0) & (layer == 0))
    def _():
        hb_ref[...] = normed_bf16(x_ref[...], layer_row("norm_gain"))
    x = xcur_ref[...]
    hb = hb_ref[...]

    anchor_on = sinks_ref[layer, ATTN_HEADS] != 0.0

    ones64 = ones_ref[...]

    def post_q(g):
        def post(res):
            for b in range(PROJ_GROUP // LANES):
                c0 = g * PROJ_GROUP + b * LANES
                qn = _group_rms_scale(res[:, b * LANES:(b + 1) * LANES], HEAD_DIM, ones64) * q_gain
                qlo_ref[:, c0:c0 + LANES] = jnp.where(lo_half, qn, 0.0).astype(_BF)
                qhi_ref[:, c0:c0 + LANES] = jnp.where(lo_half, 0.0, qn).astype(_BF)
        return post

    def post_kv(res):
        kn = _group_rms_scale(res[:, 0:KV_DIM], HEAD_DIM, ones64) * k_gain
        vv = res[:, KV_DIM:2 * KV_DIM]
        kdup = dup_halves(kn).astype(_BF)
        vdup = dup_halves(vv).astype(_BF)
        for s in range(n_seq):
            kd_ref[s, WINDOW:WINDOW + seq_rows, :] = kdup[s * seq_rows:(s + 1) * seq_rows]
            store_v(s, WINDOW, vdup[s * seq_rows:(s + 1) * seq_rows])
        if carry:
            nk_ref[0] = kn[tile - WINDOW:tile]
            nv_ref[0] = vv[tile - WINDOW:tile]
        else:
            for s in range(n_seq):
                nk_ref[s, 0:WINDOW - seq_rows, :] = hk_ref[s, seq_rows:WINDOW, :]
                nv_ref[s, 0:WINDOW - seq_rows, :] = hv_ref[s, seq_rows:WINDOW, :]
                nk_ref[s, WINDOW - seq_rows:WINDOW, :] = kn[s * seq_rows:(s + 1) * seq_rows]
                nv_ref[s, WINDOW - seq_rows:WINDOW, :] = vv[s * seq_rows:(s + 1) * seq_rows]

    def post_gate(col0):
        def post(res):
            gate_ref[:, col0:col0 + PROJ_GROUP] = _silu(res)
        return post

    group_post = {C_AG // PROJ_GROUP: post_gate(0), C_AG // PROJ_GROUP + 1: post_gate(PROJ_GROUP),
                  C_GG // PROJ_GROUP: post_gate(ATTN_DIM), C_CG // PROJ_GROUP: post_gate(ATTN_DIM + GLA_VDIM)}

    def w_in_cols(c0, c1):
        if c1 <= C_CV:
            return w_in_a[layer, :, c0:c1]
        if c1 <= C_GLR:
            return w_in_b[layer, :, c0 - C_CV:c1 - C_CV]
        return w_in_c[layer, :, c0 - C_GLR:c1 - C_GLR]

    def compute_group(g):
        c0, c1 = g * PROJ_GROUP, min((g + 1) * PROJ_GROUP, IN_COLS_PACKED)
        res = _dot(hb, w_in_cols(c0, c1))
        if g in group_post:
            group_post[g](res)
        else:
            proj_ref[:, c0:c1] = res
        return jnp.where(anchor_on, res[0:1, 0:CONV_DIM], 0.0)

    def proj(c0, width):
        return proj_ref[:, c0:c0 + width]

    compute_group(C_CV // PROJ_GROUP)
    compute_group(C_CGL // PROJ_GROUP)
    u = proj(C_CV, CONV_DIM) * jax.nn.sigmoid(proj(C_CGL, CONV_DIM))
    for s in range(n_seq):
        u_ref[s, CONV_PAD:CONV_PAD + seq_rows, :] = u[s * seq_rows:(s + 1) * seq_rows]
    conv_acc = [jnp.broadcast_to(layer_row("conv_b"), (seq_rows, CONV_DIM))] * n_seq
    for rho in range(SUBLANES):
        anchor = compute_group(rho)
        for s in range(n_seq):
            frame_rows = seq_rows + (SUBLANES if rho else 0)
            frame = None
            for j in range(CONV_WIDTH):
                off = CONV_PAD - CONV_HIST + j
                if off % SUBLANES != rho:
                    continue
                term = (convw_ref[j:j + 1, :] + anchor) * u_ref[s, off - rho:off - rho + frame_rows, :]
                frame = term if frame is None else frame + term
            conv_acc[s] = conv_acc[s] + frame[rho:rho + seq_rows]
    compute_group(C_CG // PROJ_GROUP)
    compute_group(C_GLR // PROJ_GROUP)
    cc = conv_acc[0] if n_seq == 1 else jnp.concatenate(conv_acc, axis=0)

    for s in range(n_seq):
        nc_ref[s] = u_ref[s, seq_rows + CONV_PAD - CONV_HIST:seq_rows + CONV_PAD, :]
    if carry:
        u_ref[:, 0:CONV_PAD, :] = u_ref[:, seq_rows:seq_rows + CONV_PAD, :]

    mu = jnp.mean(cc, axis=-1, keepdims=True)
    cen = cc - mu
    var = jnp.mean(cen * cen, axis=-1, keepdims=True)
    ln = cen * lax.rsqrt(var + NORM_EPS) * layer_row("ln_gain") + layer_row("ln_bias")
    cpw = _dot(_silu(ln).astype(_BF), w_pw_ref[...]) + layer_row("b_pw")
    ycat_ref[:, ATTN_DIM + GLA_VDIM:D_MODEL] = cpw * gate_ref[:, ATTN_DIM + GLA_VDIM:D_MODEL]

    for g in range(ATTN_DIM // PROJ_GROUP):
        post_q(g)(proj(C_AQ + g * PROJ_GROUP, PROJ_GROUP))
    post_kv(proj(C_AK, 2 * KV_DIM))

    rows = ATTN_GROUP * CHUNK
    r_head1 = lax.broadcasted_iota(jnp.int32, (rows, 1), 0) >> CHUNK_SHIFT
    sink_cols = []
    for j in range(ATTN_KV_HEADS):
        sink = jnp.zeros((rows, 1), _F32)
        for r in range(ATTN_GROUP):
            sink = jnp.where(r_head1 == r, sinks_ref[layer, j * ATTN_GROUP + r], sink)
        sink_cols.append(sink * LOG2E)

    for s in range(n_seq):
        for c in range(n_chunk):
            r0 = s * seq_rows + c * CHUNK
            k0 = c * CHUNK
            steady = WINDOW // CHUNK
            variant = jnp.minimum(t * n_chunk + c, steady) if carry else steady
            for j in range(ATTN_KV_HEADS):
                qs = jnp.concatenate(
                    [qlo_ref[r0:r0 + CHUNK, (2 * j) * LANES:(2 * j + 1) * LANES],
                     qhi_ref[r0:r0 + CHUNK, (2 * j) * LANES:(2 * j + 1) * LANES],
                     qlo_ref[r0:r0 + CHUNK, (2 * j + 1) * LANES:(2 * j + 2) * LANES],
                     qhi_ref[r0:r0 + CHUNK, (2 * j + 1) * LANES:(2 * j + 2) * LANES]], axis=0)
                kb = kd_ref[s, k0:k0 + KBAND, j * LANES:(j + 1) * LANES]
                vb = vd_ref[s, k0:k0 + KBAND, 2 * j * LANES:(2 * j + 2) * LANES]
                qk = _dot_nt(qs, kb)
                sink = sink_cols[j]
                p_blocks, m_blocks = [], []
                for r in range(ATTN_GROUP):
                    sc = qk[r * CHUNK:(r + 1) * CHUNK] - bias_ref[variant, j, r * CHUNK:(r + 1) * CHUNK, :]
                    m_r = jnp.maximum(jnp.max(sc, axis=-1, keepdims=True), sink[r * CHUNK:(r + 1) * CHUNK])
                    p_blocks.append(jnp.exp2(sc - m_r).astype(_BF))
                    m_blocks.append(m_r)
                m = jnp.concatenate(m_blocks, axis=0)
                pv = _dot(jnp.concatenate(p_blocks, axis=0), vb)
                o = pv[:, 0:LANES] / (pv[:, LANES:2 * LANES] + jnp.exp2(sink - m))
                for pb in range(2):
                    blk = 2 * j + pb
                    ob = jnp.where(lo_half, o[(2 * pb) * CHUNK:(2 * pb + 1) * CHUNK],
                                   o[(2 * pb + 1) * CHUNK:(2 * pb + 2) * CHUNK])
                    ycat_ref[r0:r0 + CHUNK, blk * LANES:(blk + 1) * LANES] = (
                        ob * gate_ref[r0:r0 + CHUNK, blk * LANES:(blk + 1) * LANES])

    if carry:
        kd_ref[:, 0:WINDOW, :] = kd_ref[:, seq_rows:seq_rows + WINDOW, :]
        vd_ref[:, 0:WINDOW, :] = vd_ref[:, seq_rows:seq_rows + WINDOW, :]

    z = _dot(proj(C_GLR, LANES).astype(_BF), w_up_ref[...]) + layer_row("gla_b")
    log_a = (jnp.minimum(z, 0.0) - jnp.log(1.0 + jnp.exp(-jnp.abs(z)))) * (1.0 / GLA_TAU)
    tri = tri_ref[...]
    la_hi = log_a.astype(_BF)
    la_lo = (log_a - la_hi.astype(_F32)).astype(_BF)
    bcum = _dot(tri, la_hi) + _dot(tri, la_lo)
    gq = proj(C_GQ, GLA_KDIM) * GLA_DK ** -0.5
    gk = proj(C_GK, GLA_KDIM)
    gv = proj(C_GV, GLA_VDIM)
    gate_g = gate_ref[:, ATTN_DIM:ATTN_DIM + GLA_VDIM]
    gla_g = layer_row("gla_gain")

    head_of_lane = lane128 >> GLA_DK_SHIFT
    a_r = lax.broadcasted_iota(jnp.int32, (CHUNK, GLA_HEADS * CHUNK), 0)
    a_c = lax.broadcasted_iota(jnp.int32, (CHUNK, GLA_HEADS * CHUNK), 1)
    causal = (a_c & (CHUNK - 1)) <= a_r
    vblk = lax.broadcasted_iota(jnp.int32, (1, GLA_VDIM), 1) >> GLA_DV_SHIFT
    bd_mask_t = ((lax.broadcasted_iota(jnp.int32, (GLA_VDIM, GLA_KDIM), 0) >> GLA_DV_SHIFT)
                 == (lax.broadcasted_iota(jnp.int32, (GLA_VDIM, GLA_KDIM), 1) >> GLA_DK_SHIFT))

    q_all = (gq * jnp.exp(bcum)).astype(_BF)
    k_all = gk * jnp.exp(-bcum)
    intra, kv_upd, decay_rows = [], [], []
    for s in range(n_seq):
        for c in range(n_chunk):
            r0 = s * seq_rows + c * CHUNK
            bc = bcum[r0:r0 + CHUNK]
            b_last = bc[CHUNK - 1:CHUNK]
            k_end = (gk[r0:r0 + CHUNK] * jnp.exp(b_last - bc)).astype(_BF)
            vf = gv[r0:r0 + CHUNK]
            k_stack = jnp.concatenate(
                [jnp.where(head_of_lane == h, k_all[r0:r0 + CHUNK], 0.0) for h in range(GLA_HEADS)],
                axis=0).astype(_BF)
            v_bd = jnp.concatenate(
                [jnp.where(vblk == h, vf, 0.0) for h in range(GLA_HEADS)], axis=0).astype(_BF)
            a = jnp.where(causal, _dot_nt(q_all[r0:r0 + CHUNK], k_stack), 0.0).astype(_BF)
            intra.append(_dot(a, v_bd))
            kv_upd.append(jnp.where(bd_mask_t, _dot_tn(vf.astype(_BF), k_end), 0.0))
            decay_rows.append(jnp.exp(b_last))
    o_rows = []
    for s in range(n_seq):
        state = s_ref[s]
        for c in range(n_chunk):
            i = s * n_chunk + c
            r0 = i * CHUNK
            o_rows.append(intra[i] + _dot_nt(q_all[r0:r0 + CHUNK], state.astype(_BF)))
            state = decay_rows[i] * state + kv_upd[i]
        s_ref[s] = state
    o_all = jnp.concatenate(o_rows, axis=0)
    ycat_ref[:, ATTN_DIM:ATTN_DIM + GLA_VDIM] = _group_rms_scale(o_all, GLA_DV, ones64) * gla_g * gate_g

    for s in range(n_seq):
        sbd_ref[...] = s_ref[s].T
        for h in range(GLA_HEADS):
            ns_ref[s, h] = sbd_ref[h * GLA_DK:(h + 1) * GLA_DK, h * GLA_DV:(h + 1) * GLA_DV]

    last_layer = layer == pl.num_programs(2) - 1
    next_gain = layer_row("norm_gain", jnp.where(last_layer, 0, layer + 1))
    for r0 in range(0, tile, OUT_ROW_BLOCK):
        r1 = r0 + OUT_ROW_BLOCK
        y = x[r0:r1] + _dot(ycat_ref[r0:r1, :].astype(_BF), w_out_ref[...])
        y_ref[r0:r1, :] = y
        xcur_ref[r0:r1, :] = y
        hb_ref[r0:r1, :] = normed_bf16(jnp.where(last_layer, xnext_ref[r0:r1, :], y), next_gain)


def _run_stream(x2d, n_seqs, seq_len, hist, sinks, params, *, carry, q_pos0):
    depth = sinks.shape[0]
    if carry:
        assert q_pos0 == 0
        tile, n_seq, n_chunk = PROMPT_TILE, 1, PROMPT_TILE // CHUNK
        grid = (n_seqs, seq_len // tile, depth)
    else:
        assert seq_len == CHUNK and q_pos0 >= WINDOW
        tile, n_seq, n_chunk = SAMPLE_SEQS * CHUNK, SAMPLE_SEQS, 1
        grid = (n_seqs // n_seq, 1, depth)
    n_t = grid[1]
    seq_rows = n_chunk * CHUNK

    def const(shape):
        return pl.BlockSpec(shape, lambda g, t, l: (0,) * len(shape), pipeline_mode=pl.Buffered(1))

    def per_layer_group(shape):
        return pl.BlockSpec((depth, n_seq) + shape, lambda g, t, l: (0, g) + (0,) * len(shape))

    state_shapes = ((WINDOW, KV_DIM), (WINDOW, KV_DIM), (GLA_HEADS, GLA_DK, GLA_DV), (CONV_HIST, CONV_DIM))
    x_spec = pl.BlockSpec((tile, D_MODEL), lambda g, t, l: (g * n_t + t, 0))
    n_tiles = x2d.shape[0] // tile
    xnext_spec = pl.BlockSpec((tile, D_MODEL), lambda g, t, l: (jnp.minimum(g * n_t + t + 1, n_tiles - 1), 0))
    in_specs = [pl.BlockSpec(memory_space=pltpu.SMEM), x_spec, xnext_spec]
    args = [sinks, x2d, x2d]
    if not carry:
        in_specs += [per_layer_group(sh) for sh in state_shapes]
        args += list(hist)
    operands = tuple(params) + _shape_constants(tile)
    in_specs += [const(p.shape) for p in operands]
    in_specs[len(args)] = const(operands[0].shape[:2] + (GLR_SRC,))
    args += list(operands)

    out_shape = (jax.ShapeDtypeStruct(x2d.shape, _F32),) + tuple(
        jax.ShapeDtypeStruct((depth, n_seqs) + sh, _F32) for sh in state_shapes)
    out_specs = (x_spec,) + tuple(per_layer_group(sh) for sh in state_shapes)
    scratch = [pltpu.VMEM((depth, n_seq, WINDOW + seq_rows + CHUNK, ATTN_KV_HEADS * LANES), _BF),
               pltpu.VMEM((depth, n_seq, WINDOW + seq_rows + CHUNK, ATTN_KV_HEADS * 2 * LANES), _BF),
               pltpu.VMEM((depth, n_seq, GLA_VDIM, GLA_KDIM), _F32),
               pltpu.VMEM((depth, n_seq, CONV_PAD + seq_rows, CONV_DIM), _F32),
               pltpu.VMEM((tile, ATTN_DIM), _BF),
               pltpu.VMEM((tile, ATTN_DIM), _BF),
               pltpu.VMEM((tile, D_MODEL), _F32),
               pltpu.VMEM((tile, IN_COLS_PACKED), _F32),
               pltpu.VMEM((tile, D_MODEL), _F32),
               pltpu.VMEM((tile, D_MODEL), _BF),
               pltpu.VMEM((GLA_KDIM, GLA_VDIM), _F32),
               pltpu.VMEM((tile, D_MODEL), _F32)]
    body = functools.partial(_stream_kernel, tile=tile, n_seq=n_seq, n_chunk=n_chunk, carry=carry, q_pos0=q_pos0)
    return pl.pallas_call(
        body, grid=grid, in_specs=in_specs, out_specs=out_specs, out_shape=out_shape,
        scratch_shapes=scratch,
        compiler_params=pltpu.CompilerParams(dimension_semantics=("arbitrary", "arbitrary", "arbitrary"),
                                             vmem_limit_bytes=VMEM_LIMIT_BYTES),
        name="mixer_prompt" if carry else "mixer_sample",
    )(*args)


def _pack_params(norm_gain, w_in, q_norm_gain, k_norm_gain, gla_w_gate_up, gla_b_gate, gla_norm_gain,
                 conv_w, conv_b, conv_ln_gain, conv_ln_bias, conv_w_pw, conv_b_pw, w_out):
    w_in_a = w_in.astype(_BF)
    w_in_c = jnp.pad(w_in_a[:, :, GLR_SRC:GLR_SRC + GLA_LOWRANK], ((0, 0), (0, 0), (0, LANES - GLA_LOWRANK)))
    w_in_b = w_in_a[:, :, GLR_SRC + GLA_LOWRANK:]
    pieces = dict(norm_gain=norm_gain, q_gain=q_norm_gain, k_gain=k_norm_gain, gla_b=gla_b_gate,
                  gla_gain=gla_norm_gain, conv_b=conv_b, ln_gain=conv_ln_gain, ln_bias=conv_ln_bias, b_pw=conv_b_pw)
    depth = w_in.shape[0]
    parts = []
    for name, n in VEC_LAYOUT:
        parts += [pieces[name].astype(_F32)] * (n // pieces[name].shape[1])
        parts.append(jnp.zeros((depth, D_MODEL - n), _F32))
    parts.append(jnp.zeros((depth, (VEC_ROWS - len(VEC_LAYOUT)) * D_MODEL), _F32))
    vecs = jnp.concatenate(parts, axis=1).reshape(depth, VEC_ROWS, D_MODEL)
    w_up_p = jnp.pad(gla_w_gate_up, ((0, 0), (0, LANES - GLA_LOWRANK), (0, 0))).astype(_BF)
    convw = jnp.pad(conv_w.astype(_F32), ((0, 0), (0, CONV_PAD - CONV_WIDTH), (0, 0)))
    return w_in_a, w_in_b, w_in_c, w_up_p, conv_w_pw.astype(_BF), w_out.astype(_BF), vecs, convw


def kernel(x_prompt, x_sample, cache_k, cache_v, state_gla, state_conv, norm_gain, w_in, q_norm_gain, k_norm_gain, attn_sinks, gla_w_gate_up, gla_b_gate, gla_norm_gain, conv_w, conv_b, conv_ln_gain, conv_ln_bias, conv_w_pw, conv_b_pw, w_out):
    depth = w_in.shape[0]
    bp, lp, _ = x_prompt.shape
    bs, ls, _ = x_sample.shape
    params = _pack_params(norm_gain, w_in, q_norm_gain, k_norm_gain, gla_w_gate_up, gla_b_gate, gla_norm_gain,
                          conv_w, conv_b, conv_ln_gain, conv_ln_bias, conv_w_pw, conv_b_pw, w_out)
    sinks = jnp.pad(attn_sinks.astype(_F32), ((0, 0), (0, ATTN_HEADS)))
    yp, pk, pv, ps, pc = _run_stream(x_prompt.reshape(bp * lp, D_MODEL), bp, lp, None, sinks, params,
                                     carry=True, q_pos0=0)
    hist = (cache_k.reshape(depth, bs, WINDOW, KV_DIM), cache_v.reshape(depth, bs, WINDOW, KV_DIM),
            state_gla, state_conv)
    ys, sk, sv, ss, sc = _run_stream(x_sample.reshape(bs * ls, D_MODEL), bs, ls, hist, sinks, params,
                                     carry=False, q_pos0=PAST_LEN)
    kv_shape = (WINDOW, ATTN_KV_HEADS, HEAD_DIM)
    return (yp.reshape(bp, lp, D_MODEL), ys.reshape(bs, ls, D_MODEL),
            pk.reshape((depth, bp) + kv_shape), pv.reshape((depth, bp) + kv_shape), ps, pc,
            sk.reshape((depth, bs) + kv_shape), sv.reshape((depth, bs) + kv_shape), ss, sc)
```

```python
import functools
import math

import numpy as np
import jax
import jax.numpy as jnp
from jax import lax
from jax.experimental import pallas as pl
from jax.experimental.pallas import tpu as pltpu

D_MODEL = 1024
CHUNK = 64
ATTN_HEADS = 8
ATTN_KV_HEADS = 2
HEAD_DIM = 64
ATTN_GROUP = ATTN_HEADS // ATTN_KV_HEADS
ATTN_DIM = ATTN_HEADS * HEAD_DIM
KV_DIM = ATTN_KV_HEADS * HEAD_DIM
WINDOW = 128
BAND = WINDOW + CHUNK
KBAND = WINDOW + 2 * CHUNK
GLA_HEADS = 4
GLA_DK = 32
GLA_DV = 64
GLA_KDIM = GLA_HEADS * GLA_DK
GLA_VDIM = GLA_HEADS * GLA_DV
GLA_LOWRANK = 16
GLA_TAU = 16.0
CONV_DIM = 256
CONV_WIDTH = 31
CONV_HIST = CONV_WIDTH - 1
NORM_EPS = 1e-6
NEG_INF = -1e30
PAST_LEN = 4096
LOG2E = math.log2(math.e)

LANES = 128
SUBLANES = 8
CHUNK_SHIFT = CHUNK.bit_length() - 1
GLA_DK_SHIFT = GLA_DK.bit_length() - 1
GLA_DV_SHIFT = GLA_DV.bit_length() - 1
C_AQ, C_AK, C_AV, C_AG = 0, 512, 640, 768
C_GQ, C_GK, C_GV, C_GG = 1280, 1408, 1536, 1792
C_CV, C_CGL, C_CG, C_GLR = 2048, 2304, 2560, 2816
IN_COLS_PACKED = C_GLR + LANES
PROJ_GROUP = 256
CONV_PAD = 32
GLR_SRC = 2048
VEC_LAYOUT = (("norm_gain", D_MODEL), ("q_gain", LANES), ("k_gain", KV_DIM), ("gla_b", GLA_KDIM),
              ("gla_gain", GLA_VDIM), ("conv_b", CONV_DIM), ("ln_gain", CONV_DIM), ("ln_bias", CONV_DIM),
              ("b_pw", CONV_DIM))
VEC_ROWS = 16
VEC_ROW_OF = {name: (k, n) for k, (name, n) in enumerate(VEC_LAYOUT)}

OUT_ROW_BLOCK = 256
PROMPT_TILE = 512
SAMPLE_SEQS = 4
VMEM_LIMIT_BYTES = 56 * 1024 * 1024

_BF = jnp.bfloat16
_F32 = jnp.float32


def _shape_constants(tile):
    rows = np.arange(ATTN_GROUP * CHUNK)
    dist = np.abs((rows % CHUNK)[:, None] + WINDOW - np.arange(KBAND)[None, :]).astype(np.float64)
    bias = np.stack([
        (2.0 ** (-8.0 * (j * ATTN_GROUP + rows // CHUNK + 1) / ATTN_HEADS))[:, None] * LOG2E * dist
        for j in range(ATTN_KV_HEADS)]).astype(np.float32)
    bias[:, :, BAND:] = -NEG_INF
    variants = []
    for n in range(WINDOW // CHUNK + 1):
        b = bias.copy()
        b[:, :, :max(WINDOW - n * CHUNK, 0)] = -NEG_INF
        variants.append(b)
    bias = np.stack(variants)
    tr = np.arange(tile)
    tri = ((tr[:, None] // CHUNK == tr[None, :] // CHUNK) & (tr[None, :] <= tr[:, None])).astype(np.float32)
    ln = np.arange(LANES) // HEAD_DIM
    ones = (ln[:, None] == ln[None, :]).astype(np.float32)
    return jnp.asarray(bias), jnp.asarray(tri, dtype=_BF), jnp.asarray(ones, dtype=_BF)


def _group_rms_scale(x, group, ones_bd):
    outs = []
    for c0 in range(0, x.shape[1], LANES):
        blk = x[:, c0:c0 + LANES]
        ss = _dot((blk * blk).astype(_BF), ones_bd)
        outs.append(blk * lax.rsqrt(ss * (1.0 / group) + NORM_EPS))
    return outs[0] if len(outs) == 1 else jnp.concatenate(outs, axis=1)


def _silu(x):
    return x * jax.nn.sigmoid(x)


def _dot(a, b):
    return jnp.dot(a, b, preferred_element_type=_F32)


def _dot_nt(a, b):
    return lax.dot_general(a, b, (((1,), (1,)), ((), ())), preferred_element_type=_F32)


def _dot_tn(a, b):
    return lax.dot_general(a, b, (((0,), (0,)), ((), ())), preferred_element_type=_F32)


def _stream_kernel(*refs, tile, n_seq, n_chunk, carry, q_pos0):
    if carry:
        (sinks_ref, x_ref, xnext_ref, *rest) = refs
    else:
        (sinks_ref, x_ref, xnext_ref, hk_all, hv_all, hs_all, hc_all, *rest) = refs
    (w_in_a, w_in_b, w_in_c, w_up_all, w_pw_all, w_out_all, vec_all, convw_all, bias_ref, tri_ref, ones_ref,
     y_ref, nk_all, nv_all, ns_all, nc_all,
     kd_all, vd_all, s_all, u_all, qlo_ref, qhi_ref, ycat_ref, proj_ref, xcur_ref, hb_ref, sbd_ref,
     gate_ref) = rest
    t = pl.program_id(1)
    layer = pl.program_id(2)
    seq_rows = n_chunk * CHUNK
    w_up_ref, w_pw_ref, w_out_ref = (r.at[layer] for r in (w_up_all, w_pw_all, w_out_all))
    convw_ref = convw_all.at[layer]

    def layer_row(name, which=layer):
        r, n = VEC_ROW_OF[name]
        return vec_all[which, r:r + 1, 0:n]

    q_gain = layer_row("q_gain") * (LOG2E * HEAD_DIM ** -0.5)
    k_gain = layer_row("k_gain")
    kd_ref, vd_ref, s_ref, u_ref = (r.at[layer] for r in (kd_all, vd_all, s_all, u_all))
    nk_ref, nv_ref, ns_ref, nc_ref = (r.at[layer] for r in (nk_all, nv_all, ns_all, nc_all))
    if not carry:
        hk_ref, hv_ref, hs_ref, hc_ref = (r.at[layer] for r in (hk_all, hv_all, hs_all, hc_all))

    lane128 = lax.broadcasted_iota(jnp.int32, (1, LANES), 1)
    lo_half = lane128 < HEAD_DIM

    def dup_halves(a):
        sw = pltpu.roll(a, HEAD_DIM, 1)
        return jnp.concatenate([jnp.where(lo_half, a, sw), jnp.where(lo_half, sw, a)], axis=1)

    buf_rows = WINDOW + seq_rows + CHUNK
    zeros_blk = jnp.zeros((buf_rows, LANES), _BF)
    ones_blk = jnp.ones((buf_rows, LANES), _BF)
    v_blank = jnp.concatenate([zeros_blk, ones_blk] * ATTN_KV_HEADS, axis=1)

    def store_v(s, row0, vdup_rows):
        for j in range(ATTN_KV_HEADS):
            vd_ref[s, row0:row0 + vdup_rows.shape[0], 2 * j * LANES:(2 * j + 1) * LANES] = (
                vdup_rows[:, j * LANES:(j + 1) * LANES])

    if carry:
        @pl.when(t == 0)
        def _():
            kd_ref[...] = jnp.zeros(kd_ref.shape, _BF)
            for s in range(n_seq):
                vd_ref[s] = v_blank
            s_ref[...] = jnp.zeros(s_ref.shape, _F32)
            u_ref[:, 0:CONV_PAD, :] = jnp.zeros((n_seq, CONV_PAD, CONV_DIM), _F32)
    else:
        for s in range(n_seq):
            kd_ref[s, 0:WINDOW, :] = dup_halves(hk_ref[s]).astype(_BF)
            kd_ref[s, WINDOW + seq_rows:, :] = jnp.zeros((CHUNK, 2 * LANES), _BF)
            vd_ref[s] = v_blank
            store_v(s, 0, dup_halves(hv_ref[s]).astype(_BF))
            sbd_ref[...] = jnp.zeros((GLA_KDIM, GLA_VDIM), _F32)
            for h in range(GLA_HEADS):
                sbd_ref[h * GLA_DK:(h + 1) * GLA_DK, h * GLA_DV:(h + 1) * GLA_DV] = hs_ref[s, h]
            s_ref[s] = sbd_ref[...].T
            u_ref[s, 0:SUBLANES, :] = jnp.zeros((SUBLANES, CONV_DIM), _F32)
            u_ref[s, CONV_PAD - CONV_HIST:CONV_PAD, :] = hc_ref[s]

    def normed_bf16(v, gain_row):
        ms = jnp.mean(v * v, axis=-1, keepdims=True)
        return (v * lax.rsqrt(ms + NORM_EPS) * gain_row).astype(_BF)

    @pl.when(layer == 0)
    def _():
        xcur_ref[...] = x_ref[...]

    @pl.when((pl.program_id(0) == 0) & (t == 0) & (layer == 0))
    def _():
        hb_ref[...] = normed_bf16(x_ref[...], layer_row("norm_gain"))
    x = xcur_ref[...]
    hb = hb_ref[...]

    anchor_on = sinks_ref[layer, ATTN_HEADS] != 0.0

    ones64 = ones_ref[...]

    def post_q(g):
        def post(res):
            for b in range(PROJ_GROUP // LANES):
                c0 = g * PROJ_GROUP + b * LANES
                qn = _group_rms_scale(res[:, b * LANES:(b + 1) * LANES], HEAD_DIM, ones64) * q_gain
                qlo_ref[:, c0:c0 + LANES] = jnp.where(lo_half, qn, 0.0).astype(_BF)
                qhi_ref[:, c0:c0 + LANES] = jnp.where(lo_half, 0.0, qn).astype(_BF)
        return post

    def post_kv(res):
        kn = _group_rms_scale(res[:, 0:KV_DIM], HEAD_DIM, ones64) * k_gain
        vv = res[:, KV_DIM:2 * KV_DIM]
        kdup = dup_halves(kn).astype(_BF)
        vdup = dup_halves(vv).astype(_BF)
        for s in range(n_seq):
            kd_ref[s, WINDOW:WINDOW + seq_rows, :] = kdup[s * seq_rows:(s + 1) * seq_rows]
            store_v(s, WINDOW, vdup[s * seq_rows:(s + 1) * seq_rows])
        if carry:
            nk_ref[0] = kn[tile - WINDOW:tile]
            nv_ref[0] = vv[tile - WINDOW:tile]
        else:
            for s in range(n_seq):
                nk_ref[s, 0:WINDOW - seq_rows, :] = hk_ref[s, seq_rows:WINDOW, :]
                nv_ref[s, 0:WINDOW - seq_rows, :] = hv_ref[s, seq_rows:WINDOW, :]
                nk_ref[s, WINDOW - seq_rows:WINDOW, :] = kn[s * seq_rows:(s + 1) * seq_rows]
                nv_ref[s, WINDOW - seq_rows:WINDOW, :] = vv[s * seq_rows:(s + 1) * seq_rows]

    def post_gate(col0):
        def post(res):
            gate_ref[:, col0:col0 + PROJ_GROUP] = _silu(res)
        return post

    group_post = {C_AG // PROJ_GROUP: post_gate(0), C_AG // PROJ_GROUP + 1: post_gate(PROJ_GROUP),
                  C_GG // PROJ_GROUP: post_gate(ATTN_DIM), C_CG // PROJ_GROUP: post_gate(ATTN_DIM + GLA_VDIM)}

    def w_in_cols(c0, c1):
        if c1 <= C_CV:
            return w_in_a[layer, :, c0:c1]
        if c1 <= C_GLR:
            return w_in_b[layer, :, c0 - C_CV:c1 - C_CV]
        return w_in_c[layer, :, c0 - C_GLR:c1 - C_GLR]

    def compute_group(g):
        c0, c1 = g * PROJ_GROUP, min((g + 1) * PROJ_GROUP, IN_COLS_PACKED)
        res = _dot(hb, w_in_cols(c0, c1))
        if g in group_post:
            group_post[g](res)
        else:
            proj_ref[:, c0:c1] = res
        return jnp.where(anchor_on, res[0:1, 0:CONV_DIM], 0.0)

    def proj(c0, width):
        return proj_ref[:, c0:c0 + width]

    compute_group(C_CV // PROJ_GROUP)
    compute_group(C_CGL // PROJ_GROUP)
    u = proj(C_CV, CONV_DIM) * jax.nn.sigmoid(proj(C_CGL, CONV_DIM))
    for s in range(n_seq):
        u_ref[s, CONV_PAD:CONV_PAD + seq_rows, :] = u[s * seq_rows:(s + 1) * seq_rows]
    conv_acc = [jnp.broadcast_to(layer_row("conv_b"), (seq_rows, CONV_DIM))] * n_seq
    for rho in range(SUBLANES):
        anchor = compute_group(rho)
        for s in range(n_seq):
            frame_rows = seq_rows + (SUBLANES if rho else 0)
            frame = None
            for j in range(CONV_WIDTH):
                off = CONV_PAD - CONV_HIST + j
                if off % SUBLANES != rho:
                    continue
                term = (convw_ref[j:j + 1, :] + anchor) * u_ref[s, off - rho:off - rho + frame_rows, :]
                frame = term if frame is None else frame + term
            conv_acc[s] = conv_acc[s] + frame[rho:rho + seq_rows]
    compute_group(C_CG // PROJ_GROUP)
    compute_group(C_GLR // PROJ_GROUP)
    cc = conv_acc[0] if n_seq == 1 else jnp.concatenate(conv_acc, axis=0)

    for s in range(n_seq):
        nc_ref[s] = u_ref[s, seq_rows + CONV_PAD - CONV_HIST:seq_rows + CONV_PAD, :]
    if carry:
        u_ref[:, 0:CONV_PAD, :] = u_ref[:, seq_rows:seq_rows + CONV_PAD, :]

    mu = jnp.mean(cc, axis=-1, keepdims=True)
    cen = cc - mu
    var = jnp.mean(cen * cen, axis=-1, keepdims=True)
    ln = cen * lax.rsqrt(var + NORM_EPS) * layer_row("ln_gain") + layer_row("ln_bias")
    cpw = _dot(_silu(ln).astype(_BF), w_pw_ref[...]) + layer_row("b_pw")
    ycat_ref[:, ATTN_DIM + GLA_VDIM:D_MODEL] = cpw * gate_ref[:, ATTN_DIM + GLA_VDIM:D_MODEL]

    z = _dot(proj(C_GLR, LANES).astype(_BF), w_up_ref[...]) + layer_row("gla_b")
    log_a = (jnp.minimum(z, 0.0) - jnp.log(1.0 + jnp.exp(-jnp.abs(z)))) * (1.0 / GLA_TAU)
    tri = tri_ref[...]
    la_hi = log_a.astype(_BF)
    la_lo = (log_a - la_hi.astype(_F32)).astype(_BF)
    bcum = _dot(tri, la_hi) + _dot(tri, la_lo)
    gq = proj(C_GQ, GLA_KDIM) * GLA_DK ** -0.5
    gk = proj(C_GK, GLA_KDIM)
    gv = proj(C_GV, GLA_VDIM)
    gate_g = gate_ref[:, ATTN_DIM:ATTN_DIM + GLA_VDIM]
    gla_g = layer_row("gla_gain")

    for g in range(ATTN_DIM // PROJ_GROUP):
        post_q(g)(proj(C_AQ + g * PROJ_GROUP, PROJ_GROUP))
    post_kv(proj(C_AK, 2 * KV_DIM))

    head_of_lane = lane128 >> GLA_DK_SHIFT
    a_r = lax.broadcasted_iota(jnp.int32, (CHUNK, GLA_HEADS * CHUNK), 0)
    a_c = lax.broadcasted_iota(jnp.int32, (CHUNK, GLA_HEADS * CHUNK), 1)
    causal = (a_c & (CHUNK - 1)) <= a_r
    vblk = lax.broadcasted_iota(jnp.int32, (1, GLA_VDIM), 1) >> GLA_DV_SHIFT
    bd_mask_t = ((lax.broadcasted_iota(jnp.int32, (GLA_VDIM, GLA_KDIM), 0) >> GLA_DV_SHIFT)
                 == (lax.broadcasted_iota(jnp.int32, (GLA_VDIM, GLA_KDIM), 1) >> GLA_DK_SHIFT))
    q_all = (gq * jnp.exp(bcum)).astype(_BF)
    k_all = gk * jnp.exp(-bcum)
    intra, kv_upd, decay_rows = [], [], []
    for s in range(n_seq):
        for c in range(n_chunk):
            r0 = s * seq_rows + c * CHUNK
            bc = bcum[r0:r0 + CHUNK]
            b_last = bc[CHUNK - 1:CHUNK]
            k_end = (gk[r0:r0 + CHUNK] * jnp.exp(b_last - bc)).astype(_BF)
            vf = gv[r0:r0 + CHUNK]
            k_stack = jnp.concatenate(
                [jnp.where(head_of_lane == h, k_all[r0:r0 + CHUNK], 0.0) for h in range(GLA_HEADS)],
                axis=0).astype(_BF)
            v_bd = jnp.concatenate(
                [jnp.where(vblk == h, vf, 0.0) for h in range(GLA_HEADS)], axis=0).astype(_BF)
            a = jnp.where(causal, _dot_nt(q_all[r0:r0 + CHUNK], k_stack), 0.0).astype(_BF)
            intra.append(_dot(a, v_bd))
            kv_upd.append(jnp.where(bd_mask_t, _dot_tn(vf.astype(_BF), k_end), 0.0))
            decay_rows.append(jnp.exp(b_last))

    rows = ATTN_GROUP * CHUNK
    r_head1 = lax.broadcasted_iota(jnp.int32, (rows, 1), 0) >> CHUNK_SHIFT
    sink_cols = []
    for j in range(ATTN_KV_HEADS):
        sink = jnp.zeros((rows, 1), _F32)
        for r in range(ATTN_GROUP):
            sink = jnp.where(r_head1 == r, sinks_ref[layer, j * ATTN_GROUP + r], sink)
        sink_cols.append(sink * LOG2E)

    for s in range(n_seq):
        for c in range(n_chunk):
            r0 = s * seq_rows + c * CHUNK
            k0 = c * CHUNK
            steady = WINDOW // CHUNK
            variant = jnp.minimum(t * n_chunk + c, steady) if carry else steady
            for j in range(ATTN_KV_HEADS):
                qs = jnp.concatenate(
                    [qlo_ref[r0:r0 + CHUNK, (2 * j) * LANES:(2 * j + 1) * LANES],
                     qhi_ref[r0:r0 + CHUNK, (2 * j) * LANES:(2 * j + 1) * LANES],
                     qlo_ref[r0:r0 + CHUNK, (2 * j + 1) * LANES:(2 * j + 2) * LANES],
                     qhi_ref[r0:r0 + CHUNK, (2 * j + 1) * LANES:(2 * j + 2) * LANES]], axis=0)
                kb = kd_ref[s, k0:k0 + KBAND, j * LANES:(j + 1) * LANES]
                vb = vd_ref[s, k0:k0 + KBAND, 2 * j * LANES:(2 * j + 2) * LANES]
                qk = _dot_nt(qs, kb)
                sink = sink_cols[j]
                p_blocks, m_blocks = [], []
                for r in range(ATTN_GROUP):
                    sc = qk[r * CHUNK:(r + 1) * CHUNK] - bias_ref[variant, j, r * CHUNK:(r + 1) * CHUNK, :]
                    m_r = jnp.maximum(jnp.max(sc, axis=-1, keepdims=True), sink[r * CHUNK:(r + 1) * CHUNK])
                    p_blocks.append(jnp.exp2(sc - m_r).astype(_BF))
                    m_blocks.append(m_r)
                m = jnp.concatenate(m_blocks, axis=0)
                pv = _dot(jnp.concatenate(p_blocks, axis=0), vb)
                o = pv[:, 0:LANES] / (pv[:, LANES:2 * LANES] + jnp.exp2(sink - m))
                for pb in range(2):
                    blk = 2 * j + pb
                    ob = jnp.where(lo_half, o[(2 * pb) * CHUNK:(2 * pb + 1) * CHUNK],
                                   o[(2 * pb + 1) * CHUNK:(2 * pb + 2) * CHUNK])
                    ycat_ref[r0:r0 + CHUNK, blk * LANES:(blk + 1) * LANES] = (
                        ob * gate_ref[r0:r0 + CHUNK, blk * LANES:(blk + 1) * LANES])

    if carry:
        kd_ref[:, 0:WINDOW, :] = kd_ref[:, seq_rows:seq_rows + WINDOW, :]
        vd_ref[:, 0:WINDOW, :] = vd_ref[:, seq_rows:seq_rows + WINDOW, :]

    o_rows = []
    for s in range(n_seq):
        state = s_ref[s]
        for c in range(n_chunk):
            i = s * n_chunk + c
            r0 = i * CHUNK
            o_rows.append(intra[i] + _dot_nt(q_all[r0:r0 + CHUNK], state.astype(_BF)))
            state = decay_rows[i] * state + kv_upd[i]
        s_ref[s] = state
    o_all = jnp.concatenate(o_rows, axis=0)
    ycat_ref[:, ATTN_DIM:ATTN_DIM + GLA_VDIM] = _group_rms_scale(o_all, GLA_DV, ones64) * gla_g * gate_g

    for s in range(n_seq):
        sbd_ref[...] = s_ref[s].T
        for h in range(GLA_HEADS):
            ns_ref[s, h] = sbd_ref[h * GLA_DK:(h + 1) * GLA_DK, h * GLA_DV:(h + 1) * GLA_DV]

    last_layer = layer == pl.num_programs(2) - 1
    next_gain = layer_row("norm_gain", jnp.where(last_layer, 0, layer + 1))
    for r0 in range(0, tile, OUT_ROW_BLOCK):
        r1 = r0 + OUT_ROW_BLOCK
        y = x[r0:r1] + _dot(ycat_ref[r0:r1, :].astype(_BF), w_out_ref[...])
        y_ref[r0:r1, :] = y
        xcur_ref[r0:r1, :] = y
        hb_ref[r0:r1, :] = normed_bf16(jnp.where(last_layer, xnext_ref[r0:r1, :], y), next_gain)


def _run_stream(x2d, n_seqs, seq_len, hist, sinks, params, *, carry, q_pos0):
    depth = sinks.shape[0]
    if carry:
        assert q_pos0 == 0
        tile, n_seq, n_chunk = PROMPT_TILE, 1, PROMPT_TILE // CHUNK
        grid = (n_seqs, seq_len // tile, depth)
    else:
        assert seq_len == CHUNK and q_pos0 >= WINDOW
        tile, n_seq, n_chunk = SAMPLE_SEQS * CHUNK, SAMPLE_SEQS, 1
        grid = (n_seqs // n_seq, 1, depth)
    n_t = grid[1]
    seq_rows = n_chunk * CHUNK

    def const(shape):
        return pl.BlockSpec(shape, lambda g, t, l: (0,) * len(shape), pipeline_mode=pl.Buffered(1))

    def per_layer_group(shape):
        return pl.BlockSpec((depth, n_seq) + shape, lambda g, t, l: (0, g) + (0,) * len(shape))

    state_shapes = ((WINDOW, KV_DIM), (WINDOW, KV_DIM), (GLA_HEADS, GLA_DK, GLA_DV), (CONV_HIST, CONV_DIM))
    x_spec = pl.BlockSpec((tile, D_MODEL), lambda g, t, l: (g * n_t + t, 0))
    n_tiles = x2d.shape[0] // tile
    xnext_spec = pl.BlockSpec((tile, D_MODEL), lambda g, t, l: (jnp.minimum(g * n_t + t + 1, n_tiles - 1), 0))
    in_specs = [pl.BlockSpec(memory_space=pltpu.SMEM), x_spec, xnext_spec]
    args = [sinks, x2d, x2d]
    if not carry:
        in_specs += [per_layer_group(sh) for sh in state_shapes]
        args += list(hist)
    operands = tuple(params) + _shape_constants(tile)
    in_specs += [const(p.shape) for p in operands]
    in_specs[len(args)] = const(operands[0].shape[:2] + (GLR_SRC,))
    args += list(operands)

    out_shape = (jax.ShapeDtypeStruct(x2d.shape, _F32),) + tuple(
        jax.ShapeDtypeStruct((depth, n_seqs) + sh, _F32) for sh in state_shapes)
    out_specs = (x_spec,) + tuple(per_layer_group(sh) for sh in state_shapes)
    scratch = [pltpu.VMEM((depth, n_seq, WINDOW + seq_rows + CHUNK, ATTN_KV_HEADS * LANES), _BF),
               pltpu.VMEM((depth, n_seq, WINDOW + seq_rows + CHUNK, ATTN_KV_HEADS * 2 * LANES), _BF),
               pltpu.VMEM((depth, n_seq, GLA_VDIM, GLA_KDIM), _F32),
               pltpu.VMEM((depth, n_seq, CONV_PAD + seq_rows, CONV_DIM), _F32),
               pltpu.VMEM((tile, ATTN_DIM), _BF),
               pltpu.VMEM((tile, ATTN_DIM), _BF),
               pltpu.VMEM((tile, D_MODEL), _F32),
               pltpu.VMEM((tile, IN_COLS_PACKED), _F32),
               pltpu.VMEM((tile, D_MODEL), _F32),
               pltpu.VMEM((tile, D_MODEL), _BF),
               pltpu.VMEM((GLA_KDIM, GLA_VDIM), _F32),
               pltpu.VMEM((tile, D_MODEL), _F32)]
    body = functools.partial(_stream_kernel, tile=tile, n_seq=n_seq, n_chunk=n_chunk, carry=carry, q_pos0=q_pos0)
    return pl.pallas_call(
        body, grid=grid, in_specs=in_specs, out_specs=out_specs, out_shape=out_shape,
        scratch_shapes=scratch,
        compiler_params=pltpu.CompilerParams(dimension_semantics=("arbitrary", "arbitrary", "arbitrary"),
                                             vmem_limit_bytes=VMEM_LIMIT_BYTES),
        name="mixer_prompt" if carry else "mixer_sample",
    )(*args)


def _pack_params(norm_gain, w_in, q_norm_gain, k_norm_gain, gla_w_gate_up, gla_b_gate, gla_norm_gain,
                 conv_w, conv_b, conv_ln_gain, conv_ln_bias, conv_w_pw, conv_b_pw, w_out):
    w_in_a = w_in.astype(_BF)
    w_in_c = jnp.pad(w_in_a[:, :, GLR_SRC:GLR_SRC + GLA_LOWRANK], ((0, 0), (0, 0), (0, LANES - GLA_LOWRANK)))
    w_in_b = w_in_a[:, :, GLR_SRC + GLA_LOWRANK:]
    pieces = dict(norm_gain=norm_gain, q_gain=q_norm_gain, k_gain=k_norm_gain, gla_b=gla_b_gate,
                  gla_gain=gla_norm_gain, conv_b=conv_b, ln_gain=conv_ln_gain, ln_bias=conv_ln_bias, b_pw=conv_b_pw)
    depth = w_in.shape[0]
    parts = []
    for name, n in VEC_LAYOUT:
        parts += [pieces[name].astype(_F32)] * (n // pieces[name].shape[1])
        parts.append(jnp.zeros((depth, D_MODEL - n), _F32))
    parts.append(jnp.zeros((depth, (VEC_ROWS - len(VEC_LAYOUT)) * D_MODEL), _F32))
    vecs = jnp.concatenate(parts, axis=1).reshape(depth, VEC_ROWS, D_MODEL)
    w_up_p = jnp.pad(gla_w_gate_up, ((0, 0), (0, LANES - GLA_LOWRANK), (0, 0))).astype(_BF)
    convw = jnp.pad(conv_w.astype(_F32), ((0, 0), (0, CONV_PAD - CONV_WIDTH), (0, 0)))
    return w_in_a, w_in_b, w_in_c, w_up_p, conv_w_pw.astype(_BF), w_out.astype(_BF), vecs, convw


def kernel(x_prompt, x_sample, cache_k, cache_v, state_gla, state_conv, norm_gain, w_in, q_norm_gain, k_norm_gain, attn_sinks, gla_w_gate_up, gla_b_gate, gla_norm_gain, conv_w, conv_b, conv_ln_gain, conv_ln_bias, conv_w_pw, conv_b_pw, w_out):
    depth = w_in.shape[0]
    bp, lp, _ = x_prompt.shape
    bs, ls, _ = x_sample.shape
    params = _pack_params(norm_gain, w_in, q_norm_gain, k_norm_gain, gla_w_gate_up, gla_b_gate, gla_norm_gain,
                          conv_w, conv_b, conv_ln_gain, conv_ln_bias, conv_w_pw, conv_b_pw, w_out)
    sinks = jnp.pad(attn_sinks.astype(_F32), ((0, 0), (0, ATTN_HEADS)))
    yp, pk, pv, ps, pc = _run_stream(x_prompt.reshape(bp * lp, D_MODEL), bp, lp, None, sinks, params,
                                     carry=True, q_pos0=0)
    hist = (cache_k.reshape(depth, bs, WINDOW, KV_DIM), cache_v.reshape(depth, bs, WINDOW, KV_DIM),
            state_gla, state_conv)
    ys, sk, sv, ss, sc = _run_stream(x_sample.reshape(bs * ls, D_MODEL), bs, ls, hist, sinks, params,
                                     carry=False, q_pos0=PAST_LEN)
    kv_shape = (WINDOW, ATTN_KV_HEADS, HEAD_DIM)
    return (yp.reshape(bp, lp, D_MODEL), ys.reshape(bs, ls, D_MODEL),
            pk.reshape((depth, bp) + kv_shape), pv.reshape((depth, bp) + kv_shape), ps, pc,
            sk.reshape((depth, bs) + kv_shape), sv.reshape((depth, bs) + kv_shape), ss, sc)
```

```python
import functools
import math

import numpy as np
import jax
import jax.numpy as jnp
from jax import lax
from jax.experimental import pallas as pl
from jax.experimental.pallas import tpu as pltpu

D_MODEL = 1024
CHUNK = 64
ATTN_HEADS = 8
ATTN_KV_HEADS = 2
HEAD_DIM = 64
ATTN_GROUP = ATTN_HEADS // ATTN_KV_HEADS
ATTN_DIM = ATTN_HEADS * HEAD_DIM
KV_DIM = ATTN_KV_HEADS * HEAD_DIM
WINDOW = 128
BAND = WINDOW + CHUNK
KBAND = WINDOW + 2 * CHUNK
GLA_HEADS = 4
GLA_DK = 32
GLA_DV = 64
GLA_KDIM = GLA_HEADS * GLA_DK
GLA_VDIM = GLA_HEADS * GLA_DV
GLA_LOWRANK = 16
GLA_TAU = 16.0
CONV_DIM = 256
CONV_WIDTH = 31
CONV_HIST = CONV_WIDTH - 1
NORM_EPS = 1e-6
NEG_INF = -1e30
PAST_LEN = 4096
LOG2E = math.log2(math.e)

LANES = 128
SUBLANES = 8
CHUNK_SHIFT = CHUNK.bit_length() - 1
GLA_DK_SHIFT = GLA_DK.bit_length() - 1
GLA_DV_SHIFT = GLA_DV.bit_length() - 1
C_AQ, C_AK, C_AV, C_AG = 0, 512, 640, 768
C_GQ, C_GK, C_GV, C_GG = 1280, 1408, 1536, 1792
C_CV, C_CGL, C_CG, C_GLR = 2048, 2304, 2560, 2816
IN_COLS_PACKED = C_GLR + LANES
PROJ_GROUP = 256
CONV_PAD = 32
GLR_SRC = 2048
VEC_LAYOUT = (("norm_gain", D_MODEL), ("q_gain", LANES), ("k_gain", KV_DIM), ("gla_b", GLA_KDIM),
              ("gla_gain", GLA_VDIM), ("conv_b", CONV_DIM), ("ln_gain", CONV_DIM), ("ln_bias", CONV_DIM),
              ("b_pw", CONV_DIM))
VEC_ROWS = 16
VEC_ROW_OF = {name: (k, n) for k, (name, n) in enumerate(VEC_LAYOUT)}

OUT_ROW_BLOCK = 256
PROMPT_TILE = 512
SAMPLE_SEQS = 4
VMEM_LIMIT_BYTES = 56 * 1024 * 1024

_BF = jnp.bfloat16
_F32 = jnp.float32


def _shape_constants(tile):
    rows = np.arange(ATTN_GROUP * CHUNK)
    dist = np.abs((rows % CHUNK)[:, None] + WINDOW - np.arange(KBAND)[None, :]).astype(np.float64)
    bias = np.stack([
        (2.0 ** (-8.0 * (j * ATTN_GROUP + rows // CHUNK + 1) / ATTN_HEADS))[:, None] * LOG2E * dist
        for j in range(ATTN_KV_HEADS)]).astype(np.float32)
    bias[:, :, BAND:] = -NEG_INF
    variants = []
    for n in range(WINDOW // CHUNK + 1):
        b = bias.copy()
        b[:, :, :max(WINDOW - n * CHUNK, 0)] = -NEG_INF
        variants.append(b)
    bias = np.stack(variants)
    tr = np.arange(tile)
    tri = ((tr[:, None] // CHUNK == tr[None, :] // CHUNK) & (tr[None, :] <= tr[:, None])).astype(np.float32)
    ln = np.arange(LANES) // HEAD_DIM
    ones = (ln[:, None] == ln[None, :]).astype(np.float32)
    return jnp.asarray(bias), jnp.asarray(tri, dtype=_BF), jnp.asarray(ones, dtype=_BF)


def _group_rms_scale(x, group, ones_bd):
    outs = []
    for c0 in range(0, x.shape[1], LANES):
        blk = x[:, c0:c0 + LANES]
        ss = _dot((blk * blk).astype(_BF), ones_bd)
        outs.append(blk * lax.rsqrt(ss * (1.0 / group) + NORM_EPS))
    return outs[0] if len(outs) == 1 else jnp.concatenate(outs, axis=1)


def _silu(x):
    return x * jax.nn.sigmoid(x)


def _dot(a, b):
    return jnp.dot(a, b, preferred_element_type=_F32)


def _dot_nt(a, b):
    return lax.dot_general(a, b, (((1,), (1,)), ((), ())), preferred_element_type=_F32)


def _dot_tn(a, b):
    return lax.dot_general(a, b, (((0,), (0,)), ((), ())), preferred_element_type=_F32)


def _stream_kernel(*refs, tile, n_seq, n_chunk, carry, q_pos0):
    if carry:
        (sinks_ref, x_ref, xnext_ref, *rest) = refs
    else:
        (sinks_ref, x_ref, xnext_ref, hk_all, hv_all, hs_all, hc_all, *rest) = refs
    (w_in_a, w_in_b, w_in_c, w_up_all, w_pw_all, w_out_all, vec_all, convw_all, bias_ref, tri_ref, ones_ref,
     y_ref, nk_all, nv_all, ns_all, nc_all,
     kd_all, vd_all, s_all, u_all, qlo_ref, qhi_ref, ycat_ref, proj_ref, xcur_ref, hb_ref, sbd_ref,
     gate_ref) = rest
    t = pl.program_id(1)
    layer = pl.program_id(2)
    seq_rows = n_chunk * CHUNK
    w_up_ref, w_pw_ref, w_out_ref = (r.at[layer] for r in (w_up_all, w_pw_all, w_out_all))
    convw_ref = convw_all.at[layer]

    def layer_row(name, which=layer):
        r, n = VEC_ROW_OF[name]
        return vec_all[which, r:r + 1, 0:n]

    q_gain = layer_row("q_gain") * (LOG2E * HEAD_DIM ** -0.5)
    k_gain = layer_row("k_gain")
    kd_ref, vd_ref, s_ref, u_ref = (r.at[layer] for r in (kd_all, vd_all, s_all, u_all))
    nk_ref, nv_ref, ns_ref, nc_ref = (r.at[layer] for r in (nk_all, nv_all, ns_all, nc_all))
    if not carry:
        hk_ref, hv_ref, hs_ref, hc_ref = (r.at[layer] for r in (hk_all, hv_all, hs_all, hc_all))

    lane128 = lax.broadcasted_iota(jnp.int32, (1, LANES), 1)
    lo_half = lane128 < HEAD_DIM

    def dup_halves(a):
        sw = pltpu.roll(a, HEAD_DIM, 1)
        return jnp.concatenate([jnp.where(lo_half, a, sw), jnp.where(lo_half, sw, a)], axis=1)

    buf_rows = WINDOW + seq_rows + CHUNK
    zeros_blk = jnp.zeros((buf_rows, LANES), _BF)
    ones_blk = jnp.ones((buf_rows, LANES), _BF)
    v_blank = jnp.concatenate([zeros_blk, ones_blk] * ATTN_KV_HEADS, axis=1)

    def store_v(s, row0, vdup_rows):
        for j in range(ATTN_KV_HEADS):
            vd_ref[s, row0:row0 + vdup_rows.shape[0], 2 * j * LANES:(2 * j + 1) * LANES] = (
                vdup_rows[:, j * LANES:(j + 1) * LANES])

    if carry:
        @pl.when(t == 0)
        def _():
            kd_ref[...] = jnp.zeros(kd_ref.shape, _BF)
            for s in range(n_seq):
                vd_ref[s] = v_blank
            s_ref[...] = jnp.zeros(s_ref.shape, _F32)
            u_ref[:, 0:CONV_PAD, :] = jnp.zeros((n_seq, CONV_PAD, CONV_DIM), _F32)
    else:
        for s in range(n_seq):
            kd_ref[s, 0:WINDOW, :] = dup_halves(hk_ref[s]).astype(_BF)
            kd_ref[s, WINDOW + seq_rows:, :] = jnp.zeros((CHUNK, 2 * LANES), _BF)
            vd_ref[s] = v_blank
            store_v(s, 0, dup_halves(hv_ref[s]).astype(_BF))
            sbd_ref[...] = jnp.zeros((GLA_KDIM, GLA_VDIM), _F32)
            for h in range(GLA_HEADS):
                sbd_ref[h * GLA_DK:(h + 1) * GLA_DK, h * GLA_DV:(h + 1) * GLA_DV] = hs_ref[s, h]
            s_ref[s] = sbd_ref[...].T
            u_ref[s, 0:SUBLANES, :] = jnp.zeros((SUBLANES, CONV_DIM), _F32)
            u_ref[s, CONV_PAD - CONV_HIST:CONV_PAD, :] = hc_ref[s]

    def normed_bf16(v, gain_row):
        ms = jnp.mean(v * v, axis=-1, keepdims=True)
        return (v * lax.rsqrt(ms + NORM_EPS) * gain_row).astype(_BF)

    @pl.when(layer == 0)
    def _():
        xcur_ref[...] = x_ref[...]

    @pl.when((pl.program_id(0) == 0) & (t == 0) & (layer == 0))
    def _():
        hb_ref[...] = normed_bf16(x_ref[...], layer_row("norm_gain"))
    x = xcur_ref[...]
    hb = hb_ref[...]

    anchor_on = sinks_ref[layer, ATTN_HEADS] != 0.0

    ones64 = ones_ref[...]

    def post_q(g):
        def post(res):
            for b in range(PROJ_GROUP // LANES):
                c0 = g * PROJ_GROUP + b * LANES
                qn = _group_rms_scale(res[:, b * LANES:(b + 1) * LANES], HEAD_DIM, ones64) * q_gain
                qlo_ref[:, c0:c0 + LANES] = jnp.where(lo_half, qn, 0.0).astype(_BF)
                qhi_ref[:, c0:c0 + LANES] = jnp.where(lo_half, 0.0, qn).astype(_BF)
        return post

    def post_kv(res):
        kn = _group_rms_scale(res[:, 0:KV_DIM], HEAD_DIM, ones64) * k_gain
        vv = res[:, KV_DIM:2 * KV_DIM]
        kdup = dup_halves(kn).astype(_BF)
        vdup = dup_halves(vv).astype(_BF)
        for s in range(n_seq):
            kd_ref[s, WINDOW:WINDOW + seq_rows, :] = kdup[s * seq_rows:(s + 1) * seq_rows]
            store_v(s, WINDOW, vdup[s * seq_rows:(s + 1) * seq_rows])
        if carry:
            nk_ref[0] = kn[tile - WINDOW:tile]
            nv_ref[0] = vv[tile - WINDOW:tile]
        else:
            for s in range(n_seq):
                nk_ref[s, 0:WINDOW - seq_rows, :] = hk_ref[s, seq_rows:WINDOW, :]
                nv_ref[s, 0:WINDOW - seq_rows, :] = hv_ref[s, seq_rows:WINDOW, :]
                nk_ref[s, WINDOW - seq_rows:WINDOW, :] = kn[s * seq_rows:(s + 1) * seq_rows]
                nv_ref[s, WINDOW - seq_rows:WINDOW, :] = vv[s * seq_rows:(s + 1) * seq_rows]

    def post_gate(col0):
        def post(res):
            gate_ref[:, col0:col0 + PROJ_GROUP] = _silu(res)
        return post

    group_post = {C_AG // PROJ_GROUP: post_gate(0), C_AG // PROJ_GROUP + 1: post_gate(PROJ_GROUP),
                  C_GG // PROJ_GROUP: post_gate(ATTN_DIM), C_CG // PROJ_GROUP: post_gate(ATTN_DIM + GLA_VDIM)}

    def w_in_cols(c0, c1):
        if c1 <= C_CV:
            return w_in_a[layer, :, c0:c1]
        if c1 <= C_GLR:
            return w_in_b[layer, :, c0 - C_CV:c1 - C_CV]
        return w_in_c[layer, :, c0 - C_GLR:c1 - C_GLR]

    def compute_group(g):
        c0, c1 = g * PROJ_GROUP, min((g + 1) * PROJ_GROUP, IN_COLS_PACKED)
        res = _dot(hb, w_in_cols(c0, c1))
        if g in group_post:
            group_post[g](res)
        else:
            proj_ref[:, c0:c1] = res
        return jnp.where(anchor_on, res[0:1, 0:CONV_DIM], 0.0)

    def proj(c0, width):
        return proj_ref[:, c0:c0 + width]

    compute_group(C_CV // PROJ_GROUP)
    compute_group(C_CGL // PROJ_GROUP)
    u = proj(C_CV, CONV_DIM) * jax.nn.sigmoid(proj(C_CGL, CONV_DIM))
    for s in range(n_seq):
        u_ref[s, CONV_PAD:CONV_PAD + seq_rows, :] = u[s * seq_rows:(s + 1) * seq_rows]
    conv_acc = [jnp.broadcast_to(layer_row("conv_b"), (seq_rows, CONV_DIM))] * n_seq
    for rho in range(SUBLANES):
        anchor = compute_group(rho)
        for s in range(n_seq):
            frame_rows = seq_rows + (SUBLANES if rho else 0)
            frame = None
            for j in range(CONV_WIDTH):
                off = CONV_PAD - CONV_HIST + j
                if off % SUBLANES != rho:
                    continue
                term = (convw_ref[j:j + 1, :] + anchor) * u_ref[s, off - rho:off - rho + frame_rows, :]
                frame = term if frame is None else frame + term
            conv_acc[s] = conv_acc[s] + frame[rho:rho + seq_rows]
    compute_group(C_CG // PROJ_GROUP)
    compute_group(C_GLR // PROJ_GROUP)
    cc = conv_acc[0] if n_seq == 1 else jnp.concatenate(conv_acc, axis=0)

    for s in range(n_seq):
        nc_ref[s] = u_ref[s, seq_rows + CONV_PAD - CONV_HIST:seq_rows + CONV_PAD, :]
    if carry:
        u_ref[:, 0:CONV_PAD, :] = u_ref[:, seq_rows:seq_rows + CONV_PAD, :]

    mu = jnp.mean(cc, axis=-1, keepdims=True)
    cen = cc - mu
    var = jnp.mean(cen * cen, axis=-1, keepdims=True)
    ln = cen * lax.rsqrt(var + NORM_EPS) * layer_row("ln_gain") + layer_row("ln_bias")
    cpw = _dot(_silu(ln).astype(_BF), w_pw_ref[...]) + layer_row("b_pw")
    ycat_ref[:, ATTN_DIM + GLA_VDIM:D_MODEL] = cpw * gate_ref[:, ATTN_DIM + GLA_VDIM:D_MODEL]

    z = _dot(proj(C_GLR, LANES).astype(_BF), w_up_ref[...]) + layer_row("gla_b")
    log_a = (jnp.minimum(z, 0.0) - jnp.log(1.0 + jnp.exp(-jnp.abs(z)))) * (1.0 / GLA_TAU)
    tri = tri_ref[...]
    la_hi = log_a.astype(_BF)
    la_lo = (log_a - la_hi.astype(_F32)).astype(_BF)
    bcum = _dot(tri, la_hi) + _dot(tri, la_lo)
    gq = proj(C_GQ, GLA_KDIM) * GLA_DK ** -0.5
    gk = proj(C_GK, GLA_KDIM)
    gv = proj(C_GV, GLA_VDIM)
    gate_g = gate_ref[:, ATTN_DIM:ATTN_DIM + GLA_VDIM]
    gla_g = layer_row("gla_gain")

    for g in range(ATTN_DIM // PROJ_GROUP):
        post_q(g)(proj(C_AQ + g * PROJ_GROUP, PROJ_GROUP))
    post_kv(proj(C_AK, 2 * KV_DIM))

    head_of_lane = lane128 >> GLA_DK_SHIFT
    a_r = lax.broadcasted_iota(jnp.int32, (CHUNK, GLA_HEADS * CHUNK), 0)
    a_c = lax.broadcasted_iota(jnp.int32, (CHUNK, GLA_HEADS * CHUNK), 1)
    causal = (a_c & (CHUNK - 1)) <= a_r
    vblk = lax.broadcasted_iota(jnp.int32, (1, GLA_VDIM), 1) >> GLA_DV_SHIFT
    bd_mask_t = ((lax.broadcasted_iota(jnp.int32, (GLA_VDIM, GLA_KDIM), 0) >> GLA_DV_SHIFT)
                 == (lax.broadcasted_iota(jnp.int32, (GLA_VDIM, GLA_KDIM), 1) >> GLA_DK_SHIFT))
    q_all = (gq * jnp.exp(bcum)).astype(_BF)
    k_all = gk * jnp.exp(-bcum)
    intra, kv_upd, decay_rows = [], [], []
    for s in range(n_seq):
        for c in range(n_chunk):
            r0 = s * seq_rows + c * CHUNK
            bc = bcum[r0:r0 + CHUNK]
            b_last = bc[CHUNK - 1:CHUNK]
            k_end = (gk[r0:r0 + CHUNK] * jnp.exp(b_last - bc)).astype(_BF)
            vf = gv[r0:r0 + CHUNK]
            k_stack = jnp.concatenate(
                [jnp.where(head_of_lane == h, k_all[r0:r0 + CHUNK], 0.0) for h in range(GLA_HEADS)],
                axis=0).astype(_BF)
            v_bd = jnp.concatenate(
                [jnp.where(vblk == h, vf, 0.0) for h in range(GLA_HEADS)], axis=0).astype(_BF)
            a = jnp.where(causal, _dot_nt(q_all[r0:r0 + CHUNK], k_stack), 0.0).astype(_BF)
            intra.append(_dot(a, v_bd))
            kv_upd.append(jnp.where(bd_mask_t, _dot_tn(vf.astype(_BF), k_end), 0.0))
            decay_rows.append(jnp.exp(b_last))

    rows = ATTN_GROUP * CHUNK
    r_head1 = lax.broadcasted_iota(jnp.int32, (rows, 1), 0) >> CHUNK_SHIFT
    sink_cols = []
    for j in range(ATTN_KV_HEADS):
        sink = jnp.zeros((rows, 1), _F32)
        for r in range(ATTN_GROUP):
            sink = jnp.where(r_head1 == r, sinks_ref[layer, j * ATTN_GROUP + r], sink)
        sink_cols.append(sink * LOG2E)

    o_rows = []

    def gla_state_steps():
        for s in range(n_seq):
            state = s_ref[s]
            for c in range(n_chunk):
                i = s * n_chunk + c
                r0 = i * CHUNK
                o_rows.append(intra[i] + _dot_nt(q_all[r0:r0 + CHUNK], state.astype(_BF)))
                state = decay_rows[i] * state + kv_upd[i]
            s_ref[s] = state

    n_blocks = n_seq * n_chunk
    gla_steps_at = (3 * n_blocks) // 4 if carry else n_blocks
    for s in range(n_seq):
        for c in range(n_chunk):
            if s * n_chunk + c == gla_steps_at:
                gla_state_steps()
            r0 = s * seq_rows + c * CHUNK
            k0 = c * CHUNK
            steady = WINDOW // CHUNK
            variant = jnp.minimum(t * n_chunk + c, steady) if carry else steady
            for j in range(ATTN_KV_HEADS):
                qs = jnp.concatenate(
                    [qlo_ref[r0:r0 + CHUNK, (2 * j) * LANES:(2 * j + 1) * LANES],
                     qhi_ref[r0:r0 + CHUNK, (2 * j) * LANES:(2 * j + 1) * LANES],
                     qlo_ref[r0:r0 + CHUNK, (2 * j + 1) * LANES:(2 * j + 2) * LANES],
                     qhi_ref[r0:r0 + CHUNK, (2 * j + 1) * LANES:(2 * j + 2) * LANES]], axis=0)
                kb = kd_ref[s, k0:k0 + KBAND, j * LANES:(j + 1) * LANES]
                vb = vd_ref[s, k0:k0 + KBAND, 2 * j * LANES:(2 * j + 2) * LANES]
                qk = _dot_nt(qs, kb)
                sink = sink_cols[j]
                p_blocks, m_blocks = [], []
                for r in range(ATTN_GROUP):
                    sc = qk[r * CHUNK:(r + 1) * CHUNK] - bias_ref[variant, j, r * CHUNK:(r + 1) * CHUNK, :]
                    m_r = jnp.maximum(jnp.max(sc, axis=-1, keepdims=True), sink[r * CHUNK:(r + 1) * CHUNK])
                    p_blocks.append(jnp.exp2(sc - m_r).astype(_BF))
                    m_blocks.append(m_r)
                m = jnp.concatenate(m_blocks, axis=0)
                pv = _dot(jnp.concatenate(p_blocks, axis=0), vb)
                o = pv[:, 0:LANES] / (pv[:, LANES:2 * LANES] + jnp.exp2(sink - m))
                for pb in range(2):
                    blk = 2 * j + pb
                    ob = jnp.where(lo_half, o[(2 * pb) * CHUNK:(2 * pb + 1) * CHUNK],
                                   o[(2 * pb + 1) * CHUNK:(2 * pb + 2) * CHUNK])
                    ycat_ref[r0:r0 + CHUNK, blk * LANES:(blk + 1) * LANES] = (
                        ob * gate_ref[r0:r0 + CHUNK, blk * LANES:(blk + 1) * LANES])

    if gla_steps_at == n_blocks:
        gla_state_steps()

    if carry:
        kd_ref[:, 0:WINDOW, :] = kd_ref[:, seq_rows:seq_rows + WINDOW, :]
        vd_ref[:, 0:WINDOW, :] = vd_ref[:, seq_rows:seq_rows + WINDOW, :]

    o_all = jnp.concatenate(o_rows, axis=0)
    ycat_ref[:, ATTN_DIM:ATTN_DIM + GLA_VDIM] = _group_rms_scale(o_all, GLA_DV, ones64) * gla_g * gate_g

    for s in range(n_seq):
        sbd_ref[...] = s_ref[s].T
        for h in range(GLA_HEADS):
            ns_ref[s, h] = sbd_ref[h * GLA_DK:(h + 1) * GLA_DK, h * GLA_DV:(h + 1) * GLA_DV]

    last_layer = layer == pl.num_programs(2) - 1
    next_gain = layer_row("norm_gain", jnp.where(last_layer, 0, layer + 1))
    for r0 in range(0, tile, OUT_ROW_BLOCK):
        r1 = r0 + OUT_ROW_BLOCK
        y = x[r0:r1] + _dot(ycat_ref[r0:r1, :].astype(_BF), w_out_ref[...])
        y_ref[r0:r1, :] = y
        xcur_ref[r0:r1, :] = y
        hb_ref[r0:r1, :] = normed_bf16(jnp.where(last_layer, xnext_ref[r0:r1, :], y), next_gain)


def _run_stream(x2d, n_seqs, seq_len, hist, sinks, params, *, carry, q_pos0):
    depth = sinks.shape[0]
    if carry:
        assert q_pos0 == 0
        tile, n_seq, n_chunk = PROMPT_TILE, 1, PROMPT_TILE // CHUNK
        grid = (n_seqs, seq_len // tile, depth)
    else:
        assert seq_len == CHUNK and q_pos0 >= WINDOW
        tile, n_seq, n_chunk = SAMPLE_SEQS * CHUNK, SAMPLE_SEQS, 1
        grid = (n_seqs // n_seq, 1, depth)
    n_t = grid[1]
    seq_rows = n_chunk * CHUNK

    def const(shape):
        return pl.BlockSpec(shape, lambda g, t, l: (0,) * len(shape), pipeline_mode=pl.Buffered(1))

    def per_layer_group(shape):
        return pl.BlockSpec((depth, n_seq) + shape, lambda g, t, l: (0, g) + (0,) * len(shape))

    state_shapes = ((WINDOW, KV_DIM), (WINDOW, KV_DIM), (GLA_HEADS, GLA_DK, GLA_DV), (CONV_HIST, CONV_DIM))
    x_spec = pl.BlockSpec((tile, D_MODEL), lambda g, t, l: (g * n_t + t, 0))
    n_tiles = x2d.shape[0] // tile
    xnext_spec = pl.BlockSpec((tile, D_MODEL), lambda g, t, l: (jnp.minimum(g * n_t + t + 1, n_tiles - 1), 0))
    in_specs = [pl.BlockSpec(memory_space=pltpu.SMEM), x_spec, xnext_spec]
    args = [sinks, x2d, x2d]
    if not carry:
        in_specs += [per_layer_group(sh) for sh in state_shapes]
        args += list(hist)
    operands = tuple(params) + _shape_constants(tile)
    in_specs += [const(p.shape) for p in operands]
    in_specs[len(args)] = const(operands[0].shape[:2] + (GLR_SRC,))
    args += list(operands)

    out_shape = (jax.ShapeDtypeStruct(x2d.shape, _F32),) + tuple(
        jax.ShapeDtypeStruct((depth, n_seqs) + sh, _F32) for sh in state_shapes)
    out_specs = (x_spec,) + tuple(per_layer_group(sh) for sh in state_shapes)
    scratch = [pltpu.VMEM((depth, n_seq, WINDOW + seq_rows + CHUNK, ATTN_KV_HEADS * LANES), _BF),
               pltpu.VMEM((depth, n_seq, WINDOW + seq_rows + CHUNK, ATTN_KV_HEADS * 2 * LANES), _BF),
               pltpu.VMEM((depth, n_seq, GLA_VDIM, GLA_KDIM), _F32),
               pltpu.VMEM((depth, n_seq, CONV_PAD + seq_rows, CONV_DIM), _F32),
               pltpu.VMEM((tile, ATTN_DIM), _BF),
               pltpu.VMEM((tile, ATTN_DIM), _BF),
               pltpu.VMEM((tile, D_MODEL), _F32),
               pltpu.VMEM((tile, IN_COLS_PACKED), _F32),
               pltpu.VMEM((tile, D_MODEL), _F32),
               pltpu.VMEM((tile, D_MODEL), _BF),
               pltpu.VMEM((GLA_KDIM, GLA_VDIM), _F32),
               pltpu.VMEM((tile, D_MODEL), _F32)]
    body = functools.partial(_stream_kernel, tile=tile, n_seq=n_seq, n_chunk=n_chunk, carry=carry, q_pos0=q_pos0)
    return pl.pallas_call(
        body, grid=grid, in_specs=in_specs, out_specs=out_specs, out_shape=out_shape,
        scratch_shapes=scratch,
        compiler_params=pltpu.CompilerParams(dimension_semantics=("arbitrary", "arbitrary", "arbitrary"),
                                             vmem_limit_bytes=VMEM_LIMIT_BYTES),
        name="mixer_prompt" if carry else "mixer_sample",
    )(*args)


def _pack_params(norm_gain, w_in, q_norm_gain, k_norm_gain, gla_w_gate_up, gla_b_gate, gla_norm_gain,
                 conv_w, conv_b, conv_ln_gain, conv_ln_bias, conv_w_pw, conv_b_pw, w_out):
    w_in_a = w_in.astype(_BF)
    w_in_c = jnp.pad(w_in_a[:, :, GLR_SRC:GLR_SRC + GLA_LOWRANK], ((0, 0), (0, 0), (0, LANES - GLA_LOWRANK)))
    w_in_b = w_in_a[:, :, GLR_SRC + GLA_LOWRANK:]
    pieces = dict(norm_gain=norm_gain, q_gain=q_norm_gain, k_gain=k_norm_gain, gla_b=gla_b_gate,
                  gla_gain=gla_norm_gain, conv_b=conv_b, ln_gain=conv_ln_gain, ln_bias=conv_ln_bias, b_pw=conv_b_pw)
    depth = w_in.shape[0]
    parts = []
    for name, n in VEC_LAYOUT:
        parts += [pieces[name].astype(_F32)] * (n // pieces[name].shape[1])
        parts.append(jnp.zeros((depth, D_MODEL - n), _F32))
    parts.append(jnp.zeros((depth, (VEC_ROWS - len(VEC_LAYOUT)) * D_MODEL), _F32))
    vecs = jnp.concatenate(parts, axis=1).reshape(depth, VEC_ROWS, D_MODEL)
    w_up_p = jnp.pad(gla_w_gate_up, ((0, 0), (0, LANES - GLA_LOWRANK), (0, 0))).astype(_BF)
    convw = jnp.pad(conv_w.astype(_F32), ((0, 0), (0, CONV_PAD - CONV_WIDTH), (0, 0)))
    return w_in_a, w_in_b, w_in_c, w_up_p, conv_w_pw.astype(_BF), w_out.astype(_BF), vecs, convw


def kernel(x_prompt, x_sample, cache_k, cache_v, state_gla, state_conv, norm_gain, w_in, q_norm_gain, k_norm_gain, attn_sinks, gla_w_gate_up, gla_b_gate, gla_norm_gain, conv_w, conv_b, conv_ln_gain, conv_ln_bias, conv_w_pw, conv_b_pw, w_out):
    depth = w_in.shape[0]
    bp, lp, _ = x_prompt.shape
    bs, ls, _ = x_sample.shape
    params = _pack_params(norm_gain, w_in, q_norm_gain, k_norm_gain, gla_w_gate_up, gla_b_gate, gla_norm_gain,
                          conv_w, conv_b, conv_ln_gain, conv_ln_bias, conv_w_pw, conv_b_pw, w_out)
    sinks = jnp.pad(attn_sinks.astype(_F32), ((0, 0), (0, ATTN_HEADS)))
    yp, pk, pv, ps, pc = _run_stream(x_prompt.reshape(bp * lp, D_MODEL), bp, lp, None, sinks, params,
                                     carry=True, q_pos0=0)
    hist = (cache_k.reshape(depth, bs, WINDOW, KV_DIM), cache_v.reshape(depth, bs, WINDOW, KV_DIM),
            state_gla, state_conv)
    ys, sk, sv, ss, sc = _run_stream(x_sample.reshape(bs * ls, D_MODEL), bs, ls, hist, sinks, params,
                                     carry=False, q_pos0=PAST_LEN)
    kv_shape = (WINDOW, ATTN_KV_HEADS, HEAD_DIM)
    return (yp.reshape(bp, lp, D_MODEL), ys.reshape(bs, ls, D_MODEL),
            pk.reshape((depth, bp) + kv_shape), pv.reshape((depth, bp) + kv_shape), ps, pc,
            sk.reshape((depth, bs) + kv_shape), sv.reshape((depth, bs) + kv_shape), ss, sc)
```

```python
import functools
import math

import numpy as np
import jax
import jax.numpy as jnp
from jax import lax
from jax.experimental import pallas as pl
from jax.experimental.pallas import tpu as pltpu

D_MODEL = 1024
CHUNK = 64
ATTN_HEADS = 8
ATTN_KV_HEADS = 2
HEAD_DIM = 64
ATTN_GROUP = ATTN_HEADS // ATTN_KV_HEADS
ATTN_DIM = ATTN_HEADS * HEAD_DIM
KV_DIM = ATTN_KV_HEADS * HEAD_DIM
WINDOW = 128
BAND = WINDOW + CHUNK
KBAND = WINDOW + 2 * CHUNK
GLA_HEADS = 4
GLA_DK = 32
GLA_DV = 64
GLA_KDIM = GLA_HEADS * GLA_DK
GLA_VDIM = GLA_HEADS * GLA_DV
GLA_LOWRANK = 16
GLA_TAU = 16.0
CONV_DIM = 256
CONV_WIDTH = 31
CONV_HIST = CONV_WIDTH - 1
NORM_EPS = 1e-6
NEG_INF = -1e30
PAST_LEN = 4096
LOG2E = math.log2(math.e)

LANES = 128
SUBLANES = 8
CHUNK_SHIFT = CHUNK.bit_length() - 1
GLA_DK_SHIFT = GLA_DK.bit_length() - 1
GLA_DV_SHIFT = GLA_DV.bit_length() - 1
C_AQ, C_AK, C_AV, C_AG = 0, 512, 640, 768
C_GQ, C_GK, C_GV, C_GG = 1280, 1408, 1536, 1792
C_CV, C_CGL, C_CG, C_GLR = 2048, 2304, 2560, 2816
IN_COLS_PACKED = C_GLR + LANES
PROJ_GROUP = 256
CONV_PAD = 32
GLR_SRC = 2048
VEC_LAYOUT = (("norm_gain", D_MODEL), ("q_gain", LANES), ("k_gain", KV_DIM), ("gla_b", GLA_KDIM),
              ("gla_gain", GLA_VDIM), ("conv_b", CONV_DIM), ("ln_gain", CONV_DIM), ("ln_bias", CONV_DIM),
              ("b_pw", CONV_DIM))
VEC_ROWS = 16
VEC_ROW_OF = {name: (k, n) for k, (name, n) in enumerate(VEC_LAYOUT)}

OUT_ROW_BLOCK = 256
PROMPT_TILE = 512
SAMPLE_SEQS = 4
VMEM_LIMIT_BYTES = 56 * 1024 * 1024

_BF = jnp.bfloat16
_F32 = jnp.float32


def _shape_constants(tile):
    rows = np.arange(ATTN_GROUP * CHUNK)
    dist = np.abs((rows % CHUNK)[:, None] + WINDOW - np.arange(KBAND)[None, :]).astype(np.float64)
    bias = np.stack([
        (2.0 ** (-8.0 * (j * ATTN_GROUP + rows // CHUNK + 1) / ATTN_HEADS))[:, None] * LOG2E * dist
        for j in range(ATTN_KV_HEADS)]).astype(np.float32)
    bias[:, :, BAND:] = -NEG_INF
    variants = []
    for n in range(WINDOW // CHUNK + 1):
        b = bias.copy()
        b[:, :, :max(WINDOW - n * CHUNK, 0)] = -NEG_INF
        variants.append(b)
    bias = np.stack(variants)
    tr = np.arange(tile)
    tri = ((tr[:, None] // CHUNK == tr[None, :] // CHUNK) & (tr[None, :] <= tr[:, None])).astype(np.float32)
    ln = np.arange(LANES) // HEAD_DIM
    ones = (ln[:, None] == ln[None, :]).astype(np.float32)
    return jnp.asarray(bias), jnp.asarray(tri, dtype=_BF), jnp.asarray(ones, dtype=_BF)


def _group_rms_scale(x, group, ones_bd):
    outs = []
    for c0 in range(0, x.shape[1], LANES):
        blk = x[:, c0:c0 + LANES]
        ss = _dot((blk * blk).astype(_BF), ones_bd)
        outs.append(blk * lax.rsqrt(ss * (1.0 / group) + NORM_EPS))
    return outs[0] if len(outs) == 1 else jnp.concatenate(outs, axis=1)


def _silu(x):
    return x * jax.nn.sigmoid(x)


def _dot(a, b):
    return jnp.dot(a, b, preferred_element_type=_F32)


def _dot_nt(a, b):
    return lax.dot_general(a, b, (((1,), (1,)), ((), ())), preferred_element_type=_F32)


def _dot_tn(a, b):
    return lax.dot_general(a, b, (((0,), (0,)), ((), ())), preferred_element_type=_F32)


def _stream_kernel(*refs, tile, n_seq, n_chunk, carry, q_pos0):
    if carry:
        (sinks_ref, x_ref, xnext_ref, *rest) = refs
    else:
        (sinks_ref, x_ref, xnext_ref, hk_all, hv_all, hs_all, hc_all, *rest) = refs
    (w_in_a, w_in_b, w_in_c, w_up_all, w_pw_all, w_out_all, vec_all, convw_all, bias_ref, tri_ref, ones_ref,
     y_ref, nk_all, nv_all, ns_all, nc_all,
     kd_all, vd_all, s_all, u_all, qlo_ref, qhi_ref, ycat_ref, proj_ref, xcur_ref, hb_ref, sbd_ref,
     gate_ref) = rest
    t = pl.program_id(1)
    layer = pl.program_id(2)
    seq_rows = n_chunk * CHUNK
    w_up_ref, w_pw_ref, w_out_ref = (r.at[layer] for r in (w_up_all, w_pw_all, w_out_all))
    convw_ref = convw_all.at[layer]

    def layer_row(name, which=layer):
        r, n = VEC_ROW_OF[name]
        return vec_all[which, r:r + 1, 0:n]

    q_gain = layer_row("q_gain") * (LOG2E * HEAD_DIM ** -0.5)
    k_gain = layer_row("k_gain")
    kd_ref, vd_ref, s_ref, u_ref = (r.at[layer] for r in (kd_all, vd_all, s_all, u_all))
    nk_ref, nv_ref, ns_ref, nc_ref = (r.at[layer] for r in (nk_all, nv_all, ns_all, nc_all))
    if not carry:
        hk_ref, hv_ref, hs_ref, hc_ref = (r.at[layer] for r in (hk_all, hv_all, hs_all, hc_all))

    lane128 = lax.broadcasted_iota(jnp.int32, (1, LANES), 1)
    lo_half = lane128 < HEAD_DIM

    def dup_halves(a):
        sw = pltpu.roll(a, HEAD_DIM, 1)
        return jnp.concatenate([jnp.where(lo_half, a, sw), jnp.where(lo_half, sw, a)], axis=1)

    buf_rows = WINDOW + seq_rows + CHUNK
    zeros_blk = jnp.zeros((buf_rows, LANES), _BF)
    ones_blk = jnp.ones((buf_rows, LANES), _BF)
    v_blank = jnp.concatenate([zeros_blk, ones_blk] * ATTN_KV_HEADS, axis=1)

    def store_v(s, row0, vdup_rows):
        for j in range(ATTN_KV_HEADS):
            vd_ref[s, row0:row0 + vdup_rows.shape[0], 2 * j * LANES:(2 * j + 1) * LANES] = (
                vdup_rows[:, j * LANES:(j + 1) * LANES])

    if carry:
        @pl.when(t == 0)
        def _():
            kd_ref[...] = jnp.zeros(kd_ref.shape, _BF)
            for s in range(n_seq):
                vd_ref[s] = v_blank
            s_ref[...] = jnp.zeros(s_ref.shape, _F32)
            u_ref[:, 0:CONV_PAD, :] = jnp.zeros((n_seq, CONV_PAD, CONV_DIM), _F32)
    else:
        for s in range(n_seq):
            kd_ref[s, 0:WINDOW, :] = dup_halves(hk_ref[s]).astype(_BF)
            kd_ref[s, WINDOW + seq_rows:, :] = jnp.zeros((CHUNK, 2 * LANES), _BF)
            vd_ref[s] = v_blank
            store_v(s, 0, dup_halves(hv_ref[s]).astype(_BF))
            sbd_ref[...] = jnp.zeros((GLA_KDIM, GLA_VDIM), _F32)
            for h in range(GLA_HEADS):
                sbd_ref[h * GLA_DK:(h + 1) * GLA_DK, h * GLA_DV:(h + 1) * GLA_DV] = hs_ref[s, h]
            s_ref[s] = sbd_ref[...].T
            u_ref[s, 0:SUBLANES, :] = jnp.zeros((SUBLANES, CONV_DIM), _F32)
            u_ref[s, CONV_PAD - CONV_HIST:CONV_PAD, :] = hc_ref[s]

    def normed_bf16(v, gain_row):
        ms = jnp.mean(v * v, axis=-1, keepdims=True)
        return (v * lax.rsqrt(ms + NORM_EPS) * gain_row).astype(_BF)

    @pl.when(layer == 0)
    def _():
        xcur_ref[...] = x_ref[...]

    @pl.when((pl.program_id(0) == 0) & (t == 0) & (layer == 0))
    def _():
        hb_ref[...] = normed_bf16(x_ref[...], layer_row("norm_gain"))
    x = xcur_ref[...]
    hb = hb_ref[...]

    anchor_on = sinks_ref[layer, ATTN_HEADS] != 0.0

    ones64 = ones_ref[...]

    def post_q(g):
        def post(res):
            for b in range(PROJ_GROUP // LANES):
                c0 = g * PROJ_GROUP + b * LANES
                qn = _group_rms_scale(res[:, b * LANES:(b + 1) * LANES], HEAD_DIM, ones64) * q_gain
                qlo_ref[:, c0:c0 + LANES] = jnp.where(lo_half, qn, 0.0).astype(_BF)
                qhi_ref[:, c0:c0 + LANES] = jnp.where(lo_half, 0.0, qn).astype(_BF)
        return post

    def post_kv(res):
        kn = _group_rms_scale(res[:, 0:KV_DIM], HEAD_DIM, ones64) * k_gain
        vv = res[:, KV_DIM:2 * KV_DIM]
        kdup = dup_halves(kn).astype(_BF)
        vdup = dup_halves(vv).astype(_BF)
        for s in range(n_seq):
            kd_ref[s, WINDOW:WINDOW + seq_rows, :] = kdup[s * seq_rows:(s + 1) * seq_rows]
            store_v(s, WINDOW, vdup[s * seq_rows:(s + 1) * seq_rows])
        if carry:
            nk_ref[0] = kn[tile - WINDOW:tile]
            nv_ref[0] = vv[tile - WINDOW:tile]
        else:
            for s in range(n_seq):
                nk_ref[s, 0:WINDOW - seq_rows, :] = hk_ref[s, seq_rows:WINDOW, :]
                nv_ref[s, 0:WINDOW - seq_rows, :] = hv_ref[s, seq_rows:WINDOW, :]
                nk_ref[s, WINDOW - seq_rows:WINDOW, :] = kn[s * seq_rows:(s + 1) * seq_rows]
                nv_ref[s, WINDOW - seq_rows:WINDOW, :] = vv[s * seq_rows:(s + 1) * seq_rows]

    def post_gate(col0):
        def post(res):
            gate_ref[:, col0:col0 + PROJ_GROUP] = _silu(res)
        return post

    group_post = {C_AG // PROJ_GROUP: post_gate(0), C_AG // PROJ_GROUP + 1: post_gate(PROJ_GROUP),
                  C_GG // PROJ_GROUP: post_gate(ATTN_DIM), C_CG // PROJ_GROUP: post_gate(ATTN_DIM + GLA_VDIM)}

    def w_in_cols(c0, c1):
        if c1 <= C_CV:
            return w_in_a[layer, :, c0:c1]
        if c1 <= C_GLR:
            return w_in_b[layer, :, c0 - C_CV:c1 - C_CV]
        return w_in_c[layer, :, c0 - C_GLR:c1 - C_GLR]

    def compute_group(g):
        c0, c1 = g * PROJ_GROUP, min((g + 1) * PROJ_GROUP, IN_COLS_PACKED)
        res = _dot(hb, w_in_cols(c0, c1))
        if g in group_post:
            group_post[g](res)
        else:
            proj_ref[:, c0:c1] = res
        return jnp.where(anchor_on, res[0:1, 0:CONV_DIM], 0.0)

    def proj(c0, width):
        return proj_ref[:, c0:c0 + width]

    compute_group(C_CV // PROJ_GROUP)
    prev_anchor = compute_group(C_CGL // PROJ_GROUP)
    u = proj(C_CV, CONV_DIM) * jax.nn.sigmoid(proj(C_CGL, CONV_DIM))
    for s in range(n_seq):
        u_ref[s, CONV_PAD:CONV_PAD + seq_rows, :] = u[s * seq_rows:(s + 1) * seq_rows]
    conv_acc = [jnp.broadcast_to(layer_row("conv_b"), (seq_rows, CONV_DIM))] * n_seq
    for rho in range(SUBLANES):
        anchor, prev_anchor = prev_anchor, compute_group(rho)
        for s in range(n_seq):
            frame_rows = seq_rows + (SUBLANES if rho else 0)
            frame = None
            for j in range(CONV_WIDTH):
                off = CONV_PAD - CONV_HIST + j
                if off % SUBLANES != rho:
                    continue
                term = (convw_ref[j:j + 1, :] + anchor) * u_ref[s, off - rho:off - rho + frame_rows, :]
                frame = term if frame is None else frame + term
            conv_acc[s] = conv_acc[s] + frame[rho:rho + seq_rows]
    compute_group(C_CG // PROJ_GROUP)
    compute_group(C_GLR // PROJ_GROUP)
    cc = conv_acc[0] if n_seq == 1 else jnp.concatenate(conv_acc, axis=0)

    for s in range(n_seq):
        nc_ref[s] = u_ref[s, seq_rows + CONV_PAD - CONV_HIST:seq_rows + CONV_PAD, :]
    if carry:
        u_ref[:, 0:CONV_PAD, :] = u_ref[:, seq_rows:seq_rows + CONV_PAD, :]

    mu = jnp.mean(cc, axis=-1, keepdims=True)
    cen = cc - mu
    var = jnp.mean(cen * cen, axis=-1, keepdims=True)
    ln = cen * lax.rsqrt(var + NORM_EPS) * layer_row("ln_gain") + layer_row("ln_bias")
    cpw = _dot(_silu(ln).astype(_BF), w_pw_ref[...]) + layer_row("b_pw")
    ycat_ref[:, ATTN_DIM + GLA_VDIM:D_MODEL] = cpw * gate_ref[:, ATTN_DIM + GLA_VDIM:D_MODEL]

    z = _dot(proj(C_GLR, LANES).astype(_BF), w_up_ref[...]) + layer_row("gla_b")
    log_a = (jnp.minimum(z, 0.0) - jnp.log(1.0 + jnp.exp(-jnp.abs(z)))) * (1.0 / GLA_TAU)
    tri = tri_ref[...]
    la_hi = log_a.astype(_BF)
    la_lo = (log_a - la_hi.astype(_F32)).astype(_BF)
    bcum = _dot(tri, la_hi) + _dot(tri, la_lo)
    gq = proj(C_GQ, GLA_KDIM) * GLA_DK ** -0.5
    gk = proj(C_GK, GLA_KDIM)
    gv = proj(C_GV, GLA_VDIM)
    gate_g = gate_ref[:, ATTN_DIM:ATTN_DIM + GLA_VDIM]
    gla_g = layer_row("gla_gain")

    for g in range(ATTN_DIM // PROJ_GROUP):
        post_q(g)(proj(C_AQ + g * PROJ_GROUP, PROJ_GROUP))
    post_kv(proj(C_AK, 2 * KV_DIM))

    head_of_lane = lane128 >> GLA_DK_SHIFT
    a_r = lax.broadcasted_iota(jnp.int32, (CHUNK, GLA_HEADS * CHUNK), 0)
    a_c = lax.broadcasted_iota(jnp.int32, (CHUNK, GLA_HEADS * CHUNK), 1)
    causal = (a_c & (CHUNK - 1)) <= a_r
    vblk = lax.broadcasted_iota(jnp.int32, (1, GLA_VDIM), 1) >> GLA_DV_SHIFT
    bd_mask_t = ((lax.broadcasted_iota(jnp.int32, (GLA_VDIM, GLA_KDIM), 0) >> GLA_DV_SHIFT)
                 == (lax.broadcasted_iota(jnp.int32, (GLA_VDIM, GLA_KDIM), 1) >> GLA_DK_SHIFT))
    q_all = (gq * jnp.exp(bcum)).astype(_BF)
    k_all = gk * jnp.exp(-bcum)
    intra, kv_upd, decay_rows = [], [], []
    for s in range(n_seq):
        for c in range(n_chunk):
            r0 = s * seq_rows + c * CHUNK
            bc = bcum[r0:r0 + CHUNK]
            b_last = bc[CHUNK - 1:CHUNK]
            k_end = (gk[r0:r0 + CHUNK] * jnp.exp(b_last - bc)).astype(_BF)
            vf = gv[r0:r0 + CHUNK]
            k_stack = jnp.concatenate(
                [jnp.where(head_of_lane == h, k_all[r0:r0 + CHUNK], 0.0) for h in range(GLA_HEADS)],
                axis=0).astype(_BF)
            v_bd = jnp.concatenate(
                [jnp.where(vblk == h, vf, 0.0) for h in range(GLA_HEADS)], axis=0).astype(_BF)
            a = jnp.where(causal, _dot_nt(q_all[r0:r0 + CHUNK], k_stack), 0.0).astype(_BF)
            intra.append(_dot(a, v_bd))
            kv_upd.append(jnp.where(bd_mask_t, _dot_tn(vf.astype(_BF), k_end), 0.0))
            decay_rows.append(jnp.exp(b_last))

    rows = ATTN_GROUP * CHUNK
    r_head1 = lax.broadcasted_iota(jnp.int32, (rows, 1), 0) >> CHUNK_SHIFT
    sink_cols = []
    for j in range(ATTN_KV_HEADS):
        sink = jnp.zeros((rows, 1), _F32)
        for r in range(ATTN_GROUP):
            sink = jnp.where(r_head1 == r, sinks_ref[layer, j * ATTN_GROUP + r], sink)
        sink_cols.append(sink * LOG2E)

    o_rows = []

    def gla_state_steps():
        for s in range(n_seq):
            state = s_ref[s]
            for c in range(n_chunk):
                i = s * n_chunk + c
                r0 = i * CHUNK
                o_rows.append(intra[i] + _dot_nt(q_all[r0:r0 + CHUNK], state.astype(_BF)))
                state = decay_rows[i] * state + kv_upd[i]
            s_ref[s] = state

    n_blocks = n_seq * n_chunk
    gla_steps_at = (3 * n_blocks) // 4 if carry else n_blocks
    for s in range(n_seq):
        for c in range(n_chunk):
            if s * n_chunk + c == gla_steps_at:
                gla_state_steps()
            r0 = s * seq_rows + c * CHUNK
            k0 = c * CHUNK
            steady = WINDOW // CHUNK
            variant = jnp.minimum(t * n_chunk + c, steady) if carry else steady
            for j in range(ATTN_KV_HEADS):
                qs = jnp.concatenate(
                    [qlo_ref[r0:r0 + CHUNK, (2 * j) * LANES:(2 * j + 1) * LANES],
                     qhi_ref[r0:r0 + CHUNK, (2 * j) * LANES:(2 * j + 1) * LANES],
                     qlo_ref[r0:r0 + CHUNK, (2 * j + 1) * LANES:(2 * j + 2) * LANES],
                     qhi_ref[r0:r0 + CHUNK, (2 * j + 1) * LANES:(2 * j + 2) * LANES]], axis=0)
                kb = kd_ref[s, k0:k0 + KBAND, j * LANES:(j + 1) * LANES]
                vb = vd_ref[s, k0:k0 + KBAND, 2 * j * LANES:(2 * j + 2) * LANES]
                qk = _dot_nt(qs, kb)
                sink = sink_cols[j]
                p_blocks, m_blocks = [], []
                for r in range(ATTN_GROUP):
                    sc = qk[r * CHUNK:(r + 1) * CHUNK] - bias_ref[variant, j, r * CHUNK:(r + 1) * CHUNK, :]
                    m_r = jnp.maximum(jnp.max(sc, axis=-1, keepdims=True), sink[r * CHUNK:(r + 1) * CHUNK])
                    p_blocks.append(jnp.exp2(sc - m_r).astype(_BF))
                    m_blocks.append(m_r)
                m = jnp.concatenate(m_blocks, axis=0)
                pv = _dot(jnp.concatenate(p_blocks, axis=0), vb)
                o = pv[:, 0:LANES] / (pv[:, LANES:2 * LANES] + jnp.exp2(sink - m))
                for pb in range(2):
                    blk = 2 * j + pb
                    ob = jnp.where(lo_half, o[(2 * pb) * CHUNK:(2 * pb + 1) * CHUNK],
                                   o[(2 * pb + 1) * CHUNK:(2 * pb + 2) * CHUNK])
                    ycat_ref[r0:r0 + CHUNK, blk * LANES:(blk + 1) * LANES] = (
                        ob * gate_ref[r0:r0 + CHUNK, blk * LANES:(blk + 1) * LANES])

    if gla_steps_at == n_blocks:
        gla_state_steps()

    if carry:
        kd_ref[:, 0:WINDOW, :] = kd_ref[:, seq_rows:seq_rows + WINDOW, :]
        vd_ref[:, 0:WINDOW, :] = vd_ref[:, seq_rows:seq_rows + WINDOW, :]

    o_all = jnp.concatenate(o_rows, axis=0)
    ycat_ref[:, ATTN_DIM:ATTN_DIM + GLA_VDIM] = _group_rms_scale(o_all, GLA_DV, ones64) * gla_g * gate_g

    for s in range(n_seq):
        sbd_ref[...] = s_ref[s].T
        for h in range(GLA_HEADS):
            ns_ref[s, h] = sbd_ref[h * GLA_DK:(h + 1) * GLA_DK, h * GLA_DV:(h + 1) * GLA_DV]

    last_layer = layer == pl.num_programs(2) - 1
    next_gain = layer_row("norm_gain", jnp.where(last_layer, 0, layer + 1))
    for r0 in range(0, tile, OUT_ROW_BLOCK):
        r1 = r0 + OUT_ROW_BLOCK
        y = x[r0:r1] + _dot(ycat_ref[r0:r1, :].astype(_BF), w_out_ref[...])
        y_ref[r0:r1, :] = y
        xcur_ref[r0:r1, :] = y
        hb_ref[r0:r1, :] = normed_bf16(jnp.where(last_layer, xnext_ref[r0:r1, :], y), next_gain)


def _run_stream(x2d, n_seqs, seq_len, hist, sinks, params, *, carry, q_pos0):
    depth = sinks.shape[0]
    if carry:
        assert q_pos0 == 0
        tile, n_seq, n_chunk = PROMPT_TILE, 1, PROMPT_TILE // CHUNK
        grid = (n_seqs, seq_len // tile, depth)
    else:
        assert seq_len == CHUNK and q_pos0 >= WINDOW
        tile, n_seq, n_chunk = SAMPLE_SEQS * CHUNK, SAMPLE_SEQS, 1
        grid = (n_seqs // n_seq, 1, depth)
    n_t = grid[1]
    seq_rows = n_chunk * CHUNK

    def const(shape):
        return pl.BlockSpec(shape, lambda g, t, l: (0,) * len(shape), pipeline_mode=pl.Buffered(1))

    def per_layer_group(shape):
        return pl.BlockSpec((depth, n_seq) + shape, lambda g, t, l: (0, g) + (0,) * len(shape))

    state_shapes = ((WINDOW, KV_DIM), (WINDOW, KV_DIM), (GLA_HEADS, GLA_DK, GLA_DV), (CONV_HIST, CONV_DIM))
    x_spec = pl.BlockSpec((tile, D_MODEL), lambda g, t, l: (g * n_t + t, 0))
    n_tiles = x2d.shape[0] // tile
    xnext_spec = pl.BlockSpec((tile, D_MODEL), lambda g, t, l: (jnp.minimum(g * n_t + t + 1, n_tiles - 1), 0))
    in_specs = [pl.BlockSpec(memory_space=pltpu.SMEM), x_spec, xnext_spec]
    args = [sinks, x2d, x2d]
    if not carry:
        in_specs += [per_layer_group(sh) for sh in state_shapes]
        args += list(hist)
    operands = tuple(params) + _shape_constants(tile)
    in_specs += [const(p.shape) for p in operands]
    in_specs[len(args)] = const(operands[0].shape[:2] + (GLR_SRC,))
    args += list(operands)

    out_shape = (jax.ShapeDtypeStruct(x2d.shape, _F32),) + tuple(
        jax.ShapeDtypeStruct((depth, n_seqs) + sh, _F32) for sh in state_shapes)
    out_specs = (x_spec,) + tuple(per_layer_group(sh) for sh in state_shapes)
    scratch = [pltpu.VMEM((depth, n_seq, WINDOW + seq_rows + CHUNK, ATTN_KV_HEADS * LANES), _BF),
               pltpu.VMEM((depth, n_seq, WINDOW + seq_rows + CHUNK, ATTN_KV_HEADS * 2 * LANES), _BF),
               pltpu.VMEM((depth, n_seq, GLA_VDIM, GLA_KDIM), _F32),
               pltpu.VMEM((depth, n_seq, CONV_PAD + seq_rows, CONV_DIM), _F32),
               pltpu.VMEM((tile, ATTN_DIM), _BF),
               pltpu.VMEM((tile, ATTN_DIM), _BF),
               pltpu.VMEM((tile, D_MODEL), _F32),
               pltpu.VMEM((tile, IN_COLS_PACKED), _F32),
               pltpu.VMEM((tile, D_MODEL), _F32),
               pltpu.VMEM((tile, D_MODEL), _BF),
               pltpu.VMEM((GLA_KDIM, GLA_VDIM), _F32),
               pltpu.VMEM((tile, D_MODEL), _F32)]
    body = functools.partial(_stream_kernel, tile=tile, n_seq=n_seq, n_chunk=n_chunk, carry=carry, q_pos0=q_pos0)
    return pl.pallas_call(
        body, grid=grid, in_specs=in_specs, out_specs=out_specs, out_shape=out_shape,
        scratch_shapes=scratch,
        compiler_params=pltpu.CompilerParams(dimension_semantics=("arbitrary", "arbitrary", "arbitrary"),
                                             vmem_limit_bytes=VMEM_LIMIT_BYTES),
        name="mixer_prompt" if carry else "mixer_sample",
    )(*args)


def _pack_params(norm_gain, w_in, q_norm_gain, k_norm_gain, gla_w_gate_up, gla_b_gate, gla_norm_gain,
                 conv_w, conv_b, conv_ln_gain, conv_ln_bias, conv_w_pw, conv_b_pw, w_out):
    w_in_a = w_in.astype(_BF)
    w_in_c = jnp.pad(w_in_a[:, :, GLR_SRC:GLR_SRC + GLA_LOWRANK], ((0, 0), (0, 0), (0, LANES - GLA_LOWRANK)))
    w_in_b = w_in_a[:, :, GLR_SRC + GLA_LOWRANK:]
    pieces = dict(norm_gain=norm_gain, q_gain=q_norm_gain, k_gain=k_norm_gain, gla_b=gla_b_gate,
                  gla_gain=gla_norm_gain, conv_b=conv_b, ln_gain=conv_ln_gain, ln_bias=conv_ln_bias, b_pw=conv_b_pw)
    depth = w_in.shape[0]
    parts = []
    for name, n in VEC_LAYOUT:
        parts += [pieces[name].astype(_F32)] * (n // pieces[name].shape[1])
        parts.append(jnp.zeros((depth, D_MODEL - n), _F32))
    parts.append(jnp.zeros((depth, (VEC_ROWS - len(VEC_LAYOUT)) * D_MODEL), _F32))
    vecs = jnp.concatenate(parts, axis=1).reshape(depth, VEC_ROWS, D_MODEL)
    w_up_p = jnp.pad(gla_w_gate_up, ((0, 0), (0, LANES - GLA_LOWRANK), (0, 0))).astype(_BF)
    convw = jnp.pad(conv_w.astype(_F32), ((0, 0), (0, CONV_PAD - CONV_WIDTH), (0, 0)))
    return w_in_a, w_in_b, w_in_c, w_up_p, conv_w_pw.astype(_BF), w_out.astype(_BF), vecs, convw


def kernel(x_prompt, x_sample, cache_k, cache_v, state_gla, state_conv, norm_gain, w_in, q_norm_gain, k_norm_gain, attn_sinks, gla_w_gate_up, gla_b_gate, gla_norm_gain, conv_w, conv_b, conv_ln_gain, conv_ln_bias, conv_w_pw, conv_b_pw, w_out):
    depth = w_in.shape[0]
    bp, lp, _ = x_prompt.shape
    bs, ls, _ = x_sample.shape
    params = _pack_params(norm_gain, w_in, q_norm_gain, k_norm_gain, gla_w_gate_up, gla_b_gate, gla_norm_gain,
                          conv_w, conv_b, conv_ln_gain, conv_ln_bias, conv_w_pw, conv_b_pw, w_out)
    sinks = jnp.pad(attn_sinks.astype(_F32), ((0, 0), (0, ATTN_HEADS)))
    yp, pk, pv, ps, pc = _run_stream(x_prompt.reshape(bp * lp, D_MODEL), bp, lp, None, sinks, params,
                                     carry=True, q_pos0=0)
    hist = (cache_k.reshape(depth, bs, WINDOW, KV_DIM), cache_v.reshape(depth, bs, WINDOW, KV_DIM),
            state_gla, state_conv)
    ys, sk, sv, ss, sc = _run_stream(x_sample.reshape(bs * ls, D_MODEL), bs, ls, hist, sinks, params,
                                     carry=False, q_pos0=PAST_LEN)
    kv_shape = (WINDOW, ATTN_KV_HEADS, HEAD_DIM)
    return (yp.reshape(bp, lp, D_MODEL), ys.reshape(bs, ls, D_MODEL),
            pk.reshape((depth, bp) + kv_shape), pv.reshape((depth, bp) + kv_shape), ps, pc,
            sk.reshape((depth, bs) + kv_shape), sv.reshape((depth, bs) + kv_shape), ss, sc)
```

```python
import functools
import math

import numpy as np
import jax
import jax.numpy as jnp
from jax import lax
from jax.experimental import pallas as pl
from jax.experimental.pallas import tpu as pltpu

D_MODEL = 1024
CHUNK = 64
ATTN_HEADS = 8
ATTN_KV_HEADS = 2
HEAD_DIM = 64
ATTN_GROUP = ATTN_HEADS // ATTN_KV_HEADS
ATTN_DIM = ATTN_HEADS * HEAD_DIM
KV_DIM = ATTN_KV_HEADS * HEAD_DIM
WINDOW = 128
BAND = WINDOW + CHUNK
KBAND = WINDOW + 2 * CHUNK
GLA_HEADS = 4
GLA_DK = 32
GLA_DV = 64
GLA_KDIM = GLA_HEADS * GLA_DK
GLA_VDIM = GLA_HEADS * GLA_DV
GLA_LOWRANK = 16
GLA_TAU = 16.0
CONV_DIM = 256
CONV_WIDTH = 31
CONV_HIST = CONV_WIDTH - 1
NORM_EPS = 1e-6
NEG_INF = -1e30
PAST_LEN = 4096
LOG2E = math.log2(math.e)

LANES = 128
SUBLANES = 8
CHUNK_SHIFT = CHUNK.bit_length() - 1
GLA_DK_SHIFT = GLA_DK.bit_length() - 1
GLA_DV_SHIFT = GLA_DV.bit_length() - 1
C_AQ, C_AK, C_AV, C_AG = 0, 512, 640, 768
C_GQ, C_GK, C_GV, C_GG = 1280, 1408, 1536, 1792
C_CV, C_CGL, C_CG, C_GLR = 2048, 2304, 2560, 2816
IN_COLS_PACKED = C_GLR + LANES
PROJ_GROUP = 256
CONV_PAD = 32
GLR_SRC = 2048
VEC_LAYOUT = (("norm_gain", D_MODEL), ("q_gain", LANES), ("k_gain", KV_DIM), ("gla_b", GLA_KDIM),
              ("gla_gain", GLA_VDIM), ("conv_b", CONV_DIM), ("ln_gain", CONV_DIM), ("ln_bias", CONV_DIM),
              ("b_pw", CONV_DIM))
VEC_ROWS = 16
VEC_ROW_OF = {name: (k, n) for k, (name, n) in enumerate(VEC_LAYOUT)}

OUT_ROW_BLOCK = 256
PROMPT_TILE = 512
SAMPLE_SEQS = 4
VMEM_LIMIT_BYTES = 56 * 1024 * 1024

_BF = jnp.bfloat16
_F32 = jnp.float32


def _shape_constants(tile):
    rows = np.arange(ATTN_GROUP * CHUNK)
    dist = np.abs((rows % CHUNK)[:, None] + WINDOW - np.arange(KBAND)[None, :]).astype(np.float64)
    bias = np.stack([
        (2.0 ** (-8.0 * (j * ATTN_GROUP + rows // CHUNK + 1) / ATTN_HEADS))[:, None] * LOG2E * dist
        for j in range(ATTN_KV_HEADS)]).astype(np.float32)
    bias[:, :, BAND:] = -NEG_INF
    variants = []
    for n in range(WINDOW // CHUNK + 1):
        b = bias.copy()
        b[:, :, :max(WINDOW - n * CHUNK, 0)] = -NEG_INF
        variants.append(b)
    bias = np.stack(variants)
    tr = np.arange(tile)
    tri = ((tr[:, None] // CHUNK == tr[None, :] // CHUNK) & (tr[None, :] <= tr[:, None])).astype(np.float32)
    ln = np.arange(LANES) // HEAD_DIM
    ones = (ln[:, None] == ln[None, :]).astype(np.float32)
    return jnp.asarray(bias), jnp.asarray(tri, dtype=_BF), jnp.asarray(ones, dtype=_BF)


def _group_rms_scale(x, group, ones_bd):
    outs = []
    for c0 in range(0, x.shape[1], LANES):
        blk = x[:, c0:c0 + LANES]
        ss = _dot((blk * blk).astype(_BF), ones_bd)
        outs.append(blk * lax.rsqrt(ss * (1.0 / group) + NORM_EPS))
    return outs[0] if len(outs) == 1 else jnp.concatenate(outs, axis=1)


def _silu(x):
    return x * jax.nn.sigmoid(x)


def _dot(a, b):
    return jnp.dot(a, b, preferred_element_type=_F32)


def _dot_nt(a, b):
    return lax.dot_general(a, b, (((1,), (1,)), ((), ())), preferred_element_type=_F32)


def _dot_tn(a, b):
    return lax.dot_general(a, b, (((0,), (0,)), ((), ())), preferred_element_type=_F32)


def _stream_kernel(*refs, tile, n_seq, n_chunk, carry, q_pos0):
    if carry:
        (sinks_ref, x_ref, xnext_ref, *rest) = refs
    else:
        (sinks_ref, x_ref, xnext_ref, hk_all, hv_all, hs_all, hc_all, *rest) = refs
    (w_in_a, w_in_b, w_in_c, w_up_all, w_pw_all, w_out_all, vec_all, convw_all, bias_ref, tri_ref, ones_ref,
     y_ref, nk_all, nv_all, ns_all, nc_all,
     kd_all, vd_all, s_all, u_all, qlo_ref, qhi_ref, ycat_ref, proj_ref, xcur_ref, hb_ref, sbd_ref,
     gate_ref) = rest
    t = pl.program_id(1)
    layer = pl.program_id(2)
    seq_rows = n_chunk * CHUNK
    w_up_ref, w_pw_ref, w_out_ref = (r.at[layer] for r in (w_up_all, w_pw_all, w_out_all))
    convw_ref = convw_all.at[layer]

    def layer_row(name, which=layer):
        r, n = VEC_ROW_OF[name]
        return vec_all[which, r:r + 1, 0:n]

    q_gain = layer_row("q_gain") * (LOG2E * HEAD_DIM ** -0.5)
    k_gain = layer_row("k_gain")
    kd_ref, vd_ref, s_ref, u_ref = (r.at[layer] for r in (kd_all, vd_all, s_all, u_all))
    nk_ref, nv_ref, ns_ref, nc_ref = (r.at[layer] for r in (nk_all, nv_all, ns_all, nc_all))
    if not carry:
        hk_ref, hv_ref, hs_ref, hc_ref = (r.at[layer] for r in (hk_all, hv_all, hs_all, hc_all))

    lane128 = lax.broadcasted_iota(jnp.int32, (1, LANES), 1)
    lo_half = lane128 < HEAD_DIM

    def dup_halves(a):
        sw = pltpu.roll(a, HEAD_DIM, 1)
        return jnp.concatenate([jnp.where(lo_half, a, sw), jnp.where(lo_half, sw, a)], axis=1)

    buf_rows = WINDOW + seq_rows + CHUNK
    zeros_blk = jnp.zeros((buf_rows, LANES), _BF)
    ones_blk = jnp.ones((buf_rows, LANES), _BF)
    v_blank = jnp.concatenate([zeros_blk, ones_blk] * ATTN_KV_HEADS, axis=1)

    def store_v(s, row0, vdup_rows):
        for j in range(ATTN_KV_HEADS):
            vd_ref[s, row0:row0 + vdup_rows.shape[0], 2 * j * LANES:(2 * j + 1) * LANES] = (
                vdup_rows[:, j * LANES:(j + 1) * LANES])

    if carry:
        @pl.when(t == 0)
        def _():
            kd_ref[...] = jnp.zeros(kd_ref.shape, _BF)
            for s in range(n_seq):
                vd_ref[s] = v_blank
            s_ref[...] = jnp.zeros(s_ref.shape, _F32)
            u_ref[:, 0:CONV_PAD, :] = jnp.zeros((n_seq, CONV_PAD, CONV_DIM), _F32)
    else:
        for s in range(n_seq):
            kd_ref[s, 0:WINDOW, :] = dup_halves(hk_ref[s]).astype(_BF)
            kd_ref[s, WINDOW + seq_rows:, :] = jnp.zeros((CHUNK, 2 * LANES), _BF)
            vd_ref[s] = v_blank
            store_v(s, 0, dup_halves(hv_ref[s]).astype(_BF))
            sbd_ref[...] = jnp.zeros((GLA_KDIM, GLA_VDIM), _F32)
            for h in range(GLA_HEADS):
                sbd_ref[h * GLA_DK:(h + 1) * GLA_DK, h * GLA_DV:(h + 1) * GLA_DV] = hs_ref[s, h]
            s_ref[s] = sbd_ref[...].T
            u_ref[s, 0:SUBLANES, :] = jnp.zeros((SUBLANES, CONV_DIM), _F32)
            u_ref[s, CONV_PAD - CONV_HIST:CONV_PAD, :] = hc_ref[s]

    def normed_bf16(v, gain_row):
        ms = jnp.mean(v * v, axis=-1, keepdims=True)
        return (v * lax.rsqrt(ms + NORM_EPS) * gain_row).astype(_BF)

    @pl.when(layer == 0)
    def _():
        xcur_ref[...] = x_ref[...]

    @pl.when((pl.program_id(0) == 0) & (t == 0) & (layer == 0))
    def _():
        hb_ref[...] = normed_bf16(x_ref[...], layer_row("norm_gain"))
    x = xcur_ref[...]
    hb = hb_ref[...]

    anchor_on = sinks_ref[layer, ATTN_HEADS] != 0.0

    ones64 = ones_ref[...]

    def post_q(g):
        def post(res):
            for b in range(PROJ_GROUP // LANES):
                c0 = g * PROJ_GROUP + b * LANES
                qn = _group_rms_scale(res[:, b * LANES:(b + 1) * LANES], HEAD_DIM, ones64) * q_gain
                qlo_ref[:, c0:c0 + LANES] = jnp.where(lo_half, qn, 0.0).astype(_BF)
                qhi_ref[:, c0:c0 + LANES] = jnp.where(lo_half, 0.0, qn).astype(_BF)
        return post

    def post_kv(res):
        kn = _group_rms_scale(res[:, 0:KV_DIM], HEAD_DIM, ones64) * k_gain
        vv = res[:, KV_DIM:2 * KV_DIM]
        kdup = dup_halves(kn).astype(_BF)
        vdup = dup_halves(vv).astype(_BF)
        for s in range(n_seq):
            kd_ref[s, WINDOW:WINDOW + seq_rows, :] = kdup[s * seq_rows:(s + 1) * seq_rows]
            store_v(s, WINDOW, vdup[s * seq_rows:(s + 1) * seq_rows])
        if carry:
            nk_ref[0] = kn[tile - WINDOW:tile]
            nv_ref[0] = vv[tile - WINDOW:tile]
        else:
            for s in range(n_seq):
                nk_ref[s, 0:WINDOW - seq_rows, :] = hk_ref[s, seq_rows:WINDOW, :]
                nv_ref[s, 0:WINDOW - seq_rows, :] = hv_ref[s, seq_rows:WINDOW, :]
                nk_ref[s, WINDOW - seq_rows:WINDOW, :] = kn[s * seq_rows:(s + 1) * seq_rows]
                nv_ref[s, WINDOW - seq_rows:WINDOW, :] = vv[s * seq_rows:(s + 1) * seq_rows]

    def post_gate(col0):
        def post(res):
            gate_ref[:, col0:col0 + PROJ_GROUP] = _silu(res)
        return post

    group_post = {C_AG // PROJ_GROUP: post_gate(0), C_AG // PROJ_GROUP + 1: post_gate(PROJ_GROUP),
                  C_GG // PROJ_GROUP: post_gate(ATTN_DIM), C_CG // PROJ_GROUP: post_gate(ATTN_DIM + GLA_VDIM)}

    def w_in_cols(c0, c1):
        if c1 <= C_CV:
            return w_in_a[layer, :, c0:c1]
        if c1 <= C_GLR:
            return w_in_b[layer, :, c0 - C_CV:c1 - C_CV]
        return w_in_c[layer, :, c0 - C_GLR:c1 - C_GLR]

    def compute_group(g):
        c0, c1 = g * PROJ_GROUP, min((g + 1) * PROJ_GROUP, IN_COLS_PACKED)
        res = _dot(hb, w_in_cols(c0, c1))
        if g in group_post:
            group_post[g](res)
        else:
            proj_ref[:, c0:c1] = res
        return jnp.where(anchor_on, res[tile - 1:tile, 0:CONV_DIM], 0.0)

    def proj(c0, width):
        return proj_ref[:, c0:c0 + width]

    compute_group(C_CV // PROJ_GROUP)
    prev_anchor = compute_group(C_CGL // PROJ_GROUP)
    u = proj(C_CV, CONV_DIM) * jax.nn.sigmoid(proj(C_CGL, CONV_DIM))
    for s in range(n_seq):
        u_ref[s, CONV_PAD:CONV_PAD + seq_rows, :] = u[s * seq_rows:(s + 1) * seq_rows]
    conv_acc = [jnp.broadcast_to(layer_row("conv_b"), (seq_rows, CONV_DIM))] * n_seq
    for rho in range(SUBLANES):
        anchor, prev_anchor = prev_anchor, compute_group(rho)
        for s in range(n_seq):
            frame_rows = seq_rows + (SUBLANES if rho else 0)
            frame = None
            for j in range(CONV_WIDTH):
                off = CONV_PAD - CONV_HIST + j
                if off % SUBLANES != rho:
                    continue
                term = (convw_ref[j:j + 1, :] + anchor) * u_ref[s, off - rho:off - rho + frame_rows, :]
                frame = term if frame is None else frame + term
            conv_acc[s] = conv_acc[s] + frame[rho:rho + seq_rows]
    compute_group(C_CG // PROJ_GROUP)
    compute_group(C_GLR // PROJ_GROUP)
    cc = conv_acc[0] if n_seq == 1 else jnp.concatenate(conv_acc, axis=0)

    for s in range(n_seq):
        nc_ref[s] = u_ref[s, seq_rows + CONV_PAD - CONV_HIST:seq_rows + CONV_PAD, :]
    if carry:
        u_ref[:, 0:CONV_PAD, :] = u_ref[:, seq_rows:seq_rows + CONV_PAD, :]

    mu = jnp.mean(cc, axis=-1, keepdims=True)
    cen = cc - mu
    var = jnp.mean(cen * cen, axis=-1, keepdims=True)
    ln = cen * lax.rsqrt(var + NORM_EPS) * layer_row("ln_gain") + layer_row("ln_bias")
    cpw = _dot(_silu(ln).astype(_BF), w_pw_ref[...]) + layer_row("b_pw")
    ycat_ref[:, ATTN_DIM + GLA_VDIM:D_MODEL] = cpw * gate_ref[:, ATTN_DIM + GLA_VDIM:D_MODEL]

    z = _dot(proj(C_GLR, LANES).astype(_BF), w_up_ref[...]) + layer_row("gla_b")
    log_a = (jnp.minimum(z, 0.0) - jnp.log(1.0 + jnp.exp(-jnp.abs(z)))) * (1.0 / GLA_TAU)
    tri = tri_ref[...]
    la_hi = log_a.astype(_BF)
    la_lo = (log_a - la_hi.astype(_F32)).astype(_BF)
    bcum = _dot(tri, la_hi) + _dot(tri, la_lo)
    gq = proj(C_GQ, GLA_KDIM) * GLA_DK ** -0.5
    gk = proj(C_GK, GLA_KDIM)
    gv = proj(C_GV, GLA_VDIM)
    gate_g = gate_ref[:, ATTN_DIM:ATTN_DIM + GLA_VDIM]
    gla_g = layer_row("gla_gain")

    for g in range(ATTN_DIM // PROJ_GROUP):
        post_q(g)(proj(C_AQ + g * PROJ_GROUP, PROJ_GROUP))
    post_kv(proj(C_AK, 2 * KV_DIM))

    head_of_lane = lane128 >> GLA_DK_SHIFT
    a_r = lax.broadcasted_iota(jnp.int32, (CHUNK, GLA_HEADS * CHUNK), 0)
    a_c = lax.broadcasted_iota(jnp.int32, (CHUNK, GLA_HEADS * CHUNK), 1)
    causal = (a_c & (CHUNK - 1)) <= a_r
    vblk = lax.broadcasted_iota(jnp.int32, (1, GLA_VDIM), 1) >> GLA_DV_SHIFT
    bd_mask_t = ((lax.broadcasted_iota(jnp.int32, (GLA_VDIM, GLA_KDIM), 0) >> GLA_DV_SHIFT)
                 == (lax.broadcasted_iota(jnp.int32, (GLA_VDIM, GLA_KDIM), 1) >> GLA_DK_SHIFT))
    q_all = (gq * jnp.exp(bcum)).astype(_BF)
    k_all = gk * jnp.exp(-bcum)
    intra, kv_upd, decay_rows = [], [], []
    for s in range(n_seq):
        for c in range(n_chunk):
            r0 = s * seq_rows + c * CHUNK
            bc = bcum[r0:r0 + CHUNK]
            b_last = bc[CHUNK - 1:CHUNK]
            k_end = (gk[r0:r0 + CHUNK] * jnp.exp(b_last - bc)).astype(_BF)
            vf = gv[r0:r0 + CHUNK]
            k_stack = jnp.concatenate(
                [jnp.where(head_of_lane == h, k_all[r0:r0 + CHUNK], 0.0) for h in range(GLA_HEADS)],
                axis=0).astype(_BF)
            v_bd = jnp.concatenate(
                [jnp.where(vblk == h, vf, 0.0) for h in range(GLA_HEADS)], axis=0).astype(_BF)
            a = jnp.where(causal, _dot_nt(q_all[r0:r0 + CHUNK], k_stack), 0.0).astype(_BF)
            intra.append(_dot(a, v_bd))
            kv_upd.append(jnp.where(bd_mask_t, _dot_tn(vf.astype(_BF), k_end), 0.0))
            decay_rows.append(jnp.exp(b_last))

    rows = ATTN_GROUP * CHUNK
    r_head1 = lax.broadcasted_iota(jnp.int32, (rows, 1), 0) >> CHUNK_SHIFT
    sink_cols = []
    for j in range(ATTN_KV_HEADS):
        sink = jnp.zeros((rows, 1), _F32)
        for r in range(ATTN_GROUP):
            sink = jnp.where(r_head1 == r, sinks_ref[layer, j * ATTN_GROUP + r], sink)
        sink_cols.append(sink * LOG2E)

    o_rows = []

    def gla_state_steps():
        for s in range(n_seq):
            state = s_ref[s]
            for c in range(n_chunk):
                i = s * n_chunk + c
                r0 = i * CHUNK
                o_rows.append(intra[i] + _dot_nt(q_all[r0:r0 + CHUNK], state.astype(_BF)))
                state = decay_rows[i] * state + kv_upd[i]
            s_ref[s] = state

    n_blocks = n_seq * n_chunk
    gla_steps_at = (3 * n_blocks) // 4 if carry else n_blocks
    for s in range(n_seq):
        for c in range(n_chunk):
            if s * n_chunk + c == gla_steps_at:
                gla_state_steps()
            r0 = s * seq_rows + c * CHUNK
            k0 = c * CHUNK
            steady = WINDOW // CHUNK
            variant = jnp.minimum(t * n_chunk + c, steady) if carry else steady
            for j in range(ATTN_KV_HEADS):
                qs = jnp.concatenate(
                    [qlo_ref[r0:r0 + CHUNK, (2 * j) * LANES:(2 * j + 1) * LANES],
                     qhi_ref[r0:r0 + CHUNK, (2 * j) * LANES:(2 * j + 1) * LANES],
                     qlo_ref[r0:r0 + CHUNK, (2 * j + 1) * LANES:(2 * j + 2) * LANES],
                     qhi_ref[r0:r0 + CHUNK, (2 * j + 1) * LANES:(2 * j + 2) * LANES]], axis=0)
                kb = kd_ref[s, k0:k0 + KBAND, j * LANES:(j + 1) * LANES]
                vb = vd_ref[s, k0:k0 + KBAND, 2 * j * LANES:(2 * j + 2) * LANES]
                qk = _dot_nt(qs, kb)
                sink = sink_cols[j]
                p_blocks, m_blocks = [], []
                for r in range(ATTN_GROUP):
                    sc = qk[r * CHUNK:(r + 1) * CHUNK] - bias_ref[variant, j, r * CHUNK:(r + 1) * CHUNK, :]
                    m_r = jnp.maximum(jnp.max(sc, axis=-1, keepdims=True), sink[r * CHUNK:(r + 1) * CHUNK])
                    p_blocks.append(jnp.exp2(sc - m_r).astype(_BF))
                    m_blocks.append(m_r)
                m = jnp.concatenate(m_blocks, axis=0)
                pv = _dot(jnp.concatenate(p_blocks, axis=0), vb)
                o = pv[:, 0:LANES] / (pv[:, LANES:2 * LANES] + jnp.exp2(sink - m))
                for pb in range(2):
                    blk = 2 * j + pb
                    ob = jnp.where(lo_half, o[(2 * pb) * CHUNK:(2 * pb + 1) * CHUNK],
                                   o[(2 * pb + 1) * CHUNK:(2 * pb + 2) * CHUNK])
                    ycat_ref[r0:r0 + CHUNK, blk * LANES:(blk + 1) * LANES] = (
                        ob * gate_ref[r0:r0 + CHUNK, blk * LANES:(blk + 1) * LANES])

    if gla_steps_at == n_blocks:
        gla_state_steps()

    if carry:
        kd_ref[:, 0:WINDOW, :] = kd_ref[:, seq_rows:seq_rows + WINDOW, :]
        vd_ref[:, 0:WINDOW, :] = vd_ref[:, seq_rows:seq_rows + WINDOW, :]

    o_all = jnp.concatenate(o_rows, axis=0)
    ycat_ref[:, ATTN_DIM:ATTN_DIM + GLA_VDIM] = _group_rms_scale(o_all, GLA_DV, ones64) * gla_g * gate_g

    for s in range(n_seq):
        sbd_ref[...] = s_ref[s].T
        for h in range(GLA_HEADS):
            ns_ref[s, h] = sbd_ref[h * GLA_DK:(h + 1) * GLA_DK, h * GLA_DV:(h + 1) * GLA_DV]

    last_layer = layer == pl.num_programs(2) - 1
    next_gain = layer_row("norm_gain", jnp.where(last_layer, 0, layer + 1))
    for r0 in range(0, tile, OUT_ROW_BLOCK):
        r1 = r0 + OUT_ROW_BLOCK
        y = x[r0:r1] + _dot(ycat_ref[r0:r1, :].astype(_BF), w_out_ref[...])
        y_ref[r0:r1, :] = y
        xcur_ref[r0:r1, :] = y
        hb_ref[r0:r1, :] = normed_bf16(jnp.where(last_layer, xnext_ref[r0:r1, :], y), next_gain)


def _run_stream(x2d, n_seqs, seq_len, hist, sinks, params, *, carry, q_pos0):
    depth = sinks.shape[0]
    if carry:
        assert q_pos0 == 0
        tile, n_seq, n_chunk = PROMPT_TILE, 1, PROMPT_TILE // CHUNK
        grid = (n_seqs, seq_len // tile, depth)
    else:
        assert seq_len == CHUNK and q_pos0 >= WINDOW
        tile, n_seq, n_chunk = SAMPLE_SEQS * CHUNK, SAMPLE_SEQS, 1
        grid = (n_seqs // n_seq, 1, depth)
    n_t = grid[1]
    seq_rows = n_chunk * CHUNK

    def const(shape):
        return pl.BlockSpec(shape, lambda g, t, l: (0,) * len(shape), pipeline_mode=pl.Buffered(1))

    def per_layer_group(shape):
        return pl.BlockSpec((depth, n_seq) + shape, lambda g, t, l: (0, g) + (0,) * len(shape))

    state_shapes = ((WINDOW, KV_DIM), (WINDOW, KV_DIM), (GLA_HEADS, GLA_DK, GLA_DV), (CONV_HIST, CONV_DIM))
    x_spec = pl.BlockSpec((tile, D_MODEL), lambda g, t, l: (g * n_t + t, 0))
    n_tiles = x2d.shape[0] // tile
    xnext_spec = pl.BlockSpec((tile, D_MODEL), lambda g, t, l: (jnp.minimum(g * n_t + t + 1, n_tiles - 1), 0))
    in_specs = [pl.BlockSpec(memory_space=pltpu.SMEM), x_spec, xnext_spec]
    args = [sinks, x2d, x2d]
    if not carry:
        in_specs += [per_layer_group(sh) for sh in state_shapes]
        args += list(hist)
    operands = tuple(params) + _shape_constants(tile)
    in_specs += [const(p.shape) for p in operands]
    in_specs[len(args)] = const(operands[0].shape[:2] + (GLR_SRC,))
    args += list(operands)

    out_shape = (jax.ShapeDtypeStruct(x2d.shape, _F32),) + tuple(
        jax.ShapeDtypeStruct((depth, n_seqs) + sh, _F32) for sh in state_shapes)
    out_specs = (x_spec,) + tuple(per_layer_group(sh) for sh in state_shapes)
    scratch = [pltpu.VMEM((depth, n_seq, WINDOW + seq_rows + CHUNK, ATTN_KV_HEADS * LANES), _BF),
               pltpu.VMEM((depth, n_seq, WINDOW + seq_rows + CHUNK, ATTN_KV_HEADS * 2 * LANES), _BF),
               pltpu.VMEM((depth, n_seq, GLA_VDIM, GLA_KDIM), _F32),
               pltpu.VMEM((depth, n_seq, CONV_PAD + seq_rows, CONV_DIM), _F32),
               pltpu.VMEM((tile, ATTN_DIM), _BF),
               pltpu.VMEM((tile, ATTN_DIM), _BF),
               pltpu.VMEM((tile, D_MODEL), _F32),
               pltpu.VMEM((tile, IN_COLS_PACKED), _F32),
               pltpu.VMEM((tile, D_MODEL), _F32),
               pltpu.VMEM((tile, D_MODEL), _BF),
               pltpu.VMEM((GLA_KDIM, GLA_VDIM), _F32),
               pltpu.VMEM((tile, D_MODEL), _F32)]
    body = functools.partial(_stream_kernel, tile=tile, n_seq=n_seq, n_chunk=n_chunk, carry=carry, q_pos0=q_pos0)
    return pl.pallas_call(
        body, grid=grid, in_specs=in_specs, out_specs=out_specs, out_shape=out_shape,
        scratch_shapes=scratch,
        compiler_params=pltpu.CompilerParams(dimension_semantics=("arbitrary", "arbitrary", "arbitrary"),
                                             vmem_limit_bytes=VMEM_LIMIT_BYTES),
        name="mixer_prompt" if carry else "mixer_sample",
    )(*args)


def _pack_params(norm_gain, w_in, q_norm_gain, k_norm_gain, gla_w_gate_up, gla_b_gate, gla_norm_gain,
                 conv_w, conv_b, conv_ln_gain, conv_ln_bias, conv_w_pw, conv_b_pw, w_out):
    w_in_a = w_in.astype(_BF)
    w_in_c = jnp.pad(w_in_a[:, :, GLR_SRC:GLR_SRC + GLA_LOWRANK], ((0, 0), (0, 0), (0, LANES - GLA_LOWRANK)))
    w_in_b = w_in_a[:, :, GLR_SRC + GLA_LOWRANK:]
    pieces = dict(norm_gain=norm_gain, q_gain=q_norm_gain, k_gain=k_norm_gain, gla_b=gla_b_gate,
                  gla_gain=gla_norm_gain, conv_b=conv_b, ln_gain=conv_ln_gain, ln_bias=conv_ln_bias, b_pw=conv_b_pw)
    depth = w_in.shape[0]
    parts = []
    for name, n in VEC_LAYOUT:
        parts += [pieces[name].astype(_F32)] * (n // pieces[name].shape[1])
        parts.append(jnp.zeros((depth, D_MODEL - n), _F32))
    parts.append(jnp.zeros((depth, (VEC_ROWS - len(VEC_LAYOUT)) * D_MODEL), _F32))
    vecs = jnp.concatenate(parts, axis=1).reshape(depth, VEC_ROWS, D_MODEL)
    w_up_p = jnp.pad(gla_w_gate_up, ((0, 0), (0, LANES - GLA_LOWRANK), (0, 0))).astype(_BF)
    convw = jnp.pad(conv_w.astype(_F32), ((0, 0), (0, CONV_PAD - CONV_WIDTH), (0, 0)))
    return w_in_a, w_in_b, w_in_c, w_up_p, conv_w_pw.astype(_BF), w_out.astype(_BF), vecs, convw


def kernel(x_prompt, x_sample, cache_k, cache_v, state_gla, state_conv, norm_gain, w_in, q_norm_gain, k_norm_gain, attn_sinks, gla_w_gate_up, gla_b_gate, gla_norm_gain, conv_w, conv_b, conv_ln_gain, conv_ln_bias, conv_w_pw, conv_b_pw, w_out):
    depth = w_in.shape[0]
    bp, lp, _ = x_prompt.shape
    bs, ls, _ = x_sample.shape
    params = _pack_params(norm_gain, w_in, q_norm_gain, k_norm_gain, gla_w_gate_up, gla_b_gate, gla_norm_gain,
                          conv_w, conv_b, conv_ln_gain, conv_ln_bias, conv_w_pw, conv_b_pw, w_out)
    sinks = jnp.pad(attn_sinks.astype(_F32), ((0, 0), (0, ATTN_HEADS)))
    yp, pk, pv, ps, pc = _run_stream(x_prompt.reshape(bp * lp, D_MODEL), bp, lp, None, sinks, params,
                                     carry=True, q_pos0=0)
    hist = (cache_k.reshape(depth, bs, WINDOW, KV_DIM), cache_v.reshape(depth, bs, WINDOW, KV_DIM),
            state_gla, state_conv)
    ys, sk, sv, ss, sc = _run_stream(x_sample.reshape(bs * ls, D_MODEL), bs, ls, hist, sinks, params,
                                     carry=False, q_pos0=PAST_LEN)
    kv_shape = (WINDOW, ATTN_KV_HEADS, HEAD_DIM)
    return (yp.reshape(bp, lp, D_MODEL), ys.reshape(bs, ls, D_MODEL),
            pk.reshape((depth, bp) + kv_shape), pv.reshape((depth, bp) + kv_shape), ps, pc,
            sk.reshape((depth, bs) + kv_shape), sv.reshape((depth, bs) + kv_shape), ss, sc)
```

```python
import functools
import math

import numpy as np
import jax
import jax.numpy as jnp
from jax import lax
from jax.experimental import pallas as pl
from jax.experimental.pallas import tpu as pltpu

D_MODEL = 1024
CHUNK = 64
ATTN_HEADS = 8
ATTN_KV_HEADS = 2
HEAD_DIM = 64
ATTN_GROUP = ATTN_HEADS // ATTN_KV_HEADS
ATTN_DIM = ATTN_HEADS * HEAD_DIM
KV_DIM = ATTN_KV_HEADS * HEAD_DIM
WINDOW = 128
BAND = WINDOW + CHUNK
KBAND = WINDOW + 2 * CHUNK
GLA_HEADS = 4
GLA_DK = 32
GLA_DV = 64
GLA_KDIM = GLA_HEADS * GLA_DK
GLA_VDIM = GLA_HEADS * GLA_DV
GLA_LOWRANK = 16
GLA_TAU = 16.0
CONV_DIM = 256
CONV_WIDTH = 31
CONV_HIST = CONV_WIDTH - 1
NORM_EPS = 1e-6
NEG_INF = -1e30
PAST_LEN = 4096
LOG2E = math.log2(math.e)

LANES = 128
SUBLANES = 8
CHUNK_SHIFT = CHUNK.bit_length() - 1
GLA_DK_SHIFT = GLA_DK.bit_length() - 1
GLA_DV_SHIFT = GLA_DV.bit_length() - 1
C_AQ, C_AK, C_AV, C_AG = 0, 512, 640, 768
C_GQ, C_GK, C_GV, C_GG = 1280, 1408, 1536, 1792
C_CV, C_CGL, C_CG, C_GLR = 2048, 2304, 2560, 2816
IN_COLS_PACKED = C_GLR + LANES
PROJ_GROUP = 256
CONV_PAD = 32
GLR_SRC = 2048
VEC_LAYOUT = (("norm_gain", D_MODEL), ("q_gain", LANES), ("k_gain", KV_DIM), ("gla_b", GLA_KDIM),
              ("gla_gain", GLA_VDIM), ("conv_b", CONV_DIM), ("ln_gain", CONV_DIM), ("ln_bias", CONV_DIM),
              ("b_pw", CONV_DIM))
VEC_ROWS = 16
VEC_ROW_OF = {name: (k, n) for k, (name, n) in enumerate(VEC_LAYOUT)}

OUT_ROW_BLOCK = 256
PROMPT_TILE = 512
SAMPLE_SEQS = 4
VMEM_LIMIT_BYTES = 56 * 1024 * 1024

_BF = jnp.bfloat16
_F32 = jnp.float32


def _shape_constants(tile):
    rows = np.arange(ATTN_GROUP * CHUNK)
    dist = np.abs((rows % CHUNK)[:, None] + WINDOW - np.arange(KBAND)[None, :]).astype(np.float64)
    bias = np.stack([
        (2.0 ** (-8.0 * (j * ATTN_GROUP + rows // CHUNK + 1) / ATTN_HEADS))[:, None] * LOG2E * dist
        for j in range(ATTN_KV_HEADS)]).astype(np.float32)
    bias[:, :, BAND:] = -NEG_INF
    variants = []
    for n in range(WINDOW // CHUNK + 1):
        b = bias.copy()
        b[:, :, :max(WINDOW - n * CHUNK, 0)] = -NEG_INF
        variants.append(b)
    bias = np.stack(variants)
    tr = np.arange(tile)
    tri = ((tr[:, None] // CHUNK == tr[None, :] // CHUNK) & (tr[None, :] <= tr[:, None])).astype(np.float32)
    ln = np.arange(LANES) // HEAD_DIM
    ones = (ln[:, None] == ln[None, :]).astype(np.float32)
    return jnp.asarray(bias), jnp.asarray(tri, dtype=_BF), jnp.asarray(ones, dtype=_BF)


def _group_rms_scale(x, group, ones_bd):
    outs = []
    for c0 in range(0, x.shape[1], LANES):
        blk = x[:, c0:c0 + LANES]
        ss = _dot((blk * blk).astype(_BF), ones_bd)
        outs.append(blk * lax.rsqrt(ss * (1.0 / group) + NORM_EPS))
    return outs[0] if len(outs) == 1 else jnp.concatenate(outs, axis=1)


def _silu(x):
    return x * jax.nn.sigmoid(x)


def _dot(a, b):
    return jnp.dot(a, b, preferred_element_type=_F32)


def _dot_nt(a, b):
    return lax.dot_general(a, b, (((1,), (1,)), ((), ())), preferred_element_type=_F32)


def _dot_tn(a, b):
    return lax.dot_general(a, b, (((0,), (0,)), ((), ())), preferred_element_type=_F32)


def _stream_kernel(*refs, tile, n_seq, n_chunk, carry, q_pos0):
    if carry:
        (sinks_ref, x_ref, xnext_ref, *rest) = refs
    else:
        (sinks_ref, x_ref, xnext_ref, hk_all, hv_all, hs_all, hc_all, *rest) = refs
    (w_in_a, w_in_b, w_in_c, w_up_all, w_pw_all, w_out_all, vec_all, convw_all, bias_ref, tri_ref, ones_ref,
     y_ref, nk_all, nv_all, ns_all, nc_all,
     kd_all, vd_all, s_all, u_all, qlo_ref, qhi_ref, ycat_ref, proj_ref, xcur_ref, hb_ref, sbd_ref,
     gate_ref) = rest
    t = pl.program_id(1)
    layer = pl.program_id(2)
    seq_rows = n_chunk * CHUNK
    w_up_ref, w_pw_ref, w_out_ref = (r.at[layer] for r in (w_up_all, w_pw_all, w_out_all))
    convw_ref = convw_all.at[layer]

    def layer_row(name, which=layer):
        r, n = VEC_ROW_OF[name]
        return vec_all[which, r:r + 1, 0:n]

    q_gain = layer_row("q_gain") * (LOG2E * HEAD_DIM ** -0.5)
    k_gain = layer_row("k_gain")
    kd_ref, vd_ref, s_ref, u_ref = (r.at[layer] for r in (kd_all, vd_all, s_all, u_all))
    nk_ref, nv_ref, ns_ref, nc_ref = (r.at[layer] for r in (nk_all, nv_all, ns_all, nc_all))
    if not carry:
        hk_ref, hv_ref, hs_ref, hc_ref = (r.at[layer] for r in (hk_all, hv_all, hs_all, hc_all))

    lane128 = lax.broadcasted_iota(jnp.int32, (1, LANES), 1)
    lo_half = lane128 < HEAD_DIM

    def dup_halves(a):
        sw = pltpu.roll(a, HEAD_DIM, 1)
        return jnp.concatenate([jnp.where(lo_half, a, sw), jnp.where(lo_half, sw, a)], axis=1)

    buf_rows = WINDOW + seq_rows + CHUNK
    zeros_blk = jnp.zeros((buf_rows, LANES), _BF)
    ones_blk = jnp.ones((buf_rows, LANES), _BF)
    v_blank = jnp.concatenate([zeros_blk, ones_blk] * ATTN_KV_HEADS, axis=1)

    def store_v(s, row0, vdup_rows):
        for j in range(ATTN_KV_HEADS):
            vd_ref[s, row0:row0 + vdup_rows.shape[0], 2 * j * LANES:(2 * j + 1) * LANES] = (
                vdup_rows[:, j * LANES:(j + 1) * LANES])

    if carry:
        @pl.when(t == 0)
        def _():
            kd_ref[...] = jnp.zeros(kd_ref.shape, _BF)
            for s in range(n_seq):
                vd_ref[s] = v_blank
            s_ref[...] = jnp.zeros(s_ref.shape, _F32)
            u_ref[:, 0:CONV_PAD, :] = jnp.zeros((n_seq, CONV_PAD, CONV_DIM), _F32)
    else:
        for s in range(n_seq):
            u_ref[s, 0:SUBLANES, :] = jnp.zeros((SUBLANES, CONV_DIM), _F32)
            u_ref[s, CONV_PAD - CONV_HIST:CONV_PAD, :] = hc_ref[s]

    def normed_bf16(v, gain_row):
        ms = jnp.mean(v * v, axis=-1, keepdims=True)
        return (v * lax.rsqrt(ms + NORM_EPS) * gain_row).astype(_BF)

    @pl.when(layer == 0)
    def _():
        xcur_ref[...] = x_ref[...]

    @pl.when((pl.program_id(0) == 0) & (t == 0) & (layer == 0))
    def _():
        hb_ref[...] = normed_bf16(x_ref[...], layer_row("norm_gain"))
    x = xcur_ref[...]
    hb = hb_ref[...]

    anchor_on = sinks_ref[layer, ATTN_HEADS] != 0.0

    ones64 = ones_ref[...]

    def post_q(g):
        def post(res):
            for b in range(PROJ_GROUP // LANES):
                c0 = g * PROJ_GROUP + b * LANES
                qn = _group_rms_scale(res[:, b * LANES:(b + 1) * LANES], HEAD_DIM, ones64) * q_gain
                qlo_ref[:, c0:c0 + LANES] = jnp.where(lo_half, qn, 0.0).astype(_BF)
                qhi_ref[:, c0:c0 + LANES] = jnp.where(lo_half, 0.0, qn).astype(_BF)
        return post

    def post_kv(res):
        kn = _group_rms_scale(res[:, 0:KV_DIM], HEAD_DIM, ones64) * k_gain
        vv = res[:, KV_DIM:2 * KV_DIM]
        kdup = dup_halves(kn).astype(_BF)
        vdup = dup_halves(vv).astype(_BF)
        for s in range(n_seq):
            kd_ref[s, WINDOW:WINDOW + seq_rows, :] = kdup[s * seq_rows:(s + 1) * seq_rows]
            store_v(s, WINDOW, vdup[s * seq_rows:(s + 1) * seq_rows])
        if carry:
            nk_ref[0] = kn[tile - WINDOW:tile]
            nv_ref[0] = vv[tile - WINDOW:tile]
        else:
            for s in range(n_seq):
                nk_ref[s, 0:WINDOW - seq_rows, :] = hk_ref[s, seq_rows:WINDOW, :]
                nv_ref[s, 0:WINDOW - seq_rows, :] = hv_ref[s, seq_rows:WINDOW, :]
                nk_ref[s, WINDOW - seq_rows:WINDOW, :] = kn[s * seq_rows:(s + 1) * seq_rows]
                nv_ref[s, WINDOW - seq_rows:WINDOW, :] = vv[s * seq_rows:(s + 1) * seq_rows]

    def post_gate(col0):
        def post(res):
            gate_ref[:, col0:col0 + PROJ_GROUP] = _silu(res)
        return post

    group_post = {C_AG // PROJ_GROUP: post_gate(0), C_AG // PROJ_GROUP + 1: post_gate(PROJ_GROUP),
                  C_GG // PROJ_GROUP: post_gate(ATTN_DIM), C_CG // PROJ_GROUP: post_gate(ATTN_DIM + GLA_VDIM)}

    def w_in_cols(c0, c1):
        if c1 <= C_CV:
            return w_in_a[layer, :, c0:c1]
        if c1 <= C_GLR:
            return w_in_b[layer, :, c0 - C_CV:c1 - C_CV]
        return w_in_c[layer, :, c0 - C_GLR:c1 - C_GLR]

    def compute_group(g):
        c0, c1 = g * PROJ_GROUP, min((g + 1) * PROJ_GROUP, IN_COLS_PACKED)
        res = _dot(hb, w_in_cols(c0, c1))
        if g in group_post:
            group_post[g](res)
        else:
            proj_ref[:, c0:c1] = res
        return jnp.where(anchor_on, res[tile - 1:tile, 0:CONV_DIM], 0.0)

    def proj(c0, width):
        return proj_ref[:, c0:c0 + width]

    compute_group(C_CV // PROJ_GROUP)
    prev_anchor = compute_group(C_CGL // PROJ_GROUP)
    u = proj(C_CV, CONV_DIM) * jax.nn.sigmoid(proj(C_CGL, CONV_DIM))
    for s in range(n_seq):
        u_ref[s, CONV_PAD:CONV_PAD + seq_rows, :] = u[s * seq_rows:(s + 1) * seq_rows]
    conv_acc = [jnp.broadcast_to(layer_row("conv_b"), (seq_rows, CONV_DIM))] * n_seq
    for rho in range(SUBLANES):
        anchor, prev_anchor = prev_anchor, compute_group(rho)
        for s in range(n_seq):
            frame_rows = seq_rows + (SUBLANES if rho else 0)
            frame = None
            for j in range(CONV_WIDTH):
                off = CONV_PAD - CONV_HIST + j
                if off % SUBLANES != rho:
                    continue
                term = (convw_ref[j:j + 1, :] + anchor) * u_ref[s, off - rho:off - rho + frame_rows, :]
                frame = term if frame is None else frame + term
            conv_acc[s] = conv_acc[s] + frame[rho:rho + seq_rows]
    compute_group(C_CG // PROJ_GROUP)
    compute_group(C_GLR // PROJ_GROUP)
    cc = conv_acc[0] if n_seq == 1 else jnp.concatenate(conv_acc, axis=0)

    for s in range(n_seq):
        nc_ref[s] = u_ref[s, seq_rows + CONV_PAD - CONV_HIST:seq_rows + CONV_PAD, :]
    if carry:
        u_ref[:, 0:CONV_PAD, :] = u_ref[:, seq_rows:seq_rows + CONV_PAD, :]

    mu = jnp.mean(cc, axis=-1, keepdims=True)
    cen = cc - mu
    var = jnp.mean(cen * cen, axis=-1, keepdims=True)
    ln = cen * lax.rsqrt(var + NORM_EPS) * layer_row("ln_gain") + layer_row("ln_bias")
    cpw = _dot(_silu(ln).astype(_BF), w_pw_ref[...]) + layer_row("b_pw")
    ycat_ref[:, ATTN_DIM + GLA_VDIM:D_MODEL] = cpw * gate_ref[:, ATTN_DIM + GLA_VDIM:D_MODEL]

    if not carry:
        for s in range(n_seq):
            kd_ref[s, 0:WINDOW, :] = dup_halves(hk_ref[s]).astype(_BF)
            kd_ref[s, WINDOW + seq_rows:, :] = jnp.zeros((CHUNK, 2 * LANES), _BF)
            vd_ref[s] = v_blank
            store_v(s, 0, dup_halves(hv_ref[s]).astype(_BF))
            sbd_ref[...] = jnp.zeros((GLA_KDIM, GLA_VDIM), _F32)
            for h in range(GLA_HEADS):
                sbd_ref[h * GLA_DK:(h + 1) * GLA_DK, h * GLA_DV:(h + 1) * GLA_DV] = hs_ref[s, h]
            s_ref[s] = sbd_ref[...].T

    z = _dot(proj(C_GLR, LANES).astype(_BF), w_up_ref[...]) + layer_row("gla_b")
    log_a = (jnp.minimum(z, 0.0) - jnp.log(1.0 + jnp.exp(-jnp.abs(z)))) * (1.0 / GLA_TAU)
    tri = tri_ref[...]
    la_hi = log_a.astype(_BF)
    la_lo = (log_a - la_hi.astype(_F32)).astype(_BF)
    bcum = _dot(tri, la_hi) + _dot(tri, la_lo)
    gq = proj(C_GQ, GLA_KDIM) * GLA_DK ** -0.5
    gk = proj(C_GK, GLA_KDIM)
    gv = proj(C_GV, GLA_VDIM)
    gate_g = gate_ref[:, ATTN_DIM:ATTN_DIM + GLA_VDIM]
    gla_g = layer_row("gla_gain")

    for g in range(ATTN_DIM // PROJ_GROUP):
        post_q(g)(proj(C_AQ + g * PROJ_GROUP, PROJ_GROUP))
    post_kv(proj(C_AK, 2 * KV_DIM))

    head_of_lane = lane128 >> GLA_DK_SHIFT
    a_r = lax.broadcasted_iota(jnp.int32, (CHUNK, GLA_HEADS * CHUNK), 0)
    a_c = lax.broadcasted_iota(jnp.int32, (CHUNK, GLA_HEADS * CHUNK), 1)
    causal = (a_c & (CHUNK - 1)) <= a_r
    vblk = lax.broadcasted_iota(jnp.int32, (1, GLA_VDIM), 1) >> GLA_DV_SHIFT
    bd_mask_t = ((lax.broadcasted_iota(jnp.int32, (GLA_VDIM, GLA_KDIM), 0) >> GLA_DV_SHIFT)
                 == (lax.broadcasted_iota(jnp.int32, (GLA_VDIM, GLA_KDIM), 1) >> GLA_DK_SHIFT))
    q_all = (gq * jnp.exp(bcum)).astype(_BF)
    k_all = gk * jnp.exp(-bcum)
    intra, kv_upd, decay_rows = [], [], []
    for s in range(n_seq):
        for c in range(n_chunk):
            r0 = s * seq_rows + c * CHUNK
            bc = bcum[r0:r0 + CHUNK]
            b_last = bc[CHUNK - 1:CHUNK]
            k_end = (gk[r0:r0 + CHUNK] * jnp.exp(b_last - bc)).astype(_BF)
            vf = gv[r0:r0 + CHUNK]
            k_stack = jnp.concatenate(
                [jnp.where(head_of_lane == h, k_all[r0:r0 + CHUNK], 0.0) for h in range(GLA_HEADS)],
                axis=0).astype(_BF)
            v_bd = jnp.concatenate(
                [jnp.where(vblk == h, vf, 0.0) for h in range(GLA_HEADS)], axis=0).astype(_BF)
            a = jnp.where(causal, _dot_nt(q_all[r0:r0 + CHUNK], k_stack), 0.0).astype(_BF)
            intra.append(_dot(a, v_bd))
            kv_upd.append(jnp.where(bd_mask_t, _dot_tn(vf.astype(_BF), k_end), 0.0))
            decay_rows.append(jnp.exp(b_last))

    rows = ATTN_GROUP * CHUNK
    r_head1 = lax.broadcasted_iota(jnp.int32, (rows, 1), 0) >> CHUNK_SHIFT
    sink_cols = []
    for j in range(ATTN_KV_HEADS):
        sink = jnp.zeros((rows, 1), _F32)
        for r in range(ATTN_GROUP):
            sink = jnp.where(r_head1 == r, sinks_ref[layer, j * ATTN_GROUP + r], sink)
        sink_cols.append(sink * LOG2E)

    o_rows = []

    def gla_state_steps():
        for s in range(n_seq):
            state = s_ref[s]
            for c in range(n_chunk):
                i = s * n_chunk + c
                r0 = i * CHUNK
                o_rows.append(intra[i] + _dot_nt(q_all[r0:r0 + CHUNK], state.astype(_BF)))
                state = decay_rows[i] * state + kv_upd[i]
            s_ref[s] = state

    n_blocks = n_seq * n_chunk
    gla_steps_at = (3 * n_blocks) // 4 if carry else n_blocks
    for s in range(n_seq):
        for c in range(n_chunk):
            if s * n_chunk + c == gla_steps_at:
                gla_state_steps()
            r0 = s * seq_rows + c * CHUNK
            k0 = c * CHUNK
            steady = WINDOW // CHUNK
            variant = jnp.minimum(t * n_chunk + c, steady) if carry else steady
            for j in range(ATTN_KV_HEADS):
                qs = jnp.concatenate(
                    [qlo_ref[r0:r0 + CHUNK, (2 * j) * LANES:(2 * j + 1) * LANES],
                     qhi_ref[r0:r0 + CHUNK, (2 * j) * LANES:(2 * j + 1) * LANES],
                     qlo_ref[r0:r0 + CHUNK, (2 * j + 1) * LANES:(2 * j + 2) * LANES],
                     qhi_ref[r0:r0 + CHUNK, (2 * j + 1) * LANES:(2 * j + 2) * LANES]], axis=0)
                kb = kd_ref[s, k0:k0 + KBAND, j * LANES:(j + 1) * LANES]
                vb = vd_ref[s, k0:k0 + KBAND, 2 * j * LANES:(2 * j + 2) * LANES]
                qk = _dot_nt(qs, kb)
                sink = sink_cols[j]
                p_blocks, m_blocks = [], []
                for r in range(ATTN_GROUP):
                    sc = qk[r * CHUNK:(r + 1) * CHUNK] - bias_ref[variant, j, r * CHUNK:(r + 1) * CHUNK, :]
                    m_r = jnp.maximum(jnp.max(sc, axis=-1, keepdims=True), sink[r * CHUNK:(r + 1) * CHUNK])
                    p_blocks.append(jnp.exp2(sc - m_r).astype(_BF))
                    m_blocks.append(m_r)
                m = jnp.concatenate(m_blocks, axis=0)
                pv = _dot(jnp.concatenate(p_blocks, axis=0), vb)
                o = pv[:, 0:LANES] / (pv[:, LANES:2 * LANES] + jnp.exp2(sink - m))
                for pb in range(2):
                    blk = 2 * j + pb
                    ob = jnp.where(lo_half, o[(2 * pb) * CHUNK:(2 * pb + 1) * CHUNK],
                                   o[(2 * pb + 1) * CHUNK:(2 * pb + 2) * CHUNK])
                    ycat_ref[r0:r0 + CHUNK, blk * LANES:(blk + 1) * LANES] = (
                        ob * gate_ref[r0:r0 + CHUNK, blk * LANES:(blk + 1) * LANES])

    if gla_steps_at == n_blocks:
        gla_state_steps()

    if carry:
        kd_ref[:, 0:WINDOW, :] = kd_ref[:, seq_rows:seq_rows + WINDOW, :]
        vd_ref[:, 0:WINDOW, :] = vd_ref[:, seq_rows:seq_rows + WINDOW, :]

    o_all = jnp.concatenate(o_rows, axis=0)
    ycat_ref[:, ATTN_DIM:ATTN_DIM + GLA_VDIM] = _group_rms_scale(o_all, GLA_DV, ones64) * gla_g * gate_g

    for s in range(n_seq):
        sbd_ref[...] = s_ref[s].T
        for h in range(GLA_HEADS):
            ns_ref[s, h] = sbd_ref[h * GLA_DK:(h + 1) * GLA_DK, h * GLA_DV:(h + 1) * GLA_DV]

    last_layer = layer == pl.num_programs(2) - 1
    next_gain = layer_row("norm_gain", jnp.where(last_layer, 0, layer + 1))
    for r0 in range(0, tile, OUT_ROW_BLOCK):
        r1 = r0 + OUT_ROW_BLOCK
        y = x[r0:r1] + _dot(ycat_ref[r0:r1, :].astype(_BF), w_out_ref[...])
        y_ref[r0:r1, :] = y
        xcur_ref[r0:r1, :] = y
        hb_ref[r0:r1, :] = normed_bf16(jnp.where(last_layer, xnext_ref[r0:r1, :], y), next_gain)


def _run_stream(x2d, n_seqs, seq_len, hist, sinks, params, *, carry, q_pos0):
    depth = sinks.shape[0]
    if carry:
        assert q_pos0 == 0
        tile, n_seq, n_chunk = PROMPT_TILE, 1, PROMPT_TILE // CHUNK
        grid = (n_seqs, seq_len // tile, depth)
    else:
        assert seq_len == CHUNK and q_pos0 >= WINDOW
        tile, n_seq, n_chunk = SAMPLE_SEQS * CHUNK, SAMPLE_SEQS, 1
        grid = (n_seqs // n_seq, 1, depth)
    n_t = grid[1]
    seq_rows = n_chunk * CHUNK

    def const(shape):
        return pl.BlockSpec(shape, lambda g, t, l: (0,) * len(shape), pipeline_mode=pl.Buffered(1))

    def per_layer_group(shape):
        return pl.BlockSpec((depth, n_seq) + shape, lambda g, t, l: (0, g) + (0,) * len(shape))

    state_shapes = ((WINDOW, KV_DIM), (WINDOW, KV_DIM), (GLA_HEADS, GLA_DK, GLA_DV), (CONV_HIST, CONV_DIM))
    x_spec = pl.BlockSpec((tile, D_MODEL), lambda g, t, l: (g * n_t + t, 0))
    n_tiles = x2d.shape[0] // tile
    xnext_spec = pl.BlockSpec((tile, D_MODEL), lambda g, t, l: (jnp.minimum(g * n_t + t + 1, n_tiles - 1), 0))
    in_specs = [pl.BlockSpec(memory_space=pltpu.SMEM), x_spec, xnext_spec]
    args = [sinks, x2d, x2d]
    if not carry:
        in_specs += [per_layer_group(sh) for sh in state_shapes]
        args += list(hist)
    operands = tuple(params) + _shape_constants(tile)
    in_specs += [const(p.shape) for p in operands]
    in_specs[len(args)] = const(operands[0].shape[:2] + (GLR_SRC,))
    args += list(operands)

    out_shape = (jax.ShapeDtypeStruct(x2d.shape, _F32),) + tuple(
        jax.ShapeDtypeStruct((depth, n_seqs) + sh, _F32) for sh in state_shapes)
    out_specs = (x_spec,) + tuple(per_layer_group(sh) for sh in state_shapes)
    scratch = [pltpu.VMEM((depth, n_seq, WINDOW + seq_rows + CHUNK, ATTN_KV_HEADS * LANES), _BF),
               pltpu.VMEM((depth, n_seq, WINDOW + seq_rows + CHUNK, ATTN_KV_HEADS * 2 * LANES), _BF),
               pltpu.VMEM((depth, n_seq, GLA_VDIM, GLA_KDIM), _F32),
               pltpu.VMEM((depth, n_seq, CONV_PAD + seq_rows, CONV_DIM), _F32),
               pltpu.VMEM((tile, ATTN_DIM), _BF),
               pltpu.VMEM((tile, ATTN_DIM), _BF),
               pltpu.VMEM((tile, D_MODEL), _F32),
               pltpu.VMEM((tile, IN_COLS_PACKED), _F32),
               pltpu.VMEM((tile, D_MODEL), _F32),
               pltpu.VMEM((tile, D_MODEL), _BF),
               pltpu.VMEM((GLA_KDIM, GLA_VDIM), _F32),
               pltpu.VMEM((tile, D_MODEL), _F32)]
    body = functools.partial(_stream_kernel, tile=tile, n_seq=n_seq, n_chunk=n_chunk, carry=carry, q_pos0=q_pos0)
    return pl.pallas_call(
        body, grid=grid, in_specs=in_specs, out_specs=out_specs, out_shape=out_shape,
        scratch_shapes=scratch,
        compiler_params=pltpu.CompilerParams(dimension_semantics=("arbitrary", "arbitrary", "arbitrary"),
                                             vmem_limit_bytes=VMEM_LIMIT_BYTES),
        name="mixer_prompt" if carry else "mixer_sample",
    )(*args)


def _pack_params(norm_gain, w_in, q_norm_gain, k_norm_gain, gla_w_gate_up, gla_b_gate, gla_norm_gain,
                 conv_w, conv_b, conv_ln_gain, conv_ln_bias, conv_w_pw, conv_b_pw, w_out):
    w_in_a = w_in.astype(_BF)
    w_in_c = jnp.pad(w_in_a[:, :, GLR_SRC:GLR_SRC + GLA_LOWRANK], ((0, 0), (0, 0), (0, LANES - GLA_LOWRANK)))
    w_in_b = w_in_a[:, :, GLR_SRC + GLA_LOWRANK:]
    pieces = dict(norm_gain=norm_gain, q_gain=q_norm_gain, k_gain=k_norm_gain, gla_b=gla_b_gate,
                  gla_gain=gla_norm_gain, conv_b=conv_b, ln_gain=conv_ln_gain, ln_bias=conv_ln_bias, b_pw=conv_b_pw)
    depth = w_in.shape[0]
    parts = []
    for name, n in VEC_LAYOUT:
        parts += [pieces[name].astype(_F32)] * (n // pieces[name].shape[1])
        parts.append(jnp.zeros((depth, D_MODEL - n), _F32))
    parts.append(jnp.zeros((depth, (VEC_ROWS - len(VEC_LAYOUT)) * D_MODEL), _F32))
    vecs = jnp.concatenate(parts, axis=1).reshape(depth, VEC_ROWS, D_MODEL)
    w_up_p = jnp.pad(gla_w_gate_up, ((0, 0), (0, LANES - GLA_LOWRANK), (0, 0))).astype(_BF)
    convw = jnp.pad(conv_w.astype(_F32), ((0, 0), (0, CONV_PAD - CONV_WIDTH), (0, 0)))
    return w_in_a, w_in_b, w_in_c, w_up_p, conv_w_pw.astype(_BF), w_out.astype(_BF), vecs, convw


def kernel(x_prompt, x_sample, cache_k, cache_v, state_gla, state_conv, norm_gain, w_in, q_norm_gain, k_norm_gain, attn_sinks, gla_w_gate_up, gla_b_gate, gla_norm_gain, conv_w, conv_b, conv_ln_gain, conv_ln_bias, conv_w_pw, conv_b_pw, w_out):
    depth = w_in.shape[0]
    bp, lp, _ = x_prompt.shape
    bs, ls, _ = x_sample.shape
    params = _pack_params(norm_gain, w_in, q_norm_gain, k_norm_gain, gla_w_gate_up, gla_b_gate, gla_norm_gain,
                          conv_w, conv_b, conv_ln_gain, conv_ln_bias, conv_w_pw, conv_b_pw, w_out)
    sinks = jnp.pad(attn_sinks.astype(_F32), ((0, 0), (0, ATTN_HEADS)))
    yp, pk, pv, ps, pc = _run_stream(x_prompt.reshape(bp * lp, D_MODEL), bp, lp, None, sinks, params,
                                     carry=True, q_pos0=0)
    hist = (cache_k.reshape(depth, bs, WINDOW, KV_DIM), cache_v.reshape(depth, bs, WINDOW, KV_DIM),
            state_gla, state_conv)
    ys, sk, sv, ss, sc = _run_stream(x_sample.reshape(bs * ls, D_MODEL), bs, ls, hist, sinks, params,
                                     carry=False, q_pos0=PAST_LEN)
    kv_shape = (WINDOW, ATTN_KV_HEADS, HEAD_DIM)
    return (yp.reshape(bp, lp, D_MODEL), ys.reshape(bs, ls, D_MODEL),
            pk.reshape((depth, bp) + kv_shape), pv.reshape((depth, bp) + kv_shape), ps, pc,
            sk.reshape((depth, bs) + kv_shape), sv.reshape((depth, bs) + kv_shape), ss, sc)
```

```python
import functools
import math

import numpy as np
import jax
import jax.numpy as jnp
from jax import lax
from jax.experimental import pallas as pl
from jax.experimental.pallas import tpu as pltpu

D_MODEL = 1024
CHUNK = 64
ATTN_HEADS = 8
ATTN_KV_HEADS = 2
HEAD_DIM = 64
ATTN_GROUP = ATTN_HEADS // ATTN_KV_HEADS
ATTN_DIM = ATTN_HEADS * HEAD_DIM
KV_DIM = ATTN_KV_HEADS * HEAD_DIM
WINDOW = 128
BAND = WINDOW + CHUNK
KBAND = WINDOW + 2 * CHUNK
GLA_HEADS = 4
GLA_DK = 32
GLA_DV = 64
GLA_KDIM = GLA_HEADS * GLA_DK
GLA_VDIM = GLA_HEADS * GLA_DV
GLA_LOWRANK = 16
GLA_TAU = 16.0
CONV_DIM = 256
CONV_WIDTH = 31
CONV_HIST = CONV_WIDTH - 1
NORM_EPS = 1e-6
NEG_INF = -1e30
PAST_LEN = 4096
LOG2E = math.log2(math.e)

LANES = 128
SUBLANES = 8
CHUNK_SHIFT = CHUNK.bit_length() - 1
GLA_DK_SHIFT = GLA_DK.bit_length() - 1
GLA_DV_SHIFT = GLA_DV.bit_length() - 1
C_AQ, C_AK, C_AV, C_AG = 0, 512, 640, 768
C_GQ, C_GK, C_GV, C_GG = 1280, 1408, 1536, 1792
C_CV, C_CGL, C_CG, C_GLR = 2048, 2304, 2560, 2816
IN_COLS_PACKED = C_GLR + LANES
PROJ_GROUP = 256
CONV_PAD = 32
GLR_SRC = 2048
VEC_LAYOUT = (("norm_gain", D_MODEL), ("q_gain", LANES), ("k_gain", KV_DIM), ("gla_b", GLA_KDIM),
              ("gla_gain", GLA_VDIM), ("conv_b", CONV_DIM), ("ln_gain", CONV_DIM), ("ln_bias", CONV_DIM),
              ("b_pw", CONV_DIM))
VEC_ROWS = 16
VEC_ROW_OF = {name: (k, n) for k, (name, n) in enumerate(VEC_LAYOUT)}

OUT_ROW_BLOCK = 256
PROMPT_TILE = 512
SAMPLE_SEQS = 4
VMEM_LIMIT_BYTES = 56 * 1024 * 1024

_BF = jnp.bfloat16
_F32 = jnp.float32


def _shape_constants(tile):
    rows = np.arange(ATTN_GROUP * CHUNK)
    dist = np.abs((rows % CHUNK)[:, None] + WINDOW - np.arange(KBAND)[None, :]).astype(np.float64)
    bias = np.stack([
        (2.0 ** (-8.0 * (j * ATTN_GROUP + rows // CHUNK + 1) / ATTN_HEADS))[:, None] * LOG2E * dist
        for j in range(ATTN_KV_HEADS)]).astype(np.float32)
    bias[:, :, BAND:] = -NEG_INF
    variants = []
    for n in range(WINDOW // CHUNK + 1):
        b = bias.copy()
        b[:, :, :max(WINDOW - n * CHUNK, 0)] = -NEG_INF
        variants.append(b)
    bias = np.stack(variants)
    tr = np.arange(tile)
    tri = ((tr[:, None] // CHUNK == tr[None, :] // CHUNK) & (tr[None, :] <= tr[:, None])).astype(np.float32)
    ln = np.arange(LANES) // HEAD_DIM
    ones = (ln[:, None] == ln[None, :]).astype(np.float32)
    return jnp.asarray(bias), jnp.asarray(tri, dtype=_BF), jnp.asarray(ones, dtype=_BF)


def _group_rms_scale(x, group, ones_bd):
    outs = []
    for c0 in range(0, x.shape[1], LANES):
        blk = x[:, c0:c0 + LANES]
        ss = _dot((blk * blk).astype(_BF), ones_bd)
        outs.append(blk * lax.rsqrt(ss * (1.0 / group) + NORM_EPS))
    return outs[0] if len(outs) == 1 else jnp.concatenate(outs, axis=1)


def _silu(x):
    return x * jax.nn.sigmoid(x)


def _dot(a, b):
    return jnp.dot(a, b, preferred_element_type=_F32)


def _dot_nt(a, b):
    return lax.dot_general(a, b, (((1,), (1,)), ((), ())), preferred_element_type=_F32)


def _dot_tn(a, b):
    return lax.dot_general(a, b, (((0,), (0,)), ((), ())), preferred_element_type=_F32)


def _stream_kernel(*refs, tile, n_seq, n_chunk, carry, q_pos0):
    if carry:
        (sinks_ref, x_ref, xnext_ref, *rest) = refs
    else:
        (sinks_ref, x_ref, xnext_ref, hk_all, hv_all, hs_all, hc_all, *rest) = refs
    (w_in_a, w_in_b, w_in_c, w_up_all, w_pw_all, w_out_all, vec_all, convw_all, bias_ref, tri_ref, ones_ref,
     y_ref, nk_all, nv_all, ns_all, nc_all,
     proj_ref, kd_all, vd_all, s_all, u_all, qlo_ref, qhi_ref, ycat_ref, xcur_ref, hb_ref, sbd_ref,
     gate_ref) = rest
    t = pl.program_id(1)
    layer = pl.program_id(2)
    seq_rows = n_chunk * CHUNK
    w_up_ref, w_pw_ref, w_out_ref = (r.at[layer] for r in (w_up_all, w_pw_all, w_out_all))
    convw_ref = convw_all.at[layer]

    def layer_row(name, which=layer):
        r, n = VEC_ROW_OF[name]
        return vec_all[which, r:r + 1, 0:n]

    q_gain = layer_row("q_gain") * (LOG2E * HEAD_DIM ** -0.5)
    k_gain = layer_row("k_gain")
    kd_ref, vd_ref, s_ref, u_ref = (r.at[layer] for r in (kd_all, vd_all, s_all, u_all))
    nk_ref, nv_ref, ns_ref, nc_ref = (r.at[layer] for r in (nk_all, nv_all, ns_all, nc_all))
    if not carry:
        hk_ref, hv_ref, hs_ref, hc_ref = (r.at[layer] for r in (hk_all, hv_all, hs_all, hc_all))

    lane128 = lax.broadcasted_iota(jnp.int32, (1, LANES), 1)
    lo_half = lane128 < HEAD_DIM

    def dup_halves(a):
        sw = pltpu.roll(a, HEAD_DIM, 1)
        return jnp.concatenate([jnp.where(lo_half, a, sw), jnp.where(lo_half, sw, a)], axis=1)

    buf_rows = WINDOW + seq_rows + CHUNK
    zeros_blk = jnp.zeros((buf_rows, LANES), _BF)
    ones_blk = jnp.ones((buf_rows, LANES), _BF)
    v_blank = jnp.concatenate([zeros_blk, ones_blk] * ATTN_KV_HEADS, axis=1)

    def store_v(s, row0, vdup_rows):
        for j in range(ATTN_KV_HEADS):
            vd_ref[s, row0:row0 + vdup_rows.shape[0], 2 * j * LANES:(2 * j + 1) * LANES] = (
                vdup_rows[:, j * LANES:(j + 1) * LANES])

    if carry:
        @pl.when(t == 0)
        def _():
            kd_ref[...] = jnp.zeros(kd_ref.shape, _BF)
            for s in range(n_seq):
                vd_ref[s] = v_blank
            s_ref[...] = jnp.zeros(s_ref.shape, _F32)
            u_ref[:, 0:CONV_PAD, :] = jnp.zeros((n_seq, CONV_PAD, CONV_DIM), _F32)
    else:
        for s in range(n_seq):
            u_ref[s, 0:SUBLANES, :] = jnp.zeros((SUBLANES, CONV_DIM), _F32)
            u_ref[s, CONV_PAD - CONV_HIST:CONV_PAD, :] = hc_ref[s]

    def normed_bf16(v, gain_row):
        ms = jnp.mean(v * v, axis=-1, keepdims=True)
        return (v * lax.rsqrt(ms + NORM_EPS) * gain_row).astype(_BF)

    @pl.when(layer == 0)
    def _():
        xcur_ref[...] = x_ref[...]

    @pl.when((pl.program_id(0) == 0) & (t == 0) & (layer == 0))
    def _():
        hb_ref[...] = normed_bf16(x_ref[...], layer_row("norm_gain"))
    x = xcur_ref[...]
    hb = hb_ref[...]

    anchor_on = sinks_ref[layer, ATTN_HEADS] != 0.0

    ones64 = ones_ref[...]

    def post_q(g):
        def post(res):
            for b in range(PROJ_GROUP // LANES):
                c0 = g * PROJ_GROUP + b * LANES
                qn = _group_rms_scale(res[:, b * LANES:(b + 1) * LANES], HEAD_DIM, ones64) * q_gain
                qlo_ref[:, c0:c0 + LANES] = jnp.where(lo_half, qn, 0.0).astype(_BF)
                qhi_ref[:, c0:c0 + LANES] = jnp.where(lo_half, 0.0, qn).astype(_BF)
        return post

    def post_kv(res):
        kn = _group_rms_scale(res[:, 0:KV_DIM], HEAD_DIM, ones64) * k_gain
        vv = res[:, KV_DIM:2 * KV_DIM]
        kdup = dup_halves(kn).astype(_BF)
        vdup = dup_halves(vv).astype(_BF)
        for s in range(n_seq):
            kd_ref[s, WINDOW:WINDOW + seq_rows, :] = kdup[s * seq_rows:(s + 1) * seq_rows]
            store_v(s, WINDOW, vdup[s * seq_rows:(s + 1) * seq_rows])
        if carry:
            nk_ref[0] = kn[tile - WINDOW:tile]
            nv_ref[0] = vv[tile - WINDOW:tile]
        else:
            for s in range(n_seq):
                nk_ref[s, 0:WINDOW - seq_rows, :] = hk_ref[s, seq_rows:WINDOW, :]
                nv_ref[s, 0:WINDOW - seq_rows, :] = hv_ref[s, seq_rows:WINDOW, :]
                nk_ref[s, WINDOW - seq_rows:WINDOW, :] = kn[s * seq_rows:(s + 1) * seq_rows]
                nv_ref[s, WINDOW - seq_rows:WINDOW, :] = vv[s * seq_rows:(s + 1) * seq_rows]

    def post_gate(col0):
        def post(res):
            gate_ref[:, col0:col0 + PROJ_GROUP] = _silu(res)
        return post

    group_post = {C_AG // PROJ_GROUP: post_gate(0), C_AG // PROJ_GROUP + 1: post_gate(PROJ_GROUP),
                  C_GG // PROJ_GROUP: post_gate(ATTN_DIM), C_CG // PROJ_GROUP: post_gate(ATTN_DIM + GLA_VDIM)}

    def w_in_cols(c0, c1):
        if c1 <= C_CV:
            return w_in_a[layer, :, c0:c1]
        if c1 <= C_GLR:
            return w_in_b[layer, :, c0 - C_CV:c1 - C_CV]
        return w_in_c[layer, :, c0 - C_GLR:c1 - C_GLR]

    def compute_group(g):
        c0, c1 = g * PROJ_GROUP, min((g + 1) * PROJ_GROUP, IN_COLS_PACKED)
        res = _dot(hb, w_in_cols(c0, c1))
        if g in group_post:
            group_post[g](res)
        else:
            proj_ref[:, c0:c1] = res
        return jnp.where(anchor_on, res[tile - 1:tile, 0:CONV_DIM], 0.0)

    def proj(c0, width):
        return proj_ref[:, c0:c0 + width]

    compute_group(C_CV // PROJ_GROUP)
    prev_anchor = compute_group(C_CGL // PROJ_GROUP)
    u = proj(C_CV, CONV_DIM) * jax.nn.sigmoid(proj(C_CGL, CONV_DIM))
    for s in range(n_seq):
        u_ref[s, CONV_PAD:CONV_PAD + seq_rows, :] = u[s * seq_rows:(s + 1) * seq_rows]
    conv_acc = [jnp.broadcast_to(layer_row("conv_b"), (seq_rows, CONV_DIM))] * n_seq
    for rho in range(SUBLANES):
        anchor, prev_anchor = prev_anchor, compute_group(rho)
        for s in range(n_seq):
            frame_rows = seq_rows + (SUBLANES if rho else 0)
            frame = None
            for j in range(CONV_WIDTH):
                off = CONV_PAD - CONV_HIST + j
                if off % SUBLANES != rho:
                    continue
                term = (convw_ref[j:j + 1, :] + anchor) * u_ref[s, off - rho:off - rho + frame_rows, :]
                frame = term if frame is None else frame + term
            conv_acc[s] = conv_acc[s] + frame[rho:rho + seq_rows]
    compute_group(C_CG // PROJ_GROUP)
    compute_group(C_GLR // PROJ_GROUP)
    cc = conv_acc[0] if n_seq == 1 else jnp.concatenate(conv_acc, axis=0)

    for s in range(n_seq):
        nc_ref[s] = u_ref[s, seq_rows + CONV_PAD - CONV_HIST:seq_rows + CONV_PAD, :]
    if carry:
        u_ref[:, 0:CONV_PAD, :] = u_ref[:, seq_rows:seq_rows + CONV_PAD, :]

    mu = jnp.mean(cc, axis=-1, keepdims=True)
    cen = cc - mu
    var = jnp.mean(cen * cen, axis=-1, keepdims=True)
    ln = cen * lax.rsqrt(var + NORM_EPS) * layer_row("ln_gain") + layer_row("ln_bias")
    cpw = _dot(_silu(ln).astype(_BF), w_pw_ref[...]) + layer_row("b_pw")
    ycat_ref[:, ATTN_DIM + GLA_VDIM:D_MODEL] = cpw * gate_ref[:, ATTN_DIM + GLA_VDIM:D_MODEL]

    if not carry:
        for s in range(n_seq):
            kd_ref[s, 0:WINDOW, :] = dup_halves(hk_ref[s]).astype(_BF)
            kd_ref[s, WINDOW + seq_rows:, :] = jnp.zeros((CHUNK, 2 * LANES), _BF)
            vd_ref[s] = v_blank
            store_v(s, 0, dup_halves(hv_ref[s]).astype(_BF))
            sbd_ref[...] = jnp.zeros((GLA_KDIM, GLA_VDIM), _F32)
            for h in range(GLA_HEADS):
                sbd_ref[h * GLA_DK:(h + 1) * GLA_DK, h * GLA_DV:(h + 1) * GLA_DV] = hs_ref[s, h]
            s_ref[s] = sbd_ref[...].T

    z = _dot(proj(C_GLR, LANES).astype(_BF), w_up_ref[...]) + layer_row("gla_b")
    log_a = (jnp.minimum(z, 0.0) - jnp.log(1.0 + jnp.exp(-jnp.abs(z)))) * (1.0 / GLA_TAU)
    tri = tri_ref[...]
    la_hi = log_a.astype(_BF)
    la_lo = (log_a - la_hi.astype(_F32)).astype(_BF)
    bcum = _dot(tri, la_hi) + _dot(tri, la_lo)
    gq = proj(C_GQ, GLA_KDIM) * GLA_DK ** -0.5
    gk = proj(C_GK, GLA_KDIM)
    gv = proj(C_GV, GLA_VDIM)
    gate_g = gate_ref[:, ATTN_DIM:ATTN_DIM + GLA_VDIM]
    gla_g = layer_row("gla_gain")

    for g in range(ATTN_DIM // PROJ_GROUP):
        post_q(g)(proj(C_AQ + g * PROJ_GROUP, PROJ_GROUP))
    post_kv(proj(C_AK, 2 * KV_DIM))

    head_of_lane = lane128 >> GLA_DK_SHIFT
    a_r = lax.broadcasted_iota(jnp.int32, (CHUNK, GLA_HEADS * CHUNK), 0)
    a_c = lax.broadcasted_iota(jnp.int32, (CHUNK, GLA_HEADS * CHUNK), 1)
    causal = (a_c & (CHUNK - 1)) <= a_r
    vblk = lax.broadcasted_iota(jnp.int32, (1, GLA_VDIM), 1) >> GLA_DV_SHIFT
    bd_mask_t = ((lax.broadcasted_iota(jnp.int32, (GLA_VDIM, GLA_KDIM), 0) >> GLA_DV_SHIFT)
                 == (lax.broadcasted_iota(jnp.int32, (GLA_VDIM, GLA_KDIM), 1) >> GLA_DK_SHIFT))
    q_all = (gq * jnp.exp(bcum)).astype(_BF)
    k_all = gk * jnp.exp(-bcum)
    intra, kv_upd, decay_rows = [], [], []
    for s in range(n_seq):
        for c in range(n_chunk):
            r0 = s * seq_rows + c * CHUNK
            bc = bcum[r0:r0 + CHUNK]
            b_last = bc[CHUNK - 1:CHUNK]
            k_end = (gk[r0:r0 + CHUNK] * jnp.exp(b_last - bc)).astype(_BF)
            vf = gv[r0:r0 + CHUNK]
            k_stack = jnp.concatenate(
                [jnp.where(head_of_lane == h, k_all[r0:r0 + CHUNK], 0.0) for h in range(GLA_HEADS)],
                axis=0).astype(_BF)
            v_bd = jnp.concatenate(
                [jnp.where(vblk == h, vf, 0.0) for h in range(GLA_HEADS)], axis=0).astype(_BF)
            a = jnp.where(causal, _dot_nt(q_all[r0:r0 + CHUNK], k_stack), 0.0).astype(_BF)
            intra.append(_dot(a, v_bd))
            kv_upd.append(jnp.where(bd_mask_t, _dot_tn(vf.astype(_BF), k_end), 0.0))
            decay_rows.append(jnp.exp(b_last))

    rows = ATTN_GROUP * CHUNK
    r_head1 = lax.broadcasted_iota(jnp.int32, (rows, 1), 0) >> CHUNK_SHIFT
    sink_cols = []
    for j in range(ATTN_KV_HEADS):
        sink = jnp.zeros((rows, 1), _F32)
        for r in range(ATTN_GROUP):
            sink = jnp.where(r_head1 == r, sinks_ref[layer, j * ATTN_GROUP + r], sink)
        sink_cols.append(sink * LOG2E)

    o_rows = []

    def gla_state_steps():
        for s in range(n_seq):
            state = s_ref[s]
            for c in range(n_chunk):
                i = s * n_chunk + c
                r0 = i * CHUNK
                o_rows.append(intra[i] + _dot_nt(q_all[r0:r0 + CHUNK], state.astype(_BF)))
                state = decay_rows[i] * state + kv_upd[i]
            s_ref[s] = state

    n_blocks = n_seq * n_chunk
    gla_steps_at = (3 * n_blocks) // 4 if carry else n_blocks
    for s in range(n_seq):
        for c in range(n_chunk):
            if s * n_chunk + c == gla_steps_at:
                gla_state_steps()
            r0 = s * seq_rows + c * CHUNK
            k0 = c * CHUNK
            steady = WINDOW // CHUNK
            variant = jnp.minimum(t * n_chunk + c, steady) if carry else steady
            for j in range(ATTN_KV_HEADS):
                qs = jnp.concatenate(
                    [qlo_ref[r0:r0 + CHUNK, (2 * j) * LANES:(2 * j + 1) * LANES],
                     qhi_ref[r0:r0 + CHUNK, (2 * j) * LANES:(2 * j + 1) * LANES],
                     qlo_ref[r0:r0 + CHUNK, (2 * j + 1) * LANES:(2 * j + 2) * LANES],
                     qhi_ref[r0:r0 + CHUNK, (2 * j + 1) * LANES:(2 * j + 2) * LANES]], axis=0)
                kb = kd_ref[s, k0:k0 + KBAND, j * LANES:(j + 1) * LANES]
                vb = vd_ref[s, k0:k0 + KBAND, 2 * j * LANES:(2 * j + 2) * LANES]
                qk = _dot_nt(qs, kb)
                sink = sink_cols[j]
                p_blocks, m_blocks = [], []
                for r in range(ATTN_GROUP):
                    sc = qk[r * CHUNK:(r + 1) * CHUNK] - bias_ref[variant, j, r * CHUNK:(r + 1) * CHUNK, :]
                    m_r = jnp.maximum(jnp.max(sc, axis=-1, keepdims=True), sink[r * CHUNK:(r + 1) * CHUNK])
                    p_blocks.append(jnp.exp2(sc - m_r).astype(_BF))
                    m_blocks.append(m_r)
                m = jnp.concatenate(m_blocks, axis=0)
                pv = _dot(jnp.concatenate(p_blocks, axis=0), vb)
                o = pv[:, 0:LANES] / (pv[:, LANES:2 * LANES] + jnp.exp2(sink - m))
                for pb in range(2):
                    blk = 2 * j + pb
                    ob = jnp.where(lo_half, o[(2 * pb) * CHUNK:(2 * pb + 1) * CHUNK],
                                   o[(2 * pb + 1) * CHUNK:(2 * pb + 2) * CHUNK])
                    ycat_ref[r0:r0 + CHUNK, blk * LANES:(blk + 1) * LANES] = (
                        ob * gate_ref[r0:r0 + CHUNK, blk * LANES:(blk + 1) * LANES])

    if gla_steps_at == n_blocks:
        gla_state_steps()

    if carry:
        kd_ref[:, 0:WINDOW, :] = kd_ref[:, seq_rows:seq_rows + WINDOW, :]
        vd_ref[:, 0:WINDOW, :] = vd_ref[:, seq_rows:seq_rows + WINDOW, :]

    o_all = jnp.concatenate(o_rows, axis=0)
    ycat_ref[:, ATTN_DIM:ATTN_DIM + GLA_VDIM] = _group_rms_scale(o_all, GLA_DV, ones64) * gla_g * gate_g

    for s in range(n_seq):
        sbd_ref[...] = s_ref[s].T
        for h in range(GLA_HEADS):
            ns_ref[s, h] = sbd_ref[h * GLA_DK:(h + 1) * GLA_DK, h * GLA_DV:(h + 1) * GLA_DV]

    last_layer = layer == pl.num_programs(2) - 1
    next_gain = layer_row("norm_gain", jnp.where(last_layer, 0, layer + 1))
    for r0 in range(0, tile, OUT_ROW_BLOCK):
        r1 = r0 + OUT_ROW_BLOCK
        y = x[r0:r1] + _dot(ycat_ref[r0:r1, :].astype(_BF), w_out_ref[...])
        y_ref[r0:r1, :] = y
        xcur_ref[r0:r1, :] = y
        hb_ref[r0:r1, :] = normed_bf16(jnp.where(last_layer, xnext_ref[r0:r1, :], y), next_gain)


def _run_stream(x2d, n_seqs, seq_len, hist, sinks, params, *, carry, q_pos0):
    depth = sinks.shape[0]
    if carry:
        assert q_pos0 == 0
        tile, n_seq, n_chunk = PROMPT_TILE, 1, PROMPT_TILE // CHUNK
        grid = (n_seqs, seq_len // tile, depth)
    else:
        assert seq_len == CHUNK and q_pos0 >= WINDOW
        tile, n_seq, n_chunk = SAMPLE_SEQS * CHUNK, SAMPLE_SEQS, 1
        grid = (n_seqs // n_seq, 1, depth)
    n_t = grid[1]
    seq_rows = n_chunk * CHUNK

    def const(shape):
        return pl.BlockSpec(shape, lambda g, t, l: (0,) * len(shape), pipeline_mode=pl.Buffered(1))

    def per_layer_group(shape):
        return pl.BlockSpec((depth, n_seq) + shape, lambda g, t, l: (0, g) + (0,) * len(shape))

    state_shapes = ((WINDOW, KV_DIM), (WINDOW, KV_DIM), (GLA_HEADS, GLA_DK, GLA_DV), (CONV_HIST, CONV_DIM))
    x_spec = pl.BlockSpec((tile, D_MODEL), lambda g, t, l: (g * n_t + t, 0))
    n_tiles = x2d.shape[0] // tile
    xnext_spec = pl.BlockSpec((tile, D_MODEL), lambda g, t, l: (jnp.minimum(g * n_t + t + 1, n_tiles - 1), 0))
    in_specs = [pl.BlockSpec(memory_space=pltpu.SMEM), x_spec, xnext_spec]
    args = [sinks, x2d, x2d]
    if not carry:
        in_specs += [per_layer_group(sh) for sh in state_shapes]
        args += list(hist)
    operands = tuple(params) + _shape_constants(tile)
    in_specs += [const(p.shape) for p in operands]
    in_specs[len(args)] = const(operands[0].shape[:2] + (GLR_SRC,))
    args += list(operands)

    out_shape = (jax.ShapeDtypeStruct(x2d.shape, _F32),) + tuple(
        jax.ShapeDtypeStruct((depth, n_seqs) + sh, _F32) for sh in state_shapes)
    out_specs = (x_spec,) + tuple(per_layer_group(sh) for sh in state_shapes)
    scratch = [pltpu.VMEM((tile, IN_COLS_PACKED), _F32),
               pltpu.VMEM((depth, n_seq, WINDOW + seq_rows + CHUNK, ATTN_KV_HEADS * LANES), _BF),
               pltpu.VMEM((depth, n_seq, WINDOW + seq_rows + CHUNK, ATTN_KV_HEADS * 2 * LANES), _BF),
               pltpu.VMEM((depth, n_seq, GLA_VDIM, GLA_KDIM), _F32),
               pltpu.VMEM((depth, n_seq, CONV_PAD + seq_rows, CONV_DIM), _F32),
               pltpu.VMEM((tile, ATTN_DIM), _BF),
               pltpu.VMEM((tile, ATTN_DIM), _BF),
               pltpu.VMEM((tile, D_MODEL), _F32),
               pltpu.VMEM((tile, D_MODEL), _F32),
               pltpu.VMEM((tile, D_MODEL), _BF),
               pltpu.VMEM((GLA_KDIM, GLA_VDIM), _F32),
               pltpu.VMEM((tile, D_MODEL), _F32)]
    body = functools.partial(_stream_kernel, tile=tile, n_seq=n_seq, n_chunk=n_chunk, carry=carry, q_pos0=q_pos0)
    return pl.pallas_call(
        body, grid=grid, in_specs=in_specs, out_specs=out_specs, out_shape=out_shape,
        scratch_shapes=scratch,
        compiler_params=pltpu.CompilerParams(dimension_semantics=("arbitrary", "arbitrary", "arbitrary"),
                                             vmem_limit_bytes=VMEM_LIMIT_BYTES),
        name="mixer_prompt" if carry else "mixer_sample",
    )(*args)


def _pack_params(norm_gain, w_in, q_norm_gain, k_norm_gain, gla_w_gate_up, gla_b_gate, gla_norm_gain,
                 conv_w, conv_b, conv_ln_gain, conv_ln_bias, conv_w_pw, conv_b_pw, w_out):
    w_in_a = w_in.astype(_BF)
    w_in_c = jnp.pad(w_in_a[:, :, GLR_SRC:GLR_SRC + GLA_LOWRANK], ((0, 0), (0, 0), (0, LANES - GLA_LOWRANK)))
    w_in_b = w_in_a[:, :, GLR_SRC + GLA_LOWRANK:]
    pieces = dict(norm_gain=norm_gain, q_gain=q_norm_gain, k_gain=k_norm_gain, gla_b=gla_b_gate,
                  gla_gain=gla_norm_gain, conv_b=conv_b, ln_gain=conv_ln_gain, ln_bias=conv_ln_bias, b_pw=conv_b_pw)
    depth = w_in.shape[0]
    parts = []
    for name, n in VEC_LAYOUT:
        parts += [pieces[name].astype(_F32)] * (n // pieces[name].shape[1])
        parts.append(jnp.zeros((depth, D_MODEL - n), _F32))
    parts.append(jnp.zeros((depth, (VEC_ROWS - len(VEC_LAYOUT)) * D_MODEL), _F32))
    vecs = jnp.concatenate(parts, axis=1).reshape(depth, VEC_ROWS, D_MODEL)
    w_up_p = jnp.pad(gla_w_gate_up, ((0, 0), (0, LANES - GLA_LOWRANK), (0, 0))).astype(_BF)
    convw = jnp.pad(conv_w.astype(_F32), ((0, 0), (0, CONV_PAD - CONV_WIDTH), (0, 0)))
    return w_in_a, w_in_b, w_in_c, w_up_p, conv_w_pw.astype(_BF), w_out.astype(_BF), vecs, convw


def kernel(x_prompt, x_sample, cache_k, cache_v, state_gla, state_conv, norm_gain, w_in, q_norm_gain, k_norm_gain, attn_sinks, gla_w_gate_up, gla_b_gate, gla_norm_gain, conv_w, conv_b, conv_ln_gain, conv_ln_bias, conv_w_pw, conv_b_pw, w_out):
    depth = w_in.shape[0]
    bp, lp, _ = x_prompt.shape
    bs, ls, _ = x_sample.shape
    params = _pack_params(norm_gain, w_in, q_norm_gain, k_norm_gain, gla_w_gate_up, gla_b_gate, gla_norm_gain,
                          conv_w, conv_b, conv_ln_gain, conv_ln_bias, conv_w_pw, conv_b_pw, w_out)
    sinks = jnp.pad(attn_sinks.astype(_F32), ((0, 0), (0, ATTN_HEADS)))
    yp, pk, pv, ps, pc = _run_stream(x_prompt.reshape(bp * lp, D_MODEL), bp, lp, None, sinks, params,
                                     carry=True, q_pos0=0)
    hist = (cache_k.reshape(depth, bs, WINDOW, KV_DIM), cache_v.reshape(depth, bs, WINDOW, KV_DIM),
            state_gla, state_conv)
    ys, sk, sv, ss, sc = _run_stream(x_sample.reshape(bs * ls, D_MODEL), bs, ls, hist, sinks, params,
                                     carry=False, q_pos0=PAST_LEN)
    kv_shape = (WINDOW, ATTN_KV_HEADS, HEAD_DIM)
    return (yp.reshape(bp, lp, D_MODEL), ys.reshape(bs, ls, D_MODEL),
            pk.reshape((depth, bp) + kv_shape), pv.reshape((depth, bp) + kv_shape), ps, pc,
            sk.reshape((depth, bs) + kv_shape), sv.reshape((depth, bs) + kv_shape), ss, sc)
```

```python
import functools
import math

import numpy as np
import jax
import jax.numpy as jnp
from jax import lax
from jax.experimental import pallas as pl
from jax.experimental.pallas import tpu as pltpu

D_MODEL = 1024
CHUNK = 64
ATTN_HEADS = 8
ATTN_KV_HEADS = 2
HEAD_DIM = 64
ATTN_GROUP = ATTN_HEADS // ATTN_KV_HEADS
ATTN_DIM = ATTN_HEADS * HEAD_DIM
KV_DIM = ATTN_KV_HEADS * HEAD_DIM
WINDOW = 128
BAND = WINDOW + CHUNK
KBAND = WINDOW + 2 * CHUNK
GLA_HEADS = 4
GLA_DK = 32
GLA_DV = 64
GLA_KDIM = GLA_HEADS * GLA_DK
GLA_VDIM = GLA_HEADS * GLA_DV
GLA_LOWRANK = 16
GLA_TAU = 16.0
CONV_DIM = 256
CONV_WIDTH = 31
CONV_HIST = CONV_WIDTH - 1
NORM_EPS = 1e-6
NEG_INF = -1e30
PAST_LEN = 4096
LOG2E = math.log2(math.e)

LANES = 128
SUBLANES = 8
CHUNK_SHIFT = CHUNK.bit_length() - 1
GLA_DK_SHIFT = GLA_DK.bit_length() - 1
GLA_DV_SHIFT = GLA_DV.bit_length() - 1
C_AQ, C_AK, C_AV, C_AG = 0, 512, 640, 768
C_GQ, C_GK, C_GV, C_GG = 1280, 1408, 1536, 1792
C_CV, C_CGL, C_CG, C_GLR = 2048, 2304, 2560, 2816
IN_COLS_PACKED = C_GLR + LANES
PROJ_GROUP = 256
CONV_PAD = 32
GLR_SRC = 2048
VEC_LAYOUT = (("norm_gain", D_MODEL), ("q_gain", LANES), ("k_gain", KV_DIM), ("gla_b", GLA_KDIM),
              ("gla_gain", GLA_VDIM), ("conv_b", CONV_DIM), ("ln_gain", CONV_DIM), ("ln_bias", CONV_DIM),
              ("b_pw", CONV_DIM))
VEC_ROWS = 16
VEC_ROW_OF = {name: (k, n) for k, (name, n) in enumerate(VEC_LAYOUT)}

OUT_ROW_BLOCK = 256
PROMPT_TILE = 512
SAMPLE_SEQS = 4
VMEM_LIMIT_BYTES = 56 * 1024 * 1024

_BF = jnp.bfloat16
_F32 = jnp.float32


def _shape_constants(tile):
    rows = np.arange(ATTN_GROUP * CHUNK)
    dist = np.abs((rows % CHUNK)[:, None] + WINDOW - np.arange(KBAND)[None, :]).astype(np.float64)
    bias = np.stack([
        (2.0 ** (-8.0 * (j * ATTN_GROUP + rows // CHUNK + 1) / ATTN_HEADS))[:, None] * LOG2E * dist
        for j in range(ATTN_KV_HEADS)]).astype(np.float32)
    bias[:, :, BAND:] = -NEG_INF
    variants = []
    for n in range(WINDOW // CHUNK + 1):
        b = bias.copy()
        b[:, :, :max(WINDOW - n * CHUNK, 0)] = -NEG_INF
        variants.append(b)
    bias = np.stack(variants)
    tr = np.arange(tile)
    tri = ((tr[:, None] // CHUNK == tr[None, :] // CHUNK) & (tr[None, :] <= tr[:, None])).astype(np.float32)
    ln = np.arange(LANES) // HEAD_DIM
    ones = (ln[:, None] == ln[None, :]).astype(np.float32)
    return jnp.asarray(bias), jnp.asarray(tri, dtype=_BF), jnp.asarray(ones, dtype=_BF)


def _group_rms_scale(x, group, ones_bd):
    outs = []
    for c0 in range(0, x.shape[1], LANES):
        blk = x[:, c0:c0 + LANES]
        ss = _dot((blk * blk).astype(_BF), ones_bd)
        outs.append(blk * lax.rsqrt(ss * (1.0 / group) + NORM_EPS))
    return outs[0] if len(outs) == 1 else jnp.concatenate(outs, axis=1)


def _silu(x):
    return x * jax.nn.sigmoid(x)


def _dot(a, b):
    return jnp.dot(a, b, preferred_element_type=_F32)


def _dot_nt(a, b):
    return lax.dot_general(a, b, (((1,), (1,)), ((), ())), preferred_element_type=_F32)


def _dot_tn(a, b):
    return lax.dot_general(a, b, (((0,), (0,)), ((), ())), preferred_element_type=_F32)


def _stream_kernel(*refs, tile, n_seq, n_chunk, carry, q_pos0):
    if carry:
        (sinks_ref, x_ref, xnext_ref, *rest) = refs
    else:
        (sinks_ref, x_ref, xnext_ref, hk_all, hv_all, hs_all, hc_all, *rest) = refs
    (w_in_a, w_in_b, w_in_c, w_up_all, w_pw_all, w_out_all, vec_all, convw_all, bias_ref, tri_ref, ones_ref,
     y_ref, nk_all, nv_all, ns_all, nc_all,
     kd_all, vd_all, s_all, u_all, qlo_ref, qhi_ref, ycat_ref, proj_ref, xcur_ref, hb_ref, sbd_ref,
     gate_ref, wgate_ref) = rest
    t = pl.program_id(1)
    layer = pl.program_id(2)
    seq_rows = n_chunk * CHUNK
    w_up_ref, w_pw_ref, w_out_ref = (r.at[layer] for r in (w_up_all, w_pw_all, w_out_all))
    convw_ref = convw_all.at[layer]

    def layer_row(name, which=layer):
        r, n = VEC_ROW_OF[name]
        return vec_all[which, r:r + 1, 0:n]

    q_gain = layer_row("q_gain") * (LOG2E * HEAD_DIM ** -0.5)
    k_gain = layer_row("k_gain")
    kd_ref, vd_ref, s_ref, u_ref = (r.at[layer] for r in (kd_all, vd_all, s_all, u_all))
    nk_ref, nv_ref, ns_ref, nc_ref = (r.at[layer] for r in (nk_all, nv_all, ns_all, nc_all))
    if not carry:
        hk_ref, hv_ref, hs_ref, hc_ref = (r.at[layer] for r in (hk_all, hv_all, hs_all, hc_all))

    lane128 = lax.broadcasted_iota(jnp.int32, (1, LANES), 1)
    lo_half = lane128 < HEAD_DIM

    def dup_halves(a):
        sw = pltpu.roll(a, HEAD_DIM, 1)
        return jnp.concatenate([jnp.where(lo_half, a, sw), jnp.where(lo_half, sw, a)], axis=1)

    buf_rows = WINDOW + seq_rows + CHUNK
    zeros_blk = jnp.zeros((buf_rows, LANES), _BF)
    ones_blk = jnp.ones((buf_rows, LANES), _BF)
    v_blank = jnp.concatenate([zeros_blk, ones_blk] * ATTN_KV_HEADS, axis=1)

    def store_v(s, row0, vdup_rows):
        for j in range(ATTN_KV_HEADS):
            vd_ref[s, row0:row0 + vdup_rows.shape[0], 2 * j * LANES:(2 * j + 1) * LANES] = (
                vdup_rows[:, j * LANES:(j + 1) * LANES])

    if carry:
        @pl.when(t == 0)
        def _():
            kd_ref[...] = jnp.zeros(kd_ref.shape, _BF)
            for s in range(n_seq):
                vd_ref[s] = v_blank
            s_ref[...] = jnp.zeros(s_ref.shape, _F32)
            u_ref[:, 0:CONV_PAD, :] = jnp.zeros((n_seq, CONV_PAD, CONV_DIM), _F32)
    else:
        for s in range(n_seq):
            u_ref[s, 0:SUBLANES, :] = jnp.zeros((SUBLANES, CONV_DIM), _F32)
            u_ref[s, CONV_PAD - CONV_HIST:CONV_PAD, :] = hc_ref[s]

    def normed_bf16(v, gain_row):
        ms = jnp.mean(v * v, axis=-1, keepdims=True)
        return (v * lax.rsqrt(ms + NORM_EPS) * gain_row).astype(_BF)

    @pl.when(layer == 0)
    def _():
        xcur_ref[...] = x_ref[...]

    @pl.when((pl.program_id(0) == 0) & (t == 0) & (layer == 0))
    def _():
        hb_ref[...] = normed_bf16(x_ref[...], layer_row("norm_gain"))
        for l in range(wgate_ref.shape[0]):
            wgate_ref[l] = _dot(w_in_c[l], w_up_all[l]).astype(_BF)
    x = xcur_ref[...]
    hb = hb_ref[...]

    anchor_on = sinks_ref[layer, ATTN_HEADS] != 0.0

    ones64 = ones_ref[...]

    def post_q(g):
        def post(res):
            for b in range(PROJ_GROUP // LANES):
                c0 = g * PROJ_GROUP + b * LANES
                qn = _group_rms_scale(res[:, b * LANES:(b + 1) * LANES], HEAD_DIM, ones64) * q_gain
                qlo_ref[:, c0:c0 + LANES] = jnp.where(lo_half, qn, 0.0).astype(_BF)
                qhi_ref[:, c0:c0 + LANES] = jnp.where(lo_half, 0.0, qn).astype(_BF)
        return post

    def post_kv(res):
        kn = _group_rms_scale(res[:, 0:KV_DIM], HEAD_DIM, ones64) * k_gain
        vv = res[:, KV_DIM:2 * KV_DIM]
        kdup = dup_halves(kn).astype(_BF)
        vdup = dup_halves(vv).astype(_BF)
        for s in range(n_seq):
            kd_ref[s, WINDOW:WINDOW + seq_rows, :] = kdup[s * seq_rows:(s + 1) * seq_rows]
            store_v(s, WINDOW, vdup[s * seq_rows:(s + 1) * seq_rows])
        if carry:
            nk_ref[0] = kn[tile - WINDOW:tile]
            nv_ref[0] = vv[tile - WINDOW:tile]
        else:
            for s in range(n_seq):
                nk_ref[s, 0:WINDOW - seq_rows, :] = hk_ref[s, seq_rows:WINDOW, :]
                nv_ref[s, 0:WINDOW - seq_rows, :] = hv_ref[s, seq_rows:WINDOW, :]
                nk_ref[s, WINDOW - seq_rows:WINDOW, :] = kn[s * seq_rows:(s + 1) * seq_rows]
                nv_ref[s, WINDOW - seq_rows:WINDOW, :] = vv[s * seq_rows:(s + 1) * seq_rows]

    def post_gate(col0):
        def post(res):
            gate_ref[:, col0:col0 + PROJ_GROUP] = _silu(res)
        return post

    group_post = {C_AG // PROJ_GROUP: post_gate(0), C_AG // PROJ_GROUP + 1: post_gate(PROJ_GROUP),
                  C_GG // PROJ_GROUP: post_gate(ATTN_DIM), C_CG // PROJ_GROUP: post_gate(ATTN_DIM + GLA_VDIM)}

    def w_in_cols(c0, c1):
        if c1 <= C_CV:
            return w_in_a[layer, :, c0:c1]
        if c1 <= C_GLR:
            return w_in_b[layer, :, c0 - C_CV:c1 - C_CV]
        return wgate_ref[layer]

    def compute_group(g):
        c0, c1 = g * PROJ_GROUP, min((g + 1) * PROJ_GROUP, IN_COLS_PACKED)
        res = _dot(hb, w_in_cols(c0, c1))
        if g in group_post:
            group_post[g](res)
        else:
            proj_ref[:, c0:c1] = res
        return jnp.where(anchor_on, res[tile - 1:tile, 0:CONV_DIM], 0.0)

    def proj(c0, width):
        return proj_ref[:, c0:c0 + width]

    compute_group(C_CV // PROJ_GROUP)
    prev_anchor = compute_group(C_CGL // PROJ_GROUP)
    u = proj(C_CV, CONV_DIM) * jax.nn.sigmoid(proj(C_CGL, CONV_DIM))
    for s in range(n_seq):
        u_ref[s, CONV_PAD:CONV_PAD + seq_rows, :] = u[s * seq_rows:(s + 1) * seq_rows]
    conv_acc = [jnp.broadcast_to(layer_row("conv_b"), (seq_rows, CONV_DIM))] * n_seq
    for rho in range(SUBLANES):
        anchor, prev_anchor = prev_anchor, compute_group(rho)
        for s in range(n_seq):
            frame_rows = seq_rows + (SUBLANES if rho else 0)
            frame = None
            for j in range(CONV_WIDTH):
                off = CONV_PAD - CONV_HIST + j
                if off % SUBLANES != rho:
                    continue
                term = (convw_ref[j:j + 1, :] + anchor) * u_ref[s, off - rho:off - rho + frame_rows, :]
                frame = term if frame is None else frame + term
            conv_acc[s] = conv_acc[s] + frame[rho:rho + seq_rows]
    compute_group(C_CG // PROJ_GROUP)
    compute_group(C_GLR // PROJ_GROUP)
    cc = conv_acc[0] if n_seq == 1 else jnp.concatenate(conv_acc, axis=0)

    for s in range(n_seq):
        nc_ref[s] = u_ref[s, seq_rows + CONV_PAD - CONV_HIST:seq_rows + CONV_PAD, :]
    if carry:
        u_ref[:, 0:CONV_PAD, :] = u_ref[:, seq_rows:seq_rows + CONV_PAD, :]

    mu = jnp.mean(cc, axis=-1, keepdims=True)
    cen = cc - mu
    var = jnp.mean(cen * cen, axis=-1, keepdims=True)
    ln = cen * lax.rsqrt(var + NORM_EPS) * layer_row("ln_gain") + layer_row("ln_bias")
    cpw = _dot(_silu(ln).astype(_BF), w_pw_ref[...]) + layer_row("b_pw")
    ycat_ref[:, ATTN_DIM + GLA_VDIM:D_MODEL] = cpw * gate_ref[:, ATTN_DIM + GLA_VDIM:D_MODEL]

    if not carry:
        for s in range(n_seq):
            kd_ref[s, 0:WINDOW, :] = dup_halves(hk_ref[s]).astype(_BF)
            kd_ref[s, WINDOW + seq_rows:, :] = jnp.zeros((CHUNK, 2 * LANES), _BF)
            vd_ref[s] = v_blank
            store_v(s, 0, dup_halves(hv_ref[s]).astype(_BF))
            sbd_ref[...] = jnp.zeros((GLA_KDIM, GLA_VDIM), _F32)
            for h in range(GLA_HEADS):
                sbd_ref[h * GLA_DK:(h + 1) * GLA_DK, h * GLA_DV:(h + 1) * GLA_DV] = hs_ref[s, h]
            s_ref[s] = sbd_ref[...].T

    z = proj(C_GLR, LANES) + layer_row("gla_b")
    log_a = (jnp.minimum(z, 0.0) - jnp.log(1.0 + jnp.exp(-jnp.abs(z)))) * (1.0 / GLA_TAU)
    tri = tri_ref[...]
    la_hi = log_a.astype(_BF)
    la_lo = (log_a - la_hi.astype(_F32)).astype(_BF)
    bcum = _dot(tri, la_hi) + _dot(tri, la_lo)
    gq = proj(C_GQ, GLA_KDIM) * GLA_DK ** -0.5
    gk = proj(C_GK, GLA_KDIM)
    gv = proj(C_GV, GLA_VDIM)
    gate_g = gate_ref[:, ATTN_DIM:ATTN_DIM + GLA_VDIM]
    gla_g = layer_row("gla_gain")

    for g in range(ATTN_DIM // PROJ_GROUP):
        post_q(g)(proj(C_AQ + g * PROJ_GROUP, PROJ_GROUP))
    post_kv(proj(C_AK, 2 * KV_DIM))

    head_of_lane = lane128 >> GLA_DK_SHIFT
    a_r = lax.broadcasted_iota(jnp.int32, (CHUNK, GLA_HEADS * CHUNK), 0)
    a_c = lax.broadcasted_iota(jnp.int32, (CHUNK, GLA_HEADS * CHUNK), 1)
    causal = (a_c & (CHUNK - 1)) <= a_r
    vblk = lax.broadcasted_iota(jnp.int32, (1, GLA_VDIM), 1) >> GLA_DV_SHIFT
    bd_mask_t = ((lax.broadcasted_iota(jnp.int32, (GLA_VDIM, GLA_KDIM), 0) >> GLA_DV_SHIFT)
                 == (lax.broadcasted_iota(jnp.int32, (GLA_VDIM, GLA_KDIM), 1) >> GLA_DK_SHIFT))
    q_all = (gq * jnp.exp(bcum)).astype(_BF)
    k_all = gk * jnp.exp(-bcum)
    intra, kv_upd, decay_rows = [], [], []
    for s in range(n_seq):
        for c in range(n_chunk):
            r0 = s * seq_rows + c * CHUNK
            bc = bcum[r0:r0 + CHUNK]
            b_last = bc[CHUNK - 1:CHUNK]
            k_end = (gk[r0:r0 + CHUNK] * jnp.exp(b_last - bc)).astype(_BF)
            vf = gv[r0:r0 + CHUNK]
            k_stack = jnp.concatenate(
                [jnp.where(head_of_lane == h, k_all[r0:r0 + CHUNK], 0.0) for h in range(GLA_HEADS)],
                axis=0).astype(_BF)
            v_bd = jnp.concatenate(
                [jnp.where(vblk == h, vf, 0.0) for h in range(GLA_HEADS)], axis=0).astype(_BF)
            a = jnp.where(causal, _dot_nt(q_all[r0:r0 + CHUNK], k_stack), 0.0).astype(_BF)
            intra.append(_dot(a, v_bd))
            kv_upd.append(jnp.where(bd_mask_t, _dot_tn(vf.astype(_BF), k_end), 0.0))
            decay_rows.append(jnp.exp(b_last))

    rows = ATTN_GROUP * CHUNK
    r_head1 = lax.broadcasted_iota(jnp.int32, (rows, 1), 0) >> CHUNK_SHIFT
    sink_cols = []
    for j in range(ATTN_KV_HEADS):
        sink = jnp.zeros((rows, 1), _F32)
        for r in range(ATTN_GROUP):
            sink = jnp.where(r_head1 == r, sinks_ref[layer, j * ATTN_GROUP + r], sink)
        sink_cols.append(sink * LOG2E)

    o_rows = []

    def gla_state_steps():
        for s in range(n_seq):
            state = s_ref[s]
            for c in range(n_chunk):
                i = s * n_chunk + c
                r0 = i * CHUNK
                o_rows.append(intra[i] + _dot_nt(q_all[r0:r0 + CHUNK], state.astype(_BF)))
                state = decay_rows[i] * state + kv_upd[i]
            s_ref[s] = state

    n_blocks = n_seq * n_chunk
    gla_steps_at = (3 * n_blocks) // 4 if carry else n_blocks
    for s in range(n_seq):
        for c in range(n_chunk):
            if s * n_chunk + c == gla_steps_at:
                gla_state_steps()
            r0 = s * seq_rows + c * CHUNK
            k0 = c * CHUNK
            steady = WINDOW // CHUNK
            variant = jnp.minimum(t * n_chunk + c, steady) if carry else steady
            for j in range(ATTN_KV_HEADS):
                qs = jnp.concatenate(
                    [qlo_ref[r0:r0 + CHUNK, (2 * j) * LANES:(2 * j + 1) * LANES],
                     qhi_ref[r0:r0 + CHUNK, (2 * j) * LANES:(2 * j + 1) * LANES],
                     qlo_ref[r0:r0 + CHUNK, (2 * j + 1) * LANES:(2 * j + 2) * LANES],
                     qhi_ref[r0:r0 + CHUNK, (2 * j + 1) * LANES:(2 * j + 2) * LANES]], axis=0)
                kb = kd_ref[s, k0:k0 + KBAND, j * LANES:(j + 1) * LANES]
                vb = vd_ref[s, k0:k0 + KBAND, 2 * j * LANES:(2 * j + 2) * LANES]
                qk = _dot_nt(qs, kb)
                sink = sink_cols[j]
                p_blocks, m_blocks = [], []
                for r in range(ATTN_GROUP):
                    sc = qk[r * CHUNK:(r + 1) * CHUNK] - bias_ref[variant, j, r * CHUNK:(r + 1) * CHUNK, :]
                    m_r = jnp.maximum(jnp.max(sc, axis=-1, keepdims=True), sink[r * CHUNK:(r + 1) * CHUNK])
                    p_blocks.append(jnp.exp2(sc - m_r).astype(_BF))
                    m_blocks.append(m_r)
                m = jnp.concatenate(m_blocks, axis=0)
                pv = _dot(jnp.concatenate(p_blocks, axis=0), vb)
                o = pv[:, 0:LANES] / (pv[:, LANES:2 * LANES] + jnp.exp2(sink - m))
                for pb in range(2):
                    blk = 2 * j + pb
                    ob = jnp.where(lo_half, o[(2 * pb) * CHUNK:(2 * pb + 1) * CHUNK],
                                   o[(2 * pb + 1) * CHUNK:(2 * pb + 2) * CHUNK])
                    ycat_ref[r0:r0 + CHUNK, blk * LANES:(blk + 1) * LANES] = (
                        ob * gate_ref[r0:r0 + CHUNK, blk * LANES:(blk + 1) * LANES])

    if gla_steps_at == n_blocks:
        gla_state_steps()

    if carry:
        kd_ref[:, 0:WINDOW, :] = kd_ref[:, seq_rows:seq_rows + WINDOW, :]
        vd_ref[:, 0:WINDOW, :] = vd_ref[:, seq_rows:seq_rows + WINDOW, :]

    o_all = jnp.concatenate(o_rows, axis=0)
    ycat_ref[:, ATTN_DIM:ATTN_DIM + GLA_VDIM] = _group_rms_scale(o_all, GLA_DV, ones64) * gla_g * gate_g

    for s in range(n_seq):
        sbd_ref[...] = s_ref[s].T
        for h in range(GLA_HEADS):
            ns_ref[s, h] = sbd_ref[h * GLA_DK:(h + 1) * GLA_DK, h * GLA_DV:(h + 1) * GLA_DV]

    last_layer = layer == pl.num_programs(2) - 1
    next_gain = layer_row("norm_gain", jnp.where(last_layer, 0, layer + 1))
    for r0 in range(0, tile, OUT_ROW_BLOCK):
        r1 = r0 + OUT_ROW_BLOCK
        y = x[r0:r1] + _dot(ycat_ref[r0:r1, :].astype(_BF), w_out_ref[...])
        y_ref[r0:r1, :] = y
        xcur_ref[r0:r1, :] = y
        hb_ref[r0:r1, :] = normed_bf16(jnp.where(last_layer, xnext_ref[r0:r1, :], y), next_gain)


def _run_stream(x2d, n_seqs, seq_len, hist, sinks, params, *, carry, q_pos0):
    depth = sinks.shape[0]
    if carry:
        assert q_pos0 == 0
        tile, n_seq, n_chunk = PROMPT_TILE, 1, PROMPT_TILE // CHUNK
        grid = (n_seqs, seq_len // tile, depth)
    else:
        assert seq_len == CHUNK and q_pos0 >= WINDOW
        tile, n_seq, n_chunk = SAMPLE_SEQS * CHUNK, SAMPLE_SEQS, 1
        grid = (n_seqs // n_seq, 1, depth)
    n_t = grid[1]
    seq_rows = n_chunk * CHUNK

    def const(shape):
        return pl.BlockSpec(shape, lambda g, t, l: (0,) * len(shape), pipeline_mode=pl.Buffered(1))

    def per_layer_group(shape):
        return pl.BlockSpec((depth, n_seq) + shape, lambda g, t, l: (0, g) + (0,) * len(shape))

    state_shapes = ((WINDOW, KV_DIM), (WINDOW, KV_DIM), (GLA_HEADS, GLA_DK, GLA_DV), (CONV_HIST, CONV_DIM))
    x_spec = pl.BlockSpec((tile, D_MODEL), lambda g, t, l: (g * n_t + t, 0))
    n_tiles = x2d.shape[0] // tile
    xnext_spec = pl.BlockSpec((tile, D_MODEL), lambda g, t, l: (jnp.minimum(g * n_t + t + 1, n_tiles - 1), 0))
    in_specs = [pl.BlockSpec(memory_space=pltpu.SMEM), x_spec, xnext_spec]
    args = [sinks, x2d, x2d]
    if not carry:
        in_specs += [per_layer_group(sh) for sh in state_shapes]
        args += list(hist)
    operands = tuple(params) + _shape_constants(tile)
    in_specs += [const(p.shape) for p in operands]
    in_specs[len(args)] = const(operands[0].shape[:2] + (GLR_SRC,))
    args += list(operands)

    out_shape = (jax.ShapeDtypeStruct(x2d.shape, _F32),) + tuple(
        jax.ShapeDtypeStruct((depth, n_seqs) + sh, _F32) for sh in state_shapes)
    out_specs = (x_spec,) + tuple(per_layer_group(sh) for sh in state_shapes)
    scratch = [pltpu.VMEM((depth, n_seq, WINDOW + seq_rows + CHUNK, ATTN_KV_HEADS * LANES), _BF),
               pltpu.VMEM((depth, n_seq, WINDOW + seq_rows + CHUNK, ATTN_KV_HEADS * 2 * LANES), _BF),
               pltpu.VMEM((depth, n_seq, GLA_VDIM, GLA_KDIM), _F32),
               pltpu.VMEM((depth, n_seq, CONV_PAD + seq_rows, CONV_DIM), _F32),
               pltpu.VMEM((tile, ATTN_DIM), _BF),
               pltpu.VMEM((tile, ATTN_DIM), _BF),
               pltpu.VMEM((tile, D_MODEL), _F32),
               pltpu.VMEM((tile, IN_COLS_PACKED), _F32),
               pltpu.VMEM((tile, D_MODEL), _F32),
               pltpu.VMEM((tile, D_MODEL), _BF),
               pltpu.VMEM((GLA_KDIM, GLA_VDIM), _F32),
               pltpu.VMEM((tile, D_MODEL), _F32),
               pltpu.VMEM((depth, D_MODEL, LANES), _BF)]
    body = functools.partial(_stream_kernel, tile=tile, n_seq=n_seq, n_chunk=n_chunk, carry=carry, q_pos0=q_pos0)
    return pl.pallas_call(
        body, grid=grid, in_specs=in_specs, out_specs=out_specs, out_shape=out_shape,
        scratch_shapes=scratch,
        compiler_params=pltpu.CompilerParams(dimension_semantics=("arbitrary", "arbitrary", "arbitrary"),
                                             vmem_limit_bytes=VMEM_LIMIT_BYTES),
        name="mixer_prompt" if carry else "mixer_sample",
    )(*args)


def _pack_params(norm_gain, w_in, q_norm_gain, k_norm_gain, gla_w_gate_up, gla_b_gate, gla_norm_gain,
                 conv_w, conv_b, conv_ln_gain, conv_ln_bias, conv_w_pw, conv_b_pw, w_out):
    w_in_a = w_in.astype(_BF)
    w_in_c = jnp.pad(w_in_a[:, :, GLR_SRC:GLR_SRC + GLA_LOWRANK], ((0, 0), (0, 0), (0, LANES - GLA_LOWRANK)))
    w_in_b = w_in_a[:, :, GLR_SRC + GLA_LOWRANK:]
    pieces = dict(norm_gain=norm_gain, q_gain=q_norm_gain, k_gain=k_norm_gain, gla_b=gla_b_gate,
                  gla_gain=gla_norm_gain, conv_b=conv_b, ln_gain=conv_ln_gain, ln_bias=conv_ln_bias, b_pw=conv_b_pw)
    depth = w_in.shape[0]
    parts = []
    for name, n in VEC_LAYOUT:
        parts += [pieces[name].astype(_F32)] * (n // pieces[name].shape[1])
        parts.append(jnp.zeros((depth, D_MODEL - n), _F32))
    parts.append(jnp.zeros((depth, (VEC_ROWS - len(VEC_LAYOUT)) * D_MODEL), _F32))
    vecs = jnp.concatenate(parts, axis=1).reshape(depth, VEC_ROWS, D_MODEL)
    w_up_p = jnp.pad(gla_w_gate_up, ((0, 0), (0, LANES - GLA_LOWRANK), (0, 0))).astype(_BF)
    convw = jnp.pad(conv_w.astype(_F32), ((0, 0), (0, CONV_PAD - CONV_WIDTH), (0, 0)))
    return w_in_a, w_in_b, w_in_c, w_up_p, conv_w_pw.astype(_BF), w_out.astype(_BF), vecs, convw


def kernel(x_prompt, x_sample, cache_k, cache_v, state_gla, state_conv, norm_gain, w_in, q_norm_gain, k_norm_gain, attn_sinks, gla_w_gate_up, gla_b_gate, gla_norm_gain, conv_w, conv_b, conv_ln_gain, conv_ln_bias, conv_w_pw, conv_b_pw, w_out):
    depth = w_in.shape[0]
    bp, lp, _ = x_prompt.shape
    bs, ls, _ = x_sample.shape
    params = _pack_params(norm_gain, w_in, q_norm_gain, k_norm_gain, gla_w_gate_up, gla_b_gate, gla_norm_gain,
                          conv_w, conv_b, conv_ln_gain, conv_ln_bias, conv_w_pw, conv_b_pw, w_out)
    sinks = jnp.pad(attn_sinks.astype(_F32), ((0, 0), (0, ATTN_HEADS)))
    yp, pk, pv, ps, pc = _run_stream(x_prompt.reshape(bp * lp, D_MODEL), bp, lp, None, sinks, params,
                                     carry=True, q_pos0=0)
    hist = (cache_k.reshape(depth, bs, WINDOW, KV_DIM), cache_v.reshape(depth, bs, WINDOW, KV_DIM),
            state_gla, state_conv)
    ys, sk, sv, ss, sc = _run_stream(x_sample.reshape(bs * ls, D_MODEL), bs, ls, hist, sinks, params,
                                     carry=False, q_pos0=PAST_LEN)
    kv_shape = (WINDOW, ATTN_KV_HEADS, HEAD_DIM)
    return (yp.reshape(bp, lp, D_MODEL), ys.reshape(bs, ls, D_MODEL),
            pk.reshape((depth, bp) + kv_shape), pv.reshape((depth, bp) + kv_shape), ps, pc,
            sk.reshape((depth, bs) + kv_shape), sv.reshape((depth, bs) + kv_shape), ss, sc)
```
